```python
import jax
import jax.numpy as jnp
from jax import lax
import numpy as np

D_MODEL = 1024
BATCH = 8
SEQ = 4096
DEPTH = 1

HEAD_DIM = 64
N_Q_HEADS = D_MODEL // HEAD_DIM
N_KV_HEADS = N_Q_HEADS // 8
WINDOW = 128
ROPE_THETA = 10000.0
ATTN_WIDTH = N_Q_HEADS * HEAD_DIM
KV_WIDTH = N_KV_HEADS * HEAD_DIM
POOL_WINDOWS = (2, 4, 8, 16)
POOL_WIDTH = D_MODEL // 2
POOL_GROUP = POOL_WIDTH // len(POOL_WINDOWS)
N_BRANCHES = 2
IN_WIDTH = ATTN_WIDTH + 2 * KV_WIDTH + POOL_WIDTH + N_BRANCHES * D_MODEL
IN_SPLITS = (ATTN_WIDTH, ATTN_WIDTH + KV_WIDTH, ATTN_WIDTH + 2 * KV_WIDTH,
             ATTN_WIDTH + 2 * KV_WIDTH + POOL_WIDTH)
N_EXPERTS = 32
TOP_K = 4
D_FF = D_MODEL
SWIGLU_LIMIT = 7.0
SWIGLU_ALPHA = 1.702
MOE_BLOCK = 256
RMS_EPS = 1e-5
MAX_POS_OFFSET = 1024

kernel_name = 'hybrid_swa_sink_pool_moe_block'


def rms_norm(x, g):
    xf = x.astype(jnp.float32)
    y = xf * lax.rsqrt(jnp.mean(xf * xf, axis=-1, keepdims=True) + RMS_EPS)
    return (y * g.astype(jnp.float32)).astype(x.dtype)


def apply_rope(t, positions):
    inv_freq = ROPE_THETA ** (-jnp.arange(0, HEAD_DIM, 2, dtype=jnp.float32) / HEAD_DIM)
    ang = positions.astype(jnp.float32)[..., None] * inv_freq
    cos = jnp.cos(ang)[:, :, None, :]
    sin = jnp.sin(ang)[:, :, None, :]
    tf = t.astype(jnp.float32)
    t1, t2 = tf[..., :HEAD_DIM // 2], tf[..., HEAD_DIM // 2:]
    out = jnp.concatenate([t1 * cos - t2 * sin, t2 * cos + t1 * sin], axis=-1)
    return out.astype(t.dtype)


def sliding_window_attention(q, k, v, sinks):
    B, S = q.shape[0], q.shape[1]
    nb = S // WINDOW
    grp = N_Q_HEADS // N_KV_HEADS
    qb = q.reshape(B, nb, WINDOW, N_KV_HEADS, grp, HEAD_DIM)

    def band(t):
        tb = t.reshape(B, nb, WINDOW, N_KV_HEADS, HEAD_DIM)
        prev = jnp.concatenate([jnp.zeros_like(tb[:, :1]), tb[:, :-1]], axis=1)
        return jnp.concatenate([prev, tb], axis=2)

    kk, vv = band(k), band(v)
    scale = HEAD_DIM ** -0.5
    s = jnp.einsum('bnqhgd,bnkhd->bnhgqk', qb, kk,
                   preferred_element_type=jnp.float32) * scale
    qi = jnp.arange(WINDOW)[:, None]
    kj = jnp.arange(2 * WINDOW)[None, :]
    dist = qi + WINDOW - kj
    band_mask = (dist >= 0) & (dist < WINDOW)
    key_valid = (jnp.arange(nb)[:, None] * WINDOW - WINDOW + jnp.arange(2 * WINDOW)[None, :]) >= 0
    mask = band_mask[None] & key_valid[:, None, :]
    s = jnp.where(mask[None, :, None, None], s, -jnp.inf)
    sink = sinks.astype(jnp.float32).reshape(1, 1, N_KV_HEADS, grp, 1, 1)
    m = jnp.maximum(jnp.max(s, axis=-1, keepdims=True), sink)
    p = jnp.exp(s - m)
    denom = jnp.sum(p, axis=-1, keepdims=True) + jnp.exp(sink - m)
    probs = (p / denom).astype(v.dtype)
    o = jnp.einsum('bnhgqk,bnkhd->bnqhgd', probs, vv)
    return o.reshape(B, S, N_Q_HEADS * HEAD_DIM)


def multiscale_pool(u):
    B, S, _ = u.shape
    uf = u.astype(jnp.float32)
    c = jnp.cumsum(uf, axis=1)
    t = jnp.arange(S)
    outs = []
    for gi, w in enumerate(POOL_WINDOWS):
        sl = slice(gi * POOL_GROUP, (gi + 1) * POOL_GROUP)
        cg = c[..., sl]
        lower = jnp.concatenate([jnp.zeros((B, w, POOL_GROUP), jnp.float32), cg[:, :S - w]], axis=1)
        cnt = jnp.minimum(t + 1, w).astype(jnp.float32)[None, :, None]
        outs.append((cg - lower) / cnt - uf[..., sl])
    return jnp.stack(outs, axis=2).astype(u.dtype)


def token_mixer(h, positions, w_in, b_in, attn_sinks, w_o_attn, b_o_attn,
                w_pool_mix, pool_scale, w_pool_up, w_out):
    B, S, _ = h.shape
    z = h @ w_in + b_in
    q, k, v, u, gate_logits = jnp.split(z, IN_SPLITS, axis=-1)
    q = apply_rope(q.reshape(B, S, N_Q_HEADS, HEAD_DIM), positions)
    k = apply_rope(k.reshape(B, S, N_KV_HEADS, HEAD_DIM), positions)
    v = v.reshape(B, S, N_KV_HEADS, HEAD_DIM)
    y_attn = sliding_window_attention(q, k, v, attn_sinks) @ w_o_attn + b_o_attn
    pooled = multiscale_pool(u)
    mixed = jnp.einsum('bsgc,gcd->bsgd', pooled, w_pool_mix).reshape(B, S, POOL_WIDTH)
    y_pool = (mixed * pool_scale) @ w_pool_up
    gates = jax.nn.sigmoid(gate_logits.astype(jnp.float32)).astype(h.dtype)
    gates = gates.reshape(B, S, N_BRANCHES, D_MODEL)
    merged = gates[:, :, 0] * y_attn + gates[:, :, 1] * y_pool
    return merged @ w_out


def moe_ffn(h, w_router, b_router, w_gate, b_gate, w_up, b_up, w_down, b_down):
    B, S, D = h.shape
    N = B * S
    hf = h.reshape(N, D)
    logits = (hf @ w_router + b_router).astype(jnp.float32)
    top_v, top_e = lax.top_k(logits, TOP_K)
    comb = jax.nn.softmax(top_v, axis=-1)
    M = N * TOP_K
    flat_e = top_e.reshape(M).astype(jnp.int32)
    flat_w = comb.reshape(M)
    flat_tok = jnp.arange(M, dtype=jnp.int32) // TOP_K
    order = jnp.argsort(flat_e)
    s_e, s_tok, s_w = flat_e[order], flat_tok[order], flat_w[order]
    sizes = jnp.bincount(flat_e, length=N_EXPERTS).astype(jnp.int32)
    starts = jnp.cumsum(sizes) - sizes
    padded = (sizes + MOE_BLOCK - 1) // MOE_BLOCK * MOE_BLOCK
    pends = jnp.cumsum(padded)
    pstarts = pends - padded
    dest = pstarts[s_e] + jnp.arange(M, dtype=jnp.int32) - starts[s_e]
    n_blocks = (M + N_EXPERTS * (MOE_BLOCK - 1) + MOE_BLOCK - 1) // MOE_BLOCK
    P = n_blocks * MOE_BLOCK
    row_tok = jnp.full((P,), N, jnp.int32).at[dest].set(s_tok)
    row_w = jnp.zeros((P,), jnp.float32).at[dest].set(s_w)
    block_e = jnp.searchsorted(pends, jnp.arange(n_blocks, dtype=jnp.int32) * MOE_BLOCK, side='right')
    block_e = jnp.minimum(block_e, N_EXPERTS - 1).astype(jnp.int32)
    h_pad = jnp.concatenate([hf, jnp.zeros((1, D), hf.dtype)], axis=0)
    xs = h_pad[row_tok].reshape(n_blocks, MOE_BLOCK, D)

    def expert_block(args):
        xb, e = args
        g = xb @ w_gate[e] + b_gate[e]
        u = xb @ w_up[e] + b_up[e]
        g = jnp.minimum(g, SWIGLU_LIMIT)
        u = jnp.clip(u, -SWIGLU_LIMIT, SWIGLU_LIMIT)
        a = g * jax.nn.sigmoid(SWIGLU_ALPHA * g) * (u + 1.0)
        return a @ w_down[e] + b_down[e]

    ys = lax.map(expert_block, (xs, block_e)).reshape(P, D)
    ys = ys * row_w.astype(ys.dtype)[:, None]
    out = jax.ops.segment_sum(ys, row_tok, num_segments=N + 1)[:N]
    return out.reshape(B, S, D)


def setup_inputs(seed: int = 0) -> dict:
    key = jax.random.key(seed)
    ks = jax.random.split(key, 24)
    f32 = jnp.float32
    L = DEPTH

    def nrm(k, shape, scale):
        return jax.random.normal(k, shape, f32) * scale

    x = nrm(ks[0], (BATCH, SEQ, D_MODEL), 1.0)
    offsets = jax.random.randint(ks[1], (BATCH, 1), 0, MAX_POS_OFFSET, dtype=jnp.int32)
    positions = offsets + jnp.arange(SEQ, dtype=jnp.int32)[None, :]
    return {
        'x': x,
        'positions': positions,
        'norm_mix_g': 1.0 + nrm(ks[2], (L, D_MODEL), 0.02),
        'w_in': nrm(ks[3], (L, D_MODEL, IN_WIDTH), D_MODEL ** -0.5),
        'b_in': nrm(ks[4], (L, IN_WIDTH), 0.02),
        'attn_sinks': nrm(ks[5], (L, N_Q_HEADS), 0.5),
        'w_o_attn': nrm(ks[6], (L, ATTN_WIDTH, D_MODEL), ATTN_WIDTH ** -0.5),
        'b_o_attn': nrm(ks[7], (L, D_MODEL), 0.02),
        'w_pool_mix': nrm(ks[8], (L, len(POOL_WINDOWS), POOL_GROUP, POOL_GROUP), POOL_GROUP ** -0.5),
        'pool_scale': 1.0 + nrm(ks[9], (L, POOL_WIDTH), 0.02),
        'w_pool_up': nrm(ks[10], (L, POOL_WIDTH, D_MODEL), POOL_WIDTH ** -0.5),
        'w_out': nrm(ks[11], (L, D_MODEL, D_MODEL), D_MODEL ** -0.5),
        'norm_ffn_g': 1.0 + nrm(ks[12], (L, D_MODEL), 0.02),
        'w_router': nrm(ks[13], (L, D_MODEL, N_EXPERTS), D_MODEL ** -0.5),
        'b_router': nrm(ks[14], (L, N_EXPERTS), 0.01),
        'w_gate': nrm(ks[15], (L, N_EXPERTS, D_MODEL, D_FF), D_MODEL ** -0.5),
        'b_gate': nrm(ks[16], (L, N_EXPERTS, D_FF), 0.02),
        'w_up': nrm(ks[17], (L, N_EXPERTS, D_MODEL, D_FF), D_MODEL ** -0.5),
        'b_up': nrm(ks[18], (L, N_EXPERTS, D_FF), 0.02),
        'w_down': nrm(ks[19], (L, N_EXPERTS, D_FF, D_MODEL), D_FF ** -0.5),
        'b_down': nrm(ks[20], (L, N_EXPERTS, D_MODEL), 0.02),
        'norm_final_g': 1.0 + nrm(ks[21], (D_MODEL,), 0.02),
    }


def reference(x, positions, norm_mix_g, w_in, b_in, attn_sinks, w_o_attn, b_o_attn,
              w_pool_mix, pool_scale, w_pool_up, w_out, norm_ffn_g, w_router, b_router,
              w_gate, b_gate, w_up, b_up, w_down, b_down, norm_final_g):
    for l in range(DEPTH):
        h = rms_norm(x, norm_mix_g[l])
        x = x + token_mixer(h, positions, w_in[l], b_in[l], attn_sinks[l], w_o_attn[l], b_o_attn[l],
                            w_pool_mix[l], pool_scale[l], w_pool_up[l], w_out[l])
        h = rms_norm(x, norm_ffn_g[l])
        x = x + moe_ffn(h, w_router[l], b_router[l], w_gate[l], b_gate[l], w_up[l], b_up[l],
                        w_down[l], b_down[l])
    return rms_norm(x, norm_final_g)
```

```python
import functools

import numpy as np
import jax
import jax.numpy as jnp
from jax import lax
from jax.experimental import pallas as pl
from jax.experimental.pallas import tpu as pltpu

F32 = jnp.float32
BF16 = jnp.bfloat16
I32 = jnp.int32

D_MODEL = 1024
SEQ = 4096
HEAD_DIM = 64
N_Q_HEADS = 16
WINDOW = 128
ROPE_THETA = 10000.0
Q_WIDTH = N_Q_HEADS * HEAD_DIM
KV_WIDTH = 128
POOL_WINDOWS = (2, 4, 8, 16)
POOL_WIDTH = 512
POOL_GROUP = 128
POOL_HALO = 16
N_EXPERTS = 32
TOP_K = 4
SWIGLU_LIMIT = 7.0
SWIGLU_ALPHA = 1.702
RMS_EPS = 1e-5
NEG_BIG = -1e30

LANES = 128
TM_IN = 512
TQ = 512
TM_POST = 512
TM_MOE = 512
TT = 256
VMEM_LIMIT = 56 * 1024 * 1024


def _rms_norm(x, g):
    ms = jnp.mean(x * x, axis=-1, keepdims=True)
    return (x * lax.rsqrt(ms + RMS_EPS)) * g


def _dot(a, b):
    return jnp.dot(a, b, preferred_element_type=F32)


def _inproj_body(x_ref, pos_ref, invf_ref, g_ref, w_ref, b_ref,
                 q_ref, kb_ref, vb_ref, u_ref, gate_ref):
    h = _rms_norm(x_ref[...], g_ref[...]).astype(BF16)

    theta = pos_ref[...].astype(F32) * invf_ref[...]
    cos = jnp.cos(theta)
    sin = jnp.sin(theta)
    lane = lax.broadcasted_iota(I32, theta.shape, 1)
    first_half = (lane & 32) == 0
    sin_signed = jnp.where(first_half, -sin, sin)
    low_head = lane < HEAD_DIM

    def rope(t):
        swapped = jnp.where(first_half, pltpu.roll(t, 96, 1), pltpu.roll(t, 32, 1))
        return t * cos + swapped * sin_signed

    def band_layout(t, out_ref):
        tr = pltpu.roll(t, 64, 1)
        zero = jnp.zeros_like(t)
        chunks = (jnp.where(low_head, t, zero), jnp.where(low_head, zero, tr),
                  jnp.where(low_head, tr, zero), jnp.where(low_head, zero, t))
        for c, val in enumerate(chunks):
            out_ref[:, c * LANES:(c + 1) * LANES] = val.astype(BF16)

    qk_w = Q_WIDTH + KV_WIDTH
    zqk = _dot(h, w_ref[:, :qk_w]) + b_ref[:, :qk_w]
    scale = HEAD_DIM ** -0.5
    for j in range(Q_WIDTH // LANES):
        sl = slice(j * LANES, (j + 1) * LANES)
        q_ref[:, sl] = (rope(zqk[:, sl]) * scale).astype(BF16)
    band_layout(rope(zqk[:, Q_WIDTH:qk_w]), kb_ref)

    v0 = qk_w
    zv = _dot(h, w_ref[:, v0:v0 + KV_WIDTH]) + b_ref[:, v0:v0 + KV_WIDTH]
    band_layout(zv, vb_ref)

    u0 = v0 + KV_WIDTH
    u_ref[...] = _dot(h, w_ref[:, u0:u0 + POOL_WIDTH]) + b_ref[:, u0:u0 + POOL_WIDTH]

    g0 = u0 + POOL_WIDTH
    for c in range(2):
        sl = slice(g0 + c * D_MODEL, g0 + (c + 1) * D_MODEL)
        zg = _dot(h, w_ref[:, sl]) + b_ref[:, sl]
        gate_ref[:, c * D_MODEL:(c + 1) * D_MODEL] = jax.nn.sigmoid(zg).astype(BF16)


def _inproj(x2, pos_b, invf, g, w, b):
    n = x2.shape[0]
    in_width = w.shape[1]
    row = lambda i: (i, 0)
    const = lambda i: (0, 0)
    return pl.pallas_call(
        _inproj_body,
        grid=(n // TM_IN,),
        in_specs=[
            pl.BlockSpec((TM_IN, D_MODEL), row),
            pl.BlockSpec((TM_IN, LANES), row),
            pl.BlockSpec((1, LANES), const),
            pl.BlockSpec((1, D_MODEL), const),
            pl.BlockSpec((D_MODEL, in_width), const),
            pl.BlockSpec((1, in_width), const),
        ],
        out_specs=[
            pl.BlockSpec((TM_IN, Q_WIDTH), row),
            pl.BlockSpec((TM_IN, 4 * LANES), row),
            pl.BlockSpec((TM_IN, 4 * LANES), row),
            pl.BlockSpec((TM_IN, POOL_WIDTH), row),
            pl.BlockSpec((TM_IN, 2 * D_MODEL), row),
        ],
        out_shape=[
            jax.ShapeDtypeStruct((n, Q_WIDTH), BF16),
            jax.ShapeDtypeStruct((n, 4 * LANES), BF16),
            jax.ShapeDtypeStruct((n, 4 * LANES), BF16),
            jax.ShapeDtypeStruct((n, POOL_WIDTH), F32),
            jax.ShapeDtypeStruct((n, 2 * D_MODEL), BF16),
        ],
        compiler_params=pltpu.CompilerParams(
            dimension_semantics=("arbitrary",), vmem_limit_bytes=VMEM_LIMIT),
        name="inproj",
    )(x2, pos_b, invf, g, w, b)


def _attn_body(sinks_ref, q_ref, kbc_ref, kbp_ref, vbc_ref, vbp_ref, bias_ref,
               o_ref, p_scr):
    t = pl.program_id(0)
    seq_first = (t % (SEQ // TQ)) == 0
    lane = lax.broadcasted_iota(I32, (WINDOW, LANES), 1)
    low_head = lane < HEAD_DIM
    for n in range(TQ // WINDOW):
        rows = slice(n * WINDOW, (n + 1) * WINDOW)
        if n == 0:
            k_prev, v_prev = kbp_ref[...], vbp_ref[...]
            bias = jnp.where(seq_first, bias_ref[1], bias_ref[0])
        else:
            prev_rows = slice((n - 1) * WINDOW, n * WINDOW)
            k_prev, v_prev = kbc_ref[prev_rows, :], vbc_ref[prev_rows, :]
            bias = bias_ref[0]
        k_cur, v_cur = kbc_ref[rows, :], vbc_ref[rows, :]
        for g in range(2):
            def stack(prev, cur):
                lo = slice((2 * g) * LANES, (2 * g + 1) * LANES)
                hi = slice((2 * g + 1) * LANES, (2 * g + 2) * LANES)
                return jnp.concatenate([prev[:, lo], cur[:, lo], prev[:, hi], cur[:, hi]], axis=0)
            kmat = stack(k_prev, k_cur)
            vmat = stack(v_prev, v_cur)
            qs = jnp.concatenate(
                [q_ref[rows, (4 * g + p) * LANES:(4 * g + p + 1) * LANES] for p in range(4)],
                axis=0)
            s = lax.dot_general(qs, kmat, (((1,), (1,)), ((), ())),
                                preferred_element_type=F32) + bias
            inv = []
            for p in range(4):
                pr = slice(p * WINDOW, (p + 1) * WINDOW)
                inv_p = []
                for j in range(2):
                    cols = slice(j * 2 * WINDOW, (j + 1) * 2 * WINDOW)
                    sp = s[pr, cols]
                    sink = sinks_ref[8 * g + 2 * p + j]
                    m = jnp.maximum(jnp.max(sp, axis=-1, keepdims=True), sink)
                    e = jnp.exp(sp - m)
                    den = jnp.sum(e, axis=-1, keepdims=True) + jnp.exp(sink - m)
                    p_scr[pr, cols] = e.astype(BF16)
                    inv_p.append(1.0 / den)
                inv.append(inv_p)
            o = _dot(p_scr[...], vmat)
            for p in range(4):
                pr = slice(p * WINDOW, (p + 1) * WINDOW)
                norm = jnp.where(low_head, inv[p][0], inv[p][1])
                o_ref[rows, (4 * g + p) * LANES:(4 * g + p + 1) * LANES] = (o[pr, :] * norm).astype(BF16)


def _attn_bias():
    r = np.arange(4 * WINDOW)[:, None] % WINDOW
    c = np.arange(4 * WINDOW)[None, :] % (2 * WINDOW)
    band = (c > r) & (c <= r + WINDOW)
    first = band & (c >= WINDOW)
    return np.stack([np.where(band, 0.0, NEG_BIG), np.where(first, 0.0, NEG_BIG)]).astype(np.float32)


def _attn(sinks, q, kb, vb):
    n = q.shape[0]
    blocks_per_tile = TQ // WINDOW
    cur = lambda t: (t, 0)
    prev = lambda t: (jnp.maximum(t * blocks_per_tile - 1, 0), 0)
    bias = jnp.asarray(_attn_bias())
    return pl.pallas_call(
        _attn_body,
        grid=(n // TQ,),
        in_specs=[
            pl.BlockSpec(memory_space=pltpu.SMEM),
            pl.BlockSpec((TQ, Q_WIDTH), cur),
            pl.BlockSpec((TQ, 4 * LANES), cur),
            pl.BlockSpec((WINDOW, 4 * LANES), prev),
            pl.BlockSpec((TQ, 4 * LANES), cur),
            pl.BlockSpec((WINDOW, 4 * LANES), prev),
            pl.BlockSpec((2, 4 * WINDOW, 4 * WINDOW), lambda t: (0, 0, 0)),
        ],
        out_specs=pl.BlockSpec((TQ, Q_WIDTH), cur),
        out_shape=jax.ShapeDtypeStruct((n, Q_WIDTH), BF16),
        scratch_shapes=[pltpu.VMEM((4 * WINDOW, 4 * WINDOW), BF16)],
        compiler_params=pltpu.CompilerParams(
            dimension_semantics=("arbitrary",), vmem_limit_bytes=VMEM_LIMIT),
        name="attn",
    )(sinks, q, kb, kb, vb, vb, bias)


def _post_body(attn_ref, u_ref, uprev_ref, gate_ref, x_ref,
               wo_ref, bo_ref, wmix_ref, pscale_ref, wup_ref, wout_ref,
               gffn_ref, wr_ref, br_ref,
               x1_ref, h2_ref, ri_ref, rf_ref, cnt_ref,
               ubuf, carry):
    i = pl.program_id(0)
    tiles_per_seq = SEQ // TM_POST
    seq_first = (i % tiles_per_seq) == 0

    @pl.when(i == 0)
    def _():
        carry[...] = jnp.zeros_like(carry)

    ubuf[0:POOL_HALO, :] = jnp.where(seq_first, 0.0, uprev_ref[...])
    ubuf[POOL_HALO:, :] = u_ref[...]
    row = lax.broadcasted_iota(I32, (TM_POST, 1), 0)
    tpos = (i % tiles_per_seq) * TM_POST + row
    mixed = []
    for gi, w in enumerate(POOL_WINDOWS):
        cols = slice(gi * POOL_GROUP, (gi + 1) * POOL_GROUP)
        acc = ubuf[POOL_HALO:, cols]
        for d in range(1, w):
            acc = acc + ubuf[POOL_HALO - d:POOL_HALO - d + TM_POST, cols]
        cnt = jnp.minimum(tpos + 1, w).astype(F32)
        pooled = acc / cnt - ubuf[POOL_HALO:, cols]
        mixed.append(_dot(pooled.astype(BF16), wmix_ref[gi]) * pscale_ref[:, cols])
    mixed = jnp.concatenate(mixed, axis=-1).astype(BF16)
    y_pool = _dot(mixed, wup_ref[...])
    y_attn = _dot(attn_ref[...], wo_ref[...]) + bo_ref[...]
    merged = (gate_ref[:, :D_MODEL].astype(F32) * y_attn
              + gate_ref[:, D_MODEL:].astype(F32) * y_pool)
    x1 = x_ref[...] + _dot(merged.astype(BF16), wout_ref[...])
    x1_ref[...] = x1
    h2 = _rms_norm(x1, gffn_ref[...])
    h2_ref[...] = h2

    lane = lax.broadcasted_iota(I32, (TM_POST, LANES), 1)
    logits = _dot(h2.astype(BF16), wr_ref[...]) + br_ref[...]
    work = jnp.where(lane < N_EXPERTS, logits, -jnp.inf)
    top_v, top_e, onehots = [], [], []
    for _ in range(TOP_K):
        m = jnp.max(work, axis=-1, keepdims=True)
        idx = jnp.min(jnp.where(work == m, lane, LANES), axis=-1, keepdims=True)
        hit = lane == idx
        top_v.append(m)
        top_e.append(idx)
        onehots.append(hit)
        work = jnp.where(hit, -jnp.inf, work)
    exps = [jnp.exp(v - top_v[0]) for v in top_v]
    denom = exps[0] + exps[1] + exps[2] + exps[3]
    comb = [e / denom for e in exps]

    chosen = jnp.zeros((TM_POST, LANES), F32)
    for hit in onehots:
        chosen = chosen + hit.astype(F32)
    r_i = lax.broadcasted_iota(I32, (TM_POST, TM_POST), 0)
    c_i = lax.broadcasted_iota(I32, (TM_POST, TM_POST), 1)
    tril = (r_i > c_i).astype(BF16)
    before = _dot(tril, chosen.astype(BF16)) + carry[0:1, :]
    ri = jnp.zeros((TM_POST, LANES), I32)
    rf = jnp.zeros((TM_POST, LANES), F32)
    for k in range(TOP_K):
        rank = jnp.sum(jnp.where(onehots[k], before, 0.0), axis=-1, keepdims=True)
        ri = jnp.where(lane == k, top_e[k], ri)
        ri = jnp.where(lane == TOP_K + k, rank.astype(I32), ri)
        rf = jnp.where(lane == k, comb[k], rf)
    ri_ref[...] = ri
    rf_ref[...] = rf
    total = carry[0:1, :] + jnp.sum(chosen, axis=0, keepdims=True)
    carry[...] = jnp.broadcast_to(total, carry.shape)
    cnt_ref[...] = jnp.broadcast_to(total, cnt_ref.shape)


def _post(attn, u, gates, x2, wo, bo, wmix, pscale, wup, wout, gffn, wr, br):
    n = x2.shape[0]
    row = lambda i: (i, 0)
    const = lambda i: (0, 0)
    halo_blocks = TM_POST // POOL_HALO
    prev = lambda i: (jnp.maximum(i * halo_blocks - 1, 0), 0)
    return pl.pallas_call(
        _post_body,
        grid=(n // TM_POST,),
        in_specs=[
            pl.BlockSpec((TM_POST, Q_WIDTH), row),
            pl.BlockSpec((TM_POST, POOL_WIDTH), row),
            pl.BlockSpec((POOL_HALO, POOL_WIDTH), prev),
            pl.BlockSpec((TM_POST, 2 * D_MODEL), row),
            pl.BlockSpec((TM_POST, D_MODEL), row),
            pl.BlockSpec((Q_WIDTH, D_MODEL), const),
            pl.BlockSpec((1, D_MODEL), const),
            pl.BlockSpec((len(POOL_WINDOWS), POOL_GROUP, POOL_GROUP), lambda i: (0, 0, 0)),
            pl.BlockSpec((1, POOL_WIDTH), const),
            pl.BlockSpec((POOL_WIDTH, D_MODEL), const),
            pl.BlockSpec((D_MODEL, D_MODEL), const),
            pl.BlockSpec((1, D_MODEL), const),
            pl.BlockSpec((D_MODEL, LANES), const),
            pl.BlockSpec((1, LANES), const),
        ],
        out_specs=[
            pl.BlockSpec((TM_POST, D_MODEL), row),
            pl.BlockSpec((TM_POST, D_MODEL), row),
            pl.BlockSpec((TM_POST, LANES), row),
            pl.BlockSpec((TM_POST, LANES), row),
            pl.BlockSpec((8, LANES), const),
        ],
        out_shape=[
            jax.ShapeDtypeStruct((n, D_MODEL), F32),
            jax.ShapeDtypeStruct((n, D_MODEL), F32),
            jax.ShapeDtypeStruct((n, LANES), I32),
            jax.ShapeDtypeStruct((n, LANES), F32),
            jax.ShapeDtypeStruct((8, LANES), F32),
        ],
        scratch_shapes=[pltpu.VMEM((TM_POST + POOL_HALO, POOL_WIDTH), F32),
                        pltpu.VMEM((8, LANES), F32)],
        compiler_params=pltpu.CompilerParams(
            dimension_semantics=("arbitrary",), vmem_limit_bytes=VMEM_LIMIT),
        name="post",
    )(attn, u, u, gates, x2, wo, bo, wmix, pscale, wup, wout, gffn, wr, br)


def _row_copy(src, dst, sem):
    return pltpu.make_async_copy(src, dst, sem)


def _dispatch_body(dest_ref, h2_ref, xs_hbm, sem):
    def issue(r, c):
        for k in range(TOP_K):
            d = dest_ref[0, 0, r * TOP_K + k]
            _row_copy(h2_ref.at[pl.ds(r, 1)], xs_hbm.at[pl.ds(d, 1)], sem).start()
        return c
    lax.fori_loop(0, TT, issue, 0)

    def drain(r, c):
        for k in range(TOP_K):
            d = dest_ref[0, 0, r * TOP_K + k]
            _row_copy(h2_ref.at[pl.ds(r, 1)], xs_hbm.at[pl.ds(d, 1)], sem).wait()
        return c
    lax.fori_loop(0, TT, drain, 0)


def _dispatch(dest3, h2, n_rows):
    n = h2.shape[0]
    return pl.pallas_call(
        _dispatch_body,
        grid=(n // TT,),
        in_specs=[
            pl.BlockSpec((1, 1, TT * TOP_K), lambda i: (i, 0, 0), memory_space=pltpu.SMEM),
            pl.BlockSpec((TT, D_MODEL), lambda i: (i, 0)),
        ],
        out_specs=pl.BlockSpec(memory_space=pl.ANY),
        out_shape=jax.ShapeDtypeStruct((n_rows, D_MODEL), F32),
        scratch_shapes=[pltpu.SemaphoreType.DMA(())],
        compiler_params=pltpu.CompilerParams(
            dimension_semantics=("arbitrary",), vmem_limit_bytes=VMEM_LIMIT),
        name="dispatch",
    )(dest3, h2)


def _moe_body(te_ref, nused_ref, rows_ref, xs_ref, wg_ref, bg_ref, wu_ref, bu_ref,
              wd_ref, bd_ref, y_ref):
    i = pl.program_id(0)

    @pl.when(i < nused_ref[0])
    def _():
        row = lax.broadcasted_iota(I32, (TM_MOE, 1), 0)
        x = jnp.where(row < rows_ref[i], xs_ref[...], 0.0).astype(BF16)
        g = _dot(x, wg_ref[0]) + bg_ref[0]
        u = _dot(x, wu_ref[0]) + bu_ref[0]
        g = jnp.minimum(g, SWIGLU_LIMIT)
        u = jnp.clip(u, -SWIGLU_LIMIT, SWIGLU_LIMIT)
        a = g * jax.nn.sigmoid(SWIGLU_ALPHA * g) * (u + 1.0)
        y_ref[...] = _dot(a.astype(BF16), wd_ref[0]) + bd_ref[0]


def _moe(tile_expert, n_used, tile_rows, xs, wg, bg, wu, bu, wd, bd):
    n_rows = xs.shape[0]
    n_tiles = n_rows // TM_MOE
    d_ff = wg.shape[2]
    row = lambda i, te, nu, tr: (jnp.minimum(i, nu[0] - 1), 0)
    wsel = lambda i, te, nu, tr: (te[i], 0, 0)
    grid_spec = pltpu.PrefetchScalarGridSpec(
        num_scalar_prefetch=3,
        grid=(n_tiles,),
        in_specs=[
            pl.BlockSpec((TM_MOE, D_MODEL), row),
            pl.BlockSpec((1, D_MODEL, d_ff), wsel),
            pl.BlockSpec((1, 1, d_ff), wsel),
            pl.BlockSpec((1, D_MODEL, d_ff), wsel),
            pl.BlockSpec((1, 1, d_ff), wsel),
            pl.BlockSpec((1, d_ff, D_MODEL), wsel),
            pl.BlockSpec((1, 1, D_MODEL), wsel),
        ],
        out_specs=pl.BlockSpec((TM_MOE, D_MODEL), row),
    )
    return pl.pallas_call(
        _moe_body,
        grid_spec=grid_spec,
        out_shape=jax.ShapeDtypeStruct((n_rows, D_MODEL), F32),
        compiler_params=pltpu.CompilerParams(
            dimension_semantics=("arbitrary",), vmem_limit_bytes=VMEM_LIMIT),
        name="moe",
    )(tile_expert, n_used, tile_rows, xs, wg, bg, wu, bu, wd, bd)


def _combine_body(dest_ref, x1_ref, rf_ref, gfin_ref, ys_hbm, o_ref, buf, sem):
    def issue(r, c):
        for k in range(TOP_K):
            d = dest_ref[0, 0, r * TOP_K + k]
            _row_copy(ys_hbm.at[pl.ds(d, 1)], buf.at[k, pl.ds(r, 1)], sem).start()
        return c
    lax.fori_loop(0, TT, issue, 0)

    def drain(r, c):
        for k in range(TOP_K):
            d = dest_ref[0, 0, r * TOP_K + k]
            _row_copy(ys_hbm.at[pl.ds(d, 1)], buf.at[k, pl.ds(r, 1)], sem).wait()
        return c
    lax.fori_loop(0, TT, drain, 0)

    acc = x1_ref[...]
    for k in range(TOP_K):
        acc = acc + rf_ref[:, k:k + 1] * buf[k]
    o_ref[...] = _rms_norm(acc, gfin_ref[...])


def _combine(dest3, x1, rf, gfin, ys):
    n = x1.shape[0]
    return pl.pallas_call(
        _combine_body,
        grid=(n // TT,),
        in_specs=[
            pl.BlockSpec((1, 1, TT * TOP_K), lambda i: (i, 0, 0), memory_space=pltpu.SMEM),
            pl.BlockSpec((TT, D_MODEL), lambda i: (i, 0)),
            pl.BlockSpec((TT, LANES), lambda i: (i, 0)),
            pl.BlockSpec((1, D_MODEL), lambda i: (0, 0)),
            pl.BlockSpec(memory_space=pl.ANY),
        ],
        out_specs=pl.BlockSpec((TT, D_MODEL), lambda i: (i, 0)),
        out_shape=jax.ShapeDtypeStruct((n, D_MODEL), F32),
        scratch_shapes=[pltpu.VMEM((TOP_K, TT, D_MODEL), F32),
                        pltpu.SemaphoreType.DMA(())],
        compiler_params=pltpu.CompilerParams(
            dimension_semantics=("arbitrary",), vmem_limit_bytes=VMEM_LIMIT),
        name="combine",
    )(dest3, x1, rf, gfin, ys)


def _layer(x2, pos_b, invf, norm_mix_g, w_in, b_in, attn_sinks, w_o_attn, b_o_attn,
           w_pool_mix, pool_scale, w_pool_up, w_out, norm_ffn_g, w_router, b_router,
           w_gate, b_gate, w_up, b_up, w_down, b_down, out_g):
    n = x2.shape[0]
    q, kb, vb, u, gates = _inproj(
        x2, pos_b, invf, norm_mix_g[None, :], w_in.astype(BF16), b_in[None, :])
    attn = _attn(attn_sinks, q, kb, vb)

    wr = jnp.zeros((D_MODEL, LANES), BF16).at[:, :N_EXPERTS].set(w_router.astype(BF16))
    br = jnp.zeros((1, LANES), F32).at[0, :N_EXPERTS].set(b_router)
    x1, h2, ri, rf, cnt = _post(
        attn, u, gates, x2, w_o_attn.astype(BF16), b_o_attn[None, :],
        w_pool_mix.astype(BF16), pool_scale[None, :], w_pool_up.astype(BF16),
        w_out.astype(BF16), norm_ffn_g[None, :], wr, br)

    top_e = ri[:, :TOP_K]
    rank = ri[:, TOP_K:2 * TOP_K]
    sizes = cnt[0, :N_EXPERTS].astype(I32)
    padded = (sizes + TM_MOE - 1) // TM_MOE * TM_MOE
    pends = jnp.cumsum(padded)
    pstarts = pends - padded
    sel = top_e[..., None] == jnp.arange(N_EXPERTS, dtype=I32)
    dest = jnp.sum(jnp.where(sel, pstarts, 0), axis=-1) + rank
    m = n * TOP_K
    n_tiles = (m + N_EXPERTS * (TM_MOE - 1) + TM_MOE - 1) // TM_MOE
    tile_start = jnp.arange(n_tiles, dtype=I32) * TM_MOE
    tile_expert = jnp.minimum(
        jnp.sum(tile_start[:, None] >= pends[None, :], axis=-1), N_EXPERTS - 1).astype(I32)
    tile_rows = jnp.clip(sizes[tile_expert] - (tile_start - pstarts[tile_expert]), 0, TM_MOE)
    n_used = (pends[-1] // TM_MOE).astype(I32)[None]
    dest3 = dest.reshape(n // TT, 1, TT * TOP_K)

    xs = _dispatch(dest3, h2, n_tiles * TM_MOE)
    ys = _moe(tile_expert, n_used, tile_rows.astype(I32), xs,
              w_gate.astype(BF16), b_gate[:, None, :], w_up.astype(BF16), b_up[:, None, :],
              w_down.astype(BF16), b_down[:, None, :])
    return _combine(dest3, x1, rf, out_g[None, :], ys)


def kernel(x, positions, norm_mix_g, w_in, b_in, attn_sinks, w_o_attn, b_o_attn, w_pool_mix,
           pool_scale, w_pool_up, w_out, norm_ffn_g, w_router, b_router, w_gate, b_gate,
           w_up, b_up, w_down, b_down, norm_final_g):
    b, s, d = x.shape
    depth = w_in.shape[0]
    assert (s, d, depth) == (SEQ, D_MODEL, 1)
    n = b * s
    x2 = x.reshape(n, d)
    pos_b = jnp.broadcast_to(positions.reshape(n, 1), (n, LANES))
    inv_freq = ROPE_THETA ** (-jnp.arange(0, HEAD_DIM, 2, dtype=F32) / HEAD_DIM)
    invf = jnp.tile(inv_freq, LANES // (HEAD_DIM // 2))[None, :]
    out = _layer(x2, pos_b, invf, norm_mix_g[0], w_in[0], b_in[0], attn_sinks[0], w_o_attn[0],
                 b_o_attn[0], w_pool_mix[0], pool_scale[0], w_pool_up[0], w_out[0],
                 norm_ffn_g[0], w_router[0], b_router[0], w_gate[0], b_gate[0], w_up[0],
                 b_up[0], w_down[0], b_down[0], norm_final_g)
    return out.reshape(b, s, d)
```

```python
import functools

import numpy as np
import jax
import jax.numpy as jnp
from jax import lax
from jax.experimental import pallas as pl
from jax.experimental.pallas import tpu as pltpu

F32 = jnp.float32
BF16 = jnp.bfloat16
I32 = jnp.int32

D_MODEL = 1024
SEQ = 4096
HEAD_DIM = 64
N_Q_HEADS = 16
WINDOW = 128
ROPE_THETA = 10000.0
Q_WIDTH = N_Q_HEADS * HEAD_DIM
KV_WIDTH = 128
POOL_WINDOWS = (2, 4, 8, 16)
POOL_WIDTH = 512
POOL_GROUP = 128
POOL_HALO = 16
N_EXPERTS = 32
TOP_K = 4
SWIGLU_LIMIT = 7.0
SWIGLU_ALPHA = 1.702
RMS_EPS = 1e-5
NEG_BIG = -1e30

LANES = 128
TM_IN = 512
TQ = 512
TM_POST = 512
TM_MOE = 512
TT = 256
SEG_ALIGN = 8
LROWS = TT * TOP_K + N_EXPERTS * SEG_ALIGN
VMEM_LIMIT = 56 * 1024 * 1024


def _rms_norm(x, g):
    ms = jnp.mean(x * x, axis=-1, keepdims=True)
    return (x * lax.rsqrt(ms + RMS_EPS)) * g


def _dot(a, b):
    return jnp.dot(a, b, preferred_element_type=F32)


def _inproj_body(x_ref, pos_ref, invf_ref, g_ref, w_ref, b_ref,
                 q_ref, kb_ref, vb_ref, u_ref, gate_ref):
    h = _rms_norm(x_ref[...], g_ref[...]).astype(BF16)

    theta = pos_ref[...].astype(F32) * invf_ref[...]
    cos = jnp.cos(theta)
    sin = jnp.sin(theta)
    lane = lax.broadcasted_iota(I32, theta.shape, 1)
    first_half = (lane & 32) == 0
    sin_signed = jnp.where(first_half, -sin, sin)
    low_head = lane < HEAD_DIM

    def rope(t):
        swapped = jnp.where(first_half, pltpu.roll(t, 96, 1), pltpu.roll(t, 32, 1))
        return t * cos + swapped * sin_signed

    def band_layout(t, out_ref):
        tr = pltpu.roll(t, 64, 1)
        zero = jnp.zeros_like(t)
        chunks = (jnp.where(low_head, t, zero), jnp.where(low_head, zero, tr),
                  jnp.where(low_head, tr, zero), jnp.where(low_head, zero, t))
        for c, val in enumerate(chunks):
            out_ref[:, c * LANES:(c + 1) * LANES] = val.astype(BF16)

    qk_w = Q_WIDTH + KV_WIDTH
    zqk = _dot(h, w_ref[:, :qk_w]) + b_ref[:, :qk_w]
    scale = HEAD_DIM ** -0.5
    for j in range(Q_WIDTH // LANES):
        sl = slice(j * LANES, (j + 1) * LANES)
        q_ref[:, sl] = (rope(zqk[:, sl]) * scale).astype(BF16)
    band_layout(rope(zqk[:, Q_WIDTH:qk_w]), kb_ref)

    v0 = qk_w
    zv = _dot(h, w_ref[:, v0:v0 + KV_WIDTH]) + b_ref[:, v0:v0 + KV_WIDTH]
    band_layout(zv, vb_ref)

    u0 = v0 + KV_WIDTH
    u_ref[...] = _dot(h, w_ref[:, u0:u0 + POOL_WIDTH]) + b_ref[:, u0:u0 + POOL_WIDTH]

    g0 = u0 + POOL_WIDTH
    for c in range(2):
        sl = slice(g0 + c * D_MODEL, g0 + (c + 1) * D_MODEL)
        zg = _dot(h, w_ref[:, sl]) + b_ref[:, sl]
        gate_ref[:, c * D_MODEL:(c + 1) * D_MODEL] = jax.nn.sigmoid(zg).astype(BF16)


def _inproj(x2, pos_b, invf, g, w, b):
    n = x2.shape[0]
    in_width = w.shape[1]
    row = lambda i: (i, 0)
    const = lambda i: (0, 0)
    return pl.pallas_call(
        _inproj_body,
        grid=(n // TM_IN,),
        in_specs=[
            pl.BlockSpec((TM_IN, D_MODEL), row),
            pl.BlockSpec((TM_IN, LANES), row),
            pl.BlockSpec((1, LANES), const),
            pl.BlockSpec((1, D_MODEL), const),
            pl.BlockSpec((D_MODEL, in_width), const),
            pl.BlockSpec((1, in_width), const),
        ],
        out_specs=[
            pl.BlockSpec((TM_IN, Q_WIDTH), row),
            pl.BlockSpec((TM_IN, 4 * LANES), row),
            pl.BlockSpec((TM_IN, 4 * LANES), row),
            pl.BlockSpec((TM_IN, POOL_WIDTH), row),
            pl.BlockSpec((TM_IN, 2 * D_MODEL), row),
        ],
        out_shape=[
            jax.ShapeDtypeStruct((n, Q_WIDTH), BF16),
            jax.ShapeDtypeStruct((n, 4 * LANES), BF16),
            jax.ShapeDtypeStruct((n, 4 * LANES), BF16),
            jax.ShapeDtypeStruct((n, POOL_WIDTH), F32),
            jax.ShapeDtypeStruct((n, 2 * D_MODEL), BF16),
        ],
        compiler_params=pltpu.CompilerParams(
            dimension_semantics=("arbitrary",), vmem_limit_bytes=VMEM_LIMIT),
        name="inproj",
    )(x2, pos_b, invf, g, w, b)


def _attn_body(sinks_ref, q_ref, kbc_ref, kbp_ref, vbc_ref, vbp_ref, bias_ref,
               o_ref, p_scr):
    t = pl.program_id(0)
    seq_first = (t % (SEQ // TQ)) == 0
    lane = lax.broadcasted_iota(I32, (WINDOW, LANES), 1)
    low_head = lane < HEAD_DIM
    for n in range(TQ // WINDOW):
        rows = slice(n * WINDOW, (n + 1) * WINDOW)
        if n == 0:
            k_prev, v_prev = kbp_ref[...], vbp_ref[...]
            bias = jnp.where(seq_first, bias_ref[1], bias_ref[0])
        else:
            prev_rows = slice((n - 1) * WINDOW, n * WINDOW)
            k_prev, v_prev = kbc_ref[prev_rows, :], vbc_ref[prev_rows, :]
            bias = bias_ref[0]
        k_cur, v_cur = kbc_ref[rows, :], vbc_ref[rows, :]
        for g in range(2):
            def stack(prev, cur):
                lo = slice((2 * g) * LANES, (2 * g + 1) * LANES)
                hi = slice((2 * g + 1) * LANES, (2 * g + 2) * LANES)
                return jnp.concatenate([prev[:, lo], cur[:, lo], prev[:, hi], cur[:, hi]], axis=0)
            kmat = stack(k_prev, k_cur)
            vmat = stack(v_prev, v_cur)
            qs = jnp.concatenate(
                [q_ref[rows, (4 * g + p) * LANES:(4 * g + p + 1) * LANES] for p in range(4)],
                axis=0)
            s = lax.dot_general(qs, kmat, (((1,), (1,)), ((), ())),
                                preferred_element_type=F32) + bias
            inv = []
            for p in range(4):
                pr = slice(p * WINDOW, (p + 1) * WINDOW)
                inv_p = []
                for j in range(2):
                    cols = slice(j * 2 * WINDOW, (j + 1) * 2 * WINDOW)
                    sp = s[pr, cols]
                    sink = sinks_ref[8 * g + 2 * p + j]
                    m = jnp.maximum(jnp.max(sp, axis=-1, keepdims=True), sink)
                    e = jnp.exp(sp - m)
                    den = jnp.sum(e, axis=-1, keepdims=True) + jnp.exp(sink - m)
                    p_scr[pr, cols] = e.astype(BF16)
                    inv_p.append(1.0 / den)
                inv.append(inv_p)
            o = _dot(p_scr[...], vmat)
            for p in range(4):
                pr = slice(p * WINDOW, (p + 1) * WINDOW)
                norm = jnp.where(low_head, inv[p][0], inv[p][1])
                o_ref[rows, (4 * g + p) * LANES:(4 * g + p + 1) * LANES] = (o[pr, :] * norm).astype(BF16)


def _attn_bias():
    r = np.arange(4 * WINDOW)[:, None] % WINDOW
    c = np.arange(4 * WINDOW)[None, :] % (2 * WINDOW)
    band = (c > r) & (c <= r + WINDOW)
    first = band & (c >= WINDOW)
    return np.stack([np.where(band, 0.0, NEG_BIG), np.where(first, 0.0, NEG_BIG)]).astype(np.float32)


def _attn(sinks, q, kb, vb):
    n = q.shape[0]
    blocks_per_tile = TQ // WINDOW
    cur = lambda t: (t, 0)
    prev = lambda t: (jnp.maximum(t * blocks_per_tile - 1, 0), 0)
    bias = jnp.asarray(_attn_bias())
    return pl.pallas_call(
        _attn_body,
        grid=(n // TQ,),
        in_specs=[
            pl.BlockSpec(memory_space=pltpu.SMEM),
            pl.BlockSpec((TQ, Q_WIDTH), cur),
            pl.BlockSpec((TQ, 4 * LANES), cur),
            pl.BlockSpec((WINDOW, 4 * LANES), prev),
            pl.BlockSpec((TQ, 4 * LANES), cur),
            pl.BlockSpec((WINDOW, 4 * LANES), prev),
            pl.BlockSpec((2, 4 * WINDOW, 4 * WINDOW), lambda t: (0, 0, 0)),
        ],
        out_specs=pl.BlockSpec((TQ, Q_WIDTH), cur),
        out_shape=jax.ShapeDtypeStruct((n, Q_WIDTH), BF16),
        scratch_shapes=[pltpu.VMEM((4 * WINDOW, 4 * WINDOW), BF16)],
        compiler_params=pltpu.CompilerParams(
            dimension_semantics=("arbitrary",), vmem_limit_bytes=VMEM_LIMIT),
        name="attn",
    )(sinks, q, kb, kb, vb, vb, bias)


def _post_body(attn_ref, u_ref, uprev_ref, gate_ref, x_ref,
               wo_ref, bo_ref, wmix_ref, pscale_ref, wup_ref, wout_ref,
               gffn_ref, wr_ref, br_ref,
               x1_ref, h2_ref, ri_ref, rf_ref,
               ubuf, carry):
    i = pl.program_id(0)
    tiles_per_seq = SEQ // TM_POST
    seq_first = (i % tiles_per_seq) == 0

    @pl.when(i == 0)
    def _():
        carry[...] = jnp.zeros_like(carry)

    ubuf[0:POOL_HALO, :] = jnp.where(seq_first, 0.0, uprev_ref[...])
    ubuf[POOL_HALO:, :] = u_ref[...]
    row = lax.broadcasted_iota(I32, (TM_POST, 1), 0)
    tpos = (i % tiles_per_seq) * TM_POST + row
    mixed = []
    for gi, w in enumerate(POOL_WINDOWS):
        cols = slice(gi * POOL_GROUP, (gi + 1) * POOL_GROUP)
        acc = ubuf[POOL_HALO:, cols]
        for d in range(1, w):
            acc = acc + ubuf[POOL_HALO - d:POOL_HALO - d + TM_POST, cols]
        cnt = jnp.minimum(tpos + 1, w).astype(F32)
        pooled = acc / cnt - ubuf[POOL_HALO:, cols]
        mixed.append(_dot(pooled.astype(BF16), wmix_ref[gi]) * pscale_ref[:, cols])
    mixed = jnp.concatenate(mixed, axis=-1).astype(BF16)
    y_pool = _dot(mixed, wup_ref[...])
    y_attn = _dot(attn_ref[...], wo_ref[...]) + bo_ref[...]
    merged = (gate_ref[:, :D_MODEL].astype(F32) * y_attn
              + gate_ref[:, D_MODEL:].astype(F32) * y_pool)
    x1 = x_ref[...] + _dot(merged.astype(BF16), wout_ref[...])
    x1_ref[...] = x1
    h2 = _rms_norm(x1, gffn_ref[...])
    h2_bf = h2.astype(BF16)
    h2_ref[...] = h2_bf

    lane = lax.broadcasted_iota(I32, (TM_POST, LANES), 1)
    logits = _dot(h2_bf, wr_ref[...]) + br_ref[...]
    work = jnp.where(lane < N_EXPERTS, logits, -jnp.inf)
    top_v, top_e, onehots = [], [], []
    for _ in range(TOP_K):
        m = jnp.max(work, axis=-1, keepdims=True)
        idx = jnp.min(jnp.where(work == m, lane, LANES), axis=-1, keepdims=True)
        hit = lane == idx
        top_v.append(m)
        top_e.append(idx)
        onehots.append(hit)
        work = jnp.where(hit, -jnp.inf, work)
    exps = [jnp.exp(v - top_v[0]) for v in top_v]
    denom = exps[0] + exps[1] + exps[2] + exps[3]
    comb = [e / denom for e in exps]

    chosen = jnp.zeros((TM_POST, LANES), F32)
    for hit in onehots:
        chosen = chosen + hit.astype(F32)
    r_i = lax.broadcasted_iota(I32, (TM_POST, TM_POST), 0)
    c_i = lax.broadcasted_iota(I32, (TM_POST, TM_POST), 1)
    tril = (r_i > c_i).astype(BF16)
    before = _dot(tril, chosen.astype(BF16)) + carry[0:1, :]
    ri = jnp.zeros((TM_POST, LANES), I32)
    rf = jnp.zeros((TM_POST, LANES), F32)
    for k in range(TOP_K):
        rank = jnp.sum(jnp.where(onehots[k], before, 0.0), axis=-1, keepdims=True)
        ri = jnp.where(lane == k, top_e[k], ri)
        ri = jnp.where(lane == TOP_K + k, rank.astype(I32), ri)
        rf = jnp.where(lane == k, comb[k], rf)
    ri_ref[...] = ri
    rf_ref[...] = rf
    total = carry[0:1, :] + jnp.sum(chosen, axis=0, keepdims=True)
    carry[...] = jnp.broadcast_to(total, carry.shape)


def _post(attn, u, gates, x2, wo, bo, wmix, pscale, wup, wout, gffn, wr, br):
    n = x2.shape[0]
    row = lambda i: (i, 0)
    const = lambda i: (0, 0)
    halo_blocks = TM_POST // POOL_HALO
    prev = lambda i: (jnp.maximum(i * halo_blocks - 1, 0), 0)
    return pl.pallas_call(
        _post_body,
        grid=(n // TM_POST,),
        in_specs=[
            pl.BlockSpec((TM_POST, Q_WIDTH), row),
            pl.BlockSpec((TM_POST, POOL_WIDTH), row),
            pl.BlockSpec((POOL_HALO, POOL_WIDTH), prev),
            pl.BlockSpec((TM_POST, 2 * D_MODEL), row),
            pl.BlockSpec((TM_POST, D_MODEL), row),
            pl.BlockSpec((Q_WIDTH, D_MODEL), const),
            pl.BlockSpec((1, D_MODEL), const),
            pl.BlockSpec((len(POOL_WINDOWS), POOL_GROUP, POOL_GROUP), lambda i: (0, 0, 0)),
            pl.BlockSpec((1, POOL_WIDTH), const),
            pl.BlockSpec((POOL_WIDTH, D_MODEL), const),
            pl.BlockSpec((D_MODEL, D_MODEL), const),
            pl.BlockSpec((1, D_MODEL), const),
            pl.BlockSpec((D_MODEL, LANES), const),
            pl.BlockSpec((1, LANES), const),
        ],
        out_specs=[
            pl.BlockSpec((TM_POST, D_MODEL), row),
            pl.BlockSpec((TM_POST, D_MODEL), row),
            pl.BlockSpec((TM_POST, LANES), row),
            pl.BlockSpec((TM_POST, LANES), row),
        ],
        out_shape=[
            jax.ShapeDtypeStruct((n, D_MODEL), F32),
            jax.ShapeDtypeStruct((n, D_MODEL), BF16),
            jax.ShapeDtypeStruct((n, LANES), I32),
            jax.ShapeDtypeStruct((n, LANES), F32),
        ],
        scratch_shapes=[pltpu.VMEM((TM_POST + POOL_HALO, POOL_WIDTH), F32),
                        pltpu.VMEM((8, LANES), F32)],
        compiler_params=pltpu.CompilerParams(
            dimension_semantics=("arbitrary",), vmem_limit_bytes=VMEM_LIMIT),
        name="post",
    )(attn, u, u, gates, x2, wo, bo, wmix, pscale, wup, wout, gffn, wr, br)


def _segment_copies(i, glob_ref, loc_ref, cnt_ref, make_copy, action):
    def body(e, c):
        idx = i * N_EXPERTS + e
        rows = pl.multiple_of(cnt_ref[idx], SEG_ALIGN)

        @pl.when(rows > 0)
        def _():
            loc = pl.multiple_of(loc_ref[idx], SEG_ALIGN)
            glob = pl.multiple_of(glob_ref[idx], SEG_ALIGN)
            action(make_copy(pl.ds(loc, rows), pl.ds(glob, rows)))
        return c
    lax.fori_loop(0, N_EXPERTS, body, 0)


def _dispatch_body(glob_ref, loc_ref, cnt_ref, h2_ref, pos_ref, xs_hbm, local, sem):
    i = pl.program_id(0)
    j = lax.broadcasted_iota(I32, (LROWS, TT), 0)
    hit = j == pos_ref[0, 0:1, :]
    for k in range(1, TOP_K):
        hit = hit | (j == pos_ref[0, k:k + 1, :])
    perm = jnp.where(hit, 1.0, 0.0).astype(BF16)
    local[...] = _dot(perm, h2_ref[...])

    def make_copy(loc_rows, glob_rows):
        return pltpu.make_async_copy(local.at[loc_rows], xs_hbm.at[glob_rows], sem)
    _segment_copies(i, glob_ref, loc_ref, cnt_ref, make_copy, lambda cp: cp.start())
    _segment_copies(i, glob_ref, loc_ref, cnt_ref, make_copy, lambda cp: cp.wait())


def _dispatch(seg_glob, seg_loc, seg_cnt, h2, pos_t, n_rows):
    n = h2.shape[0]
    grid_spec = pltpu.PrefetchScalarGridSpec(
        num_scalar_prefetch=3,
        grid=(n // TT,),
        in_specs=[
            pl.BlockSpec((TT, D_MODEL), lambda i, *_: (i, 0)),
            pl.BlockSpec((1, 8, TT), lambda i, *_: (i, 0, 0)),
        ],
        out_specs=pl.BlockSpec(memory_space=pl.ANY),
        scratch_shapes=[pltpu.VMEM((LROWS, D_MODEL), F32),
                        pltpu.SemaphoreType.DMA(())],
    )
    return pl.pallas_call(
        _dispatch_body,
        grid_spec=grid_spec,
        out_shape=jax.ShapeDtypeStruct((n_rows, D_MODEL), F32),
        compiler_params=pltpu.CompilerParams(
            dimension_semantics=("arbitrary",), vmem_limit_bytes=VMEM_LIMIT),
        name="dispatch",
    )(seg_glob, seg_loc, seg_cnt, h2, pos_t)


def _moe_body(te_ref, nused_ref, rows_ref, xs_ref, wg_ref, bg_ref, wu_ref, bu_ref,
              wd_ref, bd_ref, y_ref):
    i = pl.program_id(0)

    @pl.when(i < nused_ref[0])
    def _():
        row = lax.broadcasted_iota(I32, (TM_MOE, 1), 0)
        x = jnp.where(row < rows_ref[i], xs_ref[...], 0.0).astype(BF16)
        g = _dot(x, wg_ref[0]) + bg_ref[0]
        u = _dot(x, wu_ref[0]) + bu_ref[0]
        g = jnp.minimum(g, SWIGLU_LIMIT)
        u = jnp.clip(u, -SWIGLU_LIMIT, SWIGLU_LIMIT)
        a = g * jax.nn.sigmoid(SWIGLU_ALPHA * g) * (u + 1.0)
        y_ref[...] = _dot(a.astype(BF16), wd_ref[0]) + bd_ref[0]


def _moe(tile_expert, n_used, tile_rows, xs, wg, bg, wu, bu, wd, bd):
    n_rows = xs.shape[0]
    n_tiles = n_rows // TM_MOE
    d_ff = wg.shape[2]
    row = lambda i, te, nu, tr: (jnp.minimum(i, nu[0] - 1), 0)
    wsel = lambda i, te, nu, tr: (te[i], 0, 0)
    grid_spec = pltpu.PrefetchScalarGridSpec(
        num_scalar_prefetch=3,
        grid=(n_tiles,),
        in_specs=[
            pl.BlockSpec((TM_MOE, D_MODEL), row),
            pl.BlockSpec((1, D_MODEL, d_ff), wsel),
            pl.BlockSpec((1, 1, d_ff), wsel),
            pl.BlockSpec((1, D_MODEL, d_ff), wsel),
            pl.BlockSpec((1, 1, d_ff), wsel),
            pl.BlockSpec((1, d_ff, D_MODEL), wsel),
            pl.BlockSpec((1, 1, D_MODEL), wsel),
        ],
        out_specs=pl.BlockSpec((TM_MOE, D_MODEL), row),
    )
    return pl.pallas_call(
        _moe_body,
        grid_spec=grid_spec,
        out_shape=jax.ShapeDtypeStruct((n_rows, D_MODEL), F32),
        compiler_params=pltpu.CompilerParams(
            dimension_semantics=("arbitrary",), vmem_limit_bytes=VMEM_LIMIT),
        name="moe",
    )(tile_expert, n_used, tile_rows, xs, wg, bg, wu, bu, wd, bd)


def _combine_body(glob_ref, loc_ref, cnt_ref, x1_ref, pos_ref, rf_ref, gfin_ref, ys_hbm,
                  o_ref, local, sem):
    i = pl.program_id(0)

    @pl.when(i == 0)
    def _():
        local[...] = jnp.zeros_like(local)

    def make_copy(loc_rows, glob_rows):
        return pltpu.make_async_copy(ys_hbm.at[glob_rows], local.at[loc_rows], sem)
    _segment_copies(i, glob_ref, loc_ref, cnt_ref, make_copy, lambda cp: cp.start())
    _segment_copies(i, glob_ref, loc_ref, cnt_ref, make_copy, lambda cp: cp.wait())

    j = lax.broadcasted_iota(I32, (TT, LROWS), 1)
    wc = jnp.zeros((TT, LROWS), F32)
    for k in range(TOP_K):
        wc = jnp.where(j == pos_ref[:, k:k + 1], rf_ref[:, k:k + 1], wc)
    acc = x1_ref[...] + _dot(wc.astype(BF16), local[...].astype(BF16))
    o_ref[...] = _rms_norm(acc, gfin_ref[...])


def _combine(seg_glob, seg_loc, seg_cnt, x1, pos, rf, gfin, ys):
    n = x1.shape[0]
    grid_spec = pltpu.PrefetchScalarGridSpec(
        num_scalar_prefetch=3,
        grid=(n // TT,),
        in_specs=[
            pl.BlockSpec((TT, D_MODEL), lambda i, *_: (i, 0)),
            pl.BlockSpec((TT, LANES), lambda i, *_: (i, 0)),
            pl.BlockSpec((TT, LANES), lambda i, *_: (i, 0)),
            pl.BlockSpec((1, D_MODEL), lambda i, *_: (0, 0)),
            pl.BlockSpec(memory_space=pl.ANY),
        ],
        out_specs=pl.BlockSpec((TT, D_MODEL), lambda i, *_: (i, 0)),
        scratch_shapes=[pltpu.VMEM((LROWS, D_MODEL), F32),
                        pltpu.SemaphoreType.DMA(())],
    )
    return pl.pallas_call(
        _combine_body,
        grid_spec=grid_spec,
        out_shape=jax.ShapeDtypeStruct((n, D_MODEL), F32),
        compiler_params=pltpu.CompilerParams(
            dimension_semantics=("arbitrary",), vmem_limit_bytes=VMEM_LIMIT),
        name="combine",
    )(seg_glob, seg_loc, seg_cnt, x1, pos, rf, gfin, ys)


def _layer(x2, pos_b, invf, norm_mix_g, w_in, b_in, attn_sinks, w_o_attn, b_o_attn,
           w_pool_mix, pool_scale, w_pool_up, w_out, norm_ffn_g, w_router, b_router,
           w_gate, b_gate, w_up, b_up, w_down, b_down, out_g):
    n = x2.shape[0]
    q, kb, vb, u, gates = _inproj(
        x2, pos_b, invf, norm_mix_g[None, :], w_in.astype(BF16), b_in[None, :])
    attn = _attn(attn_sinks, q, kb, vb)

    wr = jnp.zeros((D_MODEL, LANES), BF16).at[:, :N_EXPERTS].set(w_router.astype(BF16))
    br = jnp.zeros((1, LANES), F32).at[0, :N_EXPERTS].set(b_router)
    x1, h2, ri, rf = _post(
        attn, u, gates, x2, w_o_attn.astype(BF16), b_o_attn[None, :],
        w_pool_mix.astype(BF16), pool_scale[None, :], w_pool_up.astype(BF16),
        w_out.astype(BF16), norm_ffn_g[None, :], wr, br)

    top_e = ri[:, :TOP_K]
    rank = ri[:, TOP_K:2 * TOP_K]
    nt = n // TT
    sel = top_e[..., None] == jnp.arange(N_EXPERTS, dtype=I32)
    seg_cnt = jnp.sum(sel.reshape(nt, TT * TOP_K, N_EXPERTS), axis=1, dtype=I32)
    before = jnp.cumsum(seg_cnt, axis=0) - seg_cnt
    seg_pad = (seg_cnt + SEG_ALIGN - 1) // SEG_ALIGN * SEG_ALIGN
    sizes = jnp.sum(seg_pad, axis=0)
    padded = (sizes + TM_MOE - 1) // TM_MOE * TM_MOE
    pends = jnp.cumsum(padded)
    pstarts = pends - padded
    seg_loc = jnp.cumsum(seg_pad, axis=1) - seg_pad
    seg_glob = pstarts[None, :] + jnp.cumsum(seg_pad, axis=0) - seg_pad
    seg_cnt = seg_pad
    shift = jnp.repeat(seg_loc - before, TT, axis=0)
    pos = rank + jnp.sum(jnp.where(sel, shift[:, None, :], 0), axis=-1)
    pos_t = jnp.pad(pos.reshape(nt, TT, TOP_K).transpose(0, 2, 1),
                    ((0, 0), (0, 8 - TOP_K), (0, 0)), constant_values=-1)
    pos_l = jnp.pad(pos, ((0, 0), (0, LANES - TOP_K)), constant_values=-1)
    seg_glob, seg_loc, seg_cnt = (a.reshape(-1) for a in (seg_glob, seg_loc, seg_cnt))
    m = n * TOP_K
    max_rows = m + nt * N_EXPERTS * (SEG_ALIGN - 1) + N_EXPERTS * (TM_MOE - 1)
    n_tiles = (max_rows + TM_MOE - 1) // TM_MOE
    tile_start = jnp.arange(n_tiles, dtype=I32) * TM_MOE
    tile_expert = jnp.minimum(
        jnp.sum(tile_start[:, None] >= pends[None, :], axis=-1), N_EXPERTS - 1).astype(I32)
    tile_rows = jnp.clip(sizes[tile_expert] - (tile_start - pstarts[tile_expert]), 0, TM_MOE)
    n_used = (pends[-1] // TM_MOE).astype(I32)[None]

    xs = _dispatch(seg_glob, seg_loc, seg_cnt, h2, pos_t, n_tiles * TM_MOE)
    ys = _moe(tile_expert, n_used, tile_rows.astype(I32), xs,
              w_gate.astype(BF16), b_gate[:, None, :], w_up.astype(BF16), b_up[:, None, :],
              w_down.astype(BF16), b_down[:, None, :])
    return _combine(seg_glob, seg_loc, seg_cnt, x1, pos_l, rf, out_g[None, :], ys)


def kernel(x, positions, norm_mix_g, w_in, b_in, attn_sinks, w_o_attn, b_o_attn, w_pool_mix,
           pool_scale, w_pool_up, w_out, norm_ffn_g, w_router, b_router, w_gate, b_gate,
           w_up, b_up, w_down, b_down, norm_final_g):
    b, s, d = x.shape
    depth = w_in.shape[0]
    assert (s, d, depth) == (SEQ, D_MODEL, 1)
    n = b * s
    x2 = x.reshape(n, d)
    pos_b = jnp.broadcast_to(positions.reshape(n, 1), (n, LANES))
    inv_freq = ROPE_THETA ** (-jnp.arange(0, HEAD_DIM, 2, dtype=F32) / HEAD_DIM)
    invf = jnp.tile(inv_freq, LANES // (HEAD_DIM // 2))[None, :]
    out = _layer(x2, pos_b, invf, norm_mix_g[0], w_in[0], b_in[0], attn_sinks[0], w_o_attn[0],
                 b_o_attn[0], w_pool_mix[0], pool_scale[0], w_pool_up[0], w_out[0],
                 norm_ffn_g[0], w_router[0], b_router[0], w_gate[0], b_gate[0], w_up[0],
                 b_up[0], w_down[0], b_down[0], norm_final_g)
    return out.reshape(b, s, d)
```

```python
import functools

import numpy as np
import jax
import jax.numpy as jnp
from jax import lax
from jax.experimental import pallas as pl
from jax.experimental.pallas import tpu as pltpu

F32 = jnp.float32
BF16 = jnp.bfloat16
I32 = jnp.int32
U32 = jnp.uint32

D_MODEL = 1024
SEQ = 4096
HEAD_DIM = 64
N_Q_HEADS = 16
WINDOW = 128
ROPE_THETA = 10000.0
Q_WIDTH = N_Q_HEADS * HEAD_DIM
KV_WIDTH = 128
POOL_WINDOWS = (2, 4, 8, 16)
POOL_WIDTH = 512
POOL_GROUP = 128
POOL_HALO = 16
N_EXPERTS = 32
TOP_K = 4
SWIGLU_LIMIT = 7.0
SWIGLU_ALPHA = 1.702
RMS_EPS = 1e-5
NEG_BIG = -1e30

LANES = 128
TM_IN = 512
TQ = 512
TM_POST = 512
TM_MOE = 512
TT = 256
SEG_ALIGN = 8
LROWS = TT * TOP_K + N_EXPERTS * SEG_ALIGN
PACKED = D_MODEL // 2
VMEM_LIMIT = 56 * 1024 * 1024


def _rms_norm(x, g):
    ms = jnp.mean(x * x, axis=-1, keepdims=True)
    return (x * lax.rsqrt(ms + RMS_EPS)) * g


def _dot(a, b):
    return jnp.dot(a, b, preferred_element_type=F32)


def _inproj_body(x_ref, pos_ref, invf_ref, g_ref, w_ref, b_ref,
                 q_ref, kb_ref, vb_ref, u_ref, gate_ref):
    h = _rms_norm(x_ref[...], g_ref[...]).astype(BF16)

    theta = pos_ref[...].astype(F32) * invf_ref[...]
    cos = jnp.cos(theta)
    sin = jnp.sin(theta)
    lane = lax.broadcasted_iota(I32, theta.shape, 1)
    first_half = (lane & 32) == 0
    sin_signed = jnp.where(first_half, -sin, sin)
    low_head = lane < HEAD_DIM

    def rope(t):
        swapped = jnp.where(first_half, pltpu.roll(t, 96, 1), pltpu.roll(t, 32, 1))
        return t * cos + swapped * sin_signed

    def band_layout(t, out_ref):
        tr = pltpu.roll(t, 64, 1)
        zero = jnp.zeros_like(t)
        chunks = (jnp.where(low_head, t, zero), jnp.where(low_head, zero, tr),
                  jnp.where(low_head, tr, zero), jnp.where(low_head, zero, t))
        for c, val in enumerate(chunks):
            out_ref[:, c * LANES:(c + 1) * LANES] = val.astype(BF16)

    qk_w = Q_WIDTH + KV_WIDTH
    zqk = _dot(h, w_ref[:, :qk_w]) + b_ref[:, :qk_w]
    scale = HEAD_DIM ** -0.5
    for j in range(Q_WIDTH // LANES):
        sl = slice(j * LANES, (j + 1) * LANES)
        q_ref[:, sl] = (rope(zqk[:, sl]) * scale).astype(BF16)
    band_layout(rope(zqk[:, Q_WIDTH:qk_w]), kb_ref)

    v0 = qk_w
    zv = _dot(h, w_ref[:, v0:v0 + KV_WIDTH]) + b_ref[:, v0:v0 + KV_WIDTH]
    band_layout(zv, vb_ref)

    u0 = v0 + KV_WIDTH
    u_ref[...] = _dot(h, w_ref[:, u0:u0 + POOL_WIDTH]) + b_ref[:, u0:u0 + POOL_WIDTH]

    g0 = u0 + POOL_WIDTH
    for c in range(2):
        sl = slice(g0 + c * D_MODEL, g0 + (c + 1) * D_MODEL)
        zg = _dot(h, w_ref[:, sl]) + b_ref[:, sl]
        gate_ref[:, c * D_MODEL:(c + 1) * D_MODEL] = jax.nn.sigmoid(zg).astype(BF16)


def _inproj(x2, pos_b, invf, g, w, b):
    n = x2.shape[0]
    in_width = w.shape[1]
    row = lambda i: (i, 0)
    const = lambda i: (0, 0)
    return pl.pallas_call(
        _inproj_body,
        grid=(n // TM_IN,),
        in_specs=[
            pl.BlockSpec((TM_IN, D_MODEL), row),
            pl.BlockSpec((TM_IN, LANES), row),
            pl.BlockSpec((1, LANES), const),
            pl.BlockSpec((1, D_MODEL), const),
            pl.BlockSpec((D_MODEL, in_width), const),
            pl.BlockSpec((1, in_width), const),
        ],
        out_specs=[
            pl.BlockSpec((TM_IN, Q_WIDTH), row),
            pl.BlockSpec((TM_IN, 4 * LANES), row),
            pl.BlockSpec((TM_IN, 4 * LANES), row),
            pl.BlockSpec((TM_IN, POOL_WIDTH), row),
            pl.BlockSpec((TM_IN, 2 * D_MODEL), row),
        ],
        out_shape=[
            jax.ShapeDtypeStruct((n, Q_WIDTH), BF16),
            jax.ShapeDtypeStruct((n, 4 * LANES), BF16),
            jax.ShapeDtypeStruct((n, 4 * LANES), BF16),
            jax.ShapeDtypeStruct((n, POOL_WIDTH), F32),
            jax.ShapeDtypeStruct((n, 2 * D_MODEL), BF16),
        ],
        compiler_params=pltpu.CompilerParams(
            dimension_semantics=("arbitrary",), vmem_limit_bytes=VMEM_LIMIT),
        name="inproj",
    )(x2, pos_b, invf, g, w, b)


def _attn_body(sinks_ref, q_ref, kbc_ref, kbp_ref, vbc_ref, vbp_ref, bias_ref,
               o_ref, p_scr):
    t = pl.program_id(0)
    seq_first = (t % (SEQ // TQ)) == 0
    lane = lax.broadcasted_iota(I32, (WINDOW, LANES), 1)
    low_head = lane < HEAD_DIM
    for n in range(TQ // WINDOW):
        rows = slice(n * WINDOW, (n + 1) * WINDOW)
        if n == 0:
            k_prev, v_prev = kbp_ref[...], vbp_ref[...]
            bias = jnp.where(seq_first, bias_ref[1], bias_ref[0])
        else:
            prev_rows = slice((n - 1) * WINDOW, n * WINDOW)
            k_prev, v_prev = kbc_ref[prev_rows, :], vbc_ref[prev_rows, :]
            bias = bias_ref[0]
        k_cur, v_cur = kbc_ref[rows, :], vbc_ref[rows, :]
        for g in range(2):
            def stack(prev, cur):
                lo = slice((2 * g) * LANES, (2 * g + 1) * LANES)
                hi = slice((2 * g + 1) * LANES, (2 * g + 2) * LANES)
                return jnp.concatenate([prev[:, lo], cur[:, lo], prev[:, hi], cur[:, hi]], axis=0)
            kmat = stack(k_prev, k_cur)
            vmat = stack(v_prev, v_cur)
            qs = jnp.concatenate(
                [q_ref[rows, (4 * g + p) * LANES:(4 * g + p + 1) * LANES] for p in range(4)],
                axis=0)
            s = lax.dot_general(qs, kmat, (((1,), (1,)), ((), ())),
                                preferred_element_type=F32) + bias
            inv = []
            for p in range(4):
                pr = slice(p * WINDOW, (p + 1) * WINDOW)
                inv_p = []
                for j in range(2):
                    cols = slice(j * 2 * WINDOW, (j + 1) * 2 * WINDOW)
                    sp = s[pr, cols]
                    sink = sinks_ref[8 * g + 2 * p + j]
                    m = jnp.maximum(jnp.max(sp, axis=-1, keepdims=True), sink)
                    e = jnp.exp(sp - m)
                    den = jnp.sum(e, axis=-1, keepdims=True) + jnp.exp(sink - m)
                    p_scr[pr, cols] = e.astype(BF16)
                    inv_p.append(1.0 / den)
                inv.append(inv_p)
            o = _dot(p_scr[...], vmat)
            for p in range(4):
                pr = slice(p * WINDOW, (p + 1) * WINDOW)
                norm = jnp.where(low_head, inv[p][0], inv[p][1])
                o_ref[rows, (4 * g + p) * LANES:(4 * g + p + 1) * LANES] = (o[pr, :] * norm).astype(BF16)


def _attn_bias():
    r = np.arange(4 * WINDOW)[:, None] % WINDOW
    c = np.arange(4 * WINDOW)[None, :] % (2 * WINDOW)
    band = (c > r) & (c <= r + WINDOW)
    first = band & (c >= WINDOW)
    return np.stack([np.where(band, 0.0, NEG_BIG), np.where(first, 0.0, NEG_BIG)]).astype(np.float32)


def _attn(sinks, q, kb, vb):
    n = q.shape[0]
    blocks_per_tile = TQ // WINDOW
    cur = lambda t: (t, 0)
    prev = lambda t: (jnp.maximum(t * blocks_per_tile - 1, 0), 0)
    bias = jnp.asarray(_attn_bias())
    return pl.pallas_call(
        _attn_body,
        grid=(n // TQ,),
        in_specs=[
            pl.BlockSpec(memory_space=pltpu.SMEM),
            pl.BlockSpec((TQ, Q_WIDTH), cur),
            pl.BlockSpec((TQ, 4 * LANES), cur),
            pl.BlockSpec((WINDOW, 4 * LANES), prev),
            pl.BlockSpec((TQ, 4 * LANES), cur),
            pl.BlockSpec((WINDOW, 4 * LANES), prev),
            pl.BlockSpec((2, 4 * WINDOW, 4 * WINDOW), lambda t: (0, 0, 0)),
        ],
        out_specs=pl.BlockSpec((TQ, Q_WIDTH), cur),
        out_shape=jax.ShapeDtypeStruct((n, Q_WIDTH), BF16),
        scratch_shapes=[pltpu.VMEM((4 * WINDOW, 4 * WINDOW), BF16)],
        compiler_params=pltpu.CompilerParams(
            dimension_semantics=("arbitrary",), vmem_limit_bytes=VMEM_LIMIT),
        name="attn",
    )(sinks, q, kb, kb, vb, vb, bias)


def _post_body(attn_ref, u_ref, uprev_ref, gate_ref, x_ref,
               wo_ref, bo_ref, wmix_ref, pscale_ref, wup_ref, wout_ref,
               gffn_ref, wr_ref, br_ref,
               x1_ref, h2_ref, ri_ref, rf_ref,
               ubuf, carry):
    i = pl.program_id(0)
    tiles_per_seq = SEQ // TM_POST
    seq_first = (i % tiles_per_seq) == 0

    @pl.when(i == 0)
    def _():
        carry[...] = jnp.zeros_like(carry)

    ubuf[0:POOL_HALO, :] = jnp.where(seq_first, 0.0, uprev_ref[...])
    ubuf[POOL_HALO:, :] = u_ref[...]
    row = lax.broadcasted_iota(I32, (TM_POST, 1), 0)
    tpos = (i % tiles_per_seq) * TM_POST + row
    mixed = []
    for gi, w in enumerate(POOL_WINDOWS):
        cols = slice(gi * POOL_GROUP, (gi + 1) * POOL_GROUP)
        acc = ubuf[POOL_HALO:, cols]
        for d in range(1, w):
            acc = acc + ubuf[POOL_HALO - d:POOL_HALO - d + TM_POST, cols]
        cnt = jnp.minimum(tpos + 1, w).astype(F32)
        pooled = acc / cnt - ubuf[POOL_HALO:, cols]
        mixed.append(_dot(pooled.astype(BF16), wmix_ref[gi]) * pscale_ref[:, cols])
    mixed = jnp.concatenate(mixed, axis=-1).astype(BF16)
    y_pool = _dot(mixed, wup_ref[...])
    y_attn = _dot(attn_ref[...], wo_ref[...]) + bo_ref[...]
    merged = (gate_ref[:, :D_MODEL].astype(F32) * y_attn
              + gate_ref[:, D_MODEL:].astype(F32) * y_pool)
    x1 = x_ref[...] + _dot(merged.astype(BF16), wout_ref[...])
    x1_ref[...] = x1
    h2 = _rms_norm(x1, gffn_ref[...])
    h2_bf = h2.astype(BF16)
    h2_ref[...] = h2_bf

    lane = lax.broadcasted_iota(I32, (TM_POST, LANES), 1)
    logits = _dot(h2_bf, wr_ref[...]) + br_ref[...]
    work = jnp.where(lane < N_EXPERTS, logits, -jnp.inf)
    top_v, top_e, onehots = [], [], []
    for _ in range(TOP_K):
        m = jnp.max(work, axis=-1, keepdims=True)
        idx = jnp.min(jnp.where(work == m, lane, LANES), axis=-1, keepdims=True)
        hit = lane == idx
        top_v.append(m)
        top_e.append(idx)
        onehots.append(hit)
        work = jnp.where(hit, -jnp.inf, work)
    exps = [jnp.exp(v - top_v[0]) for v in top_v]
    denom = exps[0] + exps[1] + exps[2] + exps[3]
    comb = [e / denom for e in exps]

    chosen = jnp.zeros((TM_POST, LANES), F32)
    for hit in onehots:
        chosen = chosen + hit.astype(F32)
    r_i = lax.broadcasted_iota(I32, (TM_POST, TM_POST), 0)
    c_i = lax.broadcasted_iota(I32, (TM_POST, TM_POST), 1)
    tril = (r_i > c_i).astype(BF16)
    before = _dot(tril, chosen.astype(BF16)) + carry[0:1, :]
    ri = jnp.zeros((TM_POST, LANES), I32)
    rf = jnp.zeros((TM_POST, LANES), F32)
    for k in range(TOP_K):
        rank = jnp.sum(jnp.where(onehots[k], before, 0.0), axis=-1, keepdims=True)
        ri = jnp.where(lane == k, top_e[k], ri)
        ri = jnp.where(lane == TOP_K + k, rank.astype(I32), ri)
        rf = jnp.where(lane == k, comb[k], rf)
    ri_ref[...] = ri
    rf_ref[...] = rf
    total = carry[0:1, :] + jnp.sum(chosen, axis=0, keepdims=True)
    carry[...] = jnp.broadcast_to(total, carry.shape)


def _post(attn, u, gates, x2, wo, bo, wmix, pscale, wup, wout, gffn, wr, br):
    n = x2.shape[0]
    row = lambda i: (i, 0)
    const = lambda i: (0, 0)
    halo_blocks = TM_POST // POOL_HALO
    prev = lambda i: (jnp.maximum(i * halo_blocks - 1, 0), 0)
    return pl.pallas_call(
        _post_body,
        grid=(n // TM_POST,),
        in_specs=[
            pl.BlockSpec((TM_POST, Q_WIDTH), row),
            pl.BlockSpec((TM_POST, POOL_WIDTH), row),
            pl.BlockSpec((POOL_HALO, POOL_WIDTH), prev),
            pl.BlockSpec((TM_POST, 2 * D_MODEL), row),
            pl.BlockSpec((TM_POST, D_MODEL), row),
            pl.BlockSpec((Q_WIDTH, D_MODEL), const),
            pl.BlockSpec((1, D_MODEL), const),
            pl.BlockSpec((len(POOL_WINDOWS), POOL_GROUP, POOL_GROUP), lambda i: (0, 0, 0)),
            pl.BlockSpec((1, POOL_WIDTH), const),
            pl.BlockSpec((POOL_WIDTH, D_MODEL), const),
            pl.BlockSpec((D_MODEL, D_MODEL), const),
            pl.BlockSpec((1, D_MODEL), const),
            pl.BlockSpec((D_MODEL, LANES), const),
            pl.BlockSpec((1, LANES), const),
        ],
        out_specs=[
            pl.BlockSpec((TM_POST, D_MODEL), row),
            pl.BlockSpec((TM_POST, D_MODEL), row),
            pl.BlockSpec((TM_POST, LANES), row),
            pl.BlockSpec((TM_POST, LANES), row),
        ],
        out_shape=[
            jax.ShapeDtypeStruct((n, D_MODEL), F32),
            jax.ShapeDtypeStruct((n, D_MODEL), BF16),
            jax.ShapeDtypeStruct((n, LANES), I32),
            jax.ShapeDtypeStruct((n, LANES), F32),
        ],
        scratch_shapes=[pltpu.VMEM((TM_POST + POOL_HALO, POOL_WIDTH), F32),
                        pltpu.VMEM((8, LANES), F32)],
        compiler_params=pltpu.CompilerParams(
            dimension_semantics=("arbitrary",), vmem_limit_bytes=VMEM_LIMIT),
        name="post",
    )(attn, u, u, gates, x2, wo, bo, wmix, pscale, wup, wout, gffn, wr, br)


def _segment_copies(i, glob_ref, loc_ref, cnt_ref, make_copy, action):
    def body(e, c):
        idx = i * N_EXPERTS + e
        rows = pl.multiple_of(cnt_ref[idx], SEG_ALIGN)

        @pl.when(rows > 0)
        def _():
            loc = pl.multiple_of(loc_ref[idx], SEG_ALIGN)
            glob = pl.multiple_of(glob_ref[idx], SEG_ALIGN)
            action(make_copy(pl.ds(loc, rows), pl.ds(glob, rows)))
        return c
    lax.fori_loop(0, N_EXPERTS, body, 0)


def _pack_halves(v):
    c = v.shape[1] // 2
    bits = lax.bitcast_convert_type(v, U32)
    return (bits[:, :c] >> 16) | bits[:, c:]


def _unpack_halves(w):
    lo = lax.bitcast_convert_type(w << 16, F32)
    hi = lax.bitcast_convert_type(w & jnp.uint32(0xFFFF0000), F32)
    return jnp.concatenate([lo, hi], axis=1).astype(BF16)


def _start(cp):
    cp.start()


def _wait(cp):
    cp.wait()


def _dispatch_body(glob_ref, loc_ref, cnt_ref, h2_ref, pos_ref, xs_hbm, local, sem):
    i = pl.program_id(0)
    nt = pl.num_programs(0)
    slot = i % 2

    def copies(tile, s, action):
        def make_copy(loc_rows, glob_rows):
            return pltpu.make_async_copy(local.at[s, loc_rows], xs_hbm.at[glob_rows], sem.at[s])
        _segment_copies(tile, glob_ref, loc_ref, cnt_ref, make_copy, action)

    @pl.when(i >= 2)
    def _():
        copies(i - 2, slot, _wait)

    j = lax.broadcasted_iota(I32, (LROWS, TT), 0)
    hit = j == pos_ref[0, 0:1, :]
    for k in range(1, TOP_K):
        hit = hit | (j == pos_ref[0, k:k + 1, :])
    perm = jnp.where(hit, 1.0, 0.0).astype(BF16)
    local[slot] = _pack_halves(_dot(perm, h2_ref[...]))
    copies(i, slot, _start)

    @pl.when(i == nt - 1)
    def _():
        copies(i - 1, 1 - slot, _wait)
        copies(i, slot, _wait)


def _dispatch(seg_glob, seg_loc, seg_cnt, h2, pos_t, n_rows):
    n = h2.shape[0]
    grid_spec = pltpu.PrefetchScalarGridSpec(
        num_scalar_prefetch=3,
        grid=(n // TT,),
        in_specs=[
            pl.BlockSpec((TT, D_MODEL), lambda i, *_: (i, 0)),
            pl.BlockSpec((1, 8, TT), lambda i, *_: (i, 0, 0)),
        ],
        out_specs=pl.BlockSpec(memory_space=pl.ANY),
        scratch_shapes=[pltpu.VMEM((2, LROWS, PACKED), U32),
                        pltpu.SemaphoreType.DMA((2,))],
    )
    return pl.pallas_call(
        _dispatch_body,
        grid_spec=grid_spec,
        out_shape=jax.ShapeDtypeStruct((n_rows, PACKED), U32),
        compiler_params=pltpu.CompilerParams(
            dimension_semantics=("arbitrary",), vmem_limit_bytes=VMEM_LIMIT),
        name="dispatch",
    )(seg_glob, seg_loc, seg_cnt, h2, pos_t)


def _moe_body(te_ref, nused_ref, rows_ref, xs_ref, wg_ref, bg_ref, wu_ref, bu_ref,
              wd_ref, bd_ref, y_ref, wbf):
    i = pl.program_id(0)

    @pl.when(i < nused_ref[0])
    def _():
        @pl.when((i == 0) | (te_ref[i] != te_ref[jnp.maximum(i - 1, 0)]))
        def _():
            wbf[0] = wg_ref[0].astype(BF16)
            wbf[1] = wu_ref[0].astype(BF16)
            wbf[2] = wd_ref[0].astype(BF16)

        row = lax.broadcasted_iota(I32, (TM_MOE, 1), 0)
        x = _unpack_halves(jnp.where(row < rows_ref[i], xs_ref[...], jnp.uint32(0)))
        g = _dot(x, wbf[0]) + bg_ref[0]
        u = _dot(x, wbf[1]) + bu_ref[0]
        g = jnp.minimum(g, SWIGLU_LIMIT)
        u = jnp.clip(u, -SWIGLU_LIMIT, SWIGLU_LIMIT)
        a = g * jax.nn.sigmoid(SWIGLU_ALPHA * g) * (u + 1.0)
        y = _dot(a.astype(BF16), wbf[2]) + bd_ref[0]
        y_ref[...] = _pack_halves(y.astype(BF16).astype(F32))


def _moe(tile_expert, n_used, tile_rows, xs, wg, bg, wu, bu, wd, bd):
    n_rows = xs.shape[0]
    n_tiles = n_rows // TM_MOE
    d_ff = wg.shape[2]
    row = lambda i, te, nu, tr: (jnp.minimum(i, nu[0] - 1), 0)
    wsel = lambda i, te, nu, tr: (te[i], 0, 0)
    grid_spec = pltpu.PrefetchScalarGridSpec(
        num_scalar_prefetch=3,
        grid=(n_tiles,),
        in_specs=[
            pl.BlockSpec((TM_MOE, PACKED), row),
            pl.BlockSpec((1, D_MODEL, d_ff), wsel),
            pl.BlockSpec((1, 1, d_ff), wsel),
            pl.BlockSpec((1, D_MODEL, d_ff), wsel),
            pl.BlockSpec((1, 1, d_ff), wsel),
            pl.BlockSpec((1, d_ff, D_MODEL), wsel),
            pl.BlockSpec((1, 1, D_MODEL), wsel),
        ],
        out_specs=pl.BlockSpec((TM_MOE, PACKED), row),
        scratch_shapes=[pltpu.VMEM((3, D_MODEL, d_ff), BF16)],
    )
    assert d_ff == D_MODEL
    return pl.pallas_call(
        _moe_body,
        grid_spec=grid_spec,
        out_shape=jax.ShapeDtypeStruct((n_rows, PACKED), U32),
        compiler_params=pltpu.CompilerParams(
            dimension_semantics=("arbitrary",), vmem_limit_bytes=VMEM_LIMIT),
        name="moe",
    )(tile_expert, n_used, tile_rows, xs, wg, bg, wu, bu, wd, bd)


def _combine_body(glob_ref, loc_ref, cnt_ref, x1_ref, pos_ref, rf_ref, gfin_ref, ys_hbm,
                  o_ref, local, sem):
    i = pl.program_id(0)
    nt = pl.num_programs(0)
    slot = i % 2

    def copies(tile, s, action):
        def make_copy(loc_rows, glob_rows):
            return pltpu.make_async_copy(ys_hbm.at[glob_rows], local.at[s, loc_rows], sem.at[s])
        _segment_copies(tile, glob_ref, loc_ref, cnt_ref, make_copy, action)

    @pl.when(i == 0)
    def _():
        local[...] = jnp.zeros_like(local)
        copies(0, 0, _start)

    @pl.when(i + 1 < nt)
    def _():
        copies(i + 1, 1 - slot, _start)
    copies(i, slot, _wait)

    j = lax.broadcasted_iota(I32, (TT, LROWS), 1)
    wc = jnp.zeros((TT, LROWS), F32)
    for k in range(TOP_K):
        wc = jnp.where(j == pos_ref[:, k:k + 1], rf_ref[:, k:k + 1], wc)
    acc = x1_ref[...] + _dot(wc.astype(BF16), _unpack_halves(local[slot]))
    o_ref[...] = _rms_norm(acc, gfin_ref[...])


def _combine(seg_glob, seg_loc, seg_cnt, x1, pos, rf, gfin, ys):
    n = x1.shape[0]
    grid_spec = pltpu.PrefetchScalarGridSpec(
        num_scalar_prefetch=3,
        grid=(n // TT,),
        in_specs=[
            pl.BlockSpec((TT, D_MODEL), lambda i, *_: (i, 0)),
            pl.BlockSpec((TT, LANES), lambda i, *_: (i, 0)),
            pl.BlockSpec((TT, LANES), lambda i, *_: (i, 0)),
            pl.BlockSpec((1, D_MODEL), lambda i, *_: (0, 0)),
            pl.BlockSpec(memory_space=pl.ANY),
        ],
        out_specs=pl.BlockSpec((TT, D_MODEL), lambda i, *_: (i, 0)),
        scratch_shapes=[pltpu.VMEM((2, LROWS, PACKED), U32),
                        pltpu.SemaphoreType.DMA((2,))],
    )
    return pl.pallas_call(
        _combine_body,
        grid_spec=grid_spec,
        out_shape=jax.ShapeDtypeStruct((n, D_MODEL), F32),
        compiler_params=pltpu.CompilerParams(
            dimension_semantics=("arbitrary",), vmem_limit_bytes=VMEM_LIMIT),
        name="combine",
    )(seg_glob, seg_loc, seg_cnt, x1, pos, rf, gfin, ys)


def _layer(x2, pos_b, invf, norm_mix_g, w_in, b_in, attn_sinks, w_o_attn, b_o_attn,
           w_pool_mix, pool_scale, w_pool_up, w_out, norm_ffn_g, w_router, b_router,
           w_gate, b_gate, w_up, b_up, w_down, b_down, out_g):
    n = x2.shape[0]
    q, kb, vb, u, gates = _inproj(
        x2, pos_b, invf, norm_mix_g[None, :], w_in.astype(BF16), b_in[None, :])
    attn = _attn(attn_sinks, q, kb, vb)

    wr = jnp.zeros((D_MODEL, LANES), BF16).at[:, :N_EXPERTS].set(w_router.astype(BF16))
    br = jnp.zeros((1, LANES), F32).at[0, :N_EXPERTS].set(b_router)
    x1, h2, ri, rf = _post(
        attn, u, gates, x2, w_o_attn.astype(BF16), b_o_attn[None, :],
        w_pool_mix.astype(BF16), pool_scale[None, :], w_pool_up.astype(BF16),
        w_out.astype(BF16), norm_ffn_g[None, :], wr, br)

    top_e = ri[:, :TOP_K]
    rank = ri[:, TOP_K:2 * TOP_K]
    nt = n // TT
    sel = top_e[..., None] == jnp.arange(N_EXPERTS, dtype=I32)
    seg_cnt = jnp.sum(sel.reshape(nt, TT * TOP_K, N_EXPERTS), axis=1, dtype=I32)
    before = jnp.cumsum(seg_cnt, axis=0) - seg_cnt
    seg_pad = (seg_cnt + SEG_ALIGN - 1) // SEG_ALIGN * SEG_ALIGN
    sizes = jnp.sum(seg_pad, axis=0)
    padded = (sizes + TM_MOE - 1) // TM_MOE * TM_MOE
    pends = jnp.cumsum(padded)
    pstarts = pends - padded
    seg_loc = jnp.cumsum(seg_pad, axis=1) - seg_pad
    seg_glob = pstarts[None, :] + jnp.cumsum(seg_pad, axis=0) - seg_pad
    seg_cnt = seg_pad
    shift = jnp.repeat(seg_loc - before, TT, axis=0)
    pos = rank + jnp.sum(jnp.where(sel, shift[:, None, :], 0), axis=-1)
    pos_t = jnp.pad(pos.reshape(nt, TT, TOP_K).transpose(0, 2, 1),
                    ((0, 0), (0, 8 - TOP_K), (0, 0)), constant_values=-1)
    pos_l = jnp.pad(pos, ((0, 0), (0, LANES - TOP_K)), constant_values=-1)
    seg_glob, seg_loc, seg_cnt = (a.reshape(-1) for a in (seg_glob, seg_loc, seg_cnt))
    m = n * TOP_K
    max_rows = m + nt * N_EXPERTS * (SEG_ALIGN - 1) + N_EXPERTS * (TM_MOE - 1)
    n_tiles = (max_rows + TM_MOE - 1) // TM_MOE
    tile_start = jnp.arange(n_tiles, dtype=I32) * TM_MOE
    tile_expert = jnp.minimum(
        jnp.sum(tile_start[:, None] >= pends[None, :], axis=-1), N_EXPERTS - 1).astype(I32)
    tile_rows = jnp.clip(sizes[tile_expert] - (tile_start - pstarts[tile_expert]), 0, TM_MOE)
    n_used = (pends[-1] // TM_MOE).astype(I32)[None]

    xs = _dispatch(seg_glob, seg_loc, seg_cnt, h2, pos_t, n_tiles * TM_MOE)
    ys = _moe(tile_expert, n_used, tile_rows.astype(I32), xs,
              w_gate, b_gate[:, None, :], w_up, b_up[:, None, :], w_down, b_down[:, None, :])
    return _combine(seg_glob, seg_loc, seg_cnt, x1, pos_l, rf, out_g[None, :], ys)


def kernel(x, positions, norm_mix_g, w_in, b_in, attn_sinks, w_o_attn, b_o_attn, w_pool_mix,
           pool_scale, w_pool_up, w_out, norm_ffn_g, w_router, b_router, w_gate, b_gate,
           w_up, b_up, w_down, b_down, norm_final_g):
    b, s, d = x.shape
    depth = w_in.shape[0]
    assert (s, d, depth) == (SEQ, D_MODEL, 1)
    n = b * s
    x2 = x.reshape(n, d)
    pos_b = jnp.broadcast_to(positions.reshape(n, 1), (n, LANES))
    inv_freq = ROPE_THETA ** (-jnp.arange(0, HEAD_DIM, 2, dtype=F32) / HEAD_DIM)
    invf = jnp.tile(inv_freq, LANES // (HEAD_DIM // 2))[None, :]
    out = _layer(x2, pos_b, invf, norm_mix_g[0], w_in[0], b_in[0], attn_sinks[0], w_o_attn[0],
                 b_o_attn[0], w_pool_mix[0], pool_scale[0], w_pool_up[0], w_out[0],
                 norm_ffn_g[0], w_router[0], b_router[0], w_gate[0], b_gate[0], w_up[0],
                 b_up[0], w_down[0], b_down[0], norm_final_g)
    return out.reshape(b, s, d)
```

```python
import functools

import numpy as np
import jax
import jax.numpy as jnp
from jax import lax
from jax.experimental import pallas as pl
from jax.experimental.pallas import tpu as pltpu

F32 = jnp.float32
BF16 = jnp.bfloat16
I32 = jnp.int32
U32 = jnp.uint32

D_MODEL = 1024
SEQ = 4096
HEAD_DIM = 64
N_Q_HEADS = 16
WINDOW = 128
ROPE_THETA = 10000.0
Q_WIDTH = N_Q_HEADS * HEAD_DIM
KV_WIDTH = 128
POOL_WINDOWS = (2, 4, 8, 16)
POOL_WIDTH = 512
POOL_GROUP = 128
POOL_HALO = 16
N_EXPERTS = 32
TOP_K = 4
SWIGLU_LIMIT = 7.0
SWIGLU_ALPHA = 1.702
RMS_EPS = 1e-5
NEG_BIG = -1e30

LANES = 128
TM_IN = 1024
TQ = 512
TM_POST = 512
TM_MOE = 512
TT = 256
SEG_ALIGN = 8
LROWS = TT * TOP_K + 2 * N_EXPERTS * SEG_ALIGN
PACKED = D_MODEL // 2
VMEM_LIMIT = 56 * 1024 * 1024


def _rms_norm(x, g):
    ms = jnp.mean(x * x, axis=-1, keepdims=True)
    return (x * lax.rsqrt(ms + RMS_EPS)) * g


def _dot(a, b):
    return jnp.dot(a, b, preferred_element_type=F32)


def _inproj_body(x_ref, pos_ref, invf_ref, g_ref, w_ref, b_ref,
                 q_ref, kb_ref, vb_ref, u_ref, gate_ref):
    h = _rms_norm(x_ref[...], g_ref[...]).astype(BF16)

    theta = pos_ref[...].astype(F32) * invf_ref[...]
    cos = jnp.cos(theta)
    sin = jnp.sin(theta)
    lane = lax.broadcasted_iota(I32, theta.shape, 1)
    first_half = (lane & 32) == 0
    sin_signed = jnp.where(first_half, -sin, sin)
    low_head = lane < HEAD_DIM

    def rope(t):
        swapped = jnp.where(first_half, pltpu.roll(t, 96, 1), pltpu.roll(t, 32, 1))
        return t * cos + swapped * sin_signed

    def band_layout(t, out_ref):
        tr = pltpu.roll(t, 64, 1)
        zero = jnp.zeros_like(t)
        chunks = (jnp.where(low_head, t, zero), jnp.where(low_head, zero, tr),
                  jnp.where(low_head, tr, zero), jnp.where(low_head, zero, t))
        for c, val in enumerate(chunks):
            out_ref[:, c * LANES:(c + 1) * LANES] = val.astype(BF16)

    qk_w = Q_WIDTH + KV_WIDTH
    zqk = _dot(h, w_ref[:, :qk_w]) + b_ref[:, :qk_w]
    scale = HEAD_DIM ** -0.5
    for j in range(Q_WIDTH // LANES):
        sl = slice(j * LANES, (j + 1) * LANES)
        q_ref[:, sl] = (rope(zqk[:, sl]) * scale).astype(BF16)
    band_layout(rope(zqk[:, Q_WIDTH:qk_w]), kb_ref)

    v0 = qk_w
    zv = _dot(h, w_ref[:, v0:v0 + KV_WIDTH]) + b_ref[:, v0:v0 + KV_WIDTH]
    band_layout(zv, vb_ref)

    u0 = v0 + KV_WIDTH
    u_ref[...] = _dot(h, w_ref[:, u0:u0 + POOL_WIDTH]) + b_ref[:, u0:u0 + POOL_WIDTH]

    g0 = u0 + POOL_WIDTH
    for c in range(2):
        sl = slice(g0 + c * D_MODEL, g0 + (c + 1) * D_MODEL)
        zg = _dot(h, w_ref[:, sl]) + b_ref[:, sl]
        gate_ref[:, c * D_MODEL:(c + 1) * D_MODEL] = jax.nn.sigmoid(zg).astype(BF16)


def _inproj(x2, pos_b, invf, g, w, b):
    n = x2.shape[0]
    in_width = w.shape[1]
    row = lambda i: (i, 0)
    const = lambda i: (0, 0)
    return pl.pallas_call(
        _inproj_body,
        grid=(n // TM_IN,),
        in_specs=[
            pl.BlockSpec((TM_IN, D_MODEL), row),
            pl.BlockSpec((TM_IN, LANES), row),
            pl.BlockSpec((1, LANES), const),
            pl.BlockSpec((1, D_MODEL), const),
            pl.BlockSpec((D_MODEL, in_width), const),
            pl.BlockSpec((1, in_width), const),
        ],
        out_specs=[
            pl.BlockSpec((TM_IN, Q_WIDTH), row),
            pl.BlockSpec((TM_IN, 4 * LANES), row),
            pl.BlockSpec((TM_IN, 4 * LANES), row),
            pl.BlockSpec((TM_IN, POOL_WIDTH), row),
            pl.BlockSpec((TM_IN, 2 * D_MODEL), row),
        ],
        out_shape=[
            jax.ShapeDtypeStruct((n, Q_WIDTH), BF16),
            jax.ShapeDtypeStruct((n, 4 * LANES), BF16),
            jax.ShapeDtypeStruct((n, 4 * LANES), BF16),
            jax.ShapeDtypeStruct((n, POOL_WIDTH), F32),
            jax.ShapeDtypeStruct((n, 2 * D_MODEL), BF16),
        ],
        compiler_params=pltpu.CompilerParams(
            dimension_semantics=("arbitrary",), vmem_limit_bytes=VMEM_LIMIT),
        name="inproj",
    )(x2, pos_b, invf, g, w, b)


def _attn_body(sinks_ref, q_ref, kbc_ref, kbp_ref, vbc_ref, vbp_ref, bias_ref,
               o_ref, p_scr):
    t = pl.program_id(0)
    seq_first = (t % (SEQ // TQ)) == 0
    lane = lax.broadcasted_iota(I32, (WINDOW, LANES), 1)
    low_head = lane < HEAD_DIM
    for n in range(TQ // WINDOW):
        rows = slice(n * WINDOW, (n + 1) * WINDOW)
        if n == 0:
            k_prev, v_prev = kbp_ref[...], vbp_ref[...]
            bias = jnp.where(seq_first, bias_ref[1], bias_ref[0])
        else:
            prev_rows = slice((n - 1) * WINDOW, n * WINDOW)
            k_prev, v_prev = kbc_ref[prev_rows, :], vbc_ref[prev_rows, :]
            bias = bias_ref[0]
        k_cur, v_cur = kbc_ref[rows, :], vbc_ref[rows, :]
        for g in range(2):
            def stack(prev, cur):
                lo = slice((2 * g) * LANES, (2 * g + 1) * LANES)
                hi = slice((2 * g + 1) * LANES, (2 * g + 2) * LANES)
                return jnp.concatenate([prev[:, lo], cur[:, lo], prev[:, hi], cur[:, hi]], axis=0)
            kmat = stack(k_prev, k_cur)
            vmat = stack(v_prev, v_cur)
            qs = jnp.concatenate(
                [q_ref[rows, (4 * g + p) * LANES:(4 * g + p + 1) * LANES] for p in range(4)],
                axis=0)
            s = lax.dot_general(qs, kmat, (((1,), (1,)), ((), ())),
                                preferred_element_type=F32) + bias
            inv = []
            for p in range(4):
                pr = slice(p * WINDOW, (p + 1) * WINDOW)
                inv_p = []
                for j in range(2):
                    cols = slice(j * 2 * WINDOW, (j + 1) * 2 * WINDOW)
                    sp = s[pr, cols]
                    sink = sinks_ref[8 * g + 2 * p + j]
                    m = jnp.maximum(jnp.max(sp, axis=-1, keepdims=True), sink)
                    e = jnp.exp(sp - m)
                    den = jnp.sum(e, axis=-1, keepdims=True) + jnp.exp(sink - m)
                    p_scr[pr, cols] = e.astype(BF16)
                    inv_p.append(1.0 / den)
                inv.append(inv_p)
            o = _dot(p_scr[...], vmat)
            for p in range(4):
                pr = slice(p * WINDOW, (p + 1) * WINDOW)
                norm = jnp.where(low_head, inv[p][0], inv[p][1])
                o_ref[rows, (4 * g + p) * LANES:(4 * g + p + 1) * LANES] = (o[pr, :] * norm).astype(BF16)


def _attn_bias():
    r = np.arange(4 * WINDOW)[:, None] % WINDOW
    c = np.arange(4 * WINDOW)[None, :] % (2 * WINDOW)
    band = (c > r) & (c <= r + WINDOW)
    first = band & (c >= WINDOW)
    return np.stack([np.where(band, 0.0, NEG_BIG), np.where(first, 0.0, NEG_BIG)]).astype(np.float32)


def _attn(sinks, q, kb, vb):
    n = q.shape[0]
    blocks_per_tile = TQ // WINDOW
    cur = lambda t: (t, 0)
    prev = lambda t: (jnp.maximum(t * blocks_per_tile - 1, 0), 0)
    bias = jnp.asarray(_attn_bias())
    return pl.pallas_call(
        _attn_body,
        grid=(n // TQ,),
        in_specs=[
            pl.BlockSpec(memory_space=pltpu.SMEM),
            pl.BlockSpec((TQ, Q_WIDTH), cur),
            pl.BlockSpec((TQ, 4 * LANES), cur),
            pl.BlockSpec((WINDOW, 4 * LANES), prev),
            pl.BlockSpec((TQ, 4 * LANES), cur),
            pl.BlockSpec((WINDOW, 4 * LANES), prev),
            pl.BlockSpec((2, 4 * WINDOW, 4 * WINDOW), lambda t: (0, 0, 0)),
        ],
        out_specs=pl.BlockSpec((TQ, Q_WIDTH), cur),
        out_shape=jax.ShapeDtypeStruct((n, Q_WIDTH), BF16),
        scratch_shapes=[pltpu.VMEM((4 * WINDOW, 4 * WINDOW), BF16)],
        compiler_params=pltpu.CompilerParams(
            dimension_semantics=("arbitrary",), vmem_limit_bytes=VMEM_LIMIT),
        name="attn",
    )(sinks, q, kb, kb, vb, vb, bias)


def _post_body(attn_ref, u_ref, uprev_ref, gate_ref, x_ref,
               wo_ref, bo_ref, wmix_ref, pscale_ref, wup_ref, wout_ref,
               gffn_ref, wr_ref, br_ref,
               x1_ref, h2_ref, ri_ref, rf_ref, tc_ref,
               ubuf, carry):
    i = pl.program_id(0)
    tiles_per_seq = SEQ // TM_POST
    seq_first = (i % tiles_per_seq) == 0

    @pl.when(i == 0)
    def _():
        carry[...] = jnp.zeros_like(carry)

    ubuf[0:POOL_HALO, :] = jnp.where(seq_first, 0.0, uprev_ref[...])
    ubuf[POOL_HALO:, :] = u_ref[...]
    row = lax.broadcasted_iota(I32, (TM_POST, 1), 0)
    tpos = (i % tiles_per_seq) * TM_POST + row
    mixed = []
    for gi, w in enumerate(POOL_WINDOWS):
        cols = slice(gi * POOL_GROUP, (gi + 1) * POOL_GROUP)
        acc = ubuf[POOL_HALO:, cols]
        for d in range(1, w):
            acc = acc + ubuf[POOL_HALO - d:POOL_HALO - d + TM_POST, cols]
        cnt = jnp.minimum(tpos + 1, w).astype(F32)
        pooled = acc / cnt - ubuf[POOL_HALO:, cols]
        mixed.append(_dot(pooled.astype(BF16), wmix_ref[gi]) * pscale_ref[:, cols])
    mixed = jnp.concatenate(mixed, axis=-1).astype(BF16)
    y_pool = _dot(mixed, wup_ref[...])
    y_attn = _dot(attn_ref[...], wo_ref[...]) + bo_ref[...]
    merged = (gate_ref[:, :D_MODEL].astype(F32) * y_attn
              + gate_ref[:, D_MODEL:].astype(F32) * y_pool)
    x1 = x_ref[...] + _dot(merged.astype(BF16), wout_ref[...])
    x1_ref[...] = x1
    h2 = _rms_norm(x1, gffn_ref[...])
    h2_bf = h2.astype(BF16)
    h2_ref[...] = h2_bf

    lane = lax.broadcasted_iota(I32, (TM_POST, LANES), 1)
    logits = _dot(h2_bf, wr_ref[...]) + br_ref[...]
    work = jnp.where(lane < N_EXPERTS, logits, -jnp.inf)
    top_v, top_e, onehots = [], [], []
    for _ in range(TOP_K):
        m = jnp.max(work, axis=-1, keepdims=True)
        idx = jnp.min(jnp.where(work == m, lane, LANES), axis=-1, keepdims=True)
        hit = lane == idx
        top_v.append(m)
        top_e.append(idx)
        onehots.append(hit)
        work = jnp.where(hit, -jnp.inf, work)
    exps = [jnp.exp(v - top_v[0]) for v in top_v]
    denom = exps[0] + exps[1] + exps[2] + exps[3]
    comb = [e / denom for e in exps]

    chosen = jnp.zeros((TM_POST, LANES), F32)
    for hit in onehots:
        chosen = chosen + hit.astype(F32)
    r_i = lax.broadcasted_iota(I32, (TM_POST, TM_POST), 0)
    c_i = lax.broadcasted_iota(I32, (TM_POST, TM_POST), 1)
    tril = (r_i > c_i).astype(BF16)
    before = _dot(tril, chosen.astype(BF16)) + carry[0:1, :]
    ri = jnp.zeros((TM_POST, LANES), I32)
    rf = jnp.zeros((TM_POST, LANES), F32)
    for k in range(TOP_K):
        rank = jnp.sum(jnp.where(onehots[k], before, 0.0), axis=-1, keepdims=True)
        ri = jnp.where(lane == k, top_e[k], ri)
        ri = jnp.where(lane == TOP_K + k, rank.astype(I32), ri)
        rf = jnp.where(lane == k, comb[k], rf)
    ri_ref[...] = ri
    rf_ref[...] = rf
    srow = lax.broadcasted_iota(I32, tc_ref.shape, 0)
    tc = jnp.zeros(tc_ref.shape, F32)
    total = carry[0:1, :]
    for s in range(TM_POST // TT):
        c_s = jnp.sum(chosen[s * TT:(s + 1) * TT, :], axis=0, keepdims=True)
        tc = jnp.where(srow == s, c_s, tc)
        total = total + c_s
    tc_ref[...] = tc
    carry[...] = jnp.broadcast_to(total, carry.shape)


def _post(attn, u, gates, x2, wo, bo, wmix, pscale, wup, wout, gffn, wr, br):
    n = x2.shape[0]
    row = lambda i: (i, 0)
    const = lambda i: (0, 0)
    halo_blocks = TM_POST // POOL_HALO
    prev = lambda i: (jnp.maximum(i * halo_blocks - 1, 0), 0)
    return pl.pallas_call(
        _post_body,
        grid=(n // TM_POST,),
        in_specs=[
            pl.BlockSpec((TM_POST, Q_WIDTH), row),
            pl.BlockSpec((TM_POST, POOL_WIDTH), row),
            pl.BlockSpec((POOL_HALO, POOL_WIDTH), prev),
            pl.BlockSpec((TM_POST, 2 * D_MODEL), row),
            pl.BlockSpec((TM_POST, D_MODEL), row),
            pl.BlockSpec((Q_WIDTH, D_MODEL), const),
            pl.BlockSpec((1, D_MODEL), const),
            pl.BlockSpec((len(POOL_WINDOWS), POOL_GROUP, POOL_GROUP), lambda i: (0, 0, 0)),
            pl.BlockSpec((1, POOL_WIDTH), const),
            pl.BlockSpec((POOL_WIDTH, D_MODEL), const),
            pl.BlockSpec((D_MODEL, D_MODEL), const),
            pl.BlockSpec((1, D_MODEL), const),
            pl.BlockSpec((D_MODEL, LANES), const),
            pl.BlockSpec((1, LANES), const),
        ],
        out_specs=[
            pl.BlockSpec((TM_POST, D_MODEL), row),
            pl.BlockSpec((TM_POST, D_MODEL), row),
            pl.BlockSpec((TM_POST, LANES), row),
            pl.BlockSpec((TM_POST, LANES), row),
            pl.BlockSpec((8, LANES), row),
        ],
        out_shape=[
            jax.ShapeDtypeStruct((n, D_MODEL), F32),
            jax.ShapeDtypeStruct((n, D_MODEL), BF16),
            jax.ShapeDtypeStruct((n, LANES), I32),
            jax.ShapeDtypeStruct((n, LANES), F32),
            jax.ShapeDtypeStruct((n // TM_POST * 8, LANES), F32),
        ],
        scratch_shapes=[pltpu.VMEM((TM_POST + POOL_HALO, POOL_WIDTH), F32),
                        pltpu.VMEM((8, LANES), F32)],
        compiler_params=pltpu.CompilerParams(
            dimension_semantics=("arbitrary",), vmem_limit_bytes=VMEM_LIMIT),
        name="post",
    )(attn, u, u, gates, x2, wo, bo, wmix, pscale, wup, wout, gffn, wr, br)


def _segment_copies(i, glob_ref, loc_ref, cnt_ref, make_copy, action):
    def body(e, c):
        idx = i * N_EXPERTS + e
        rows = pl.multiple_of(cnt_ref[idx], SEG_ALIGN)

        @pl.when(rows > 0)
        def _():
            loc = pl.multiple_of(loc_ref[idx], SEG_ALIGN)
            glob = pl.multiple_of(glob_ref[idx], SEG_ALIGN)
            action(make_copy(pl.ds(loc, rows), pl.ds(glob, rows)))
        return c
    lax.fori_loop(0, N_EXPERTS, body, 0)


def _pack_halves(v):
    c = v.shape[1] // 2
    bits = lax.bitcast_convert_type(v, U32)
    return (bits[:, :c] >> 16) | bits[:, c:]


def _unpack_halves(w):
    lo = lax.bitcast_convert_type(w << 16, F32)
    hi = lax.bitcast_convert_type(w & jnp.uint32(0xFFFF0000), F32)
    return jnp.concatenate([lo, hi], axis=1).astype(BF16)


def _start(cp):
    cp.start()


def _wait(cp):
    cp.wait()


def _dispatch_body(glob_ref, loc_ref, rd_ref, wr_ref, h2_ref, pos_ref, xs_hbm,
                   local, tail, sem):
    i = pl.program_id(0)
    nt = pl.num_programs(0)
    slot = i % 2

    def copies(tile, s, action):
        def make_copy(loc_rows, glob_rows):
            return pltpu.make_async_copy(local.at[s, loc_rows], xs_hbm.at[glob_rows], sem.at[s])
        _segment_copies(tile, glob_ref, loc_ref, wr_ref, make_copy, action)

    @pl.when(i == 0)
    def _():
        tail[...] = jnp.zeros_like(tail)

    @pl.when(i >= 2)
    def _():
        copies(i - 2, slot, _wait)

    j = lax.broadcasted_iota(I32, (LROWS, TT), 0)
    hit = j == pos_ref[0, 0:1, :]
    for k in range(1, TOP_K):
        hit = hit | (j == pos_ref[0, k:k + 1, :])
    perm = jnp.where(hit, 1.0, 0.0).astype(BF16)
    local[slot] = _pack_halves(_dot(perm, h2_ref[...]))

    def merge(e, c):
        idx = i * N_EXPERTS + e
        rd = rd_ref[idx]
        wr = wr_ref[idx]

        @pl.when(rd > 0)
        def _():
            loc = pl.multiple_of(loc_ref[idx], SEG_ALIGN)
            local[slot, pl.ds(loc, SEG_ALIGN), :] = local[slot, pl.ds(loc, SEG_ALIGN), :] | tail[e]
            last = pl.multiple_of(loc + wr, SEG_ALIGN)
            keep = jnp.where(wr < rd, local[slot, pl.ds(last, SEG_ALIGN), :], jnp.uint32(0))
            tail[e] = keep
        return c
    lax.fori_loop(0, N_EXPERTS, merge, 0)
    copies(i, slot, _start)

    @pl.when(i == nt - 1)
    def _():
        copies(i - 1, 1 - slot, _wait)
        copies(i, slot, _wait)


def _dispatch(seg_glob, seg_loc, seg_rd, seg_wr, h2, pos_t, n_rows):
    n = h2.shape[0]
    grid_spec = pltpu.PrefetchScalarGridSpec(
        num_scalar_prefetch=4,
        grid=(n // TT,),
        in_specs=[
            pl.BlockSpec((TT, D_MODEL), lambda i, *_: (i, 0)),
            pl.BlockSpec((1, 8, TT), lambda i, *_: (i, 0, 0)),
        ],
        out_specs=pl.BlockSpec(memory_space=pl.ANY),
        scratch_shapes=[pltpu.VMEM((2, LROWS, PACKED), U32),
                        pltpu.VMEM((N_EXPERTS, SEG_ALIGN, PACKED), U32),
                        pltpu.SemaphoreType.DMA((2,))],
    )
    return pl.pallas_call(
        _dispatch_body,
        grid_spec=grid_spec,
        out_shape=jax.ShapeDtypeStruct((n_rows, PACKED), U32),
        compiler_params=pltpu.CompilerParams(
            dimension_semantics=("arbitrary",), vmem_limit_bytes=VMEM_LIMIT),
        name="dispatch",
    )(seg_glob, seg_loc, seg_rd, seg_wr, h2, pos_t)


def _moe_body(te_ref, nused_ref, rows_ref, xs_ref, wg_ref, bg_ref, wu_ref, bu_ref,
              wd_ref, bd_ref, y_ref, wbf):
    i = pl.program_id(0)

    @pl.when(i < nused_ref[0])
    def _():
        @pl.when((i == 0) | (te_ref[i] != te_ref[jnp.maximum(i - 1, 0)]))
        def _():
            wbf[0] = wg_ref[0].astype(BF16)
            wbf[1] = wu_ref[0].astype(BF16)
            wbf[2] = wd_ref[0].astype(BF16)

        row = lax.broadcasted_iota(I32, (TM_MOE, 1), 0)
        x = _unpack_halves(jnp.where(row < rows_ref[i], xs_ref[...], jnp.uint32(0)))
        g = _dot(x, wbf[0]) + bg_ref[0]
        u = _dot(x, wbf[1]) + bu_ref[0]
        g = jnp.minimum(g, SWIGLU_LIMIT)
        u = jnp.clip(u, -SWIGLU_LIMIT, SWIGLU_LIMIT)
        a = g * jax.nn.sigmoid(SWIGLU_ALPHA * g) * (u + 1.0)
        y = _dot(a.astype(BF16), wbf[2]) + bd_ref[0]
        y_ref[...] = _pack_halves(y.astype(BF16).astype(F32))


def _moe(tile_expert, n_used, tile_rows, xs, wg, bg, wu, bu, wd, bd):
    n_rows = xs.shape[0]
    n_tiles = n_rows // TM_MOE
    d_ff = wg.shape[2]
    row = lambda i, te, nu, tr: (jnp.minimum(i, nu[0] - 1), 0)
    wsel = lambda i, te, nu, tr: (te[i], 0, 0)
    grid_spec = pltpu.PrefetchScalarGridSpec(
        num_scalar_prefetch=3,
        grid=(n_tiles,),
        in_specs=[
            pl.BlockSpec((TM_MOE, PACKED), row),
            pl.BlockSpec((1, D_MODEL, d_ff), wsel),
            pl.BlockSpec((1, 1, d_ff), wsel),
            pl.BlockSpec((1, D_MODEL, d_ff), wsel),
            pl.BlockSpec((1, 1, d_ff), wsel),
            pl.BlockSpec((1, d_ff, D_MODEL), wsel),
            pl.BlockSpec((1, 1, D_MODEL), wsel),
        ],
        out_specs=pl.BlockSpec((TM_MOE, PACKED), row),
        scratch_shapes=[pltpu.VMEM((3, D_MODEL, d_ff), BF16)],
    )
    assert d_ff == D_MODEL
    return pl.pallas_call(
        _moe_body,
        grid_spec=grid_spec,
        out_shape=jax.ShapeDtypeStruct((n_rows, PACKED), U32),
        compiler_params=pltpu.CompilerParams(
            dimension_semantics=("arbitrary",), vmem_limit_bytes=VMEM_LIMIT),
        name="moe",
    )(tile_expert, n_used, tile_rows, xs, wg, bg, wu, bu, wd, bd)


def _combine_body(glob_ref, loc_ref, cnt_ref, x1_ref, pos_ref, rf_ref, gfin_ref,
                  ys_hbm, o_ref, local, sem):
    i = pl.program_id(0)
    nt = pl.num_programs(0)
    slot = i % 2

    def copies(tile, s, action):
        def make_copy(loc_rows, glob_rows):
            return pltpu.make_async_copy(ys_hbm.at[glob_rows], local.at[s, loc_rows], sem.at[s])
        _segment_copies(tile, glob_ref, loc_ref, cnt_ref, make_copy, action)

    @pl.when(i == 0)
    def _():
        local[...] = jnp.zeros_like(local)
        copies(0, 0, _start)

    @pl.when(i + 1 < nt)
    def _():
        copies(i + 1, 1 - slot, _start)
    copies(i, slot, _wait)

    j = lax.broadcasted_iota(I32, (TT, LROWS), 1)
    wc = jnp.zeros((TT, LROWS), F32)
    for k in range(TOP_K):
        wc = jnp.where(j == pos_ref[:, k:k + 1], rf_ref[:, k:k + 1], wc)
    acc = x1_ref[...] + _dot(wc.astype(BF16), _unpack_halves(local[slot]))
    o_ref[...] = _rms_norm(acc, gfin_ref[...])


def _combine(seg_glob, seg_loc, seg_cnt, x1, pos, rf, gfin, ys):
    n = x1.shape[0]
    grid_spec = pltpu.PrefetchScalarGridSpec(
        num_scalar_prefetch=3,
        grid=(n // TT,),
        in_specs=[
            pl.BlockSpec((TT, D_MODEL), lambda i, *_: (i, 0)),
            pl.BlockSpec((TT, LANES), lambda i, *_: (i, 0)),
            pl.BlockSpec((TT, LANES), lambda i, *_: (i, 0)),
            pl.BlockSpec((1, D_MODEL), lambda i, *_: (0, 0)),
            pl.BlockSpec(memory_space=pl.ANY),
        ],
        out_specs=pl.BlockSpec((TT, D_MODEL), lambda i, *_: (i, 0)),
        scratch_shapes=[pltpu.VMEM((2, LROWS, PACKED), U32),
                        pltpu.SemaphoreType.DMA((2,))],
    )
    return pl.pallas_call(
        _combine_body,
        grid_spec=grid_spec,
        out_shape=jax.ShapeDtypeStruct((n, D_MODEL), F32),
        compiler_params=pltpu.CompilerParams(
            dimension_semantics=("arbitrary",), vmem_limit_bytes=VMEM_LIMIT),
        name="combine",
    )(seg_glob, seg_loc, seg_cnt, x1, pos, rf, gfin, ys)


def _layer(x2, pos_b, invf, norm_mix_g, w_in, b_in, attn_sinks, w_o_attn, b_o_attn,
           w_pool_mix, pool_scale, w_pool_up, w_out, norm_ffn_g, w_router, b_router,
           w_gate, b_gate, w_up, b_up, w_down, b_down, out_g):
    n = x2.shape[0]
    q, kb, vb, u, gates = _inproj(
        x2, pos_b, invf, norm_mix_g[None, :], w_in.astype(BF16), b_in[None, :])
    attn = _attn(attn_sinks, q, kb, vb)

    wr = jnp.zeros((D_MODEL, LANES), BF16).at[:, :N_EXPERTS].set(w_router.astype(BF16))
    br = jnp.zeros((1, LANES), F32).at[0, :N_EXPERTS].set(b_router)
    x1, h2, ri, rf, tc = _post(
        attn, u, gates, x2, w_o_attn.astype(BF16), b_o_attn[None, :],
        w_pool_mix.astype(BF16), pool_scale[None, :], w_pool_up.astype(BF16),
        w_out.astype(BF16), norm_ffn_g[None, :], wr, br)

    nt = n // TT
    sub = TM_POST // TT
    seg_cnt = tc.reshape(n // TM_POST, 8, LANES)[:, :sub, :N_EXPERTS].reshape(nt, N_EXPERTS).astype(I32)
    before = jnp.cumsum(seg_cnt, axis=0) - seg_cnt
    sizes = jnp.sum(seg_cnt, axis=0)
    padded = (sizes + TM_MOE - 1) // TM_MOE * TM_MOE
    pends = jnp.cumsum(padded)
    pstarts = pends - padded
    head = before % SEG_ALIGN
    last_tile = (jnp.arange(nt, dtype=I32) == nt - 1)[:, None]
    blocks_up = (head + seg_cnt + SEG_ALIGN - 1) // SEG_ALIGN
    blocks_dn = (head + seg_cnt) // SEG_ALIGN
    seg_rd = jnp.where((seg_cnt > 0) | (last_tile & (head > 0)), blocks_up, 0) * SEG_ALIGN
    seg_wr = jnp.where(last_tile, seg_rd, jnp.where(seg_cnt > 0, blocks_dn * SEG_ALIGN, 0))
    seg_loc = jnp.cumsum(seg_rd, axis=1) - seg_rd
    seg_glob = pstarts[None, :] + before - head
    shift = jnp.repeat(seg_loc + head - before, TT, axis=0)
    sel = ri[:, :TOP_K, None] == jnp.arange(N_EXPERTS, dtype=I32)
    pos = ri[:, TOP_K:2 * TOP_K] + jnp.sum(jnp.where(sel, shift[:, None, :], 0), axis=-1)
    pos_t = jnp.pad(pos.reshape(nt, TT, TOP_K).transpose(0, 2, 1),
                    ((0, 0), (0, 8 - TOP_K), (0, 0)), constant_values=-1)
    pos_l = jnp.pad(pos, ((0, 0), (0, LANES - TOP_K)), constant_values=-1)
    seg_glob, seg_loc, seg_rd, seg_wr = (a.reshape(-1) for a in (seg_glob, seg_loc, seg_rd, seg_wr))
    m = n * TOP_K
    n_tiles = (m + N_EXPERTS * (TM_MOE - 1) + TM_MOE - 1) // TM_MOE
    tile_start = jnp.arange(n_tiles, dtype=I32) * TM_MOE
    tile_expert = jnp.minimum(
        jnp.sum(tile_start[:, None] >= pends[None, :], axis=-1), N_EXPERTS - 1).astype(I32)
    tile_rows = jnp.clip(sizes[tile_expert] - (tile_start - pstarts[tile_expert]), 0, TM_MOE)
    n_used = (pends[-1] // TM_MOE).astype(I32)[None]

    xs = _dispatch(seg_glob, seg_loc, seg_rd, seg_wr, h2, pos_t, n_tiles * TM_MOE)
    ys = _moe(tile_expert, n_used, tile_rows.astype(I32), xs,
              w_gate, b_gate[:, None, :], w_up, b_up[:, None, :], w_down, b_down[:, None, :])
    return _combine(seg_glob, seg_loc, seg_rd, x1, pos_l, rf, out_g[None, :], ys)


def kernel(x, positions, norm_mix_g, w_in, b_in, attn_sinks, w_o_attn, b_o_attn, w_pool_mix,
           pool_scale, w_pool_up, w_out, norm_ffn_g, w_router, b_router, w_gate, b_gate,
           w_up, b_up, w_down, b_down, norm_final_g):
    b, s, d = x.shape
    depth = w_in.shape[0]
    assert (s, d, depth) == (SEQ, D_MODEL, 1)
    n = b * s
    x2 = x.reshape(n, d)
    pos_b = jnp.broadcast_to(positions.reshape(n, 1), (n, LANES))
    inv_freq = ROPE_THETA ** (-jnp.arange(0, HEAD_DIM, 2, dtype=F32) / HEAD_DIM)
    invf = jnp.tile(inv_freq, LANES // (HEAD_DIM // 2))[None, :]
    out = _layer(x2, pos_b, invf, norm_mix_g[0], w_in[0], b_in[0], attn_sinks[0], w_o_attn[0],
                 b_o_attn[0], w_pool_mix[0], pool_scale[0], w_pool_up[0], w_out[0],
                 norm_ffn_g[0], w_router[0], b_router[0], w_gate[0], b_gate[0], w_up[0],
                 b_up[0], w_down[0], b_down[0], norm_final_g)
    return out.reshape(b, s, d)
```

```python
import functools

import numpy as np
import jax
import jax.numpy as jnp
from jax import lax
from jax.experimental import pallas as pl
from jax.experimental.pallas import tpu as pltpu

F32 = jnp.float32
BF16 = jnp.bfloat16
I32 = jnp.int32
U32 = jnp.uint32

D_MODEL = 1024
SEQ = 4096
HEAD_DIM = 64
N_Q_HEADS = 16
WINDOW = 128
ROPE_THETA = 10000.0
Q_WIDTH = N_Q_HEADS * HEAD_DIM
KV_WIDTH = 128
POOL_WINDOWS = (2, 4, 8, 16)
POOL_WIDTH = 512
POOL_GROUP = 128
POOL_HALO = 16
N_EXPERTS = 32
TOP_K = 4
SWIGLU_LIMIT = 7.0
SWIGLU_ALPHA = 1.702
RMS_EPS = 1e-5
NEG_BIG = -1e30

LANES = 128
TM_IN = 1024
TQ = 512
TM_POST = 512
TM_MOE = 512
TT = 256
SEG_ALIGN = 8
LROWS = TT * TOP_K + 2 * N_EXPERTS * SEG_ALIGN
PACKED = D_MODEL // 2
VMEM_LIMIT = 56 * 1024 * 1024


def _rms_norm(x, g):
    ms = jnp.mean(x * x, axis=-1, keepdims=True)
    return (x * lax.rsqrt(ms + RMS_EPS)) * g


def _dot(a, b):
    return jnp.dot(a, b, preferred_element_type=F32)


def _inproj_body(x_ref, pos_ref, invf_ref, g_ref, w_ref, b_ref,
                 q_ref, kb_ref, vb_ref, u_ref, gate_ref):
    h = _rms_norm(x_ref[...], g_ref[...]).astype(BF16)

    theta = pos_ref[...].astype(F32) * invf_ref[...]
    cos = jnp.cos(theta)
    sin = jnp.sin(theta)
    lane = lax.broadcasted_iota(I32, theta.shape, 1)
    first_half = (lane & 32) == 0
    sin_signed = jnp.where(first_half, -sin, sin)
    low_head = lane < HEAD_DIM

    def rope(t):
        swapped = jnp.where(first_half, pltpu.roll(t, 96, 1), pltpu.roll(t, 32, 1))
        return t * cos + swapped * sin_signed

    def band_layout(t, out_ref):
        tr = pltpu.roll(t, 64, 1)
        zero = jnp.zeros_like(t)
        chunks = (jnp.where(low_head, t, zero), jnp.where(low_head, zero, tr),
                  jnp.where(low_head, tr, zero), jnp.where(low_head, zero, t))
        for c, val in enumerate(chunks):
            out_ref[:, c * LANES:(c + 1) * LANES] = val.astype(BF16)

    qk_w = Q_WIDTH + KV_WIDTH
    zqk = _dot(h, w_ref[:, :qk_w]) + b_ref[:, :qk_w]
    scale = HEAD_DIM ** -0.5
    for j in range(Q_WIDTH // LANES):
        sl = slice(j * LANES, (j + 1) * LANES)
        q_ref[:, sl] = (rope(zqk[:, sl]) * scale).astype(BF16)
    band_layout(rope(zqk[:, Q_WIDTH:qk_w]), kb_ref)

    v0 = qk_w
    zv = _dot(h, w_ref[:, v0:v0 + KV_WIDTH]) + b_ref[:, v0:v0 + KV_WIDTH]
    band_layout(zv, vb_ref)

    u0 = v0 + KV_WIDTH
    u_ref[...] = _dot(h, w_ref[:, u0:u0 + POOL_WIDTH]) + b_ref[:, u0:u0 + POOL_WIDTH]

    g0 = u0 + POOL_WIDTH
    for c in range(2):
        sl = slice(g0 + c * D_MODEL, g0 + (c + 1) * D_MODEL)
        zg = _dot(h, w_ref[:, sl]) + b_ref[:, sl]
        gate_ref[:, c * D_MODEL:(c + 1) * D_MODEL] = jax.nn.sigmoid(zg).astype(BF16)


def _inproj(x2, pos_b, invf, g, w, b):
    n = x2.shape[0]
    in_width = w.shape[1]
    row = lambda i: (i, 0)
    const = lambda i: (0, 0)
    return pl.pallas_call(
        _inproj_body,
        grid=(n // TM_IN,),
        in_specs=[
            pl.BlockSpec((TM_IN, D_MODEL), row),
            pl.BlockSpec((TM_IN, LANES), row),
            pl.BlockSpec((1, LANES), const),
            pl.BlockSpec((1, D_MODEL), const),
            pl.BlockSpec((D_MODEL, in_width), const),
            pl.BlockSpec((1, in_width), const),
        ],
        out_specs=[
            pl.BlockSpec((TM_IN, Q_WIDTH), row),
            pl.BlockSpec((TM_IN, 4 * LANES), row),
            pl.BlockSpec((TM_IN, 4 * LANES), row),
            pl.BlockSpec((TM_IN, POOL_WIDTH), row),
            pl.BlockSpec((TM_IN, 2 * D_MODEL), row),
        ],
        out_shape=[
            jax.ShapeDtypeStruct((n, Q_WIDTH), BF16),
            jax.ShapeDtypeStruct((n, 4 * LANES), BF16),
            jax.ShapeDtypeStruct((n, 4 * LANES), BF16),
            jax.ShapeDtypeStruct((n, POOL_WIDTH), F32),
            jax.ShapeDtypeStruct((n, 2 * D_MODEL), BF16),
        ],
        compiler_params=pltpu.CompilerParams(
            dimension_semantics=("arbitrary",), vmem_limit_bytes=VMEM_LIMIT),
        name="inproj",
    )(x2, pos_b, invf, g, w, b)


def _attn_body(sinks_ref, q_ref, kbc_ref, kbp_ref, vbc_ref, vbp_ref, bias_ref,
               o_ref, p_scr):
    t = pl.program_id(0)
    seq_first = (t % (SEQ // TQ)) == 0
    lane = lax.broadcasted_iota(I32, (WINDOW, LANES), 1)
    low_head = lane < HEAD_DIM
    for n in range(TQ // WINDOW):
        rows = slice(n * WINDOW, (n + 1) * WINDOW)
        if n == 0:
            k_prev, v_prev = kbp_ref[...], vbp_ref[...]
            bias = jnp.where(seq_first, bias_ref[1], bias_ref[0])
        else:
            prev_rows = slice((n - 1) * WINDOW, n * WINDOW)
            k_prev, v_prev = kbc_ref[prev_rows, :], vbc_ref[prev_rows, :]
            bias = bias_ref[0]
        k_cur, v_cur = kbc_ref[rows, :], vbc_ref[rows, :]
        for g in range(2):
            def stack(prev, cur):
                lo = slice((2 * g) * LANES, (2 * g + 1) * LANES)
                hi = slice((2 * g + 1) * LANES, (2 * g + 2) * LANES)
                return jnp.concatenate([prev[:, lo], cur[:, lo], prev[:, hi], cur[:, hi]], axis=0)
            kmat = stack(k_prev, k_cur)
            vmat = stack(v_prev, v_cur)
            qs = jnp.concatenate(
                [q_ref[rows, (4 * g + p) * LANES:(4 * g + p + 1) * LANES] for p in range(4)],
                axis=0)
            s = lax.dot_general(qs, kmat, (((1,), (1,)), ((), ())),
                                preferred_element_type=F32) + bias
            inv = []
            for p in range(4):
                pr = slice(p * WINDOW, (p + 1) * WINDOW)
                inv_p = []
                for j in range(2):
                    cols = slice(j * 2 * WINDOW, (j + 1) * 2 * WINDOW)
                    sp = s[pr, cols]
                    sink = sinks_ref[8 * g + 2 * p + j]
                    m = jnp.maximum(jnp.max(sp, axis=-1, keepdims=True), sink)
                    e = jnp.exp(sp - m)
                    den = jnp.sum(e, axis=-1, keepdims=True) + jnp.exp(sink - m)
                    p_scr[pr, cols] = e.astype(BF16)
                    inv_p.append(1.0 / den)
                inv.append(inv_p)
            o = _dot(p_scr[...], vmat)
            for p in range(4):
                pr = slice(p * WINDOW, (p + 1) * WINDOW)
                norm = jnp.where(low_head, inv[p][0], inv[p][1])
                o_ref[rows, (4 * g + p) * LANES:(4 * g + p + 1) * LANES] = (o[pr, :] * norm).astype(BF16)


def _attn_bias():
    r = np.arange(4 * WINDOW)[:, None] % WINDOW
    c = np.arange(4 * WINDOW)[None, :] % (2 * WINDOW)
    band = (c > r) & (c <= r + WINDOW)
    first = band & (c >= WINDOW)
    return np.stack([np.where(band, 0.0, NEG_BIG), np.where(first, 0.0, NEG_BIG)]).astype(np.float32)


def _attn(sinks, q, kb, vb):
    n = q.shape[0]
    blocks_per_tile = TQ // WINDOW
    cur = lambda t: (t, 0)
    prev = lambda t: (jnp.maximum(t * blocks_per_tile - 1, 0), 0)
    bias = jnp.asarray(_attn_bias())
    return pl.pallas_call(
        _attn_body,
        grid=(n // TQ,),
        in_specs=[
            pl.BlockSpec(memory_space=pltpu.SMEM),
            pl.BlockSpec((TQ, Q_WIDTH), cur),
            pl.BlockSpec((TQ, 4 * LANES), cur),
            pl.BlockSpec((WINDOW, 4 * LANES), prev),
            pl.BlockSpec((TQ, 4 * LANES), cur),
            pl.BlockSpec((WINDOW, 4 * LANES), prev),
            pl.BlockSpec((2, 4 * WINDOW, 4 * WINDOW), lambda t: (0, 0, 0)),
        ],
        out_specs=pl.BlockSpec((TQ, Q_WIDTH), cur),
        out_shape=jax.ShapeDtypeStruct((n, Q_WIDTH), BF16),
        scratch_shapes=[pltpu.VMEM((4 * WINDOW, 4 * WINDOW), BF16)],
        compiler_params=pltpu.CompilerParams(
            dimension_semantics=("arbitrary",), vmem_limit_bytes=VMEM_LIMIT),
        name="attn",
    )(sinks, q, kb, kb, vb, vb, bias)


def _post_body(attn_ref, u_ref, uprev_ref, gate_ref, x_ref,
               wo_ref, bo_ref, wmix_ref, pscale_ref, wup_ref, wout_ref,
               gffn_ref, wr_ref, br_ref,
               x1_ref, h2_ref, ri_ref, rf_ref, tc_ref,
               ubuf, carry):
    i = pl.program_id(0)
    tiles_per_seq = SEQ // TM_POST
    seq_first = (i % tiles_per_seq) == 0

    @pl.when(i == 0)
    def _():
        carry[...] = jnp.zeros_like(carry)

    ubuf[0:POOL_HALO, :] = jnp.where(seq_first, 0.0, uprev_ref[...])
    ubuf[POOL_HALO:, :] = u_ref[...]
    row = lax.broadcasted_iota(I32, (TM_POST, 1), 0)
    tpos = (i % tiles_per_seq) * TM_POST + row
    mixed = []
    for gi, w in enumerate(POOL_WINDOWS):
        cols = slice(gi * POOL_GROUP, (gi + 1) * POOL_GROUP)
        acc = ubuf[POOL_HALO:, cols]
        for d in range(1, w):
            acc = acc + ubuf[POOL_HALO - d:POOL_HALO - d + TM_POST, cols]
        cnt = jnp.minimum(tpos + 1, w).astype(F32)
        pooled = acc / cnt - ubuf[POOL_HALO:, cols]
        mixed.append(_dot(pooled.astype(BF16), wmix_ref[gi]) * pscale_ref[:, cols])
    mixed = jnp.concatenate(mixed, axis=-1).astype(BF16)
    y_pool = _dot(mixed, wup_ref[...])
    y_attn = _dot(attn_ref[...], wo_ref[...]) + bo_ref[...]
    merged = (gate_ref[:, :D_MODEL].astype(F32) * y_attn
              + gate_ref[:, D_MODEL:].astype(F32) * y_pool)
    x1 = x_ref[...] + _dot(merged.astype(BF16), wout_ref[...])
    x1_ref[...] = x1
    h2 = _rms_norm(x1, gffn_ref[...])
    h2_bf = h2.astype(BF16)
    h2_ref[...] = h2_bf

    lane = lax.broadcasted_iota(I32, (TM_POST, LANES), 1)
    logits = _dot(h2_bf, wr_ref[...]) + br_ref[...]
    work = jnp.where(lane < N_EXPERTS, logits, -jnp.inf)
    top_v, top_e, onehots = [], [], []
    for _ in range(TOP_K):
        m = jnp.max(work, axis=-1, keepdims=True)
        idx = jnp.min(jnp.where(work == m, lane, LANES), axis=-1, keepdims=True)
        hit = lane == idx
        top_v.append(m)
        top_e.append(idx)
        onehots.append(hit)
        work = jnp.where(hit, -jnp.inf, work)
    exps = [jnp.exp(v - top_v[0]) for v in top_v]
    denom = exps[0] + exps[1] + exps[2] + exps[3]
    comb = [e / denom for e in exps]

    chosen = jnp.zeros((TM_POST, LANES), F32)
    for hit in onehots:
        chosen = chosen + hit.astype(F32)
    r_i = lax.broadcasted_iota(I32, (TM_POST, TM_POST), 0)
    c_i = lax.broadcasted_iota(I32, (TM_POST, TM_POST), 1)
    tril = (r_i > c_i).astype(BF16)
    before = _dot(tril, chosen.astype(BF16)) + carry[0:1, :]
    ri = jnp.zeros((TM_POST, LANES), I32)
    rf = jnp.zeros((TM_POST, LANES), F32)
    for k in range(TOP_K):
        rank = jnp.sum(jnp.where(onehots[k], before, 0.0), axis=-1, keepdims=True)
        ri = jnp.where(lane == k, top_e[k], ri)
        ri = jnp.where(lane == TOP_K + k, rank.astype(I32), ri)
        rf = jnp.where(lane == k, comb[k], rf)
    ri_ref[...] = ri
    rf_ref[...] = rf
    srow = lax.broadcasted_iota(I32, tc_ref.shape, 0)
    tc = jnp.zeros(tc_ref.shape, F32)
    total = carry[0:1, :]
    for s in range(TM_POST // TT):
        c_s = jnp.sum(chosen[s * TT:(s + 1) * TT, :], axis=0, keepdims=True)
        tc = jnp.where(srow == s, c_s, tc)
        total = total + c_s
    tc_ref[...] = tc
    carry[...] = jnp.broadcast_to(total, carry.shape)


def _post(attn, u, gates, x2, wo, bo, wmix, pscale, wup, wout, gffn, wr, br):
    n = x2.shape[0]
    row = lambda i: (i, 0)
    const = lambda i: (0, 0)
    halo_blocks = TM_POST // POOL_HALO
    prev = lambda i: (jnp.maximum(i * halo_blocks - 1, 0), 0)
    return pl.pallas_call(
        _post_body,
        grid=(n // TM_POST,),
        in_specs=[
            pl.BlockSpec((TM_POST, Q_WIDTH), row),
            pl.BlockSpec((TM_POST, POOL_WIDTH), row),
            pl.BlockSpec((POOL_HALO, POOL_WIDTH), prev),
            pl.BlockSpec((TM_POST, 2 * D_MODEL), row),
            pl.BlockSpec((TM_POST, D_MODEL), row),
            pl.BlockSpec((Q_WIDTH, D_MODEL), const),
            pl.BlockSpec((1, D_MODEL), const),
            pl.BlockSpec((len(POOL_WINDOWS), POOL_GROUP, POOL_GROUP), lambda i: (0, 0, 0)),
            pl.BlockSpec((1, POOL_WIDTH), const),
            pl.BlockSpec((POOL_WIDTH, D_MODEL), const),
            pl.BlockSpec((D_MODEL, D_MODEL), const),
            pl.BlockSpec((1, D_MODEL), const),
            pl.BlockSpec((D_MODEL, LANES), const),
            pl.BlockSpec((1, LANES), const),
        ],
        out_specs=[
            pl.BlockSpec((TM_POST, D_MODEL), row),
            pl.BlockSpec((TM_POST, D_MODEL), row),
            pl.BlockSpec((TM_POST, LANES), row),
            pl.BlockSpec((TM_POST, LANES), row),
            pl.BlockSpec((8, LANES), row),
        ],
        out_shape=[
            jax.ShapeDtypeStruct((n, D_MODEL), F32),
            jax.ShapeDtypeStruct((n, D_MODEL), BF16),
            jax.ShapeDtypeStruct((n, LANES), I32),
            jax.ShapeDtypeStruct((n, LANES), F32),
            jax.ShapeDtypeStruct((n // TM_POST * 8, LANES), F32),
        ],
        scratch_shapes=[pltpu.VMEM((TM_POST + POOL_HALO, POOL_WIDTH), F32),
                        pltpu.VMEM((8, LANES), F32)],
        compiler_params=pltpu.CompilerParams(
            dimension_semantics=("arbitrary",), vmem_limit_bytes=VMEM_LIMIT),
        name="post",
    )(attn, u, u, gates, x2, wo, bo, wmix, pscale, wup, wout, gffn, wr, br)


def _start_segments(i, glob_ref, loc_ref, cnt_ref, make_copy):
    def body(e, c):
        idx = i * N_EXPERTS + e
        rows = pl.multiple_of(cnt_ref[idx], SEG_ALIGN)

        @pl.when(rows > 0)
        def _():
            loc = pl.multiple_of(loc_ref[idx], SEG_ALIGN)
            glob = pl.multiple_of(glob_ref[idx], SEG_ALIGN)
            make_copy(pl.ds(loc, rows), pl.ds(glob, rows)).start()
        return c
    lax.fori_loop(0, N_EXPERTS, body, 0, unroll=4)


def _wait_segments(i, total_ref, make_copy):
    rows = pl.multiple_of(total_ref[i], SEG_ALIGN)

    @pl.when(rows > 0)
    def _():
        make_copy(pl.ds(0, rows), pl.ds(0, rows)).wait()


def _pack_halves(v):
    c = v.shape[1] // 2
    bits = lax.bitcast_convert_type(v, U32)
    return (bits[:, :c] >> 16) | bits[:, c:]


def _unpack_halves(w):
    lo = lax.bitcast_convert_type(w << 16, F32)
    hi = lax.bitcast_convert_type(w & jnp.uint32(0xFFFF0000), F32)
    return jnp.concatenate([lo, hi], axis=1).astype(BF16)


def _dispatch_body(glob_ref, loc_ref, rd_ref, wr_ref, total_ref, h2_ref, pos_ref, xs_hbm,
                   local, tail, sem):
    i = pl.program_id(0)
    nt = pl.num_programs(0)
    slot = i % 2

    def copy_from(s):
        def make_copy(loc_rows, glob_rows):
            return pltpu.make_async_copy(local.at[s, loc_rows], xs_hbm.at[glob_rows], sem.at[s])
        return make_copy

    @pl.when(i == 0)
    def _():
        tail[...] = jnp.zeros_like(tail)

    @pl.when(i >= 2)
    def _():
        _wait_segments(i - 2, total_ref, copy_from(slot))

    j = lax.broadcasted_iota(I32, (LROWS, TT), 0)
    hit = j == pos_ref[0, 0:1, :]
    for k in range(1, TOP_K):
        hit = hit | (j == pos_ref[0, k:k + 1, :])
    perm = jnp.where(hit, 1.0, 0.0).astype(BF16)
    local[slot] = _pack_halves(_dot(perm, h2_ref[...]))

    for e in range(N_EXPERTS):
        idx = i * N_EXPERTS + e
        rd = rd_ref[idx]
        wr = wr_ref[idx]
        loc = pl.multiple_of(loc_ref[idx], SEG_ALIGN)
        first = local[slot, pl.ds(loc, SEG_ALIGN), :]
        last = local[slot, pl.ds(pl.multiple_of(loc + wr, SEG_ALIGN), SEG_ALIGN), :]
        merged = jnp.where(rd > 0, first | tail[e], first)
        local[slot, pl.ds(loc, SEG_ALIGN), :] = merged
        last = jnp.where(wr == 0, merged, last)
        tail[e] = jnp.where(rd > 0, jnp.where(wr < rd, last, jnp.uint32(0)), tail[e])
    _start_segments(i, glob_ref, loc_ref, wr_ref, copy_from(slot))

    @pl.when(i == nt - 1)
    def _():
        _wait_segments(i - 1, total_ref, copy_from(1 - slot))
        _wait_segments(i, total_ref, copy_from(slot))


def _dispatch(seg_glob, seg_loc, seg_rd, seg_wr, tile_wr, h2, pos_t, n_rows):
    n = h2.shape[0]
    grid_spec = pltpu.PrefetchScalarGridSpec(
        num_scalar_prefetch=5,
        grid=(n // TT,),
        in_specs=[
            pl.BlockSpec((TT, D_MODEL), lambda i, *_: (i, 0)),
            pl.BlockSpec((1, 8, TT), lambda i, *_: (i, 0, 0)),
        ],
        out_specs=pl.BlockSpec(memory_space=pl.ANY),
        scratch_shapes=[pltpu.VMEM((2, LROWS, PACKED), U32),
                        pltpu.VMEM((N_EXPERTS, SEG_ALIGN, PACKED), U32),
                        pltpu.SemaphoreType.DMA((2,))],
    )
    return pl.pallas_call(
        _dispatch_body,
        grid_spec=grid_spec,
        out_shape=jax.ShapeDtypeStruct((n_rows, PACKED), U32),
        compiler_params=pltpu.CompilerParams(
            dimension_semantics=("arbitrary",), vmem_limit_bytes=VMEM_LIMIT),
        name="dispatch",
    )(seg_glob, seg_loc, seg_rd, seg_wr, tile_wr, h2, pos_t)


def _moe_body(te_ref, nused_ref, rows_ref, xs_ref, wg_ref, bg_ref, wu_ref, bu_ref,
              wd_ref, bd_ref, y_ref, wbf):
    i = pl.program_id(0)

    @pl.when(i < nused_ref[0])
    def _():
        @pl.when((i == 0) | (te_ref[i] != te_ref[jnp.maximum(i - 1, 0)]))
        def _():
            wbf[0] = wg_ref[0].astype(BF16)
            wbf[1] = wu_ref[0].astype(BF16)
            wbf[2] = wd_ref[0].astype(BF16)

        row = lax.broadcasted_iota(I32, (TM_MOE, 1), 0)
        x = _unpack_halves(jnp.where(row < rows_ref[i], xs_ref[...], jnp.uint32(0)))
        g = _dot(x, wbf[0]) + bg_ref[0]
        u = _dot(x, wbf[1]) + bu_ref[0]
        g = jnp.minimum(g, SWIGLU_LIMIT)
        u = jnp.clip(u, -SWIGLU_LIMIT, SWIGLU_LIMIT)
        a = g * jax.nn.sigmoid(SWIGLU_ALPHA * g) * (u + 1.0)
        y = _dot(a.astype(BF16), wbf[2]) + bd_ref[0]
        y_ref[...] = _pack_halves(y.astype(BF16).astype(F32))


def _moe(tile_expert, n_used, tile_rows, xs, wg, bg, wu, bu, wd, bd):
    n_rows = xs.shape[0]
    n_tiles = n_rows // TM_MOE
    d_ff = wg.shape[2]
    row = lambda i, te, nu, tr: (jnp.minimum(i, nu[0] - 1), 0)
    wsel = lambda i, te, nu, tr: (te[i], 0, 0)
    grid_spec = pltpu.PrefetchScalarGridSpec(
        num_scalar_prefetch=3,
        grid=(n_tiles,),
        in_specs=[
            pl.BlockSpec((TM_MOE, PACKED), row),
            pl.BlockSpec((1, D_MODEL, d_ff), wsel),
            pl.BlockSpec((1, 1, d_ff), wsel),
            pl.BlockSpec((1, D_MODEL, d_ff), wsel),
            pl.BlockSpec((1, 1, d_ff), wsel),
            pl.BlockSpec((1, d_ff, D_MODEL), wsel),
            pl.BlockSpec((1, 1, D_MODEL), wsel),
        ],
        out_specs=pl.BlockSpec((TM_MOE, PACKED), row),
        scratch_shapes=[pltpu.VMEM((3, D_MODEL, d_ff), BF16)],
    )
    assert d_ff == D_MODEL
    return pl.pallas_call(
        _moe_body,
        grid_spec=grid_spec,
        out_shape=jax.ShapeDtypeStruct((n_rows, PACKED), U32),
        compiler_params=pltpu.CompilerParams(
            dimension_semantics=("arbitrary",), vmem_limit_bytes=VMEM_LIMIT),
        name="moe",
    )(tile_expert, n_used, tile_rows, xs, wg, bg, wu, bu, wd, bd)


def _combine_body(glob_ref, loc_ref, cnt_ref, total_ref, x1_ref, pos_ref, rf_ref, gfin_ref,
                  ys_hbm, o_ref, local, sem):
    i = pl.program_id(0)
    nt = pl.num_programs(0)
    slot = i % 2

    def copy_into(s):
        def make_copy(loc_rows, glob_rows):
            return pltpu.make_async_copy(ys_hbm.at[glob_rows], local.at[s, loc_rows], sem.at[s])
        return make_copy

    @pl.when(i == 0)
    def _():
        local[...] = jnp.zeros_like(local)
        _start_segments(0, glob_ref, loc_ref, cnt_ref, copy_into(0))

    @pl.when(i + 1 < nt)
    def _():
        _start_segments(i + 1, glob_ref, loc_ref, cnt_ref, copy_into(1 - slot))
    _wait_segments(i, total_ref, copy_into(slot))

    j = lax.broadcasted_iota(I32, (TT, LROWS), 1)
    wc = jnp.zeros((TT, LROWS), F32)
    for k in range(TOP_K):
        wc = jnp.where(j == pos_ref[:, k:k + 1], rf_ref[:, k:k + 1], wc)
    acc = x1_ref[...] + _dot(wc.astype(BF16), _unpack_halves(local[slot]))
    o_ref[...] = _rms_norm(acc, gfin_ref[...])


def _combine(seg_glob, seg_loc, seg_cnt, tile_rd, x1, pos, rf, gfin, ys):
    n = x1.shape[0]
    grid_spec = pltpu.PrefetchScalarGridSpec(
        num_scalar_prefetch=4,
        grid=(n // TT,),
        in_specs=[
            pl.BlockSpec((TT, D_MODEL), lambda i, *_: (i, 0)),
            pl.BlockSpec((TT, LANES), lambda i, *_: (i, 0)),
            pl.BlockSpec((TT, LANES), lambda i, *_: (i, 0)),
            pl.BlockSpec((1, D_MODEL), lambda i, *_: (0, 0)),
            pl.BlockSpec(memory_space=pl.ANY),
        ],
        out_specs=pl.BlockSpec((TT, D_MODEL), lambda i, *_: (i, 0)),
        scratch_shapes=[pltpu.VMEM((2, LROWS, PACKED), U32),
                        pltpu.SemaphoreType.DMA((2,))],
    )
    return pl.pallas_call(
        _combine_body,
        grid_spec=grid_spec,
        out_shape=jax.ShapeDtypeStruct((n, D_MODEL), F32),
        compiler_params=pltpu.CompilerParams(
            dimension_semantics=("arbitrary",), vmem_limit_bytes=VMEM_LIMIT),
        name="combine",
    )(seg_glob, seg_loc, seg_cnt, tile_rd, x1, pos, rf, gfin, ys)


def _layer(x2, pos_b, invf, norm_mix_g, w_in, b_in, attn_sinks, w_o_attn, b_o_attn,
           w_pool_mix, pool_scale, w_pool_up, w_out, norm_ffn_g, w_router, b_router,
           w_gate, b_gate, w_up, b_up, w_down, b_down, out_g):
    n = x2.shape[0]
    q, kb, vb, u, gates = _inproj(
        x2, pos_b, invf, norm_mix_g[None, :], w_in.astype(BF16), b_in[None, :])
    attn = _attn(attn_sinks, q, kb, vb)

    wr = jnp.zeros((D_MODEL, LANES), BF16).at[:, :N_EXPERTS].set(w_router.astype(BF16))
    br = jnp.zeros((1, LANES), F32).at[0, :N_EXPERTS].set(b_router)
    x1, h2, ri, rf, tc = _post(
        attn, u, gates, x2, w_o_attn.astype(BF16), b_o_attn[None, :],
        w_pool_mix.astype(BF16), pool_scale[None, :], w_pool_up.astype(BF16),
        w_out.astype(BF16), norm_ffn_g[None, :], wr, br)

    nt = n // TT
    sub = TM_POST // TT
    seg_cnt = tc.reshape(n // TM_POST, 8, LANES)[:, :sub, :N_EXPERTS].reshape(nt, N_EXPERTS).astype(I32)
    before = jnp.cumsum(seg_cnt, axis=0) - seg_cnt
    sizes = jnp.sum(seg_cnt, axis=0)
    padded = (sizes + TM_MOE - 1) // TM_MOE * TM_MOE
    pends = jnp.cumsum(padded)
    pstarts = pends - padded
    head = before % SEG_ALIGN
    last_tile = (jnp.arange(nt, dtype=I32) == nt - 1)[:, None]
    blocks_up = (head + seg_cnt + SEG_ALIGN - 1) // SEG_ALIGN
    blocks_dn = (head + seg_cnt) // SEG_ALIGN
    seg_rd = jnp.where((seg_cnt > 0) | (last_tile & (head > 0)), blocks_up, 0) * SEG_ALIGN
    seg_wr = jnp.where(last_tile, seg_rd, jnp.where(seg_cnt > 0, blocks_dn * SEG_ALIGN, 0))
    seg_loc = jnp.cumsum(seg_rd, axis=1) - seg_rd
    seg_glob = pstarts[None, :] + before - head
    shift = jnp.repeat(seg_loc + head - before, TT, axis=0)
    sel = ri[:, :TOP_K, None] == jnp.arange(N_EXPERTS, dtype=I32)
    pos = ri[:, TOP_K:2 * TOP_K] + jnp.sum(jnp.where(sel, shift[:, None, :], 0), axis=-1)
    pos_t = jnp.pad(pos.reshape(nt, TT, TOP_K).transpose(0, 2, 1),
                    ((0, 0), (0, 8 - TOP_K), (0, 0)), constant_values=-1)
    pos_l = jnp.pad(pos, ((0, 0), (0, LANES - TOP_K)), constant_values=-1)
    tile_rd = jnp.sum(seg_rd, axis=1)
    tile_wr = jnp.sum(seg_wr, axis=1)
    seg_glob, seg_loc, seg_rd, seg_wr = (a.reshape(-1) for a in (seg_glob, seg_loc, seg_rd, seg_wr))
    m = n * TOP_K
    n_tiles = (m + N_EXPERTS * (TM_MOE - 1) + TM_MOE - 1) // TM_MOE
    tile_start = jnp.arange(n_tiles, dtype=I32) * TM_MOE
    tile_expert = jnp.minimum(
        jnp.sum(tile_start[:, None] >= pends[None, :], axis=-1), N_EXPERTS - 1).astype(I32)
    tile_rows = jnp.clip(sizes[tile_expert] - (tile_start - pstarts[tile_expert]), 0, TM_MOE)
    n_used = (pends[-1] // TM_MOE).astype(I32)[None]

    xs = _dispatch(seg_glob, seg_loc, seg_rd, seg_wr, tile_wr, h2, pos_t, n_tiles * TM_MOE)
    ys = _moe(tile_expert, n_used, tile_rows.astype(I32), xs,
              w_gate, b_gate[:, None, :], w_up, b_up[:, None, :], w_down, b_down[:, None, :])
    return _combine(seg_glob, seg_loc, seg_rd, tile_rd, x1, pos_l, rf, out_g[None, :], ys)


def kernel(x, positions, norm_mix_g, w_in, b_in, attn_sinks, w_o_attn, b_o_attn, w_pool_mix,
           pool_scale, w_pool_up, w_out, norm_ffn_g, w_router, b_router, w_gate, b_gate,
           w_up, b_up, w_down, b_down, norm_final_g):
    b, s, d = x.shape
    depth = w_in.shape[0]
    assert (s, d, depth) == (SEQ, D_MODEL, 1)
    n = b * s
    x2 = x.reshape(n, d)
    pos_b = jnp.broadcast_to(positions.reshape(n, 1), (n, LANES))
    inv_freq = ROPE_THETA ** (-jnp.arange(0, HEAD_DIM, 2, dtype=F32) / HEAD_DIM)
    invf = jnp.tile(inv_freq, LANES // (HEAD_DIM // 2))[None, :]
    out = _layer(x2, pos_b, invf, norm_mix_g[0], w_in[0], b_in[0], attn_sinks[0], w_o_attn[0],
                 b_o_attn[0], w_pool_mix[0], pool_scale[0], w_pool_up[0], w_out[0],
                 norm_ffn_g[0], w_router[0], b_router[0], w_gate[0], b_gate[0], w_up[0],
                 b_up[0], w_down[0], b_down[0], norm_final_g)
    return out.reshape(b, s, d)
```

```python
import functools

import numpy as np
import jax
import jax.numpy as jnp
from jax import lax
from jax.experimental import pallas as pl
from jax.experimental.pallas import tpu as pltpu

F32 = jnp.float32
BF16 = jnp.bfloat16
I32 = jnp.int32
U32 = jnp.uint32

D_MODEL = 1024
SEQ = 4096
HEAD_DIM = 64
N_Q_HEADS = 16
WINDOW = 128
ROPE_THETA = 10000.0
Q_WIDTH = N_Q_HEADS * HEAD_DIM
KV_WIDTH = 128
POOL_WINDOWS = (2, 4, 8, 16)
POOL_WIDTH = 512
POOL_GROUP = 128
POOL_HALO = 16
N_EXPERTS = 32
TOP_K = 4
SWIGLU_LIMIT = 7.0
SWIGLU_ALPHA = 1.702
RMS_EPS = 1e-5
NEG_BIG = -1e30

LANES = 128
TM_IN = 1024
TQ = 512
TM_POST = 512
TM_MOE = 512
TT = 256
SEG_ALIGN = 8
LROWS = TT * TOP_K + 2 * N_EXPERTS * SEG_ALIGN
PACKED = D_MODEL // 2
VMEM_LIMIT = 56 * 1024 * 1024


def _rms_norm(x, g):
    ms = jnp.mean(x * x, axis=-1, keepdims=True)
    return (x * lax.rsqrt(ms + RMS_EPS)) * g


def _dot(a, b):
    return jnp.dot(a, b, preferred_element_type=F32)


def _inproj_body(x_ref, pos_ref, invf_ref, g_ref, w_ref, b_ref,
                 q_ref, kb_ref, vb_ref, u_ref, gate_ref):
    h = _rms_norm(x_ref[...], g_ref[...]).astype(BF16)

    theta = pos_ref[...].astype(F32) * invf_ref[...]
    cos = jnp.cos(theta)
    sin = jnp.sin(theta)
    lane = lax.broadcasted_iota(I32, theta.shape, 1)
    first_half = (lane & 32) == 0
    sin_signed = jnp.where(first_half, -sin, sin)
    low_head = lane < HEAD_DIM

    def rope(t):
        swapped = jnp.where(first_half, pltpu.roll(t, 96, 1), pltpu.roll(t, 32, 1))
        return t * cos + swapped * sin_signed

    def band_layout(t, out_ref):
        tr = pltpu.roll(t, 64, 1)
        zero = jnp.zeros_like(t)
        chunks = (jnp.where(low_head, t, zero), jnp.where(low_head, zero, tr),
                  jnp.where(low_head, tr, zero), jnp.where(low_head, zero, t))
        for c, val in enumerate(chunks):
            out_ref[:, c * LANES:(c + 1) * LANES] = val.astype(BF16)

    qk_w = Q_WIDTH + KV_WIDTH
    zqk = _dot(h, w_ref[:, :qk_w]) + b_ref[:, :qk_w]
    scale = HEAD_DIM ** -0.5
    for j in range(Q_WIDTH // LANES):
        sl = slice(j * LANES, (j + 1) * LANES)
        q_ref[:, sl] = (rope(zqk[:, sl]) * scale).astype(BF16)
    band_layout(rope(zqk[:, Q_WIDTH:qk_w]), kb_ref)

    v0 = qk_w
    zv = _dot(h, w_ref[:, v0:v0 + KV_WIDTH]) + b_ref[:, v0:v0 + KV_WIDTH]
    band_layout(zv, vb_ref)

    u0 = v0 + KV_WIDTH
    u_ref[...] = _dot(h, w_ref[:, u0:u0 + POOL_WIDTH]) + b_ref[:, u0:u0 + POOL_WIDTH]

    g0 = u0 + POOL_WIDTH
    for c in range(2):
        sl = slice(g0 + c * D_MODEL, g0 + (c + 1) * D_MODEL)
        zg = _dot(h, w_ref[:, sl]) + b_ref[:, sl]
        gate_ref[:, c * D_MODEL:(c + 1) * D_MODEL] = jax.nn.sigmoid(zg).astype(BF16)


def _inproj(x2, pos_b, invf, g, w, b):
    n = x2.shape[0]
    in_width = w.shape[1]
    row = lambda i: (i, 0)
    const = lambda i: (0, 0)
    return pl.pallas_call(
        _inproj_body,
        grid=(n // TM_IN,),
        in_specs=[
            pl.BlockSpec((TM_IN, D_MODEL), row),
            pl.BlockSpec((TM_IN, LANES), row),
            pl.BlockSpec((1, LANES), const),
            pl.BlockSpec((1, D_MODEL), const),
            pl.BlockSpec((D_MODEL, in_width), const),
            pl.BlockSpec((1, in_width), const),
        ],
        out_specs=[
            pl.BlockSpec((TM_IN, Q_WIDTH), row),
            pl.BlockSpec((TM_IN, 4 * LANES), row),
            pl.BlockSpec((TM_IN, 4 * LANES), row),
            pl.BlockSpec((TM_IN, POOL_WIDTH), row),
            pl.BlockSpec((TM_IN, 2 * D_MODEL), row),
        ],
        out_shape=[
            jax.ShapeDtypeStruct((n, Q_WIDTH), BF16),
            jax.ShapeDtypeStruct((n, 4 * LANES), BF16),
            jax.ShapeDtypeStruct((n, 4 * LANES), BF16),
            jax.ShapeDtypeStruct((n, POOL_WIDTH), F32),
            jax.ShapeDtypeStruct((n, 2 * D_MODEL), BF16),
        ],
        compiler_params=pltpu.CompilerParams(
            dimension_semantics=("arbitrary",), vmem_limit_bytes=VMEM_LIMIT),
        name="inproj",
    )(x2, pos_b, invf, g, w, b)


def _attn_body(sinks_ref, q_ref, kbc_ref, kbp_ref, vbc_ref, vbp_ref, bias_ref,
               o_ref, p_scr):
    t = pl.program_id(0)
    seq_first = (t % (SEQ // TQ)) == 0
    lane = lax.broadcasted_iota(I32, (WINDOW, LANES), 1)
    low_head = lane < HEAD_DIM
    for n in range(TQ // WINDOW):
        rows = slice(n * WINDOW, (n + 1) * WINDOW)
        if n == 0:
            k_prev, v_prev = kbp_ref[...], vbp_ref[...]
            bias = jnp.where(seq_first, bias_ref[1], bias_ref[0])
        else:
            prev_rows = slice((n - 1) * WINDOW, n * WINDOW)
            k_prev, v_prev = kbc_ref[prev_rows, :], vbc_ref[prev_rows, :]
            bias = bias_ref[0]
        k_cur, v_cur = kbc_ref[rows, :], vbc_ref[rows, :]
        for g in range(2):
            def stack(prev, cur):
                lo = slice((2 * g) * LANES, (2 * g + 1) * LANES)
                hi = slice((2 * g + 1) * LANES, (2 * g + 2) * LANES)
                return jnp.concatenate([prev[:, lo], cur[:, lo], prev[:, hi], cur[:, hi]], axis=0)
            kmat = stack(k_prev, k_cur)
            vmat = stack(v_prev, v_cur)
            qs = jnp.concatenate(
                [q_ref[rows, (4 * g + p) * LANES:(4 * g + p + 1) * LANES] for p in range(4)],
                axis=0)
            s = lax.dot_general(qs, kmat, (((1,), (1,)), ((), ())),
                                preferred_element_type=F32) + bias
            inv = []
            for p in range(4):
                pr = slice(p * WINDOW, (p + 1) * WINDOW)
                inv_p = []
                for j in range(2):
                    cols = slice(j * 2 * WINDOW, (j + 1) * 2 * WINDOW)
                    sp = s[pr, cols]
                    sink = sinks_ref[8 * g + 2 * p + j]
                    m = jnp.maximum(jnp.max(sp, axis=-1, keepdims=True), sink)
                    e = jnp.exp(sp - m)
                    den = jnp.sum(e, axis=-1, keepdims=True) + jnp.exp(sink - m)
                    p_scr[pr, cols] = e.astype(BF16)
                    inv_p.append(1.0 / den)
                inv.append(inv_p)
            o = _dot(p_scr[...], vmat)
            for p in range(4):
                pr = slice(p * WINDOW, (p + 1) * WINDOW)
                norm = jnp.where(low_head, inv[p][0], inv[p][1])
                o_ref[rows, (4 * g + p) * LANES:(4 * g + p + 1) * LANES] = (o[pr, :] * norm).astype(BF16)


def _attn_bias():
    r = np.arange(4 * WINDOW)[:, None] % WINDOW
    c = np.arange(4 * WINDOW)[None, :] % (2 * WINDOW)
    band = (c > r) & (c <= r + WINDOW)
    first = band & (c >= WINDOW)
    return np.stack([np.where(band, 0.0, NEG_BIG), np.where(first, 0.0, NEG_BIG)]).astype(np.float32)


def _attn(sinks, q, kb, vb):
    n = q.shape[0]
    blocks_per_tile = TQ // WINDOW
    cur = lambda t: (t, 0)
    prev = lambda t: (jnp.maximum(t * blocks_per_tile - 1, 0), 0)
    bias = jnp.asarray(_attn_bias())
    return pl.pallas_call(
        _attn_body,
        grid=(n // TQ,),
        in_specs=[
            pl.BlockSpec(memory_space=pltpu.SMEM),
            pl.BlockSpec((TQ, Q_WIDTH), cur),
            pl.BlockSpec((TQ, 4 * LANES), cur),
            pl.BlockSpec((WINDOW, 4 * LANES), prev),
            pl.BlockSpec((TQ, 4 * LANES), cur),
            pl.BlockSpec((WINDOW, 4 * LANES), prev),
            pl.BlockSpec((2, 4 * WINDOW, 4 * WINDOW), lambda t: (0, 0, 0)),
        ],
        out_specs=pl.BlockSpec((TQ, Q_WIDTH), cur),
        out_shape=jax.ShapeDtypeStruct((n, Q_WIDTH), BF16),
        scratch_shapes=[pltpu.VMEM((4 * WINDOW, 4 * WINDOW), BF16)],
        compiler_params=pltpu.CompilerParams(
            dimension_semantics=("arbitrary",), vmem_limit_bytes=VMEM_LIMIT),
        name="attn",
    )(sinks, q, kb, kb, vb, vb, bias)


def _post_body(attn_ref, u_ref, uprev_ref, gate_ref, x_ref,
               wo_ref, bo_ref, wmix_ref, pscale_ref, wup_ref, wout_ref,
               gffn_ref, wrt_ref, brt_ref,
               x1_ref, h2_ref, pos_t_ref, pw_ref, tc_ref,
               ubuf, carry):
    i = pl.program_id(0)
    tiles_per_seq = SEQ // TM_POST
    seq_first = (i % tiles_per_seq) == 0

    @pl.when(i == 0)
    def _():
        carry[...] = jnp.zeros_like(carry)

    ubuf[0:POOL_HALO, :] = jnp.where(seq_first, 0.0, uprev_ref[...])
    ubuf[POOL_HALO:, :] = u_ref[...]
    row = lax.broadcasted_iota(I32, (TM_POST, 1), 0)
    tpos = (i % tiles_per_seq) * TM_POST + row
    mixed = []
    for gi, w in enumerate(POOL_WINDOWS):
        cols = slice(gi * POOL_GROUP, (gi + 1) * POOL_GROUP)
        acc = ubuf[POOL_HALO:, cols]
        for d in range(1, w):
            acc = acc + ubuf[POOL_HALO - d:POOL_HALO - d + TM_POST, cols]
        cnt = jnp.minimum(tpos + 1, w).astype(F32)
        pooled = acc / cnt - ubuf[POOL_HALO:, cols]
        mixed.append(_dot(pooled.astype(BF16), wmix_ref[gi]) * pscale_ref[:, cols])
    mixed = jnp.concatenate(mixed, axis=-1).astype(BF16)
    y_pool = _dot(mixed, wup_ref[...])
    y_attn = _dot(attn_ref[...], wo_ref[...]) + bo_ref[...]
    merged = (gate_ref[:, :D_MODEL].astype(F32) * y_attn
              + gate_ref[:, D_MODEL:].astype(F32) * y_pool)
    x1 = x_ref[...] + _dot(merged.astype(BF16), wout_ref[...])
    x1_ref[...] = x1
    h2 = _rms_norm(x1, gffn_ref[...])
    h2_bf = h2.astype(BF16)
    h2_ref[...] = h2_bf

    logits_t = lax.dot_general(wrt_ref[...], h2_bf, (((1,), (1,)), ((), ())),
                               preferred_element_type=F32) + brt_ref[...]
    erow = lax.broadcasted_iota(I32, (N_EXPERTS, TM_POST), 0)
    work = logits_t
    top_v, onehots = [], []
    for _ in range(TOP_K):
        m = jnp.max(work, axis=0, keepdims=True)
        idx = jnp.min(jnp.where(work == m, erow, N_EXPERTS), axis=0, keepdims=True)
        hit = erow == idx
        top_v.append(m)
        onehots.append(hit)
        work = jnp.where(hit, -jnp.inf, work)
    exps = [jnp.exp(v - top_v[0]) for v in top_v]
    denom = exps[0] + exps[1] + exps[2] + exps[3]
    comb = [e / denom for e in exps]
    chosen = jnp.zeros((N_EXPERTS, TM_POST), F32)
    for hit in onehots:
        chosen = chosen + hit.astype(F32)

    e_r = lax.broadcasted_iota(I32, (N_EXPERTS, N_EXPERTS), 0)
    e_c = lax.broadcasted_iota(I32, (N_EXPERTS, N_EXPERTS), 1)
    earlier_expert = (e_c < e_r).astype(BF16)
    t_r = lax.broadcasted_iota(I32, (TT, TT), 0)
    t_c = lax.broadcasted_iota(I32, (TT, TT), 1)
    earlier_token = (t_r < t_c).astype(BF16)
    lane = lax.broadcasted_iota(I32, (N_EXPERTS, LANES), 1)
    counts = jnp.zeros((N_EXPERTS, LANES), F32)
    pos_parts = [[] for _ in range(TOP_K)]
    n_sub = TM_POST // TT
    for s in range(n_sub):
        cols = slice(s * TT, (s + 1) * TT)
        ch = chosen[:, cols]
        cnt = jnp.broadcast_to(jnp.sum(ch, axis=1, keepdims=True), (N_EXPERTS, LANES))
        before = carry[...]
        head = before - SEG_ALIGN * jnp.floor(before / SEG_ALIGN)
        flush = jnp.logical_and(i == pl.num_programs(0) - 1, s == n_sub - 1)
        present = (cnt > 0) | (flush & (head > 0))
        blocks = jnp.where(present, jnp.floor((head + cnt + (SEG_ALIGN - 1)) / SEG_ALIGN), 0.0)
        seg_loc = SEG_ALIGN * _dot(earlier_expert, blocks.astype(BF16))
        base = seg_loc + head
        rank_in_tile = _dot(ch.astype(BF16), earlier_token)
        full = jnp.concatenate([base] * (TT // LANES), axis=1) + rank_in_tile
        for k in range(TOP_K):
            pos_parts[k].append(
                jnp.sum(jnp.where(onehots[k][:, cols], full, 0.0), axis=0, keepdims=True))
        counts = jnp.where(lane == s, cnt, counts)
        carry[...] = before + cnt
    tc_ref[...] = counts

    pos = [jnp.concatenate(parts, axis=1) for parts in pos_parts]
    row8 = lax.broadcasted_iota(I32, (8, TM_POST), 0)
    pos_rows = jnp.full((8, TM_POST), -1.0, F32)
    both = jnp.zeros((8, TM_POST), F32)
    for k in range(TOP_K):
        pos_rows = jnp.where(row8 == k, pos[k], pos_rows)
        both = jnp.where(row8 == k, pos[k], both)
        both = jnp.where(row8 == TOP_K + k, comb[k], both)
    pos_t_ref[...] = pos_rows.astype(I32)
    padded = jnp.concatenate([both, jnp.zeros((LANES - 8, TM_POST), F32)], axis=0)
    pw_ref[...] = padded.T


def _post(attn, u, gates, x2, wo, bo, wmix, pscale, wup, wout, gffn, wr, br):
    n = x2.shape[0]
    row = lambda i: (i, 0)
    const = lambda i: (0, 0)
    halo_blocks = TM_POST // POOL_HALO
    prev = lambda i: (jnp.maximum(i * halo_blocks - 1, 0), 0)
    return pl.pallas_call(
        _post_body,
        grid=(n // TM_POST,),
        in_specs=[
            pl.BlockSpec((TM_POST, Q_WIDTH), row),
            pl.BlockSpec((TM_POST, POOL_WIDTH), row),
            pl.BlockSpec((POOL_HALO, POOL_WIDTH), prev),
            pl.BlockSpec((TM_POST, 2 * D_MODEL), row),
            pl.BlockSpec((TM_POST, D_MODEL), row),
            pl.BlockSpec((Q_WIDTH, D_MODEL), const),
            pl.BlockSpec((1, D_MODEL), const),
            pl.BlockSpec((len(POOL_WINDOWS), POOL_GROUP, POOL_GROUP), lambda i: (0, 0, 0)),
            pl.BlockSpec((1, POOL_WIDTH), const),
            pl.BlockSpec((POOL_WIDTH, D_MODEL), const),
            pl.BlockSpec((D_MODEL, D_MODEL), const),
            pl.BlockSpec((1, D_MODEL), const),
            pl.BlockSpec((N_EXPERTS, D_MODEL), const),
            pl.BlockSpec((N_EXPERTS, TM_POST), const),
        ],
        out_specs=[
            pl.BlockSpec((TM_POST, D_MODEL), row),
            pl.BlockSpec((TM_POST, D_MODEL), row),
            pl.BlockSpec((8, TM_POST), lambda i: (0, i)),
            pl.BlockSpec((TM_POST, LANES), row),
            pl.BlockSpec((N_EXPERTS, LANES), row),
        ],
        out_shape=[
            jax.ShapeDtypeStruct((n, D_MODEL), F32),
            jax.ShapeDtypeStruct((n, D_MODEL), BF16),
            jax.ShapeDtypeStruct((8, n), I32),
            jax.ShapeDtypeStruct((n, LANES), F32),
            jax.ShapeDtypeStruct((n // TM_POST * N_EXPERTS, LANES), F32),
        ],
        scratch_shapes=[pltpu.VMEM((TM_POST + POOL_HALO, POOL_WIDTH), F32),
                        pltpu.VMEM((N_EXPERTS, LANES), F32)],
        compiler_params=pltpu.CompilerParams(
            dimension_semantics=("arbitrary",), vmem_limit_bytes=VMEM_LIMIT),
        name="post",
    )(attn, u, u, gates, x2, wo, bo, wmix, pscale, wup, wout, gffn, wr, br)


def _start_segments(i, glob_ref, loc_ref, cnt_ref, make_copy):
    def body(e, c):
        idx = i * N_EXPERTS + e
        rows = pl.multiple_of(cnt_ref[idx], SEG_ALIGN)

        @pl.when(rows > 0)
        def _():
            loc = pl.multiple_of(loc_ref[idx], SEG_ALIGN)
            glob = pl.multiple_of(glob_ref[idx], SEG_ALIGN)
            make_copy(pl.ds(loc, rows), pl.ds(glob, rows)).start()
        return c
    lax.fori_loop(0, N_EXPERTS, body, 0, unroll=4)


def _wait_segments(i, total_ref, make_copy):
    rows = pl.multiple_of(total_ref[i], SEG_ALIGN)

    @pl.when(rows > 0)
    def _():
        make_copy(pl.ds(0, rows), pl.ds(0, rows)).wait()


def _pack_halves(v):
    c = v.shape[1] // 2
    bits = lax.bitcast_convert_type(v, U32)
    return (bits[:, :c] >> 16) | bits[:, c:]


def _unpack_halves(w):
    lo = lax.bitcast_convert_type(w << 16, F32)
    hi = lax.bitcast_convert_type(w & jnp.uint32(0xFFFF0000), F32)
    return jnp.concatenate([lo, hi], axis=1).astype(BF16)


def _dispatch_body(glob_ref, loc_ref, rd_ref, wr_ref, total_ref, h2_ref, pos_ref, xs_hbm,
                   local, tail, sem):
    i = pl.program_id(0)
    nt = pl.num_programs(0)
    slot = i % 2

    def copy_from(s):
        def make_copy(loc_rows, glob_rows):
            return pltpu.make_async_copy(local.at[s, loc_rows], xs_hbm.at[glob_rows], sem.at[s])
        return make_copy

    @pl.when(i == 0)
    def _():
        tail[...] = jnp.zeros_like(tail)

    @pl.when(i >= 2)
    def _():
        _wait_segments(i - 2, total_ref, copy_from(slot))

    j = lax.broadcasted_iota(I32, (LROWS, TT), 0)
    hit = j == pos_ref[0:1, :]
    for k in range(1, TOP_K):
        hit = hit | (j == pos_ref[k:k + 1, :])
    perm = jnp.where(hit, 1.0, 0.0).astype(BF16)
    local[slot] = _pack_halves(_dot(perm, h2_ref[...]))

    for e in range(N_EXPERTS):
        idx = i * N_EXPERTS + e
        rd = rd_ref[idx]
        wr = wr_ref[idx]
        loc = pl.multiple_of(loc_ref[idx], SEG_ALIGN)
        first = local[slot, pl.ds(loc, SEG_ALIGN), :]
        last = local[slot, pl.ds(pl.multiple_of(loc + wr, SEG_ALIGN), SEG_ALIGN), :]
        merged = jnp.where(rd > 0, first | tail[e], first)
        local[slot, pl.ds(loc, SEG_ALIGN), :] = merged
        last = jnp.where(wr == 0, merged, last)
        tail[e] = jnp.where(rd > 0, jnp.where(wr < rd, last, jnp.uint32(0)), tail[e])
    _start_segments(i, glob_ref, loc_ref, wr_ref, copy_from(slot))

    @pl.when(i == nt - 1)
    def _():
        _wait_segments(i - 1, total_ref, copy_from(1 - slot))
        _wait_segments(i, total_ref, copy_from(slot))


def _dispatch(seg_glob, seg_loc, seg_rd, seg_wr, tile_wr, h2, pos_t, n_rows):
    n = h2.shape[0]
    grid_spec = pltpu.PrefetchScalarGridSpec(
        num_scalar_prefetch=5,
        grid=(n // TT,),
        in_specs=[
            pl.BlockSpec((TT, D_MODEL), lambda i, *_: (i, 0)),
            pl.BlockSpec((8, TT), lambda i, *_: (0, i)),
        ],
        out_specs=pl.BlockSpec(memory_space=pl.ANY),
        scratch_shapes=[pltpu.VMEM((2, LROWS, PACKED), U32),
                        pltpu.VMEM((N_EXPERTS, SEG_ALIGN, PACKED), U32),
                        pltpu.SemaphoreType.DMA((2,))],
    )
    return pl.pallas_call(
        _dispatch_body,
        grid_spec=grid_spec,
        out_shape=jax.ShapeDtypeStruct((n_rows, PACKED), U32),
        compiler_params=pltpu.CompilerParams(
            dimension_semantics=("arbitrary",), vmem_limit_bytes=VMEM_LIMIT),
        name="dispatch",
    )(seg_glob, seg_loc, seg_rd, seg_wr, tile_wr, h2, pos_t)


def _moe_body(te_ref, nused_ref, rows_ref, xs_ref, wg_ref, bg_ref, wu_ref, bu_ref,
              wd_ref, bd_ref, y_ref, wbf):
    i = pl.program_id(0)

    @pl.when(i < nused_ref[0])
    def _():
        @pl.when((i == 0) | (te_ref[i] != te_ref[jnp.maximum(i - 1, 0)]))
        def _():
            wbf[0] = wg_ref[0].astype(BF16)
            wbf[1] = wu_ref[0].astype(BF16)
            wbf[2] = wd_ref[0].astype(BF16)

        row = lax.broadcasted_iota(I32, (TM_MOE, 1), 0)
        x = _unpack_halves(jnp.where(row < rows_ref[i], xs_ref[...], jnp.uint32(0)))
        g = _dot(x, wbf[0]) + bg_ref[0]
        u = _dot(x, wbf[1]) + bu_ref[0]
        g = jnp.minimum(g, SWIGLU_LIMIT)
        u = jnp.clip(u, -SWIGLU_LIMIT, SWIGLU_LIMIT)
        a = g * jax.nn.sigmoid(SWIGLU_ALPHA * g) * (u + 1.0)
        y = _dot(a.astype(BF16), wbf[2]) + bd_ref[0]
        y_ref[...] = _pack_halves(y.astype(BF16).astype(F32))


def _moe(tile_expert, n_used, tile_rows, xs, wg, bg, wu, bu, wd, bd):
    n_rows = xs.shape[0]
    n_tiles = n_rows // TM_MOE
    d_ff = wg.shape[2]
    row = lambda i, te, nu, tr: (jnp.minimum(i, nu[0] - 1), 0)
    wsel = lambda i, te, nu, tr: (te[i], 0, 0)
    grid_spec = pltpu.PrefetchScalarGridSpec(
        num_scalar_prefetch=3,
        grid=(n_tiles,),
        in_specs=[
            pl.BlockSpec((TM_MOE, PACKED), row),
            pl.BlockSpec((1, D_MODEL, d_ff), wsel),
            pl.BlockSpec((1, 1, d_ff), wsel),
            pl.BlockSpec((1, D_MODEL, d_ff), wsel),
            pl.BlockSpec((1, 1, d_ff), wsel),
            pl.BlockSpec((1, d_ff, D_MODEL), wsel),
            pl.BlockSpec((1, 1, D_MODEL), wsel),
        ],
        out_specs=pl.BlockSpec((TM_MOE, PACKED), row),
        scratch_shapes=[pltpu.VMEM((3, D_MODEL, d_ff), BF16)],
    )
    assert d_ff == D_MODEL
    return pl.pallas_call(
        _moe_body,
        grid_spec=grid_spec,
        out_shape=jax.ShapeDtypeStruct((n_rows, PACKED), U32),
        compiler_params=pltpu.CompilerParams(
            dimension_semantics=("arbitrary",), vmem_limit_bytes=VMEM_LIMIT),
        name="moe",
    )(tile_expert, n_used, tile_rows, xs, wg, bg, wu, bu, wd, bd)


def _combine_body(glob_ref, loc_ref, cnt_ref, total_ref, x1_ref, pw_ref, gfin_ref,
                  ys_hbm, o_ref, local, sem):
    i = pl.program_id(0)
    nt = pl.num_programs(0)
    slot = i % 2

    def copy_into(s):
        def make_copy(loc_rows, glob_rows):
            return pltpu.make_async_copy(ys_hbm.at[glob_rows], local.at[s, loc_rows], sem.at[s])
        return make_copy

    @pl.when(i == 0)
    def _():
        local[...] = jnp.zeros_like(local)
        _start_segments(0, glob_ref, loc_ref, cnt_ref, copy_into(0))

    @pl.when(i + 1 < nt)
    def _():
        _start_segments(i + 1, glob_ref, loc_ref, cnt_ref, copy_into(1 - slot))
    _wait_segments(i, total_ref, copy_into(slot))

    j = lax.broadcasted_iota(I32, (TT, LROWS), 1).astype(F32)
    wc = jnp.zeros((TT, LROWS), F32)
    for k in range(TOP_K):
        wc = jnp.where(j == pw_ref[:, k:k + 1], pw_ref[:, TOP_K + k:TOP_K + k + 1], wc)
    acc = x1_ref[...] + _dot(wc.astype(BF16), _unpack_halves(local[slot]))
    o_ref[...] = _rms_norm(acc, gfin_ref[...])


def _combine(seg_glob, seg_loc, seg_cnt, tile_rd, x1, pw, gfin, ys):
    n = x1.shape[0]
    grid_spec = pltpu.PrefetchScalarGridSpec(
        num_scalar_prefetch=4,
        grid=(n // TT,),
        in_specs=[
            pl.BlockSpec((TT, D_MODEL), lambda i, *_: (i, 0)),
            pl.BlockSpec((TT, LANES), lambda i, *_: (i, 0)),
            pl.BlockSpec((1, D_MODEL), lambda i, *_: (0, 0)),
            pl.BlockSpec(memory_space=pl.ANY),
        ],
        out_specs=pl.BlockSpec((TT, D_MODEL), lambda i, *_: (i, 0)),
        scratch_shapes=[pltpu.VMEM((2, LROWS, PACKED), U32),
                        pltpu.SemaphoreType.DMA((2,))],
    )
    return pl.pallas_call(
        _combine_body,
        grid_spec=grid_spec,
        out_shape=jax.ShapeDtypeStruct((n, D_MODEL), F32),
        compiler_params=pltpu.CompilerParams(
            dimension_semantics=("arbitrary",), vmem_limit_bytes=VMEM_LIMIT),
        name="combine",
    )(seg_glob, seg_loc, seg_cnt, tile_rd, x1, pw, gfin, ys)


def _layer(x2, pos_b, invf, norm_mix_g, w_in, b_in, attn_sinks, w_o_attn, b_o_attn,
           w_pool_mix, pool_scale, w_pool_up, w_out, norm_ffn_g, w_router, b_router,
           w_gate, b_gate, w_up, b_up, w_down, b_down, out_g):
    n = x2.shape[0]
    q, kb, vb, u, gates = _inproj(
        x2, pos_b, invf, norm_mix_g[None, :], w_in.astype(BF16), b_in[None, :])
    attn = _attn(attn_sinks, q, kb, vb)

    wrt = w_router.T.astype(BF16)
    brt = jnp.broadcast_to(b_router[:, None], (N_EXPERTS, TM_POST))
    x1, h2, pos_t, pw, tc = _post(
        attn, u, gates, x2, w_o_attn.astype(BF16), b_o_attn[None, :],
        w_pool_mix.astype(BF16), pool_scale[None, :], w_pool_up.astype(BF16),
        w_out.astype(BF16), norm_ffn_g[None, :], wrt, brt)

    nt = n // TT
    sub = TM_POST // TT
    seg_cnt = (tc.reshape(n // TM_POST, N_EXPERTS, LANES)[:, :, :sub]
               .transpose(0, 2, 1).reshape(nt, N_EXPERTS).astype(I32))
    before = jnp.cumsum(seg_cnt, axis=0) - seg_cnt
    sizes = jnp.sum(seg_cnt, axis=0)
    padded = (sizes + TM_MOE - 1) // TM_MOE * TM_MOE
    pends = jnp.cumsum(padded)
    pstarts = pends - padded
    head = before % SEG_ALIGN
    last_tile = (jnp.arange(nt, dtype=I32) == nt - 1)[:, None]
    blocks_up = (head + seg_cnt + SEG_ALIGN - 1) // SEG_ALIGN
    blocks_dn = (head + seg_cnt) // SEG_ALIGN
    seg_rd = jnp.where((seg_cnt > 0) | (last_tile & (head > 0)), blocks_up, 0) * SEG_ALIGN
    seg_wr = jnp.where(last_tile, seg_rd, jnp.where(seg_cnt > 0, blocks_dn * SEG_ALIGN, 0))
    seg_loc = jnp.cumsum(seg_rd, axis=1) - seg_rd
    seg_glob = pstarts[None, :] + before - head
    tile_rd = jnp.sum(seg_rd, axis=1)
    tile_wr = jnp.sum(seg_wr, axis=1)
    seg_glob, seg_loc, seg_rd, seg_wr = (a.reshape(-1) for a in (seg_glob, seg_loc, seg_rd, seg_wr))
    m = n * TOP_K
    n_tiles = (m + N_EXPERTS * (TM_MOE - 1) + TM_MOE - 1) // TM_MOE
    tile_start = jnp.arange(n_tiles, dtype=I32) * TM_MOE
    tile_expert = jnp.minimum(
        jnp.sum(tile_start[:, None] >= pends[None, :], axis=-1), N_EXPERTS - 1).astype(I32)
    tile_rows = jnp.clip(sizes[tile_expert] - (tile_start - pstarts[tile_expert]), 0, TM_MOE)
    n_used = (pends[-1] // TM_MOE).astype(I32)[None]

    xs = _dispatch(seg_glob, seg_loc, seg_rd, seg_wr, tile_wr, h2, pos_t, n_tiles * TM_MOE)
    ys = _moe(tile_expert, n_used, tile_rows.astype(I32), xs,
              w_gate, b_gate[:, None, :], w_up, b_up[:, None, :], w_down, b_down[:, None, :])
    return _combine(seg_glob, seg_loc, seg_rd, tile_rd, x1, pw, out_g[None, :], ys)


def kernel(x, positions, norm_mix_g, w_in, b_in, attn_sinks, w_o_attn, b_o_attn, w_pool_mix,
           pool_scale, w_pool_up, w_out, norm_ffn_g, w_router, b_router, w_gate, b_gate,
           w_up, b_up, w_down, b_down, norm_final_g):
    b, s, d = x.shape
    depth = w_in.shape[0]
    assert (s, d, depth) == (SEQ, D_MODEL, 1)
    n = b * s
    x2 = x.reshape(n, d)
    pos_b = jnp.broadcast_to(positions.reshape(n, 1), (n, LANES))
    inv_freq = ROPE_THETA ** (-jnp.arange(0, HEAD_DIM, 2, dtype=F32) / HEAD_DIM)
    invf = jnp.tile(inv_freq, LANES // (HEAD_DIM // 2))[None, :]
    out = _layer(x2, pos_b, invf, norm_mix_g[0], w_in[0], b_in[0], attn_sinks[0], w_o_attn[0],
                 b_o_attn[0], w_pool_mix[0], pool_scale[0], w_pool_up[0], w_out[0],
                 norm_ffn_g[0], w_router[0], b_router[0], w_gate[0], b_gate[0], w_up[0],
                 b_up[0], w_down[0], b_down[0], norm_final_g)
    return out.reshape(b, s, d)
```

```python
import functools

import numpy as np
import jax
import jax.numpy as jnp
from jax import lax
from jax.experimental import pallas as pl
from jax.experimental.pallas import tpu as pltpu

F32 = jnp.float32
BF16 = jnp.bfloat16
I32 = jnp.int32
U32 = jnp.uint32

D_MODEL = 1024
SEQ = 4096
HEAD_DIM = 64
N_Q_HEADS = 16
WINDOW = 128
ROPE_THETA = 10000.0
Q_WIDTH = N_Q_HEADS * HEAD_DIM
KV_WIDTH = 128
POOL_WINDOWS = (2, 4, 8, 16)
POOL_WIDTH = 512
POOL_GROUP = 128
POOL_HALO = 16
N_EXPERTS = 32
TOP_K = 4
SWIGLU_LIMIT = 7.0
SWIGLU_ALPHA = 1.702
RMS_EPS = 1e-5
NEG_BIG = -1e30

LANES = 128
TM_IN = 1024
TQ = 512
TM_POST = 512
TM_MOE = 512
MOE_CHUNK = 256
TT = 256
SEG_ALIGN = 8
LROWS = TT * TOP_K + 2 * N_EXPERTS * SEG_ALIGN
PACKED = D_MODEL // 2
VMEM_LIMIT = 56 * 1024 * 1024


def _rms_norm(x, g):
    ms = jnp.mean(x * x, axis=-1, keepdims=True)
    return (x * lax.rsqrt(ms + RMS_EPS)) * g


def _dot(a, b):
    return jnp.dot(a, b, preferred_element_type=F32)


def _inproj_body(x_ref, pos_ref, invf_ref, g_ref, w_ref, b_ref,
                 q_ref, kb_ref, vb_ref, u_ref, gate_ref, cos_scr, sin_scr):
    h = _rms_norm(x_ref[...], g_ref[...]).astype(BF16)

    theta = pos_ref[...].astype(F32) * invf_ref[...]
    lane4 = lax.broadcasted_iota(I32, theta.shape, 1)
    for packed, scr in ((jnp.cos(theta), cos_scr), (jnp.sin(theta), sin_scr)):
        for jt in range(4):
            seg = packed if jt == 0 else pltpu.roll(packed, LANES - 32 * jt, 1)
            seg = jnp.where(lane4 < 32, seg, pltpu.roll(seg, 32, 1))
            seg = jnp.where(lane4 < 64, seg, pltpu.roll(seg, 64, 1))
            scr[pl.ds(jt, TM_IN // 4, stride=4), :] = seg
    cos = cos_scr[...]
    sin = sin_scr[...]
    lane = lax.broadcasted_iota(I32, cos.shape, 1)
    first_half = (lane & 32) == 0
    sin_signed = jnp.where(first_half, -sin, sin)
    low_head = lane < HEAD_DIM

    def rope(t):
        swapped = jnp.where(first_half, pltpu.roll(t, 96, 1), pltpu.roll(t, 32, 1))
        return t * cos + swapped * sin_signed

    def band_layout(t, out_ref):
        tr = pltpu.roll(t, 64, 1)
        zero = jnp.zeros_like(t)
        chunks = (jnp.where(low_head, t, zero), jnp.where(low_head, zero, tr),
                  jnp.where(low_head, tr, zero), jnp.where(low_head, zero, t))
        for c, val in enumerate(chunks):
            out_ref[:, c * LANES:(c + 1) * LANES] = val.astype(BF16)

    qk_w = Q_WIDTH + KV_WIDTH
    zqk = _dot(h, w_ref[:, :qk_w]) + b_ref[:, :qk_w]
    scale = HEAD_DIM ** -0.5
    for j in range(Q_WIDTH // LANES):
        sl = slice(j * LANES, (j + 1) * LANES)
        q_ref[:, sl] = (rope(zqk[:, sl]) * scale).astype(BF16)
    band_layout(rope(zqk[:, Q_WIDTH:qk_w]), kb_ref)

    v0 = qk_w
    zv = _dot(h, w_ref[:, v0:v0 + KV_WIDTH]) + b_ref[:, v0:v0 + KV_WIDTH]
    band_layout(zv, vb_ref)

    u0 = v0 + KV_WIDTH
    u_ref[...] = _dot(h, w_ref[:, u0:u0 + POOL_WIDTH]) + b_ref[:, u0:u0 + POOL_WIDTH]

    g0 = u0 + POOL_WIDTH
    for c in range(2):
        sl = slice(g0 + c * D_MODEL, g0 + (c + 1) * D_MODEL)
        zg = _dot(h, w_ref[:, sl]) + b_ref[:, sl]
        gate_ref[:, c * D_MODEL:(c + 1) * D_MODEL] = jax.nn.sigmoid(zg).astype(BF16)


def _inproj(x2, pos_b, invf, g, w, b):
    n = x2.shape[0]
    in_width = w.shape[1]
    row = lambda i: (i, 0)
    const = lambda i: (0, 0)
    return pl.pallas_call(
        _inproj_body,
        grid=(n // TM_IN,),
        in_specs=[
            pl.BlockSpec((TM_IN, D_MODEL), row),
            pl.BlockSpec((TM_IN // 4, LANES), row),
            pl.BlockSpec((1, LANES), const),
            pl.BlockSpec((1, D_MODEL), const),
            pl.BlockSpec((D_MODEL, in_width), const),
            pl.BlockSpec((1, in_width), const),
        ],
        out_specs=[
            pl.BlockSpec((TM_IN, Q_WIDTH), row),
            pl.BlockSpec((TM_IN, 4 * LANES), row),
            pl.BlockSpec((TM_IN, 4 * LANES), row),
            pl.BlockSpec((TM_IN, POOL_WIDTH), row),
            pl.BlockSpec((TM_IN, 2 * D_MODEL), row),
        ],
        out_shape=[
            jax.ShapeDtypeStruct((n, Q_WIDTH), BF16),
            jax.ShapeDtypeStruct((n, 4 * LANES), BF16),
            jax.ShapeDtypeStruct((n, 4 * LANES), BF16),
            jax.ShapeDtypeStruct((n, POOL_WIDTH), F32),
            jax.ShapeDtypeStruct((n, 2 * D_MODEL), BF16),
        ],
        scratch_shapes=[pltpu.VMEM((TM_IN, LANES), F32), pltpu.VMEM((TM_IN, LANES), F32)],
        compiler_params=pltpu.CompilerParams(
            dimension_semantics=("arbitrary",), vmem_limit_bytes=VMEM_LIMIT),
        name="inproj",
    )(x2, pos_b, invf, g, w, b)


def _attn_body(sinks_ref, q_ref, kbc_ref, kbp_ref, vbc_ref, vbp_ref, bias_ref,
               o_ref, p_scr):
    t = pl.program_id(0)
    seq_first = (t % (SEQ // TQ)) == 0
    lane = lax.broadcasted_iota(I32, (WINDOW, LANES), 1)
    low_head = lane < HEAD_DIM
    for n in range(TQ // WINDOW):
        rows = slice(n * WINDOW, (n + 1) * WINDOW)
        if n == 0:
            k_prev, v_prev = kbp_ref[...], vbp_ref[...]
            bias = jnp.where(seq_first, bias_ref[1], bias_ref[0])
        else:
            prev_rows = slice((n - 1) * WINDOW, n * WINDOW)
            k_prev, v_prev = kbc_ref[prev_rows, :], vbc_ref[prev_rows, :]
            bias = bias_ref[0]
        k_cur, v_cur = kbc_ref[rows, :], vbc_ref[rows, :]
        for g in range(2):
            def stack(prev, cur):
                lo = slice((2 * g) * LANES, (2 * g + 1) * LANES)
                hi = slice((2 * g + 1) * LANES, (2 * g + 2) * LANES)
                return jnp.concatenate([prev[:, lo], cur[:, lo], prev[:, hi], cur[:, hi]], axis=0)
            kmat = stack(k_prev, k_cur)
            vmat = stack(v_prev, v_cur)
            qs = jnp.concatenate(
                [q_ref[rows, (4 * g + p) * LANES:(4 * g + p + 1) * LANES] for p in range(4)],
                axis=0)
            s = lax.dot_general(qs, kmat, (((1,), (1,)), ((), ())),
                                preferred_element_type=F32) + bias
            inv = []
            for p in range(4):
                pr = slice(p * WINDOW, (p + 1) * WINDOW)
                inv_p = []
                for j in range(2):
                    cols = slice(j * 2 * WINDOW, (j + 1) * 2 * WINDOW)
                    sp = s[pr, cols]
                    sink = sinks_ref[8 * g + 2 * p + j]
                    m = jnp.maximum(jnp.max(sp, axis=-1, keepdims=True), sink)
                    e = jnp.exp(sp - m)
                    den = jnp.sum(e, axis=-1, keepdims=True) + jnp.exp(sink - m)
                    p_scr[pr, cols] = e.astype(BF16)
                    inv_p.append(1.0 / den)
                inv.append(inv_p)
            o = _dot(p_scr[...], vmat)
            for p in range(4):
                pr = slice(p * WINDOW, (p + 1) * WINDOW)
                norm = jnp.where(low_head, inv[p][0], inv[p][1])
                o_ref[rows, (4 * g + p) * LANES:(4 * g + p + 1) * LANES] = (o[pr, :] * norm).astype(BF16)


def _attn_bias():
    r = np.arange(4 * WINDOW)[:, None] % WINDOW
    c = np.arange(4 * WINDOW)[None, :] % (2 * WINDOW)
    band = (c > r) & (c <= r + WINDOW)
    first = band & (c >= WINDOW)
    return np.stack([np.where(band, 0.0, NEG_BIG), np.where(first, 0.0, NEG_BIG)]).astype(np.float32)


def _attn(sinks, q, kb, vb):
    n = q.shape[0]
    blocks_per_tile = TQ // WINDOW
    cur = lambda t: (t, 0)
    prev = lambda t: (jnp.maximum(t * blocks_per_tile - 1, 0), 0)
    bias = jnp.asarray(_attn_bias())
    return pl.pallas_call(
        _attn_body,
        grid=(n // TQ,),
        in_specs=[
            pl.BlockSpec(memory_space=pltpu.SMEM),
            pl.BlockSpec((TQ, Q_WIDTH), cur),
            pl.BlockSpec((TQ, 4 * LANES), cur),
            pl.BlockSpec((WINDOW, 4 * LANES), prev),
            pl.BlockSpec((TQ, 4 * LANES), cur),
            pl.BlockSpec((WINDOW, 4 * LANES), prev),
            pl.BlockSpec((2, 4 * WINDOW, 4 * WINDOW), lambda t: (0, 0, 0)),
        ],
        out_specs=pl.BlockSpec((TQ, Q_WIDTH), cur),
        out_shape=jax.ShapeDtypeStruct((n, Q_WIDTH), BF16),
        scratch_shapes=[pltpu.VMEM((4 * WINDOW, 4 * WINDOW), BF16)],
        compiler_params=pltpu.CompilerParams(
            dimension_semantics=("arbitrary",), vmem_limit_bytes=VMEM_LIMIT),
        name="attn",
    )(sinks, q, kb, kb, vb, vb, bias)


def _post_body(attn_ref, u_ref, uprev_ref, gate_ref, x_ref,
               wo_ref, bo_ref, wmix_ref, pscale_ref, wup_ref, wout_ref,
               gffn_ref, wrt_ref, brt_ref,
               x1_ref, h2_ref, pos_t_ref, pw_ref, tc_ref,
               ubuf, carry):
    i = pl.program_id(0)
    tiles_per_seq = SEQ // TM_POST
    seq_first = (i % tiles_per_seq) == 0

    @pl.when(i == 0)
    def _():
        carry[...] = jnp.zeros_like(carry)

    ubuf[0:POOL_HALO, :] = jnp.where(seq_first, 0.0, uprev_ref[...])
    ubuf[POOL_HALO:, :] = u_ref[...]
    row = lax.broadcasted_iota(I32, (TM_POST, 1), 0)
    tpos = (i % tiles_per_seq) * TM_POST + row
    mixed = []
    for gi, w in enumerate(POOL_WINDOWS):
        cols = slice(gi * POOL_GROUP, (gi + 1) * POOL_GROUP)
        acc = ubuf[POOL_HALO:, cols]
        for d in range(1, w):
            acc = acc + ubuf[POOL_HALO - d:POOL_HALO - d + TM_POST, cols]
        cnt = jnp.minimum(tpos + 1, w).astype(F32)
        pooled = acc / cnt - ubuf[POOL_HALO:, cols]
        mixed.append(_dot(pooled.astype(BF16), wmix_ref[gi]) * pscale_ref[:, cols])
    mixed = jnp.concatenate(mixed, axis=-1).astype(BF16)
    y_pool = _dot(mixed, wup_ref[...])
    y_attn = _dot(attn_ref[...], wo_ref[...]) + bo_ref[...]
    merged = (gate_ref[:, :D_MODEL].astype(F32) * y_attn
              + gate_ref[:, D_MODEL:].astype(F32) * y_pool)
    x1 = x_ref[...] + _dot(merged.astype(BF16), wout_ref[...])
    x1_ref[...] = x1
    h2 = _rms_norm(x1, gffn_ref[...])
    h2_bf = h2.astype(BF16)
    h2_ref[...] = h2_bf

    logits_t = lax.dot_general(wrt_ref[...], h2_bf, (((1,), (1,)), ((), ())),
                               preferred_element_type=F32) + brt_ref[...]
    erow = lax.broadcasted_iota(I32, (N_EXPERTS, TM_POST), 0)
    work = logits_t
    top_v, onehots = [], []
    for _ in range(TOP_K):
        m = jnp.max(work, axis=0, keepdims=True)
        idx = jnp.min(jnp.where(work == m, erow, N_EXPERTS), axis=0, keepdims=True)
        hit = erow == idx
        top_v.append(m)
        onehots.append(hit)
        work = jnp.where(hit, -jnp.inf, work)
    exps = [jnp.exp(v - top_v[0]) for v in top_v]
    denom = exps[0] + exps[1] + exps[2] + exps[3]
    comb = [e / denom for e in exps]
    chosen = jnp.zeros((N_EXPERTS, TM_POST), F32)
    for hit in onehots:
        chosen = chosen + hit.astype(F32)

    e_r = lax.broadcasted_iota(I32, (N_EXPERTS, N_EXPERTS), 0)
    e_c = lax.broadcasted_iota(I32, (N_EXPERTS, N_EXPERTS), 1)
    earlier_expert = (e_c < e_r).astype(BF16)
    t_r = lax.broadcasted_iota(I32, (TT, TT), 0)
    t_c = lax.broadcasted_iota(I32, (TT, TT), 1)
    earlier_token = (t_r < t_c).astype(BF16)
    lane = lax.broadcasted_iota(I32, (N_EXPERTS, LANES), 1)
    counts = jnp.zeros((N_EXPERTS, LANES), F32)
    pos_parts = [[] for _ in range(TOP_K)]
    n_sub = TM_POST // TT
    for s in range(n_sub):
        cols = slice(s * TT, (s + 1) * TT)
        ch = chosen[:, cols]
        cnt = jnp.broadcast_to(jnp.sum(ch, axis=1, keepdims=True), (N_EXPERTS, LANES))
        before = carry[...]
        head = before - SEG_ALIGN * jnp.floor(before / SEG_ALIGN)
        flush = jnp.logical_and(i == pl.num_programs(0) - 1, s == n_sub - 1)
        present = (cnt > 0) | (flush & (head > 0))
        blocks = jnp.where(present, jnp.floor((head + cnt + (SEG_ALIGN - 1)) / SEG_ALIGN), 0.0)
        seg_loc = SEG_ALIGN * _dot(earlier_expert, blocks.astype(BF16))
        base = seg_loc + head
        rank_in_tile = _dot(ch.astype(BF16), earlier_token)
        full = jnp.concatenate([base] * (TT // LANES), axis=1) + rank_in_tile
        for k in range(TOP_K):
            pos_parts[k].append(
                jnp.sum(jnp.where(onehots[k][:, cols], full, 0.0), axis=0, keepdims=True))
        counts = jnp.where(lane == s, cnt, counts)
        carry[...] = before + cnt
    tc_ref[...] = counts

    pos = [jnp.concatenate(parts, axis=1) for parts in pos_parts]
    row8 = lax.broadcasted_iota(I32, (8, TM_POST), 0)
    pos_rows = jnp.full((8, TM_POST), -1.0, F32)
    both = jnp.zeros((8, TM_POST), F32)
    for k in range(TOP_K):
        pos_rows = jnp.where(row8 == k, pos[k], pos_rows)
        both = jnp.where(row8 == k, pos[k], both)
        both = jnp.where(row8 == TOP_K + k, comb[k], both)
    pos_t_ref[...] = pos_rows.astype(I32)
    padded = jnp.concatenate([both, jnp.zeros((LANES - 8, TM_POST), F32)], axis=0)
    pw_ref[...] = padded.T


def _post(attn, u, gates, x2, wo, bo, wmix, pscale, wup, wout, gffn, wr, br):
    n = x2.shape[0]
    row = lambda i: (i, 0)
    const = lambda i: (0, 0)
    halo_blocks = TM_POST // POOL_HALO
    prev = lambda i: (jnp.maximum(i * halo_blocks - 1, 0), 0)
    return pl.pallas_call(
        _post_body,
        grid=(n // TM_POST,),
        in_specs=[
            pl.BlockSpec((TM_POST, Q_WIDTH), row),
            pl.BlockSpec((TM_POST, POOL_WIDTH), row),
            pl.BlockSpec((POOL_HALO, POOL_WIDTH), prev),
            pl.BlockSpec((TM_POST, 2 * D_MODEL), row),
            pl.BlockSpec((TM_POST, D_MODEL), row),
            pl.BlockSpec((Q_WIDTH, D_MODEL), const),
            pl.BlockSpec((1, D_MODEL), const),
            pl.BlockSpec((len(POOL_WINDOWS), POOL_GROUP, POOL_GROUP), lambda i: (0, 0, 0)),
            pl.BlockSpec((1, POOL_WIDTH), const),
            pl.BlockSpec((POOL_WIDTH, D_MODEL), const),
            pl.BlockSpec((D_MODEL, D_MODEL), const),
            pl.BlockSpec((1, D_MODEL), const),
            pl.BlockSpec((N_EXPERTS, D_MODEL), const),
            pl.BlockSpec((N_EXPERTS, TM_POST), const),
        ],
        out_specs=[
            pl.BlockSpec((TM_POST, D_MODEL), row),
            pl.BlockSpec((TM_POST, D_MODEL), row),
            pl.BlockSpec((8, TM_POST), lambda i: (0, i)),
            pl.BlockSpec((TM_POST, LANES), row),
            pl.BlockSpec((N_EXPERTS, LANES), row),
        ],
        out_shape=[
            jax.ShapeDtypeStruct((n, D_MODEL), F32),
            jax.ShapeDtypeStruct((n, D_MODEL), BF16),
            jax.ShapeDtypeStruct((8, n), I32),
            jax.ShapeDtypeStruct((n, LANES), F32),
            jax.ShapeDtypeStruct((n // TM_POST * N_EXPERTS, LANES), F32),
        ],
        scratch_shapes=[pltpu.VMEM((TM_POST + POOL_HALO, POOL_WIDTH), F32),
                        pltpu.VMEM((N_EXPERTS, LANES), F32)],
        compiler_params=pltpu.CompilerParams(
            dimension_semantics=("arbitrary",), vmem_limit_bytes=VMEM_LIMIT),
        name="post",
    )(attn, u, u, gates, x2, wo, bo, wmix, pscale, wup, wout, gffn, wr, br)


def _start_segments(i, glob_ref, loc_ref, cnt_ref, make_copy):
    def body(e, c):
        idx = i * N_EXPERTS + e
        rows = pl.multiple_of(cnt_ref[idx], SEG_ALIGN)

        @pl.when(rows > 0)
        def _():
            loc = pl.multiple_of(loc_ref[idx], SEG_ALIGN)
            glob = pl.multiple_of(glob_ref[idx], SEG_ALIGN)
            make_copy(pl.ds(loc, rows), pl.ds(glob, rows)).start()
        return c
    lax.fori_loop(0, N_EXPERTS, body, 0, unroll=4)


def _wait_segments(i, total_ref, make_copy):
    rows = pl.multiple_of(total_ref[i], SEG_ALIGN)

    @pl.when(rows > 0)
    def _():
        make_copy(pl.ds(0, rows), pl.ds(0, rows)).wait()


def _pack_halves(v):
    c = v.shape[1] // 2
    bits = lax.bitcast_convert_type(v, U32)
    return (bits[:, :c] >> 16) | bits[:, c:]


def _unpack_halves(w):
    lo = lax.bitcast_convert_type(w << 16, F32)
    hi = lax.bitcast_convert_type(w & jnp.uint32(0xFFFF0000), F32)
    return jnp.concatenate([lo, hi], axis=1).astype(BF16)


def _dispatch_body(glob_ref, loc_ref, rd_ref, wr_ref, total_ref, h2_ref, pos_ref, xs_hbm,
                   local, tail, sem):
    i = pl.program_id(0)
    nt = pl.num_programs(0)
    slot = i % 2

    def copy_from(s):
        def make_copy(loc_rows, glob_rows):
            return pltpu.make_async_copy(local.at[s, loc_rows], xs_hbm.at[glob_rows], sem.at[s])
        return make_copy

    @pl.when(i == 0)
    def _():
        tail[...] = jnp.zeros_like(tail)

    @pl.when(i >= 2)
    def _():
        _wait_segments(i - 2, total_ref, copy_from(slot))

    j = lax.broadcasted_iota(I32, (LROWS, TT), 0)
    hit = j == pos_ref[0:1, :]
    for k in range(1, TOP_K):
        hit = hit | (j == pos_ref[k:k + 1, :])
    perm = jnp.where(hit, 1.0, 0.0).astype(BF16)
    local[slot] = _pack_halves(_dot(perm, h2_ref[...]))

    for e in range(N_EXPERTS):
        idx = i * N_EXPERTS + e
        rd = rd_ref[idx]
        wr = wr_ref[idx]
        loc = pl.multiple_of(loc_ref[idx], SEG_ALIGN)
        first = local[slot, pl.ds(loc, SEG_ALIGN), :]
        last = local[slot, pl.ds(pl.multiple_of(loc + wr, SEG_ALIGN), SEG_ALIGN), :]
        merged = jnp.where(rd > 0, first | tail[e], first)
        local[slot, pl.ds(loc, SEG_ALIGN), :] = merged
        last = jnp.where(wr == 0, merged, last)
        tail[e] = jnp.where(rd > 0, jnp.where(wr < rd, last, jnp.uint32(0)), tail[e])
    _start_segments(i, glob_ref, loc_ref, wr_ref, copy_from(slot))

    @pl.when(i == nt - 1)
    def _():
        _wait_segments(i - 1, total_ref, copy_from(1 - slot))
        _wait_segments(i, total_ref, copy_from(slot))


def _dispatch(seg_glob, seg_loc, seg_rd, seg_wr, tile_wr, h2, pos_t, n_rows):
    n = h2.shape[0]
    grid_spec = pltpu.PrefetchScalarGridSpec(
        num_scalar_prefetch=5,
        grid=(n // TT,),
        in_specs=[
            pl.BlockSpec((TT, D_MODEL), lambda i, *_: (i, 0)),
            pl.BlockSpec((8, TT), lambda i, *_: (0, i)),
        ],
        out_specs=pl.BlockSpec(memory_space=pl.ANY),
        scratch_shapes=[pltpu.VMEM((2, LROWS, PACKED), U32),
                        pltpu.VMEM((N_EXPERTS, SEG_ALIGN, PACKED), U32),
                        pltpu.SemaphoreType.DMA((2,))],
    )
    return pl.pallas_call(
        _dispatch_body,
        grid_spec=grid_spec,
        out_shape=jax.ShapeDtypeStruct((n_rows, PACKED), U32),
        compiler_params=pltpu.CompilerParams(
            dimension_semantics=("arbitrary",), vmem_limit_bytes=VMEM_LIMIT),
        name="dispatch",
    )(seg_glob, seg_loc, seg_rd, seg_wr, tile_wr, h2, pos_t)


def _moe_body(te_ref, nused_ref, rows_ref, xs_ref, wg_ref, bg_ref, wu_ref, bu_ref,
              wd_ref, bd_ref, y_ref, wbf):
    i = pl.program_id(0)

    @pl.when(i < nused_ref[0])
    def _():
        @pl.when((i == 0) | (te_ref[i] != te_ref[jnp.maximum(i - 1, 0)]))
        def _():
            wbf[0] = wg_ref[0].astype(BF16)
            wbf[1] = wu_ref[0].astype(BF16)
            wbf[2] = wd_ref[0].astype(BF16)

        row = lax.broadcasted_iota(I32, (TM_MOE, 1), 0)
        x = _unpack_halves(jnp.where(row < rows_ref[i], xs_ref[...], jnp.uint32(0)))
        acts = []
        for c in range(D_MODEL // MOE_CHUNK):
            sl = slice(c * MOE_CHUNK, (c + 1) * MOE_CHUNK)
            g = _dot(x, wbf[0, :, sl]) + bg_ref[0, :, sl]
            u = _dot(x, wbf[1, :, sl]) + bu_ref[0, :, sl]
            g = jnp.minimum(g, SWIGLU_LIMIT)
            u = jnp.clip(u, -SWIGLU_LIMIT, SWIGLU_LIMIT)
            acts.append((g * jax.nn.sigmoid(SWIGLU_ALPHA * g) * (u + 1.0)).astype(BF16))
        a = jnp.concatenate(acts, axis=1)
        y = _dot(a, wbf[2]) + bd_ref[0]
        y_ref[...] = _pack_halves(y.astype(BF16).astype(F32))


def _moe(tile_expert, n_used, tile_rows, xs, wg, bg, wu, bu, wd, bd):
    n_rows = xs.shape[0]
    n_tiles = n_rows // TM_MOE
    d_ff = wg.shape[2]
    row = lambda i, te, nu, tr: (jnp.minimum(i, nu[0] - 1), 0)
    wsel = lambda i, te, nu, tr: (te[i], 0, 0)
    grid_spec = pltpu.PrefetchScalarGridSpec(
        num_scalar_prefetch=3,
        grid=(n_tiles,),
        in_specs=[
            pl.BlockSpec((TM_MOE, PACKED), row),
            pl.BlockSpec((1, D_MODEL, d_ff), wsel),
            pl.BlockSpec((1, 1, d_ff), wsel),
            pl.BlockSpec((1, D_MODEL, d_ff), wsel),
            pl.BlockSpec((1, 1, d_ff), wsel),
            pl.BlockSpec((1, d_ff, D_MODEL), wsel),
            pl.BlockSpec((1, 1, D_MODEL), wsel),
        ],
        out_specs=pl.BlockSpec((TM_MOE, PACKED), row),
        scratch_shapes=[pltpu.VMEM((3, D_MODEL, d_ff), BF16)],
    )
    assert d_ff == D_MODEL
    return pl.pallas_call(
        _moe_body,
        grid_spec=grid_spec,
        out_shape=jax.ShapeDtypeStruct((n_rows, PACKED), U32),
        compiler_params=pltpu.CompilerParams(
            dimension_semantics=("arbitrary",), vmem_limit_bytes=VMEM_LIMIT),
        name="moe",
    )(tile_expert, n_used, tile_rows, xs, wg, bg, wu, bu, wd, bd)


def _combine_body(glob_ref, loc_ref, cnt_ref, total_ref, x1_ref, pw_ref, gfin_ref,
                  ys_hbm, o_ref, local, sem):
    i = pl.program_id(0)
    nt = pl.num_programs(0)
    slot = i % 2

    def copy_into(s):
        def make_copy(loc_rows, glob_rows):
            return pltpu.make_async_copy(ys_hbm.at[glob_rows], local.at[s, loc_rows], sem.at[s])
        return make_copy

    @pl.when(i == 0)
    def _():
        local[...] = jnp.zeros_like(local)
        _start_segments(0, glob_ref, loc_ref, cnt_ref, copy_into(0))

    @pl.when(i + 1 < nt)
    def _():
        _start_segments(i + 1, glob_ref, loc_ref, cnt_ref, copy_into(1 - slot))
    _wait_segments(i, total_ref, copy_into(slot))

    j = lax.broadcasted_iota(I32, (TT, LROWS), 1).astype(F32)
    wc = jnp.zeros((TT, LROWS), F32)
    for k in range(TOP_K):
        wc = jnp.where(j == pw_ref[:, k:k + 1], pw_ref[:, TOP_K + k:TOP_K + k + 1], wc)
    acc = x1_ref[...] + _dot(wc.astype(BF16), _unpack_halves(local[slot]))
    o_ref[...] = _rms_norm(acc, gfin_ref[...])


def _combine(seg_glob, seg_loc, seg_cnt, tile_rd, x1, pw, gfin, ys):
    n = x1.shape[0]
    grid_spec = pltpu.PrefetchScalarGridSpec(
        num_scalar_prefetch=4,
        grid=(n // TT,),
        in_specs=[
            pl.BlockSpec((TT, D_MODEL), lambda i, *_: (i, 0)),
            pl.BlockSpec((TT, LANES), lambda i, *_: (i, 0)),
            pl.BlockSpec((1, D_MODEL), lambda i, *_: (0, 0)),
            pl.BlockSpec(memory_space=pl.ANY),
        ],
        out_specs=pl.BlockSpec((TT, D_MODEL), lambda i, *_: (i, 0)),
        scratch_shapes=[pltpu.VMEM((2, LROWS, PACKED), U32),
                        pltpu.SemaphoreType.DMA((2,))],
    )
    return pl.pallas_call(
        _combine_body,
        grid_spec=grid_spec,
        out_shape=jax.ShapeDtypeStruct((n, D_MODEL), F32),
        compiler_params=pltpu.CompilerParams(
            dimension_semantics=("arbitrary",), vmem_limit_bytes=VMEM_LIMIT),
        name="combine",
    )(seg_glob, seg_loc, seg_cnt, tile_rd, x1, pw, gfin, ys)


def _layer(x2, pos_b, invf, norm_mix_g, w_in, b_in, attn_sinks, w_o_attn, b_o_attn,
           w_pool_mix, pool_scale, w_pool_up, w_out, norm_ffn_g, w_router, b_router,
           w_gate, b_gate, w_up, b_up, w_down, b_down, out_g):
    n = x2.shape[0]
    q, kb, vb, u, gates = _inproj(
        x2, pos_b, invf, norm_mix_g[None, :], w_in.astype(BF16), b_in[None, :])
    attn = _attn(attn_sinks, q, kb, vb)

    wrt = w_router.T.astype(BF16)
    brt = jnp.broadcast_to(b_router[:, None], (N_EXPERTS, TM_POST))
    x1, h2, pos_t, pw, tc = _post(
        attn, u, gates, x2, w_o_attn.astype(BF16), b_o_attn[None, :],
        w_pool_mix.astype(BF16), pool_scale[None, :], w_pool_up.astype(BF16),
        w_out.astype(BF16), norm_ffn_g[None, :], wrt, brt)

    nt = n // TT
    sub = TM_POST // TT
    seg_cnt = (tc.reshape(n // TM_POST, N_EXPERTS, LANES)[:, :, :sub]
               .transpose(0, 2, 1).reshape(nt, N_EXPERTS).astype(I32))
    before = jnp.cumsum(seg_cnt, axis=0) - seg_cnt
    sizes = jnp.sum(seg_cnt, axis=0)
    padded = (sizes + TM_MOE - 1) // TM_MOE * TM_MOE
    pends = jnp.cumsum(padded)
    pstarts = pends - padded
    head = before % SEG_ALIGN
    last_tile = (jnp.arange(nt, dtype=I32) == nt - 1)[:, None]
    blocks_up = (head + seg_cnt + SEG_ALIGN - 1) // SEG_ALIGN
    blocks_dn = (head + seg_cnt) // SEG_ALIGN
    seg_rd = jnp.where((seg_cnt > 0) | (last_tile & (head > 0)), blocks_up, 0) * SEG_ALIGN
    seg_wr = jnp.where(last_tile, seg_rd, jnp.where(seg_cnt > 0, blocks_dn * SEG_ALIGN, 0))
    seg_loc = jnp.cumsum(seg_rd, axis=1) - seg_rd
    seg_glob = pstarts[None, :] + before - head
    tile_rd = jnp.sum(seg_rd, axis=1)
    tile_wr = jnp.sum(seg_wr, axis=1)
    seg_glob, seg_loc, seg_rd, seg_wr = (a.reshape(-1) for a in (seg_glob, seg_loc, seg_rd, seg_wr))
    m = n * TOP_K
    n_tiles = (m + N_EXPERTS * (TM_MOE - 1) + TM_MOE - 1) // TM_MOE
    tile_start = jnp.arange(n_tiles, dtype=I32) * TM_MOE
    tile_expert = jnp.minimum(
        jnp.sum(tile_start[:, None] >= pends[None, :], axis=-1), N_EXPERTS - 1).astype(I32)
    own = tile_expert[:, None] == jnp.arange(N_EXPERTS, dtype=I32)[None, :]
    left = jnp.sum(jnp.where(own, (pstarts + sizes)[None, :] - tile_start[:, None], 0), axis=-1)
    tile_rows = jnp.clip(left, 0, TM_MOE)
    n_used = (pends[-1] // TM_MOE).astype(I32)[None]

    xs = _dispatch(seg_glob, seg_loc, seg_rd, seg_wr, tile_wr, h2, pos_t, n_tiles * TM_MOE)
    ys = _moe(tile_expert, n_used, tile_rows.astype(I32), xs,
              w_gate, b_gate[:, None, :], w_up, b_up[:, None, :], w_down, b_down[:, None, :])
    return _combine(seg_glob, seg_loc, seg_rd, tile_rd, x1, pw, out_g[None, :], ys)


def kernel(x, positions, norm_mix_g, w_in, b_in, attn_sinks, w_o_attn, b_o_attn, w_pool_mix,
           pool_scale, w_pool_up, w_out, norm_ffn_g, w_router, b_router, w_gate, b_gate,
           w_up, b_up, w_down, b_down, norm_final_g):
    b, s, d = x.shape
    depth = w_in.shape[0]
    assert (s, d, depth) == (SEQ, D_MODEL, 1)
    n = b * s
    x2 = x.reshape(n, d)
    pos_b = jnp.repeat(positions.reshape(n // 4, 4), LANES // 4, axis=1)
    inv_freq = ROPE_THETA ** (-jnp.arange(0, HEAD_DIM, 2, dtype=F32) / HEAD_DIM)
    invf = jnp.tile(inv_freq, LANES // (HEAD_DIM // 2))[None, :]
    out = _layer(x2, pos_b, invf, norm_mix_g[0], w_in[0], b_in[0], attn_sinks[0], w_o_attn[0],
                 b_o_attn[0], w_pool_mix[0], pool_scale[0], w_pool_up[0], w_out[0],
                 norm_ffn_g[0], w_router[0], b_router[0], w_gate[0], b_gate[0], w_up[0],
                 b_up[0], w_down[0], b_down[0], norm_final_g)
    return out.reshape(b, s, d)
```

```python
import functools

import numpy as np
import jax
import jax.numpy as jnp
from jax import lax
from jax.experimental import pallas as pl
from jax.experimental.pallas import tpu as pltpu

F32 = jnp.float32
BF16 = jnp.bfloat16
I32 = jnp.int32
U32 = jnp.uint32

D_MODEL = 1024
SEQ = 4096
HEAD_DIM = 64
N_Q_HEADS = 16
WINDOW = 128
ROPE_THETA = 10000.0
Q_WIDTH = N_Q_HEADS * HEAD_DIM
KV_WIDTH = 128
POOL_WINDOWS = (2, 4, 8, 16)
POOL_WIDTH = 512
POOL_GROUP = 128
POOL_HALO = 16
N_EXPERTS = 32
TOP_K = 4
SWIGLU_LIMIT = 7.0
SWIGLU_ALPHA = 1.702
RMS_EPS = 1e-5
NEG_BIG = -1e30

LANES = 128
TM_IN = 1024
TQ = 512
TM_POST = 512
TM_MOE = 512
TT = 256
SEG_ALIGN = 8
LROWS = TT * TOP_K + 2 * N_EXPERTS * SEG_ALIGN
PACKED = D_MODEL // 2
VMEM_LIMIT = 56 * 1024 * 1024


def _rms_norm(x, g):
    ms = jnp.mean(x * x, axis=-1, keepdims=True)
    return (x * lax.rsqrt(ms + RMS_EPS)) * g


def _dot(a, b):
    return jnp.dot(a, b, preferred_element_type=F32)


def _inproj_body(x_ref, pos_ref, invf_ref, g_ref, w_ref, b_ref,
                 q_ref, kb_ref, vb_ref, u_ref, gate_ref, cos_scr, sin_scr):
    h = _rms_norm(x_ref[...], g_ref[...]).astype(BF16)

    theta = pos_ref[...].astype(F32) * invf_ref[...]
    lane4 = lax.broadcasted_iota(I32, theta.shape, 1)
    for packed, scr in ((jnp.cos(theta), cos_scr), (jnp.sin(theta), sin_scr)):
        for jt in range(4):
            seg = packed if jt == 0 else pltpu.roll(packed, LANES - 32 * jt, 1)
            seg = jnp.where(lane4 < 32, seg, pltpu.roll(seg, 32, 1))
            seg = jnp.where(lane4 < 64, seg, pltpu.roll(seg, 64, 1))
            scr[pl.ds(jt, TM_IN // 4, stride=4), :] = seg
    cos = cos_scr[...]
    sin = sin_scr[...]
    lane = lax.broadcasted_iota(I32, cos.shape, 1)
    first_half = (lane & 32) == 0
    sin_signed = jnp.where(first_half, -sin, sin)
    low_head = lane < HEAD_DIM

    def rope(t):
        swapped = jnp.where(first_half, pltpu.roll(t, 96, 1), pltpu.roll(t, 32, 1))
        return t * cos + swapped * sin_signed

    def band_layout(t, out_ref):
        tr = pltpu.roll(t, 64, 1)
        zero = jnp.zeros_like(t)
        chunks = (jnp.where(low_head, t, zero), jnp.where(low_head, zero, tr),
                  jnp.where(low_head, tr, zero), jnp.where(low_head, zero, t))
        for c, val in enumerate(chunks):
            out_ref[:, c * LANES:(c + 1) * LANES] = val.astype(BF16)

    qk_w = Q_WIDTH + KV_WIDTH
    zqk = _dot(h, w_ref[:, :qk_w]) + b_ref[:, :qk_w]
    scale = HEAD_DIM ** -0.5
    for j in range(Q_WIDTH // LANES):
        sl = slice(j * LANES, (j + 1) * LANES)
        q_ref[:, sl] = (rope(zqk[:, sl]) * scale).astype(BF16)
    band_layout(rope(zqk[:, Q_WIDTH:qk_w]), kb_ref)

    v0 = qk_w
    zv = _dot(h, w_ref[:, v0:v0 + KV_WIDTH]) + b_ref[:, v0:v0 + KV_WIDTH]
    band_layout(zv, vb_ref)

    u0 = v0 + KV_WIDTH
    u_ref[...] = _dot(h, w_ref[:, u0:u0 + POOL_WIDTH]) + b_ref[:, u0:u0 + POOL_WIDTH]

    g0 = u0 + POOL_WIDTH
    for c in range(2):
        sl = slice(g0 + c * D_MODEL, g0 + (c + 1) * D_MODEL)
        zg = _dot(h, w_ref[:, sl]) + b_ref[:, sl]
        gate_ref[:, c * D_MODEL:(c + 1) * D_MODEL] = jax.nn.sigmoid(zg).astype(BF16)


def _inproj(x2, pos_b, invf, g, w, b):
    n = x2.shape[0]
    in_width = w.shape[1]
    row = lambda i: (i, 0)
    const = lambda i: (0, 0)
    return pl.pallas_call(
        _inproj_body,
        grid=(n // TM_IN,),
        in_specs=[
            pl.BlockSpec((TM_IN, D_MODEL), row),
            pl.BlockSpec((TM_IN // 4, LANES), row),
            pl.BlockSpec((1, LANES), const),
            pl.BlockSpec((1, D_MODEL), const),
            pl.BlockSpec((D_MODEL, in_width), const),
            pl.BlockSpec((1, in_width), const),
        ],
        out_specs=[
            pl.BlockSpec((TM_IN, Q_WIDTH), row),
            pl.BlockSpec((TM_IN, 4 * LANES), row),
            pl.BlockSpec((TM_IN, 4 * LANES), row),
            pl.BlockSpec((TM_IN, POOL_WIDTH), row),
            pl.BlockSpec((TM_IN, 2 * D_MODEL), row),
        ],
        out_shape=[
            jax.ShapeDtypeStruct((n, Q_WIDTH), BF16),
            jax.ShapeDtypeStruct((n, 4 * LANES), BF16),
            jax.ShapeDtypeStruct((n, 4 * LANES), BF16),
            jax.ShapeDtypeStruct((n, POOL_WIDTH), F32),
            jax.ShapeDtypeStruct((n, 2 * D_MODEL), BF16),
        ],
        scratch_shapes=[pltpu.VMEM((TM_IN, LANES), F32), pltpu.VMEM((TM_IN, LANES), F32)],
        compiler_params=pltpu.CompilerParams(
            dimension_semantics=("arbitrary",), vmem_limit_bytes=VMEM_LIMIT),
        name="inproj",
    )(x2, pos_b, invf, g, w, b)


def _attn_body(sinks_ref, q_ref, kbc_ref, kbp_ref, vbc_ref, vbp_ref, bias_ref,
               o_ref, p_scr):
    t = pl.program_id(0)
    seq_first = (t % (SEQ // TQ)) == 0
    lane = lax.broadcasted_iota(I32, (WINDOW, LANES), 1)
    low_head = lane < HEAD_DIM
    for n in range(TQ // WINDOW):
        rows = slice(n * WINDOW, (n + 1) * WINDOW)
        if n == 0:
            k_prev, v_prev = kbp_ref[...], vbp_ref[...]
            bias = jnp.where(seq_first, bias_ref[1], bias_ref[0])
        else:
            prev_rows = slice((n - 1) * WINDOW, n * WINDOW)
            k_prev, v_prev = kbc_ref[prev_rows, :], vbc_ref[prev_rows, :]
            bias = bias_ref[0]
        k_cur, v_cur = kbc_ref[rows, :], vbc_ref[rows, :]
        for g in range(2):
            def stack(prev, cur):
                lo = slice((2 * g) * LANES, (2 * g + 1) * LANES)
                hi = slice((2 * g + 1) * LANES, (2 * g + 2) * LANES)
                return jnp.concatenate([prev[:, lo], cur[:, lo], prev[:, hi], cur[:, hi]], axis=0)
            kmat = stack(k_prev, k_cur)
            vmat = stack(v_prev, v_cur)
            qs = jnp.concatenate(
                [q_ref[rows, (4 * g + p) * LANES:(4 * g + p + 1) * LANES] for p in range(4)],
                axis=0)
            s = lax.dot_general(qs, kmat, (((1,), (1,)), ((), ())),
                                preferred_element_type=F32) + bias
            inv = []
            for p in range(4):
                pr = slice(p * WINDOW, (p + 1) * WINDOW)
                inv_p = []
                for j in range(2):
                    cols = slice(j * 2 * WINDOW, (j + 1) * 2 * WINDOW)
                    sp = s[pr, cols]
                    sink = sinks_ref[8 * g + 2 * p + j]
                    m = jnp.maximum(jnp.max(sp, axis=-1, keepdims=True), sink)
                    e = jnp.exp(sp - m)
                    den = jnp.sum(e, axis=-1, keepdims=True) + jnp.exp(sink - m)
                    p_scr[pr, cols] = e.astype(BF16)
                    inv_p.append(1.0 / den)
                inv.append(inv_p)
            o = _dot(p_scr[...], vmat)
            for p in range(4):
                pr = slice(p * WINDOW, (p + 1) * WINDOW)
                norm = jnp.where(low_head, inv[p][0], inv[p][1])
                o_ref[rows, (4 * g + p) * LANES:(4 * g + p + 1) * LANES] = (o[pr, :] * norm).astype(BF16)


def _attn_bias():
    r = np.arange(4 * WINDOW)[:, None] % WINDOW
    c = np.arange(4 * WINDOW)[None, :] % (2 * WINDOW)
    band = (c > r) & (c <= r + WINDOW)
    first = band & (c >= WINDOW)
    return np.stack([np.where(band, 0.0, NEG_BIG), np.where(first, 0.0, NEG_BIG)]).astype(np.float32)


def _attn(sinks, q, kb, vb):
    n = q.shape[0]
    blocks_per_tile = TQ // WINDOW
    cur = lambda t: (t, 0)
    prev = lambda t: (jnp.maximum(t * blocks_per_tile - 1, 0), 0)
    bias = jnp.asarray(_attn_bias())
    return pl.pallas_call(
        _attn_body,
        grid=(n // TQ,),
        in_specs=[
            pl.BlockSpec(memory_space=pltpu.SMEM),
            pl.BlockSpec((TQ, Q_WIDTH), cur),
            pl.BlockSpec((TQ, 4 * LANES), cur),
            pl.BlockSpec((WINDOW, 4 * LANES), prev),
            pl.BlockSpec((TQ, 4 * LANES), cur),
            pl.BlockSpec((WINDOW, 4 * LANES), prev),
            pl.BlockSpec((2, 4 * WINDOW, 4 * WINDOW), lambda t: (0, 0, 0)),
        ],
        out_specs=pl.BlockSpec((TQ, Q_WIDTH), cur),
        out_shape=jax.ShapeDtypeStruct((n, Q_WIDTH), BF16),
        scratch_shapes=[pltpu.VMEM((4 * WINDOW, 4 * WINDOW), BF16)],
        compiler_params=pltpu.CompilerParams(
            dimension_semantics=("arbitrary",), vmem_limit_bytes=VMEM_LIMIT),
        name="attn",
    )(sinks, q, kb, kb, vb, vb, bias)


def _post_body(attn_ref, u_ref, uprev_ref, gate_ref, x_ref,
               wo_ref, bo_ref, wmix_ref, pscale_ref, wup_ref, wout_ref,
               gffn_ref, wrt_ref, brt_ref,
               x1_ref, h2_ref, pos_t_ref, pw_ref, tc_ref,
               ubuf, carry):
    i = pl.program_id(0)
    tiles_per_seq = SEQ // TM_POST
    seq_first = (i % tiles_per_seq) == 0

    @pl.when(i == 0)
    def _():
        carry[...] = jnp.zeros_like(carry)

    ubuf[0:POOL_HALO, :] = jnp.where(seq_first, 0.0, uprev_ref[...])
    ubuf[POOL_HALO:, :] = u_ref[...]
    row = lax.broadcasted_iota(I32, (TM_POST, 1), 0)
    tpos = (i % tiles_per_seq) * TM_POST + row
    mixed = []
    for gi, w in enumerate(POOL_WINDOWS):
        cols = slice(gi * POOL_GROUP, (gi + 1) * POOL_GROUP)
        acc = ubuf[POOL_HALO:, cols]
        for d in range(1, w):
            acc = acc + ubuf[POOL_HALO - d:POOL_HALO - d + TM_POST, cols]
        cnt = jnp.minimum(tpos + 1, w).astype(F32)
        pooled = acc / cnt - ubuf[POOL_HALO:, cols]
        mixed.append(_dot(pooled.astype(BF16), wmix_ref[gi]) * pscale_ref[:, cols])
    mixed = jnp.concatenate(mixed, axis=-1).astype(BF16)
    y_pool = _dot(mixed, wup_ref[...])
    y_attn = _dot(attn_ref[...], wo_ref[...]) + bo_ref[...]
    merged = (gate_ref[:, :D_MODEL].astype(F32) * y_attn
              + gate_ref[:, D_MODEL:].astype(F32) * y_pool)
    x1 = x_ref[...] + _dot(merged.astype(BF16), wout_ref[...])
    x1_ref[...] = x1
    h2 = _rms_norm(x1, gffn_ref[...])
    h2_bf = h2.astype(BF16)
    h2_ref[...] = h2_bf

    logits_t = lax.dot_general(wrt_ref[...], h2_bf, (((1,), (1,)), ((), ())),
                               preferred_element_type=F32) + brt_ref[...]
    erow = lax.broadcasted_iota(I32, (N_EXPERTS, TM_POST), 0)
    work = logits_t
    top_v, onehots = [], []
    for _ in range(TOP_K):
        m = jnp.max(work, axis=0, keepdims=True)
        idx = jnp.min(jnp.where(work == m, erow, N_EXPERTS), axis=0, keepdims=True)
        hit = erow == idx
        top_v.append(m)
        onehots.append(hit)
        work = jnp.where(hit, -jnp.inf, work)
    exps = [jnp.exp(v - top_v[0]) for v in top_v]
    denom = exps[0] + exps[1] + exps[2] + exps[3]
    comb = [e / denom for e in exps]
    chosen = jnp.zeros((N_EXPERTS, TM_POST), F32)
    for hit in onehots:
        chosen = chosen + hit.astype(F32)

    e_r = lax.broadcasted_iota(I32, (N_EXPERTS, N_EXPERTS), 0)
    e_c = lax.broadcasted_iota(I32, (N_EXPERTS, N_EXPERTS), 1)
    earlier_expert = (e_c < e_r).astype(BF16)
    t_r = lax.broadcasted_iota(I32, (TT, TT), 0)
    t_c = lax.broadcasted_iota(I32, (TT, TT), 1)
    earlier_token = (t_r < t_c).astype(BF16)
    lane = lax.broadcasted_iota(I32, (N_EXPERTS, LANES), 1)
    counts = jnp.zeros((N_EXPERTS, LANES), F32)
    pos_parts = [[] for _ in range(TOP_K)]
    n_sub = TM_POST // TT
    for s in range(n_sub):
        cols = slice(s * TT, (s + 1) * TT)
        ch = chosen[:, cols]
        cnt = jnp.broadcast_to(jnp.sum(ch, axis=1, keepdims=True), (N_EXPERTS, LANES))
        before = carry[...]
        head = before - SEG_ALIGN * jnp.floor(before / SEG_ALIGN)
        flush = jnp.logical_and(i == pl.num_programs(0) - 1, s == n_sub - 1)
        present = (cnt > 0) | (flush & (head > 0))
        blocks = jnp.where(present, jnp.floor((head + cnt + (SEG_ALIGN - 1)) / SEG_ALIGN), 0.0)
        seg_loc = SEG_ALIGN * _dot(earlier_expert, blocks.astype(BF16))
        base = seg_loc + head
        rank_in_tile = _dot(ch.astype(BF16), earlier_token)
        full = jnp.concatenate([base] * (TT // LANES), axis=1) + rank_in_tile
        for k in range(TOP_K):
            pos_parts[k].append(
                jnp.sum(jnp.where(onehots[k][:, cols], full, 0.0), axis=0, keepdims=True))
        counts = jnp.where(lane == s, cnt, counts)
        carry[...] = before + cnt
    tc_ref[...] = counts

    pos = [jnp.concatenate(parts, axis=1) for parts in pos_parts]
    row8 = lax.broadcasted_iota(I32, (8, TM_POST), 0)
    pos_rows = jnp.full((8, TM_POST), -1.0, F32)
    both = jnp.zeros((8, TM_POST), F32)
    for k in range(TOP_K):
        pos_rows = jnp.where(row8 == k, pos[k], pos_rows)
        both = jnp.where(row8 == k, pos[k], both)
        both = jnp.where(row8 == TOP_K + k, comb[k], both)
    pos_t_ref[...] = pos_rows.astype(I32)
    padded = jnp.concatenate([both, jnp.zeros((LANES - 8, TM_POST), F32)], axis=0)
    pw_ref[...] = padded.T


def _post(attn, u, gates, x2, wo, bo, wmix, pscale, wup, wout, gffn, wr, br):
    n = x2.shape[0]
    row = lambda i: (i, 0)
    const = lambda i: (0, 0)
    halo_blocks = TM_POST // POOL_HALO
    prev = lambda i: (jnp.maximum(i * halo_blocks - 1, 0), 0)
    return pl.pallas_call(
        _post_body,
        grid=(n // TM_POST,),
        in_specs=[
            pl.BlockSpec((TM_POST, Q_WIDTH), row),
            pl.BlockSpec((TM_POST, POOL_WIDTH), row),
            pl.BlockSpec((POOL_HALO, POOL_WIDTH), prev),
            pl.BlockSpec((TM_POST, 2 * D_MODEL), row),
            pl.BlockSpec((TM_POST, D_MODEL), row),
            pl.BlockSpec((Q_WIDTH, D_MODEL), const),
            pl.BlockSpec((1, D_MODEL), const),
            pl.BlockSpec((len(POOL_WINDOWS), POOL_GROUP, POOL_GROUP), lambda i: (0, 0, 0)),
            pl.BlockSpec((1, POOL_WIDTH), const),
            pl.BlockSpec((POOL_WIDTH, D_MODEL), const),
            pl.BlockSpec((D_MODEL, D_MODEL), const),
            pl.BlockSpec((1, D_MODEL), const),
            pl.BlockSpec((N_EXPERTS, D_MODEL), const),
            pl.BlockSpec((N_EXPERTS, TM_POST), const),
        ],
        out_specs=[
            pl.BlockSpec((TM_POST, D_MODEL), row),
            pl.BlockSpec((TM_POST, D_MODEL), row),
            pl.BlockSpec((8, TM_POST), lambda i: (0, i)),
            pl.BlockSpec((TM_POST, LANES), row),
            pl.BlockSpec((N_EXPERTS, LANES), row),
        ],
        out_shape=[
            jax.ShapeDtypeStruct((n, D_MODEL), F32),
            jax.ShapeDtypeStruct((n, D_MODEL), BF16),
            jax.ShapeDtypeStruct((8, n), I32),
            jax.ShapeDtypeStruct((n, LANES), F32),
            jax.ShapeDtypeStruct((n // TM_POST * N_EXPERTS, LANES), F32),
        ],
        scratch_shapes=[pltpu.VMEM((TM_POST + POOL_HALO, POOL_WIDTH), F32),
                        pltpu.VMEM((N_EXPERTS, LANES), F32)],
        compiler_params=pltpu.CompilerParams(
            dimension_semantics=("arbitrary",), vmem_limit_bytes=VMEM_LIMIT),
        name="post",
    )(attn, u, u, gates, x2, wo, bo, wmix, pscale, wup, wout, gffn, wr, br)


def _start_segments(i, glob_ref, loc_ref, cnt_ref, make_copy):
    def body(e, c):
        idx = i * N_EXPERTS + e
        rows = pl.multiple_of(cnt_ref[idx], SEG_ALIGN)

        @pl.when(rows > 0)
        def _():
            loc = pl.multiple_of(loc_ref[idx], SEG_ALIGN)
            glob = pl.multiple_of(glob_ref[idx], SEG_ALIGN)
            make_copy(pl.ds(loc, rows), pl.ds(glob, rows)).start()
        return c
    lax.fori_loop(0, N_EXPERTS, body, 0, unroll=4)


def _wait_segments(i, total_ref, make_copy):
    rows = pl.multiple_of(total_ref[i], SEG_ALIGN)

    @pl.when(rows > 0)
    def _():
        make_copy(pl.ds(0, rows), pl.ds(0, rows)).wait()


def _pack_halves(v):
    c = v.shape[1] // 2
    bits = lax.bitcast_convert_type(v, U32)
    return (bits[:, :c] >> 16) | bits[:, c:]


def _unpack_halves(w):
    lo = lax.bitcast_convert_type(w << 16, F32)
    hi = lax.bitcast_convert_type(w & jnp.uint32(0xFFFF0000), F32)
    return jnp.concatenate([lo, hi], axis=1).astype(BF16)


def _dispatch_body(glob_ref, loc_ref, rd_ref, wr_ref, total_ref, h2_ref, pos_ref, xs_hbm,
                   local, tail, sem):
    i = pl.program_id(0)
    nt = pl.num_programs(0)
    slot = i % 2

    def copy_from(s):
        def make_copy(loc_rows, glob_rows):
            return pltpu.make_async_copy(local.at[s, loc_rows], xs_hbm.at[glob_rows], sem.at[s])
        return make_copy

    @pl.when(i == 0)
    def _():
        tail[...] = jnp.zeros_like(tail)

    @pl.when(i >= 2)
    def _():
        _wait_segments(i - 2, total_ref, copy_from(slot))

    j = lax.broadcasted_iota(I32, (LROWS, TT), 0)
    hit = j == pos_ref[0:1, :]
    for k in range(1, TOP_K):
        hit = hit | (j == pos_ref[k:k + 1, :])
    perm = jnp.where(hit, 1.0, 0.0).astype(BF16)
    local[slot] = _pack_halves(_dot(perm, h2_ref[...]))

    for e in range(N_EXPERTS):
        idx = i * N_EXPERTS + e
        rd = rd_ref[idx]
        wr = wr_ref[idx]
        loc = pl.multiple_of(loc_ref[idx], SEG_ALIGN)
        first = local[slot, pl.ds(loc, SEG_ALIGN), :]
        last = local[slot, pl.ds(pl.multiple_of(loc + wr, SEG_ALIGN), SEG_ALIGN), :]
        merged = jnp.where(rd > 0, first | tail[e], first)
        local[slot, pl.ds(loc, SEG_ALIGN), :] = merged
        last = jnp.where(wr == 0, merged, last)
        tail[e] = jnp.where(rd > 0, jnp.where(wr < rd, last, jnp.uint32(0)), tail[e])
    _start_segments(i, glob_ref, loc_ref, wr_ref, copy_from(slot))

    @pl.when(i == nt - 1)
    def _():
        _wait_segments(i - 1, total_ref, copy_from(1 - slot))
        _wait_segments(i, total_ref, copy_from(slot))


def _dispatch(seg_glob, seg_loc, seg_rd, seg_wr, tile_wr, h2, pos_t, n_rows):
    n = h2.shape[0]
    grid_spec = pltpu.PrefetchScalarGridSpec(
        num_scalar_prefetch=5,
        grid=(n // TT,),
        in_specs=[
            pl.BlockSpec((TT, D_MODEL), lambda i, *_: (i, 0)),
            pl.BlockSpec((8, TT), lambda i, *_: (0, i)),
        ],
        out_specs=pl.BlockSpec(memory_space=pl.ANY),
        scratch_shapes=[pltpu.VMEM((2, LROWS, PACKED), U32),
                        pltpu.VMEM((N_EXPERTS, SEG_ALIGN, PACKED), U32),
                        pltpu.SemaphoreType.DMA((2,))],
    )
    return pl.pallas_call(
        _dispatch_body,
        grid_spec=grid_spec,
        out_shape=jax.ShapeDtypeStruct((n_rows, PACKED), U32),
        compiler_params=pltpu.CompilerParams(
            dimension_semantics=("arbitrary",), vmem_limit_bytes=VMEM_LIMIT),
        name="dispatch",
    )(seg_glob, seg_loc, seg_rd, seg_wr, tile_wr, h2, pos_t)


def _moe_body(te_ref, nused_ref, rows_ref, next_ref, xs_ref, bg_ref, bu_ref, bd_ref,
              wg_hbm, wu_hbm, wd_hbm, y_ref, wf32, wbf, sem, slot_ref):
    i = pl.program_id(0)

    def fetch(expert, slot):
        return [pltpu.make_async_copy(w.at[expert], wf32.at[slot, m], sem.at[slot, m])
                for m, w in enumerate((wg_hbm, wu_hbm, wd_hbm))]

    @pl.when(i == 0)
    def _():
        slot_ref[0] = 0

    @pl.when(i < nused_ref[0])
    def _():
        expert = te_ref[i]

        @pl.when((i == 0) | (expert != te_ref[jnp.maximum(i - 1, 0)]))
        def _():
            slot = slot_ref[0]

            @pl.when(i == 0)
            def _():
                for cp in fetch(expert, slot):
                    cp.start()
            for cp in fetch(expert, slot):
                cp.wait()
            nxt = next_ref[i]

            @pl.when(nxt >= 0)
            def _():
                for cp in fetch(nxt, 1 - slot):
                    cp.start()
            for m in range(3):
                wbf[m] = wf32[slot, m].astype(BF16)
            slot_ref[0] = 1 - slot

        row = lax.broadcasted_iota(I32, (TM_MOE, 1), 0)
        x = _unpack_halves(jnp.where(row < rows_ref[i], xs_ref[...], jnp.uint32(0)))
        g = _dot(x, wbf[0]) + bg_ref[0]
        u = _dot(x, wbf[1]) + bu_ref[0]
        g = jnp.minimum(g, SWIGLU_LIMIT)
        u = jnp.clip(u, -SWIGLU_LIMIT, SWIGLU_LIMIT)
        a = (g * jax.nn.sigmoid(SWIGLU_ALPHA * g) * (u + 1.0)).astype(BF16)
        y = _dot(a, wbf[2]) + bd_ref[0]
        y_ref[...] = _pack_halves(y.astype(BF16).astype(F32))


def _moe(tile_expert, n_used, tile_rows, tile_next, xs, wg, bg, wu, bu, wd, bd):
    n_rows = xs.shape[0]
    n_tiles = n_rows // TM_MOE
    d_ff = wg.shape[2]
    assert d_ff == D_MODEL
    row = lambda i, te, nu, *_: (jnp.minimum(i, nu[0] - 1), 0)
    bsel = lambda i, te, *_: (te[i], 0, 0)
    grid_spec = pltpu.PrefetchScalarGridSpec(
        num_scalar_prefetch=4,
        grid=(n_tiles,),
        in_specs=[
            pl.BlockSpec((TM_MOE, PACKED), row),
            pl.BlockSpec((1, 1, d_ff), bsel),
            pl.BlockSpec((1, 1, d_ff), bsel),
            pl.BlockSpec((1, 1, D_MODEL), bsel),
            pl.BlockSpec(memory_space=pl.ANY),
            pl.BlockSpec(memory_space=pl.ANY),
            pl.BlockSpec(memory_space=pl.ANY),
        ],
        out_specs=pl.BlockSpec((TM_MOE, PACKED), row),
        scratch_shapes=[pltpu.VMEM((2, 3, D_MODEL, D_MODEL), F32),
                        pltpu.VMEM((3, D_MODEL, D_MODEL), BF16),
                        pltpu.SemaphoreType.DMA((2, 3)),
                        pltpu.SMEM((1,), I32)],
    )
    return pl.pallas_call(
        _moe_body,
        grid_spec=grid_spec,
        out_shape=jax.ShapeDtypeStruct((n_rows, PACKED), U32),
        compiler_params=pltpu.CompilerParams(
            dimension_semantics=("arbitrary",), vmem_limit_bytes=VMEM_LIMIT),
        name="moe",
    )(tile_expert, n_used, tile_rows, tile_next, xs, bg, bu, bd, wg, wu, wd)


def _combine_body(glob_ref, loc_ref, cnt_ref, total_ref, x1_ref, pw_ref, gfin_ref,
                  ys_hbm, o_ref, local, sem):
    i = pl.program_id(0)
    nt = pl.num_programs(0)
    slot = i % 2

    def copy_into(s):
        def make_copy(loc_rows, glob_rows):
            return pltpu.make_async_copy(ys_hbm.at[glob_rows], local.at[s, loc_rows], sem.at[s])
        return make_copy

    @pl.when(i == 0)
    def _():
        local[...] = jnp.zeros_like(local)
        _start_segments(0, glob_ref, loc_ref, cnt_ref, copy_into(0))

    @pl.when(i + 1 < nt)
    def _():
        _start_segments(i + 1, glob_ref, loc_ref, cnt_ref, copy_into(1 - slot))
    _wait_segments(i, total_ref, copy_into(slot))

    j = lax.broadcasted_iota(I32, (TT, LROWS), 1).astype(F32)
    wc = jnp.zeros((TT, LROWS), F32)
    for k in range(TOP_K):
        wc = jnp.where(j == pw_ref[:, k:k + 1], pw_ref[:, TOP_K + k:TOP_K + k + 1], wc)
    acc = x1_ref[...] + _dot(wc.astype(BF16), _unpack_halves(local[slot]))
    o_ref[...] = _rms_norm(acc, gfin_ref[...])


def _combine(seg_glob, seg_loc, seg_cnt, tile_rd, x1, pw, gfin, ys):
    n = x1.shape[0]
    grid_spec = pltpu.PrefetchScalarGridSpec(
        num_scalar_prefetch=4,
        grid=(n // TT,),
        in_specs=[
            pl.BlockSpec((TT, D_MODEL), lambda i, *_: (i, 0)),
            pl.BlockSpec((TT, LANES), lambda i, *_: (i, 0)),
            pl.BlockSpec((1, D_MODEL), lambda i, *_: (0, 0)),
            pl.BlockSpec(memory_space=pl.ANY),
        ],
        out_specs=pl.BlockSpec((TT, D_MODEL), lambda i, *_: (i, 0)),
        scratch_shapes=[pltpu.VMEM((2, LROWS, PACKED), U32),
                        pltpu.SemaphoreType.DMA((2,))],
    )
    return pl.pallas_call(
        _combine_body,
        grid_spec=grid_spec,
        out_shape=jax.ShapeDtypeStruct((n, D_MODEL), F32),
        compiler_params=pltpu.CompilerParams(
            dimension_semantics=("arbitrary",), vmem_limit_bytes=VMEM_LIMIT),
        name="combine",
    )(seg_glob, seg_loc, seg_cnt, tile_rd, x1, pw, gfin, ys)


def _layer(x2, pos_b, invf, norm_mix_g, w_in, b_in, attn_sinks, w_o_attn, b_o_attn,
           w_pool_mix, pool_scale, w_pool_up, w_out, norm_ffn_g, w_router, b_router,
           w_gate, b_gate, w_up, b_up, w_down, b_down, out_g):
    n = x2.shape[0]
    q, kb, vb, u, gates = _inproj(
        x2, pos_b, invf, norm_mix_g[None, :], w_in.astype(BF16), b_in[None, :])
    attn = _attn(attn_sinks, q, kb, vb)

    wrt = w_router.T.astype(BF16)
    brt = jnp.broadcast_to(b_router[:, None], (N_EXPERTS, TM_POST))
    x1, h2, pos_t, pw, tc = _post(
        attn, u, gates, x2, w_o_attn.astype(BF16), b_o_attn[None, :],
        w_pool_mix.astype(BF16), pool_scale[None, :], w_pool_up.astype(BF16),
        w_out.astype(BF16), norm_ffn_g[None, :], wrt, brt)

    nt = n // TT
    sub = TM_POST // TT
    seg_cnt = (tc.reshape(n // TM_POST, N_EXPERTS, LANES)[:, :, :sub]
               .transpose(0, 2, 1).reshape(nt, N_EXPERTS).astype(I32))
    before = jnp.cumsum(seg_cnt, axis=0) - seg_cnt
    sizes = jnp.sum(seg_cnt, axis=0)
    padded = (sizes + TM_MOE - 1) // TM_MOE * TM_MOE
    pends = jnp.cumsum(padded)
    pstarts = pends - padded
    head = before % SEG_ALIGN
    last_tile = (jnp.arange(nt, dtype=I32) == nt - 1)[:, None]
    blocks_up = (head + seg_cnt + SEG_ALIGN - 1) // SEG_ALIGN
    blocks_dn = (head + seg_cnt) // SEG_ALIGN
    seg_rd = jnp.where((seg_cnt > 0) | (last_tile & (head > 0)), blocks_up, 0) * SEG_ALIGN
    seg_wr = jnp.where(last_tile, seg_rd, jnp.where(seg_cnt > 0, blocks_dn * SEG_ALIGN, 0))
    seg_loc = jnp.cumsum(seg_rd, axis=1) - seg_rd
    seg_glob = pstarts[None, :] + before - head
    tile_rd = jnp.sum(seg_rd, axis=1)
    tile_wr = jnp.sum(seg_wr, axis=1)
    seg_glob, seg_loc, seg_rd, seg_wr = (a.reshape(-1) for a in (seg_glob, seg_loc, seg_rd, seg_wr))
    m = n * TOP_K
    n_tiles = (m + N_EXPERTS * (TM_MOE - 1) + TM_MOE - 1) // TM_MOE
    tile_start = jnp.arange(n_tiles, dtype=I32) * TM_MOE
    tile_expert = jnp.minimum(
        jnp.sum(tile_start[:, None] >= pends[None, :], axis=-1), N_EXPERTS - 1).astype(I32)
    own = tile_expert[:, None] == jnp.arange(N_EXPERTS, dtype=I32)[None, :]
    left = jnp.sum(jnp.where(own, (pstarts + sizes)[None, :] - tile_start[:, None], 0), axis=-1)
    tile_rows = jnp.clip(left, 0, TM_MOE)
    ids = jnp.arange(N_EXPERTS, dtype=I32)
    later = jnp.where((sizes > 0)[None, :] & (ids[None, :] > ids[:, None]), ids[None, :], N_EXPERTS)
    next_expert = jnp.min(later, axis=1)
    next_expert = jnp.where(next_expert == N_EXPERTS, -1, next_expert)
    tile_next = jnp.sum(jnp.where(own, next_expert[None, :], 0), axis=-1).astype(I32)
    n_used = (pends[-1] // TM_MOE).astype(I32)[None]

    xs = _dispatch(seg_glob, seg_loc, seg_rd, seg_wr, tile_wr, h2, pos_t, n_tiles * TM_MOE)
    ys = _moe(tile_expert, n_used, tile_rows.astype(I32), tile_next, xs,
              w_gate, b_gate[:, None, :], w_up, b_up[:, None, :], w_down, b_down[:, None, :])
    return _combine(seg_glob, seg_loc, seg_rd, tile_rd, x1, pw, out_g[None, :], ys)


def kernel(x, positions, norm_mix_g, w_in, b_in, attn_sinks, w_o_attn, b_o_attn, w_pool_mix,
           pool_scale, w_pool_up, w_out, norm_ffn_g, w_router, b_router, w_gate, b_gate,
           w_up, b_up, w_down, b_down, norm_final_g):
    b, s, d = x.shape
    depth = w_in.shape[0]
    assert (s, d, depth) == (SEQ, D_MODEL, 1)
    n = b * s
    x2 = x.reshape(n, d)
    pos_b = jnp.repeat(positions.reshape(n // 4, 4), LANES // 4, axis=1)
    inv_freq = ROPE_THETA ** (-jnp.arange(0, HEAD_DIM, 2, dtype=F32) / HEAD_DIM)
    invf = jnp.tile(inv_freq, LANES // (HEAD_DIM // 2))[None, :]
    out = _layer(x2, pos_b, invf, norm_mix_g[0], w_in[0], b_in[0], attn_sinks[0], w_o_attn[0],
                 b_o_attn[0], w_pool_mix[0], pool_scale[0], w_pool_up[0], w_out[0],
                 norm_ffn_g[0], w_router[0], b_router[0], w_gate[0], b_gate[0], w_up[0],
                 b_up[0], w_down[0], b_down[0], norm_final_g)
    return out.reshape(b, s, d)
```

```python
import functools

import numpy as np
import jax
import jax.numpy as jnp
from jax import lax
from jax.experimental import pallas as pl
from jax.experimental.pallas import tpu as pltpu

F32 = jnp.float32
BF16 = jnp.bfloat16
I32 = jnp.int32
U32 = jnp.uint32

D_MODEL = 1024
SEQ = 4096
HEAD_DIM = 64
N_Q_HEADS = 16
WINDOW = 128
ROPE_THETA = 10000.0
Q_WIDTH = N_Q_HEADS * HEAD_DIM
KV_WIDTH = 128
POOL_WINDOWS = (2, 4, 8, 16)
POOL_WIDTH = 512
POOL_GROUP = 128
POOL_HALO = 16
N_EXPERTS = 32
TOP_K = 4
SWIGLU_LIMIT = 7.0
SWIGLU_ALPHA = 1.702
RMS_EPS = 1e-5
NEG_BIG = -1e30
LOG2_E = 1.4426950408889634

LANES = 128
TM_IN = 1024
TQ = 512
TM_POST = 512
TM_MOE = 512
TT = 256
SEG_ALIGN = 8
LROWS = TT * TOP_K + 2 * N_EXPERTS * SEG_ALIGN
PACKED = D_MODEL // 2
VMEM_LIMIT = 56 * 1024 * 1024


def _rms_norm(x, g):
    ms = jnp.mean(x * x, axis=-1, keepdims=True)
    return (x * lax.rsqrt(ms + RMS_EPS)) * g


def _dot(a, b):
    return jnp.dot(a, b, preferred_element_type=F32)


def _inproj_body(x_ref, pos_ref, invf_ref, g_ref, w_ref, b_ref,
                 q_ref, kb_ref, vb_ref, u_ref, gate_ref, cos_scr, sin_scr):
    h = _rms_norm(x_ref[...], g_ref[...]).astype(BF16)

    theta = pos_ref[...].astype(F32) * invf_ref[...]
    lane4 = lax.broadcasted_iota(I32, theta.shape, 1)
    for packed, scr in ((jnp.cos(theta), cos_scr), (jnp.sin(theta), sin_scr)):
        for jt in range(4):
            seg = packed if jt == 0 else pltpu.roll(packed, LANES - 32 * jt, 1)
            seg = jnp.where(lane4 < 32, seg, pltpu.roll(seg, 32, 1))
            seg = jnp.where(lane4 < 64, seg, pltpu.roll(seg, 64, 1))
            scr[pl.ds(jt, TM_IN // 4, stride=4), :] = seg
    cos = cos_scr[...]
    sin = sin_scr[...]
    lane = lax.broadcasted_iota(I32, cos.shape, 1)
    first_half = (lane & 32) == 0
    sin_signed = jnp.where(first_half, -sin, sin)
    low_head = lane < HEAD_DIM

    def rope(t):
        swapped = jnp.where(first_half, pltpu.roll(t, 96, 1), pltpu.roll(t, 32, 1))
        return t * cos + swapped * sin_signed

    def band_layout(t, out_ref):
        tr = pltpu.roll(t, 64, 1)
        zero = jnp.zeros_like(t)
        chunks = (jnp.where(low_head, t, zero), jnp.where(low_head, zero, tr),
                  jnp.where(low_head, tr, zero), jnp.where(low_head, zero, t))
        for c, val in enumerate(chunks):
            out_ref[:, c * LANES:(c + 1) * LANES] = val.astype(BF16)

    qk_w = Q_WIDTH + KV_WIDTH
    zqk = _dot(h, w_ref[:, :qk_w]) + b_ref[:, :qk_w]
    scale = HEAD_DIM ** -0.5 * LOG2_E
    for j in range(Q_WIDTH // LANES):
        sl = slice(j * LANES, (j + 1) * LANES)
        q_ref[:, sl] = (rope(zqk[:, sl]) * scale).astype(BF16)
    band_layout(rope(zqk[:, Q_WIDTH:qk_w]), kb_ref)

    v0 = qk_w
    zv = _dot(h, w_ref[:, v0:v0 + KV_WIDTH]) + b_ref[:, v0:v0 + KV_WIDTH]
    band_layout(zv, vb_ref)

    u0 = v0 + KV_WIDTH
    u_ref[...] = _dot(h, w_ref[:, u0:u0 + POOL_WIDTH]) + b_ref[:, u0:u0 + POOL_WIDTH]

    g0 = u0 + POOL_WIDTH
    for c in range(2):
        sl = slice(g0 + c * D_MODEL, g0 + (c + 1) * D_MODEL)
        zg = _dot(h, w_ref[:, sl]) + b_ref[:, sl]
        gate_ref[:, c * D_MODEL:(c + 1) * D_MODEL] = jax.nn.sigmoid(zg).astype(BF16)


def _inproj(x2, pos_b, invf, g, w, b):
    n = x2.shape[0]
    in_width = w.shape[1]
    row = lambda i: (i, 0)
    const = lambda i: (0, 0)
    return pl.pallas_call(
        _inproj_body,
        grid=(n // TM_IN,),
        in_specs=[
            pl.BlockSpec((TM_IN, D_MODEL), row),
            pl.BlockSpec((TM_IN // 4, LANES), row),
            pl.BlockSpec((1, LANES), const),
            pl.BlockSpec((1, D_MODEL), const),
            pl.BlockSpec((D_MODEL, in_width), const),
            pl.BlockSpec((1, in_width), const),
        ],
        out_specs=[
            pl.BlockSpec((TM_IN, Q_WIDTH), row),
            pl.BlockSpec((TM_IN, 4 * LANES), row),
            pl.BlockSpec((TM_IN, 4 * LANES), row),
            pl.BlockSpec((TM_IN, POOL_WIDTH), row),
            pl.BlockSpec((TM_IN, 2 * D_MODEL), row),
        ],
        out_shape=[
            jax.ShapeDtypeStruct((n, Q_WIDTH), BF16),
            jax.ShapeDtypeStruct((n, 4 * LANES), BF16),
            jax.ShapeDtypeStruct((n, 4 * LANES), BF16),
            jax.ShapeDtypeStruct((n, POOL_WIDTH), F32),
            jax.ShapeDtypeStruct((n, 2 * D_MODEL), BF16),
        ],
        scratch_shapes=[pltpu.VMEM((TM_IN, LANES), F32), pltpu.VMEM((TM_IN, LANES), F32)],
        compiler_params=pltpu.CompilerParams(
            dimension_semantics=("arbitrary",), vmem_limit_bytes=VMEM_LIMIT),
        name="inproj",
    )(x2, pos_b, invf, g, w, b)


def _attn_body(sinks_ref, q_ref, kbc_ref, kbp_ref, vbc_ref, vbp_ref, bias_ref,
               o_ref, p_scr):
    t = pl.program_id(0)
    seq_first = (t % (SEQ // TQ)) == 0
    lane = lax.broadcasted_iota(I32, (WINDOW, LANES), 1)
    low_head = lane < HEAD_DIM
    for n in range(TQ // WINDOW):
        rows = slice(n * WINDOW, (n + 1) * WINDOW)
        if n == 0:
            k_prev, v_prev = kbp_ref[...], vbp_ref[...]
            bias = jnp.where(seq_first, bias_ref[1], bias_ref[0])
        else:
            prev_rows = slice((n - 1) * WINDOW, n * WINDOW)
            k_prev, v_prev = kbc_ref[prev_rows, :], vbc_ref[prev_rows, :]
            bias = bias_ref[0]
        k_cur, v_cur = kbc_ref[rows, :], vbc_ref[rows, :]
        for g in range(2):
            def stack(prev, cur):
                lo = slice((2 * g) * LANES, (2 * g + 1) * LANES)
                hi = slice((2 * g + 1) * LANES, (2 * g + 2) * LANES)
                return jnp.concatenate([prev[:, lo], cur[:, lo], prev[:, hi], cur[:, hi]], axis=0)
            kmat = stack(k_prev, k_cur)
            vmat = stack(v_prev, v_cur)
            qs = jnp.concatenate(
                [q_ref[rows, (4 * g + p) * LANES:(4 * g + p + 1) * LANES] for p in range(4)],
                axis=0)
            s = lax.dot_general(qs, kmat, (((1,), (1,)), ((), ())),
                                preferred_element_type=F32) + bias
            inv = []
            for p in range(4):
                pr = slice(p * WINDOW, (p + 1) * WINDOW)
                ms, sums = [], []
                for j in range(2):
                    cols = slice(j * 2 * WINDOW, (j + 1) * 2 * WINDOW)
                    sp = s[pr, cols]
                    sink = sinks_ref[8 * g + 2 * p + j] * LOG2_E
                    m = jnp.maximum(jnp.max(sp, axis=-1, keepdims=True), sink)
                    e = jnp.exp2(sp - m)
                    p_scr[pr, cols] = e.astype(BF16)
                    ms.append(m)
                    sums.append(jnp.sum(e, axis=-1, keepdims=True))
                sink_pair = jnp.where(low_head, sinks_ref[8 * g + 2 * p] * LOG2_E,
                                      sinks_ref[8 * g + 2 * p + 1] * LOG2_E)
                m_pair = jnp.where(low_head, ms[0], ms[1])
                den = jnp.where(low_head, sums[0], sums[1]) + jnp.exp2(sink_pair - m_pair)
                inv.append(1.0 / den)
            o = _dot(p_scr[...], vmat)
            for p in range(4):
                pr = slice(p * WINDOW, (p + 1) * WINDOW)
                o_ref[rows, (4 * g + p) * LANES:(4 * g + p + 1) * LANES] = (o[pr, :] * inv[p]).astype(BF16)


def _attn_bias():
    r = np.arange(4 * WINDOW)[:, None] % WINDOW
    c = np.arange(4 * WINDOW)[None, :] % (2 * WINDOW)
    band = (c > r) & (c <= r + WINDOW)
    first = band & (c >= WINDOW)
    return np.stack([np.where(band, 0.0, NEG_BIG), np.where(first, 0.0, NEG_BIG)]).astype(np.float32)


def _attn(sinks, q, kb, vb):
    n = q.shape[0]
    blocks_per_tile = TQ // WINDOW
    cur = lambda t: (t, 0)
    prev = lambda t: (jnp.maximum(t * blocks_per_tile - 1, 0), 0)
    bias = jnp.asarray(_attn_bias())
    return pl.pallas_call(
        _attn_body,
        grid=(n // TQ,),
        in_specs=[
            pl.BlockSpec(memory_space=pltpu.SMEM),
            pl.BlockSpec((TQ, Q_WIDTH), cur),
            pl.BlockSpec((TQ, 4 * LANES), cur),
            pl.BlockSpec((WINDOW, 4 * LANES), prev),
            pl.BlockSpec((TQ, 4 * LANES), cur),
            pl.BlockSpec((WINDOW, 4 * LANES), prev),
            pl.BlockSpec((2, 4 * WINDOW, 4 * WINDOW), lambda t: (0, 0, 0)),
        ],
        out_specs=pl.BlockSpec((TQ, Q_WIDTH), cur),
        out_shape=jax.ShapeDtypeStruct((n, Q_WIDTH), BF16),
        scratch_shapes=[pltpu.VMEM((4 * WINDOW, 4 * WINDOW), BF16)],
        compiler_params=pltpu.CompilerParams(
            dimension_semantics=("arbitrary",), vmem_limit_bytes=VMEM_LIMIT),
        name="attn",
    )(sinks, q, kb, kb, vb, vb, bias)


def _post_body(attn_ref, u_ref, uprev_ref, gate_ref, x_ref,
               wo_ref, bo_ref, wmix_ref, pscale_ref, wup_ref, wout_ref,
               gffn_ref, wrt_ref, brt_ref,
               x1_ref, h2_ref, pos_t_ref, pw_ref, tc_ref,
               ubuf, carry):
    i = pl.program_id(0)
    tiles_per_seq = SEQ // TM_POST
    seq_first = (i % tiles_per_seq) == 0

    @pl.when(i == 0)
    def _():
        carry[...] = jnp.zeros_like(carry)

    y_attn = _dot(attn_ref[...], wo_ref[...]) + bo_ref[...]

    ubuf[0:POOL_HALO, :] = jnp.where(seq_first, 0.0, uprev_ref[...])
    ubuf[POOL_HALO:, :] = u_ref[...]
    row = lax.broadcasted_iota(I32, (TM_POST, 1), 0)
    tpos = (i % tiles_per_seq) * TM_POST + row
    mixed = []
    for gi, w in enumerate(POOL_WINDOWS):
        cols = slice(gi * POOL_GROUP, (gi + 1) * POOL_GROUP)
        acc = ubuf[POOL_HALO:, cols]
        for d in range(1, w):
            acc = acc + ubuf[POOL_HALO - d:POOL_HALO - d + TM_POST, cols]
        cnt = jnp.minimum(tpos + 1, w).astype(F32)
        pooled = acc / cnt - ubuf[POOL_HALO:, cols]
        mixed.append(_dot(pooled.astype(BF16), wmix_ref[gi]) * pscale_ref[:, cols])
    mixed = jnp.concatenate(mixed, axis=-1).astype(BF16)
    y_pool = _dot(mixed, wup_ref[...])
    merged = (gate_ref[:, :D_MODEL].astype(F32) * y_attn
              + gate_ref[:, D_MODEL:].astype(F32) * y_pool)
    x1 = x_ref[...] + _dot(merged.astype(BF16), wout_ref[...])
    x1_ref[...] = x1
    h2 = _rms_norm(x1, gffn_ref[...])
    h2_bf = h2.astype(BF16)
    h2_ref[...] = h2_bf

    logits_t = lax.dot_general(wrt_ref[...], h2_bf, (((1,), (1,)), ((), ())),
                               preferred_element_type=F32) + brt_ref[...]
    erow = lax.broadcasted_iota(I32, (N_EXPERTS, TM_POST), 0)
    work = logits_t
    top_v, onehots = [], []
    for _ in range(TOP_K):
        m = jnp.max(work, axis=0, keepdims=True)
        idx = jnp.min(jnp.where(work == m, erow, N_EXPERTS), axis=0, keepdims=True)
        hit = erow == idx
        top_v.append(m)
        onehots.append(hit)
        work = jnp.where(hit, -jnp.inf, work)
    exps = [jnp.exp(v - top_v[0]) for v in top_v]
    denom = exps[0] + exps[1] + exps[2] + exps[3]
    comb = [e / denom for e in exps]
    chosen = jnp.zeros((N_EXPERTS, TM_POST), F32)
    for hit in onehots:
        chosen = chosen + hit.astype(F32)

    e_r = lax.broadcasted_iota(I32, (N_EXPERTS, N_EXPERTS), 0)
    e_c = lax.broadcasted_iota(I32, (N_EXPERTS, N_EXPERTS), 1)
    earlier_expert = (e_c < e_r).astype(BF16)
    t_r = lax.broadcasted_iota(I32, (TT, TT), 0)
    t_c = lax.broadcasted_iota(I32, (TT, TT), 1)
    earlier_token = (t_r < t_c).astype(BF16)
    lane = lax.broadcasted_iota(I32, (N_EXPERTS, LANES), 1)
    counts = jnp.zeros((N_EXPERTS, LANES), F32)
    pos_parts = [[] for _ in range(TOP_K)]
    n_sub = TM_POST // TT
    for s in range(n_sub):
        cols = slice(s * TT, (s + 1) * TT)
        ch = chosen[:, cols]
        cnt = jnp.broadcast_to(jnp.sum(ch, axis=1, keepdims=True), (N_EXPERTS, LANES))
        before = carry[...]
        head = before - SEG_ALIGN * jnp.floor(before / SEG_ALIGN)
        flush = jnp.logical_and(i == pl.num_programs(0) - 1, s == n_sub - 1)
        present = (cnt > 0) | (flush & (head > 0))
        blocks = jnp.where(present, jnp.floor((head + cnt + (SEG_ALIGN - 1)) / SEG_ALIGN), 0.0)
        seg_loc = SEG_ALIGN * _dot(earlier_expert, blocks.astype(BF16))
        base = seg_loc + head
        rank_in_tile = _dot(ch.astype(BF16), earlier_token)
        full = jnp.concatenate([base] * (TT // LANES), axis=1) + rank_in_tile
        for k in range(TOP_K):
            pos_parts[k].append(
                jnp.sum(jnp.where(onehots[k][:, cols], full, 0.0), axis=0, keepdims=True))
        counts = jnp.where(lane == s, cnt, counts)
        carry[...] = before + cnt
    tc_ref[...] = counts

    pos = [jnp.concatenate(parts, axis=1) for parts in pos_parts]
    row8 = lax.broadcasted_iota(I32, (8, TM_POST), 0)
    pos_rows = jnp.full((8, TM_POST), -1.0, F32)
    both = jnp.zeros((8, TM_POST), F32)
    for k in range(TOP_K):
        pos_rows = jnp.where(row8 == k, pos[k], pos_rows)
        both = jnp.where(row8 == k, pos[k], both)
        both = jnp.where(row8 == TOP_K + k, comb[k], both)
    pos_t_ref[...] = pos_rows.astype(I32)
    padded = jnp.concatenate([both, jnp.zeros((LANES - 8, TM_POST), F32)], axis=0)
    pw_ref[...] = padded.T


def _post(attn, u, gates, x2, wo, bo, wmix, pscale, wup, wout, gffn, wr, br):
    n = x2.shape[0]
    row = lambda i: (i, 0)
    const = lambda i: (0, 0)
    halo_blocks = TM_POST // POOL_HALO
    prev = lambda i: (jnp.maximum(i * halo_blocks - 1, 0), 0)
    return pl.pallas_call(
        _post_body,
        grid=(n // TM_POST,),
        in_specs=[
            pl.BlockSpec((TM_POST, Q_WIDTH), row),
            pl.BlockSpec((TM_POST, POOL_WIDTH), row),
            pl.BlockSpec((POOL_HALO, POOL_WIDTH), prev),
            pl.BlockSpec((TM_POST, 2 * D_MODEL), row),
            pl.BlockSpec((TM_POST, D_MODEL), row),
            pl.BlockSpec((Q_WIDTH, D_MODEL), const),
            pl.BlockSpec((1, D_MODEL), const),
            pl.BlockSpec((len(POOL_WINDOWS), POOL_GROUP, POOL_GROUP), lambda i: (0, 0, 0)),
            pl.BlockSpec((1, POOL_WIDTH), const),
            pl.BlockSpec((POOL_WIDTH, D_MODEL), const),
            pl.BlockSpec((D_MODEL, D_MODEL), const),
            pl.BlockSpec((1, D_MODEL), const),
            pl.BlockSpec((N_EXPERTS, D_MODEL), const),
            pl.BlockSpec((N_EXPERTS, TM_POST), const),
        ],
        out_specs=[
            pl.BlockSpec((TM_POST, D_MODEL), row),
            pl.BlockSpec((TM_POST, D_MODEL), row),
            pl.BlockSpec((8, TM_POST), lambda i: (0, i)),
            pl.BlockSpec((TM_POST, LANES), row),
            pl.BlockSpec((N_EXPERTS, LANES), row),
        ],
        out_shape=[
            jax.ShapeDtypeStruct((n, D_MODEL), F32),
            jax.ShapeDtypeStruct((n, D_MODEL), BF16),
            jax.ShapeDtypeStruct((8, n), I32),
            jax.ShapeDtypeStruct((n, LANES), F32),
            jax.ShapeDtypeStruct((n // TM_POST * N_EXPERTS, LANES), F32),
        ],
        scratch_shapes=[pltpu.VMEM((TM_POST + POOL_HALO, POOL_WIDTH), F32),
                        pltpu.VMEM((N_EXPERTS, LANES), F32)],
        compiler_params=pltpu.CompilerParams(
            dimension_semantics=("arbitrary",), vmem_limit_bytes=VMEM_LIMIT),
        name="post",
    )(attn, u, u, gates, x2, wo, bo, wmix, pscale, wup, wout, gffn, wr, br)


def _start_segments(i, glob_ref, loc_ref, cnt_ref, make_copy):
    def body(e, c):
        idx = i * N_EXPERTS + e
        rows = pl.multiple_of(cnt_ref[idx], SEG_ALIGN)

        @pl.when(rows > 0)
        def _():
            loc = pl.multiple_of(loc_ref[idx], SEG_ALIGN)
            glob = pl.multiple_of(glob_ref[idx], SEG_ALIGN)
            make_copy(pl.ds(loc, rows), pl.ds(glob, rows)).start()
        return c
    lax.fori_loop(0, N_EXPERTS, body, 0, unroll=4)


def _wait_segments(i, total_ref, make_copy):
    rows = pl.multiple_of(total_ref[i], SEG_ALIGN)

    @pl.when(rows > 0)
    def _():
        make_copy(pl.ds(0, rows), pl.ds(0, rows)).wait()


def _pack_halves(v):
    c = v.shape[1] // 2
    bits = lax.bitcast_convert_type(v, U32)
    return (bits[:, :c] >> 16) | bits[:, c:]


def _unpack_halves(w):
    lo = lax.bitcast_convert_type(w << 16, F32)
    hi = lax.bitcast_convert_type(w & jnp.uint32(0xFFFF0000), F32)
    return jnp.concatenate([lo, hi], axis=1).astype(BF16)


def _dispatch_body(glob_ref, loc_ref, rd_ref, wr_ref, total_ref, h2_ref, pos_ref, xs_hbm,
                   local, tail, sem):
    i = pl.program_id(0)
    nt = pl.num_programs(0)
    slot = i % 2

    def copy_from(s):
        def make_copy(loc_rows, glob_rows):
            return pltpu.make_async_copy(local.at[s, loc_rows], xs_hbm.at[glob_rows], sem.at[s])
        return make_copy

    @pl.when(i == 0)
    def _():
        tail[...] = jnp.zeros_like(tail)

    @pl.when(i >= 2)
    def _():
        _wait_segments(i - 2, total_ref, copy_from(slot))

    j = lax.broadcasted_iota(I32, (LROWS, TT), 0)
    hit = j == pos_ref[0:1, :]
    for k in range(1, TOP_K):
        hit = hit | (j == pos_ref[k:k + 1, :])
    perm = jnp.where(hit, 1.0, 0.0).astype(BF16)
    local[slot] = _pack_halves(_dot(perm, h2_ref[...]))

    for e in range(N_EXPERTS):
        idx = i * N_EXPERTS + e
        rd = rd_ref[idx]
        wr = wr_ref[idx]
        loc = pl.multiple_of(loc_ref[idx], SEG_ALIGN)
        first = local[slot, pl.ds(loc, SEG_ALIGN), :]
        last = local[slot, pl.ds(pl.multiple_of(loc + wr, SEG_ALIGN), SEG_ALIGN), :]
        merged = jnp.where(rd > 0, first | tail[e], first)
        local[slot, pl.ds(loc, SEG_ALIGN), :] = merged
        last = jnp.where(wr == 0, merged, last)
        tail[e] = jnp.where(rd > 0, jnp.where(wr < rd, last, jnp.uint32(0)), tail[e])
    _start_segments(i, glob_ref, loc_ref, wr_ref, copy_from(slot))

    @pl.when(i == nt - 1)
    def _():
        _wait_segments(i - 1, total_ref, copy_from(1 - slot))
        _wait_segments(i, total_ref, copy_from(slot))


def _dispatch(seg_glob, seg_loc, seg_rd, seg_wr, tile_wr, h2, pos_t, n_rows):
    n = h2.shape[0]
    grid_spec = pltpu.PrefetchScalarGridSpec(
        num_scalar_prefetch=5,
        grid=(n // TT,),
        in_specs=[
            pl.BlockSpec((TT, D_MODEL), lambda i, *_: (i, 0)),
            pl.BlockSpec((8, TT), lambda i, *_: (0, i)),
        ],
        out_specs=pl.BlockSpec(memory_space=pl.ANY),
        scratch_shapes=[pltpu.VMEM((2, LROWS, PACKED), U32),
                        pltpu.VMEM((N_EXPERTS, SEG_ALIGN, PACKED), U32),
                        pltpu.SemaphoreType.DMA((2,))],
    )
    return pl.pallas_call(
        _dispatch_body,
        grid_spec=grid_spec,
        out_shape=jax.ShapeDtypeStruct((n_rows, PACKED), U32),
        compiler_params=pltpu.CompilerParams(
            dimension_semantics=("arbitrary",), vmem_limit_bytes=VMEM_LIMIT),
        name="dispatch",
    )(seg_glob, seg_loc, seg_rd, seg_wr, tile_wr, h2, pos_t)


def _moe_body(te_ref, nused_ref, rows_ref, next_ref, xs_ref, bias_ref,
              wg_hbm, wu_hbm, wd_hbm, y_ref, wf32, wbf, sem, slot_ref):
    i = pl.program_id(0)

    def fetch(expert, slot):
        return [pltpu.make_async_copy(w.at[expert], wf32.at[slot, m], sem.at[slot, m])
                for m, w in enumerate((wg_hbm, wu_hbm, wd_hbm))]

    @pl.when(i == 0)
    def _():
        slot_ref[0] = 0

    @pl.when(i < nused_ref[0])
    def _():
        expert = te_ref[i]

        @pl.when((i == 0) | (expert != te_ref[jnp.maximum(i - 1, 0)]))
        def _():
            slot = slot_ref[0]

            @pl.when(i == 0)
            def _():
                for cp in fetch(expert, slot):
                    cp.start()
            for cp in fetch(expert, slot):
                cp.wait()
            nxt = next_ref[i]

            @pl.when(nxt >= 0)
            def _():
                for cp in fetch(nxt, 1 - slot):
                    cp.start()
            for m in range(3):
                wbf[m] = wf32[slot, m].astype(BF16)
            slot_ref[0] = 1 - slot

        row = lax.broadcasted_iota(I32, (TM_MOE, 1), 0)
        x = _unpack_halves(jnp.where(row < rows_ref[i], xs_ref[...], jnp.uint32(0)))
        bias = bias_ref[expert]
        g = _dot(x, wbf[0]) + bias[0:1, :]
        u = _dot(x, wbf[1]) + bias[1:2, :]
        g = jnp.minimum(g, SWIGLU_LIMIT)
        u = jnp.clip(u, -SWIGLU_LIMIT, SWIGLU_LIMIT)
        a = (g * jax.nn.sigmoid(SWIGLU_ALPHA * g) * (u + 1.0)).astype(BF16)
        y = _dot(a, wbf[2]) + bias[2:3, :]
        y_ref[...] = _pack_halves(y.astype(BF16).astype(F32))


def _moe(tile_expert, n_used, tile_rows, tile_next, xs, wg, bg, wu, bu, wd, bd):
    n_rows = xs.shape[0]
    n_tiles = n_rows // TM_MOE
    d_ff = wg.shape[2]
    assert d_ff == D_MODEL
    row = lambda i, te, nu, *_: (jnp.minimum(i, nu[0] - 1), 0)
    bias = jnp.zeros((N_EXPERTS, 8, D_MODEL), F32)
    bias = bias.at[:, 0].set(bg).at[:, 1].set(bu).at[:, 2].set(bd)
    grid_spec = pltpu.PrefetchScalarGridSpec(
        num_scalar_prefetch=4,
        grid=(n_tiles,),
        in_specs=[
            pl.BlockSpec((TM_MOE, PACKED), row),
            pl.BlockSpec((N_EXPERTS, 8, D_MODEL), lambda i, *_: (0, 0, 0)),
            pl.BlockSpec(memory_space=pl.ANY),
            pl.BlockSpec(memory_space=pl.ANY),
            pl.BlockSpec(memory_space=pl.ANY),
        ],
        out_specs=pl.BlockSpec((TM_MOE, PACKED), row),
        scratch_shapes=[pltpu.VMEM((2, 3, D_MODEL, D_MODEL), F32),
                        pltpu.VMEM((3, D_MODEL, D_MODEL), BF16),
                        pltpu.SemaphoreType.DMA((2, 3)),
                        pltpu.SMEM((1,), I32)],
    )
    return pl.pallas_call(
        _moe_body,
        grid_spec=grid_spec,
        out_shape=jax.ShapeDtypeStruct((n_rows, PACKED), U32),
        compiler_params=pltpu.CompilerParams(
            dimension_semantics=("arbitrary",), vmem_limit_bytes=VMEM_LIMIT),
        name="moe",
    )(tile_expert, n_used, tile_rows, tile_next, xs, bias, wg, wu, wd)


def _combine_body(glob_ref, loc_ref, cnt_ref, total_ref, x1_ref, pw_ref, gfin_ref,
                  ys_hbm, o_ref, local, sem):
    i = pl.program_id(0)
    nt = pl.num_programs(0)
    slot = i % 2

    def copy_into(s):
        def make_copy(loc_rows, glob_rows):
            return pltpu.make_async_copy(ys_hbm.at[glob_rows], local.at[s, loc_rows], sem.at[s])
        return make_copy

    @pl.when(i == 0)
    def _():
        local[...] = jnp.zeros_like(local)
        _start_segments(0, glob_ref, loc_ref, cnt_ref, copy_into(0))

    @pl.when(i + 1 < nt)
    def _():
        _start_segments(i + 1, glob_ref, loc_ref, cnt_ref, copy_into(1 - slot))
    _wait_segments(i, total_ref, copy_into(slot))

    j = lax.broadcasted_iota(I32, (TT, LROWS), 1).astype(F32)
    wc = jnp.zeros((TT, LROWS), F32)
    for k in range(TOP_K):
        wc = jnp.where(j == pw_ref[:, k:k + 1], pw_ref[:, TOP_K + k:TOP_K + k + 1], wc)
    acc = x1_ref[...] + _dot(wc.astype(BF16), _unpack_halves(local[slot]))
    o_ref[...] = _rms_norm(acc, gfin_ref[...])


def _combine(seg_glob, seg_loc, seg_cnt, tile_rd, x1, pw, gfin, ys):
    n = x1.shape[0]
    grid_spec = pltpu.PrefetchScalarGridSpec(
        num_scalar_prefetch=4,
        grid=(n // TT,),
        in_specs=[
            pl.BlockSpec((TT, D_MODEL), lambda i, *_: (i, 0)),
            pl.BlockSpec((TT, LANES), lambda i, *_: (i, 0)),
            pl.BlockSpec((1, D_MODEL), lambda i, *_: (0, 0)),
            pl.BlockSpec(memory_space=pl.ANY),
        ],
        out_specs=pl.BlockSpec((TT, D_MODEL), lambda i, *_: (i, 0)),
        scratch_shapes=[pltpu.VMEM((2, LROWS, PACKED), U32),
                        pltpu.SemaphoreType.DMA((2,))],
    )
    return pl.pallas_call(
        _combine_body,
        grid_spec=grid_spec,
        out_shape=jax.ShapeDtypeStruct((n, D_MODEL), F32),
        compiler_params=pltpu.CompilerParams(
            dimension_semantics=("arbitrary",), vmem_limit_bytes=VMEM_LIMIT),
        name="combine",
    )(seg_glob, seg_loc, seg_cnt, tile_rd, x1, pw, gfin, ys)


def _layer(x2, pos_b, invf, norm_mix_g, w_in, b_in, attn_sinks, w_o_attn, b_o_attn,
           w_pool_mix, pool_scale, w_pool_up, w_out, norm_ffn_g, w_router, b_router,
           w_gate, b_gate, w_up, b_up, w_down, b_down, out_g):
    n = x2.shape[0]
    q, kb, vb, u, gates = _inproj(
        x2, pos_b, invf, norm_mix_g[None, :], w_in.astype(BF16), b_in[None, :])
    attn = _attn(attn_sinks, q, kb, vb)

    wrt = w_router.T.astype(BF16)
    brt = jnp.broadcast_to(b_router[:, None], (N_EXPERTS, TM_POST))
    x1, h2, pos_t, pw, tc = _post(
        attn, u, gates, x2, w_o_attn.astype(BF16), b_o_attn[None, :],
        w_pool_mix.astype(BF16), pool_scale[None, :], w_pool_up.astype(BF16),
        w_out.astype(BF16), norm_ffn_g[None, :], wrt, brt)

    nt = n // TT
    sub = TM_POST // TT
    seg_cnt = (tc.reshape(n // TM_POST, N_EXPERTS, LANES)[:, :, :sub]
               .transpose(0, 2, 1).reshape(nt, N_EXPERTS).astype(I32))
    before = jnp.cumsum(seg_cnt, axis=0) - seg_cnt
    sizes = jnp.sum(seg_cnt, axis=0)
    padded = (sizes + TM_MOE - 1) // TM_MOE * TM_MOE
    pends = jnp.cumsum(padded)
    pstarts = pends - padded
    head = before % SEG_ALIGN
    last_tile = (jnp.arange(nt, dtype=I32) == nt - 1)[:, None]
    blocks_up = (head + seg_cnt + SEG_ALIGN - 1) // SEG_ALIGN
    blocks_dn = (head + seg_cnt) // SEG_ALIGN
    seg_rd = jnp.where((seg_cnt > 0) | (last_tile & (head > 0)), blocks_up, 0) * SEG_ALIGN
    seg_wr = jnp.where(last_tile, seg_rd, jnp.where(seg_cnt > 0, blocks_dn * SEG_ALIGN, 0))
    seg_loc = jnp.cumsum(seg_rd, axis=1) - seg_rd
    seg_glob = pstarts[None, :] + before - head
    tile_rd = jnp.sum(seg_rd, axis=1)
    tile_wr = jnp.sum(seg_wr, axis=1)
    seg_glob, seg_loc, seg_rd, seg_wr = (a.reshape(-1) for a in (seg_glob, seg_loc, seg_rd, seg_wr))
    m = n * TOP_K
    n_tiles = (m + N_EXPERTS * (TM_MOE - 1) + TM_MOE - 1) // TM_MOE
    tile_start = jnp.arange(n_tiles, dtype=I32) * TM_MOE
    tile_expert = jnp.minimum(
        jnp.sum(tile_start[:, None] >= pends[None, :], axis=-1), N_EXPERTS - 1).astype(I32)
    own = tile_expert[:, None] == jnp.arange(N_EXPERTS, dtype=I32)[None, :]
    left = jnp.sum(jnp.where(own, (pstarts + sizes)[None, :] - tile_start[:, None], 0), axis=-1)
    tile_rows = jnp.clip(left, 0, TM_MOE)
    ids = jnp.arange(N_EXPERTS, dtype=I32)
    later = jnp.where((sizes > 0)[None, :] & (ids[None, :] > ids[:, None]), ids[None, :], N_EXPERTS)
    next_expert = jnp.min(later, axis=1)
    next_expert = jnp.where(next_expert == N_EXPERTS, -1, next_expert)
    tile_next = jnp.sum(jnp.where(own, next_expert[None, :], 0), axis=-1).astype(I32)
    n_used = (pends[-1] // TM_MOE).astype(I32)[None]

    xs = _dispatch(seg_glob, seg_loc, seg_rd, seg_wr, tile_wr, h2, pos_t, n_tiles * TM_MOE)
    ys = _moe(tile_expert, n_used, tile_rows.astype(I32), tile_next, xs,
              w_gate, b_gate, w_up, b_up, w_down, b_down)
    return _combine(seg_glob, seg_loc, seg_rd, tile_rd, x1, pw, out_g[None, :], ys)


def kernel(x, positions, norm_mix_g, w_in, b_in, attn_sinks, w_o_attn, b_o_attn, w_pool_mix,
           pool_scale, w_pool_up, w_out, norm_ffn_g, w_router, b_router, w_gate, b_gate,
           w_up, b_up, w_down, b_down, norm_final_g):
    b, s, d = x.shape
    depth = w_in.shape[0]
    assert (s, d, depth) == (SEQ, D_MODEL, 1)
    n = b * s
    x2 = x.reshape(n, d)
    pos_b = jnp.repeat(positions.reshape(n // 4, 4), LANES // 4, axis=1)
    inv_freq = ROPE_THETA ** (-jnp.arange(0, HEAD_DIM, 2, dtype=F32) / HEAD_DIM)
    invf = jnp.tile(inv_freq, LANES // (HEAD_DIM // 2))[None, :]
    out = _layer(x2, pos_b, invf, norm_mix_g[0], w_in[0], b_in[0], attn_sinks[0], w_o_attn[0],
                 b_o_attn[0], w_pool_mix[0], pool_scale[0], w_pool_up[0], w_out[0],
                 norm_ffn_g[0], w_router[0], b_router[0], w_gate[0], b_gate[0], w_up[0],
                 b_up[0], w_down[0], b_down[0], norm_final_g)
    return out.reshape(b, s, d)
```

```python
import functools

import numpy as np
import jax
import jax.numpy as jnp
from jax import lax
from jax.experimental import pallas as pl
from jax.experimental.pallas import tpu as pltpu

F32 = jnp.float32
BF16 = jnp.bfloat16
I32 = jnp.int32
U32 = jnp.uint32

D_MODEL = 1024
SEQ = 4096
HEAD_DIM = 64
N_Q_HEADS = 16
WINDOW = 128
ROPE_THETA = 10000.0
Q_WIDTH = N_Q_HEADS * HEAD_DIM
KV_WIDTH = 128
POOL_WINDOWS = (2, 4, 8, 16)
POOL_WIDTH = 512
POOL_GROUP = 128
POOL_HALO = 16
POOL_PAD = 32
N_EXPERTS = 32
TOP_K = 4
SWIGLU_LIMIT = 7.0
SWIGLU_ALPHA = 1.702
RMS_EPS = 1e-5
NEG_BIG = -1e30
LOG2_E = 1.4426950408889634

LANES = 128
TM_IN = 1024
TQ = 512
TM_POST = 512
TM_MOE = 512
TT = 256
SEG_ALIGN = 8
LROWS = TT * TOP_K + 2 * N_EXPERTS * SEG_ALIGN
LROWS_MAIN = LROWS - 256
PACKED = D_MODEL // 2
VMEM_LIMIT = 56 * 1024 * 1024


def _rms_norm(x, g):
    ms = jnp.mean(x * x, axis=-1, keepdims=True)
    return (x * lax.rsqrt(ms + RMS_EPS)) * g


def _dot(a, b):
    return jnp.dot(a, b, preferred_element_type=F32)


def _inproj_body(x_ref, pos_ref, invf_ref, g_ref, w_ref, b_ref,
                 q_ref, kb_ref, vb_ref, u_ref, gate_ref, cos_scr, sin_scr):
    h = _rms_norm(x_ref[...], g_ref[...]).astype(BF16)

    theta = pos_ref[...].astype(F32) * invf_ref[...]
    lane4 = lax.broadcasted_iota(I32, theta.shape, 1)
    for packed, scr in ((jnp.cos(theta), cos_scr), (jnp.sin(theta), sin_scr)):
        for jt in range(4):
            seg = packed if jt == 0 else pltpu.roll(packed, LANES - 32 * jt, 1)
            seg = jnp.where(lane4 < 32, seg, pltpu.roll(seg, 32, 1))
            seg = jnp.where(lane4 < 64, seg, pltpu.roll(seg, 64, 1))
            scr[pl.ds(jt, TM_IN // 4, stride=4), :] = seg
    cos = cos_scr[...]
    sin = sin_scr[...]
    lane = lax.broadcasted_iota(I32, cos.shape, 1)
    first_half = (lane & 32) == 0
    sin_signed = jnp.where(first_half, -sin, sin)
    low_head = lane < HEAD_DIM

    def rope(t):
        swapped = jnp.where(first_half, pltpu.roll(t, 96, 1), pltpu.roll(t, 32, 1))
        return t * cos + swapped * sin_signed

    def band_layout(t, out_ref):
        tr = pltpu.roll(t, 64, 1)
        zero = jnp.zeros_like(t)
        chunks = (jnp.where(low_head, t, zero), jnp.where(low_head, zero, tr),
                  jnp.where(low_head, tr, zero), jnp.where(low_head, zero, t))
        for c, val in enumerate(chunks):
            out_ref[:, c * LANES:(c + 1) * LANES] = val.astype(BF16)

    qk_w = Q_WIDTH + KV_WIDTH
    zqk = _dot(h, w_ref[:, :qk_w]) + b_ref[:, :qk_w]
    scale = HEAD_DIM ** -0.5 * LOG2_E
    for j in range(Q_WIDTH // LANES):
        sl = slice(j * LANES, (j + 1) * LANES)
        q_ref[:, sl] = (rope(zqk[:, sl]) * scale).astype(BF16)
    band_layout(rope(zqk[:, Q_WIDTH:qk_w]), kb_ref)

    v0 = qk_w
    zv = _dot(h, w_ref[:, v0:v0 + KV_WIDTH]) + b_ref[:, v0:v0 + KV_WIDTH]
    band_layout(zv, vb_ref)

    u0 = v0 + KV_WIDTH
    u_ref[...] = _dot(h, w_ref[:, u0:u0 + POOL_WIDTH]) + b_ref[:, u0:u0 + POOL_WIDTH]

    g0 = u0 + POOL_WIDTH
    for c in range(2):
        sl = slice(g0 + c * D_MODEL, g0 + (c + 1) * D_MODEL)
        zg = _dot(h, w_ref[:, sl]) + b_ref[:, sl]
        gate_ref[:, c * D_MODEL:(c + 1) * D_MODEL] = jax.nn.sigmoid(zg).astype(BF16)


def _inproj(x2, pos_b, invf, g, w, b):
    n = x2.shape[0]
    in_width = w.shape[1]
    row = lambda i: (i, 0)
    const = lambda i: (0, 0)
    return pl.pallas_call(
        _inproj_body,
        grid=(n // TM_IN,),
        in_specs=[
            pl.BlockSpec((TM_IN, D_MODEL), row),
            pl.BlockSpec((TM_IN // 4, LANES), row),
            pl.BlockSpec((1, LANES), const),
            pl.BlockSpec((1, D_MODEL), const),
            pl.BlockSpec((D_MODEL, in_width), const),
            pl.BlockSpec((1, in_width), const),
        ],
        out_specs=[
            pl.BlockSpec((TM_IN, Q_WIDTH), row),
            pl.BlockSpec((TM_IN, 4 * LANES), row),
            pl.BlockSpec((TM_IN, 4 * LANES), row),
            pl.BlockSpec((TM_IN, POOL_WIDTH), row),
            pl.BlockSpec((TM_IN, 2 * D_MODEL), row),
        ],
        out_shape=[
            jax.ShapeDtypeStruct((n, Q_WIDTH), BF16),
            jax.ShapeDtypeStruct((n, 4 * LANES), BF16),
            jax.ShapeDtypeStruct((n, 4 * LANES), BF16),
            jax.ShapeDtypeStruct((n, POOL_WIDTH), F32),
            jax.ShapeDtypeStruct((n, 2 * D_MODEL), BF16),
        ],
        scratch_shapes=[pltpu.VMEM((TM_IN, LANES), F32), pltpu.VMEM((TM_IN, LANES), F32)],
        compiler_params=pltpu.CompilerParams(
            dimension_semantics=("arbitrary",), vmem_limit_bytes=VMEM_LIMIT),
        name="inproj",
    )(x2, pos_b, invf, g, w, b)


def _attn_body(sinks_ref, q_ref, kbc_ref, kbp_ref, vbc_ref, vbp_ref, bias_ref,
               o_ref, p_scr):
    t = pl.program_id(0)
    seq_first = (t % (SEQ // TQ)) == 0
    lane = lax.broadcasted_iota(I32, (WINDOW, LANES), 1)
    low_head = lane < HEAD_DIM
    for n in range(TQ // WINDOW):
        rows = slice(n * WINDOW, (n + 1) * WINDOW)
        if n == 0:
            k_prev, v_prev = kbp_ref[...], vbp_ref[...]
            bias = jnp.where(seq_first, bias_ref[1], bias_ref[0])
        else:
            prev_rows = slice((n - 1) * WINDOW, n * WINDOW)
            k_prev, v_prev = kbc_ref[prev_rows, :], vbc_ref[prev_rows, :]
            bias = bias_ref[0]
        k_cur, v_cur = kbc_ref[rows, :], vbc_ref[rows, :]
        for g in range(2):
            def stack(prev, cur):
                lo = slice((2 * g) * LANES, (2 * g + 1) * LANES)
                hi = slice((2 * g + 1) * LANES, (2 * g + 2) * LANES)
                return jnp.concatenate([prev[:, lo], cur[:, lo], prev[:, hi], cur[:, hi]], axis=0)
            kmat = stack(k_prev, k_cur)
            vmat = stack(v_prev, v_cur)
            qs = jnp.concatenate(
                [q_ref[rows, (4 * g + p) * LANES:(4 * g + p + 1) * LANES] for p in range(4)],
                axis=0)
            s = lax.dot_general(qs, kmat, (((1,), (1,)), ((), ())),
                                preferred_element_type=F32) + bias
            inv = []
            for p in range(4):
                pr = slice(p * WINDOW, (p + 1) * WINDOW)
                ms, sums = [], []
                for j in range(2):
                    cols = slice(j * 2 * WINDOW, (j + 1) * 2 * WINDOW)
                    sp = s[pr, cols]
                    sink = sinks_ref[8 * g + 2 * p + j] * LOG2_E
                    m = jnp.maximum(jnp.max(sp, axis=-1, keepdims=True), sink)
                    e = jnp.exp2(sp - m)
                    p_scr[pr, cols] = e.astype(BF16)
                    ms.append(m)
                    sums.append(jnp.sum(e, axis=-1, keepdims=True))
                sink_pair = jnp.where(low_head, sinks_ref[8 * g + 2 * p] * LOG2_E,
                                      sinks_ref[8 * g + 2 * p + 1] * LOG2_E)
                m_pair = jnp.where(low_head, ms[0], ms[1])
                den = jnp.where(low_head, sums[0], sums[1]) + jnp.exp2(sink_pair - m_pair)
                inv.append(1.0 / den)
            o = _dot(p_scr[...], vmat)
            for p in range(4):
                pr = slice(p * WINDOW, (p + 1) * WINDOW)
                o_ref[rows, (4 * g + p) * LANES:(4 * g + p + 1) * LANES] = (o[pr, :] * inv[p]).astype(BF16)


def _attn_bias():
    r = np.arange(4 * WINDOW)[:, None] % WINDOW
    c = np.arange(4 * WINDOW)[None, :] % (2 * WINDOW)
    band = (c > r) & (c <= r + WINDOW)
    first = band & (c >= WINDOW)
    return np.stack([np.where(band, 0.0, NEG_BIG), np.where(first, 0.0, NEG_BIG)]).astype(np.float32)


def _attn(sinks, q, kb, vb):
    n = q.shape[0]
    blocks_per_tile = TQ // WINDOW
    cur = lambda t: (t, 0)
    prev = lambda t: (jnp.maximum(t * blocks_per_tile - 1, 0), 0)
    bias = jnp.asarray(_attn_bias())
    return pl.pallas_call(
        _attn_body,
        grid=(n // TQ,),
        in_specs=[
            pl.BlockSpec(memory_space=pltpu.SMEM),
            pl.BlockSpec((TQ, Q_WIDTH), cur),
            pl.BlockSpec((TQ, 4 * LANES), cur),
            pl.BlockSpec((WINDOW, 4 * LANES), prev),
            pl.BlockSpec((TQ, 4 * LANES), cur),
            pl.BlockSpec((WINDOW, 4 * LANES), prev),
            pl.BlockSpec((2, 4 * WINDOW, 4 * WINDOW), lambda t: (0, 0, 0)),
        ],
        out_specs=pl.BlockSpec((TQ, Q_WIDTH), cur),
        out_shape=jax.ShapeDtypeStruct((n, Q_WIDTH), BF16),
        scratch_shapes=[pltpu.VMEM((4 * WINDOW, 4 * WINDOW), BF16)],
        compiler_params=pltpu.CompilerParams(
            dimension_semantics=("arbitrary",), vmem_limit_bytes=VMEM_LIMIT),
        name="attn",
    )(sinks, q, kb, kb, vb, vb, bias)


def _post_body(attn_ref, u_ref, uprev_ref, gate_ref, x_ref,
               wo_ref, bo_ref, wmix_ref, pscale_ref, wup_ref, wout_ref,
               gffn_ref, wrt_ref, brt_ref,
               x1_ref, h2_ref, pos_t_ref, pw_ref, tc_ref,
               win_a, win_b, carry):
    i = pl.program_id(0)
    tiles_per_seq = SEQ // TM_POST
    seq_first = (i % tiles_per_seq) == 0

    @pl.when(i == 0)
    def _():
        carry[...] = jnp.zeros_like(carry)

    y_attn = _dot(attn_ref[...], wo_ref[...]) + bo_ref[...]

    rows = TM_POST + POOL_PAD
    win_a[0:POOL_PAD - POOL_HALO, :] = jnp.zeros((POOL_PAD - POOL_HALO, POOL_WIDTH), F32)
    win_a[POOL_PAD - POOL_HALO:POOL_PAD, :] = jnp.where(seq_first, 0.0, uprev_ref[...])
    win_a[POOL_PAD:, :] = u_ref[...]
    src, dst = win_a, win_b
    sums = {}
    for level, w in enumerate((1, 2, 4, 8)):
        lo = 8 * (level + 1)
        c0 = level * POOL_GROUP
        dst[lo:, c0:] = src[lo:, c0:] + src[lo - w:rows - w, c0:]
        sums[2 * w] = dst
        src, dst = dst, (win_a if dst is win_b else win_b)
    row = lax.broadcasted_iota(I32, (TM_POST, 1), 0)
    tpos = (i % tiles_per_seq) * TM_POST + row
    mixed = []
    for gi, w in enumerate(POOL_WINDOWS):
        cols = slice(gi * POOL_GROUP, (gi + 1) * POOL_GROUP)
        cnt = jnp.minimum(tpos + 1, w).astype(F32)
        pooled = sums[w][POOL_PAD:, cols] / cnt - u_ref[:, cols]
        mixed.append(_dot(pooled.astype(BF16), wmix_ref[gi]) * pscale_ref[:, cols])
    mixed = jnp.concatenate(mixed, axis=-1).astype(BF16)
    y_pool = _dot(mixed, wup_ref[...])
    merged = (gate_ref[:, :D_MODEL].astype(F32) * y_attn
              + gate_ref[:, D_MODEL:].astype(F32) * y_pool)
    x1 = x_ref[...] + _dot(merged.astype(BF16), wout_ref[...])
    x1_ref[...] = x1
    h2 = _rms_norm(x1, gffn_ref[...])
    h2_bf = h2.astype(BF16)
    h2_ref[...] = h2_bf

    logits_t = lax.dot_general(wrt_ref[...], h2_bf, (((1,), (1,)), ((), ())),
                               preferred_element_type=F32) + brt_ref[...]
    erow = lax.broadcasted_iota(I32, (N_EXPERTS, TM_POST), 0)
    work = logits_t
    top_v, onehots = [], []
    for _ in range(TOP_K):
        m = jnp.max(work, axis=0, keepdims=True)
        idx = jnp.min(jnp.where(work == m, erow, N_EXPERTS), axis=0, keepdims=True)
        hit = erow == idx
        top_v.append(m)
        onehots.append(hit)
        work = jnp.where(hit, -jnp.inf, work)
    exps = [jnp.exp(v - top_v[0]) for v in top_v]
    denom = exps[0] + exps[1] + exps[2] + exps[3]
    comb = [e / denom for e in exps]
    chosen = jnp.zeros((N_EXPERTS, TM_POST), F32)
    for hit in onehots:
        chosen = chosen + hit.astype(F32)

    e_r = lax.broadcasted_iota(I32, (N_EXPERTS, N_EXPERTS), 0)
    e_c = lax.broadcasted_iota(I32, (N_EXPERTS, N_EXPERTS), 1)
    earlier_expert = (e_c < e_r).astype(BF16)
    t_r = lax.broadcasted_iota(I32, (TT, TT), 0)
    t_c = lax.broadcasted_iota(I32, (TT, TT), 1)
    earlier_token = (t_r < t_c).astype(BF16)
    lane = lax.broadcasted_iota(I32, (N_EXPERTS, LANES), 1)
    counts = jnp.zeros((N_EXPERTS, LANES), F32)
    pos_parts = [[] for _ in range(TOP_K)]
    n_sub = TM_POST // TT
    for s in range(n_sub):
        cols = slice(s * TT, (s + 1) * TT)
        ch = chosen[:, cols]
        cnt = jnp.broadcast_to(jnp.sum(ch, axis=1, keepdims=True), (N_EXPERTS, LANES))
        before = carry[...]
        head = before - SEG_ALIGN * jnp.floor(before / SEG_ALIGN)
        flush = jnp.logical_and(i == pl.num_programs(0) - 1, s == n_sub - 1)
        present = (cnt > 0) | (flush & (head > 0))
        blocks = jnp.where(present, jnp.floor((head + cnt + (SEG_ALIGN - 1)) / SEG_ALIGN), 0.0)
        seg_loc = SEG_ALIGN * _dot(earlier_expert, blocks.astype(BF16))
        base = seg_loc + head
        rank_in_tile = _dot(ch.astype(BF16), earlier_token)
        full = jnp.concatenate([base] * (TT // LANES), axis=1) + rank_in_tile
        for k in range(TOP_K):
            pos_parts[k].append(
                jnp.sum(jnp.where(onehots[k][:, cols], full, 0.0), axis=0, keepdims=True))
        counts = jnp.where(lane == s, cnt, counts)
        carry[...] = before + cnt
    tc_ref[...] = counts

    pos = [jnp.concatenate(parts, axis=1) for parts in pos_parts]
    row8 = lax.broadcasted_iota(I32, (8, TM_POST), 0)
    pos_rows = jnp.full((8, TM_POST), -1.0, F32)
    both = jnp.zeros((8, TM_POST), F32)
    for k in range(TOP_K):
        pos_rows = jnp.where(row8 == k, pos[k], pos_rows)
        both = jnp.where(row8 == k, pos[k], both)
        both = jnp.where(row8 == TOP_K + k, comb[k], both)
    pos_t_ref[...] = pos_rows.astype(I32)
    padded = jnp.concatenate([both, jnp.zeros((LANES - 8, TM_POST), F32)], axis=0)
    pw_ref[...] = padded.T


def _post(attn, u, gates, x2, wo, bo, wmix, pscale, wup, wout, gffn, wr, br):
    n = x2.shape[0]
    row = lambda i: (i, 0)
    const = lambda i: (0, 0)
    halo_blocks = TM_POST // POOL_HALO
    prev = lambda i: (jnp.maximum(i * halo_blocks - 1, 0), 0)
    return pl.pallas_call(
        _post_body,
        grid=(n // TM_POST,),
        in_specs=[
            pl.BlockSpec((TM_POST, Q_WIDTH), row),
            pl.BlockSpec((TM_POST, POOL_WIDTH), row),
            pl.BlockSpec((POOL_HALO, POOL_WIDTH), prev),
            pl.BlockSpec((TM_POST, 2 * D_MODEL), row),
            pl.BlockSpec((TM_POST, D_MODEL), row),
            pl.BlockSpec((Q_WIDTH, D_MODEL), const),
            pl.BlockSpec((1, D_MODEL), const),
            pl.BlockSpec((len(POOL_WINDOWS), POOL_GROUP, POOL_GROUP), lambda i: (0, 0, 0)),
            pl.BlockSpec((1, POOL_WIDTH), const),
            pl.BlockSpec((POOL_WIDTH, D_MODEL), const),
            pl.BlockSpec((D_MODEL, D_MODEL), const),
            pl.BlockSpec((1, D_MODEL), const),
            pl.BlockSpec((N_EXPERTS, D_MODEL), const),
            pl.BlockSpec((N_EXPERTS, TM_POST), const),
        ],
        out_specs=[
            pl.BlockSpec((TM_POST, D_MODEL), row),
            pl.BlockSpec((TM_POST, D_MODEL), row),
            pl.BlockSpec((8, TM_POST), lambda i: (0, i)),
            pl.BlockSpec((TM_POST, LANES), row),
            pl.BlockSpec((N_EXPERTS, LANES), row),
        ],
        out_shape=[
            jax.ShapeDtypeStruct((n, D_MODEL), F32),
            jax.ShapeDtypeStruct((n, D_MODEL), BF16),
            jax.ShapeDtypeStruct((8, n), I32),
            jax.ShapeDtypeStruct((n, LANES), F32),
            jax.ShapeDtypeStruct((n // TM_POST * N_EXPERTS, LANES), F32),
        ],
        scratch_shapes=[pltpu.VMEM((TM_POST + POOL_PAD, POOL_WIDTH), F32),
                        pltpu.VMEM((TM_POST + POOL_PAD, POOL_WIDTH), F32),
                        pltpu.VMEM((N_EXPERTS, LANES), F32)],
        compiler_params=pltpu.CompilerParams(
            dimension_semantics=("arbitrary",), vmem_limit_bytes=VMEM_LIMIT),
        name="post",
    )(attn, u, u, gates, x2, wo, bo, wmix, pscale, wup, wout, gffn, wr, br)


def _start_segments(i, glob_ref, loc_ref, cnt_ref, make_copy):
    def body(e, c):
        idx = i * N_EXPERTS + e
        rows = pl.multiple_of(cnt_ref[idx], SEG_ALIGN)

        @pl.when(rows > 0)
        def _():
            loc = pl.multiple_of(loc_ref[idx], SEG_ALIGN)
            glob = pl.multiple_of(glob_ref[idx], SEG_ALIGN)
            make_copy(pl.ds(loc, rows), pl.ds(glob, rows)).start()
        return c
    lax.fori_loop(0, N_EXPERTS, body, 0, unroll=4)


def _wait_segments(i, total_ref, make_copy):
    rows = pl.multiple_of(total_ref[i], SEG_ALIGN)

    @pl.when(rows > 0)
    def _():
        make_copy(pl.ds(0, rows), pl.ds(0, rows)).wait()


def _pack_halves(v):
    c = v.shape[1] // 2
    bits = lax.bitcast_convert_type(v, U32)
    return (bits[:, :c] >> 16) | bits[:, c:]


def _unpack_halves(w):
    lo = lax.bitcast_convert_type(w << 16, F32)
    hi = lax.bitcast_convert_type(w & jnp.uint32(0xFFFF0000), F32)
    return jnp.concatenate([lo, hi], axis=1).astype(BF16)


def _dispatch_body(glob_ref, loc_ref, rd_ref, wr_ref, total_ref, rows_ref, h2_ref, pos_ref,
                   xs_hbm, local, tail, sem):
    i = pl.program_id(0)
    nt = pl.num_programs(0)
    slot = i % 2

    def copy_from(s):
        def make_copy(loc_rows, glob_rows):
            return pltpu.make_async_copy(local.at[s, loc_rows], xs_hbm.at[glob_rows], sem.at[s])
        return make_copy

    @pl.when(i == 0)
    def _():
        tail[...] = jnp.zeros_like(tail)

    @pl.when(i >= 2)
    def _():
        _wait_segments(i - 2, total_ref, copy_from(slot))

    def group(r0, nrows):
        j = lax.broadcasted_iota(I32, (nrows, TT), 0) + r0
        hit = j == pos_ref[0:1, :]
        for k in range(1, TOP_K):
            hit = hit | (j == pos_ref[k:k + 1, :])
        perm = jnp.where(hit, 1.0, 0.0).astype(BF16)
        local[slot, r0:r0 + nrows, :] = _pack_halves(_dot(perm, h2_ref[...]))

    group(0, LROWS_MAIN)

    @pl.when(rows_ref[i] + SEG_ALIGN > LROWS_MAIN)
    def _():
        group(LROWS_MAIN, LROWS - LROWS_MAIN)

    for e in range(N_EXPERTS):
        idx = i * N_EXPERTS + e
        rd = rd_ref[idx]
        wr = wr_ref[idx]
        loc = pl.multiple_of(loc_ref[idx], SEG_ALIGN)
        first = local[slot, pl.ds(loc, SEG_ALIGN), :]
        last = local[slot, pl.ds(pl.multiple_of(loc + wr, SEG_ALIGN), SEG_ALIGN), :]
        merged = jnp.where(rd > 0, first | tail[e], first)
        local[slot, pl.ds(loc, SEG_ALIGN), :] = merged
        last = jnp.where(wr == 0, merged, last)
        tail[e] = jnp.where(rd > 0, jnp.where(wr < rd, last, jnp.uint32(0)), tail[e])
    _start_segments(i, glob_ref, loc_ref, wr_ref, copy_from(slot))

    @pl.when(i == nt - 1)
    def _():
        _wait_segments(i - 1, total_ref, copy_from(1 - slot))
        _wait_segments(i, total_ref, copy_from(slot))


def _dispatch(seg_glob, seg_loc, seg_rd, seg_wr, tile_wr, tile_rd, h2, pos_t, n_rows):
    n = h2.shape[0]
    grid_spec = pltpu.PrefetchScalarGridSpec(
        num_scalar_prefetch=6,
        grid=(n // TT,),
        in_specs=[
            pl.BlockSpec((TT, D_MODEL), lambda i, *_: (i, 0)),
            pl.BlockSpec((8, TT), lambda i, *_: (0, i)),
        ],
        out_specs=pl.BlockSpec(memory_space=pl.ANY),
        scratch_shapes=[pltpu.VMEM((2, LROWS, PACKED), U32),
                        pltpu.VMEM((N_EXPERTS, SEG_ALIGN, PACKED), U32),
                        pltpu.SemaphoreType.DMA((2,))],
    )
    return pl.pallas_call(
        _dispatch_body,
        grid_spec=grid_spec,
        out_shape=jax.ShapeDtypeStruct((n_rows, PACKED), U32),
        compiler_params=pltpu.CompilerParams(
            dimension_semantics=("arbitrary",), vmem_limit_bytes=VMEM_LIMIT),
        name="dispatch",
    )(seg_glob, seg_loc, seg_rd, seg_wr, tile_wr, tile_rd, h2, pos_t)


def _moe_body(te_ref, nused_ref, rows_ref, next_ref, xs_ref, bias_ref,
              wg_hbm, wu_hbm, wd_hbm, y_ref, wf32, wbf, sem, slot_ref):
    i = pl.program_id(0)

    def fetch(expert, slot):
        return [pltpu.make_async_copy(w.at[expert], wf32.at[slot, m], sem.at[slot, m])
                for m, w in enumerate((wg_hbm, wu_hbm, wd_hbm))]

    @pl.when(i == 0)
    def _():
        slot_ref[0] = 0

    @pl.when(i < nused_ref[0])
    def _():
        expert = te_ref[i]

        @pl.when((i == 0) | (expert != te_ref[jnp.maximum(i - 1, 0)]))
        def _():
            slot = slot_ref[0]

            @pl.when(i == 0)
            def _():
                for cp in fetch(expert, slot):
                    cp.start()
            for cp in fetch(expert, slot):
                cp.wait()
            nxt = next_ref[i]

            @pl.when(nxt >= 0)
            def _():
                for cp in fetch(nxt, 1 - slot):
                    cp.start()
            for m in range(3):
                wbf[m] = wf32[slot, m].astype(BF16)
            slot_ref[0] = 1 - slot

        row = lax.broadcasted_iota(I32, (TM_MOE, 1), 0)
        x = _unpack_halves(jnp.where(row < rows_ref[i], xs_ref[...], jnp.uint32(0)))
        bias = bias_ref[expert]
        g = _dot(x, wbf[0]) + bias[0:1, :]
        u = _dot(x, wbf[1]) + bias[1:2, :]
        g = jnp.minimum(g, SWIGLU_LIMIT)
        u = jnp.clip(u, -SWIGLU_LIMIT, SWIGLU_LIMIT)
        a = (g * jax.nn.sigmoid(SWIGLU_ALPHA * g) * (u + 1.0)).astype(BF16)
        y = _dot(a, wbf[2]) + bias[2:3, :]
        y_ref[...] = _pack_halves(y.astype(BF16).astype(F32))


def _moe(tile_expert, n_used, tile_rows, tile_next, xs, wg, bg, wu, bu, wd, bd):
    n_rows = xs.shape[0]
    n_tiles = n_rows // TM_MOE
    d_ff = wg.shape[2]
    assert d_ff == D_MODEL
    row = lambda i, te, nu, *_: (jnp.minimum(i, nu[0] - 1), 0)
    bias = jnp.zeros((N_EXPERTS, 8, D_MODEL), F32)
    bias = bias.at[:, 0].set(bg).at[:, 1].set(bu).at[:, 2].set(bd)
    grid_spec = pltpu.PrefetchScalarGridSpec(
        num_scalar_prefetch=4,
        grid=(n_tiles,),
        in_specs=[
            pl.BlockSpec((TM_MOE, PACKED), row),
            pl.BlockSpec((N_EXPERTS, 8, D_MODEL), lambda i, *_: (0, 0, 0)),
            pl.BlockSpec(memory_space=pl.ANY),
            pl.BlockSpec(memory_space=pl.ANY),
            pl.BlockSpec(memory_space=pl.ANY),
        ],
        out_specs=pl.BlockSpec((TM_MOE, PACKED), row),
        scratch_shapes=[pltpu.VMEM((2, 3, D_MODEL, D_MODEL), F32),
                        pltpu.VMEM((3, D_MODEL, D_MODEL), BF16),
                        pltpu.SemaphoreType.DMA((2, 3)),
                        pltpu.SMEM((1,), I32)],
    )
    return pl.pallas_call(
        _moe_body,
        grid_spec=grid_spec,
        out_shape=jax.ShapeDtypeStruct((n_rows, PACKED), U32),
        compiler_params=pltpu.CompilerParams(
            dimension_semantics=("arbitrary",), vmem_limit_bytes=VMEM_LIMIT),
        name="moe",
    )(tile_expert, n_used, tile_rows, tile_next, xs, bias, wg, wu, wd)


def _combine_body(glob_ref, loc_ref, cnt_ref, total_ref, x1_ref, pw_ref, gfin_ref,
                  ys_hbm, o_ref, local, sem):
    i = pl.program_id(0)
    nt = pl.num_programs(0)
    slot = i % 2

    def copy_into(s):
        def make_copy(loc_rows, glob_rows):
            return pltpu.make_async_copy(ys_hbm.at[glob_rows], local.at[s, loc_rows], sem.at[s])
        return make_copy

    @pl.when(i == 0)
    def _():
        local[...] = jnp.zeros_like(local)
        _start_segments(0, glob_ref, loc_ref, cnt_ref, copy_into(0))

    @pl.when(i + 1 < nt)
    def _():
        _start_segments(i + 1, glob_ref, loc_ref, cnt_ref, copy_into(1 - slot))
    _wait_segments(i, total_ref, copy_into(slot))

    j = lax.broadcasted_iota(I32, (TT, LROWS), 1).astype(F32)
    wc = jnp.zeros((TT, LROWS), F32)
    for k in range(TOP_K):
        wc = jnp.where(j == pw_ref[:, k:k + 1], pw_ref[:, TOP_K + k:TOP_K + k + 1], wc)
    acc = x1_ref[...] + _dot(wc.astype(BF16), _unpack_halves(local[slot]))
    o_ref[...] = _rms_norm(acc, gfin_ref[...])


def _combine(seg_glob, seg_loc, seg_cnt, tile_rd, x1, pw, gfin, ys):
    n = x1.shape[0]
    grid_spec = pltpu.PrefetchScalarGridSpec(
        num_scalar_prefetch=4,
        grid=(n // TT,),
        in_specs=[
            pl.BlockSpec((TT, D_MODEL), lambda i, *_: (i, 0)),
            pl.BlockSpec((TT, LANES), lambda i, *_: (i, 0)),
            pl.BlockSpec((1, D_MODEL), lambda i, *_: (0, 0)),
            pl.BlockSpec(memory_space=pl.ANY),
        ],
        out_specs=pl.BlockSpec((TT, D_MODEL), lambda i, *_: (i, 0)),
        scratch_shapes=[pltpu.VMEM((2, LROWS, PACKED), U32),
                        pltpu.SemaphoreType.DMA((2,))],
    )
    return pl.pallas_call(
        _combine_body,
        grid_spec=grid_spec,
        out_shape=jax.ShapeDtypeStruct((n, D_MODEL), F32),
        compiler_params=pltpu.CompilerParams(
            dimension_semantics=("arbitrary",), vmem_limit_bytes=VMEM_LIMIT),
        name="combine",
    )(seg_glob, seg_loc, seg_cnt, tile_rd, x1, pw, gfin, ys)


def _layer(x2, pos_b, invf, norm_mix_g, w_in, b_in, attn_sinks, w_o_attn, b_o_attn,
           w_pool_mix, pool_scale, w_pool_up, w_out, norm_ffn_g, w_router, b_router,
           w_gate, b_gate, w_up, b_up, w_down, b_down, out_g):
    n = x2.shape[0]
    q, kb, vb, u, gates = _inproj(
        x2, pos_b, invf, norm_mix_g[None, :], w_in.astype(BF16), b_in[None, :])
    attn = _attn(attn_sinks, q, kb, vb)

    wrt = w_router.T.astype(BF16)
    brt = jnp.broadcast_to(b_router[:, None], (N_EXPERTS, TM_POST))
    x1, h2, pos_t, pw, tc = _post(
        attn, u, gates, x2, w_o_attn.astype(BF16), b_o_attn[None, :],
        w_pool_mix.astype(BF16), pool_scale[None, :], w_pool_up.astype(BF16),
        w_out.astype(BF16), norm_ffn_g[None, :], wrt, brt)

    nt = n // TT
    sub = TM_POST // TT
    seg_cnt = (tc.reshape(n // TM_POST, N_EXPERTS, LANES)[:, :, :sub]
               .transpose(0, 2, 1).reshape(nt, N_EXPERTS).astype(I32))
    before = jnp.cumsum(seg_cnt, axis=0) - seg_cnt
    sizes = jnp.sum(seg_cnt, axis=0)
    padded = (sizes + TM_MOE - 1) // TM_MOE * TM_MOE
    pends = jnp.cumsum(padded)
    pstarts = pends - padded
    head = before % SEG_ALIGN
    last_tile = (jnp.arange(nt, dtype=I32) == nt - 1)[:, None]
    blocks_up = (head + seg_cnt + SEG_ALIGN - 1) // SEG_ALIGN
    blocks_dn = (head + seg_cnt) // SEG_ALIGN
    seg_rd = jnp.where((seg_cnt > 0) | (last_tile & (head > 0)), blocks_up, 0) * SEG_ALIGN
    seg_wr = jnp.where(last_tile, seg_rd, jnp.where(seg_cnt > 0, blocks_dn * SEG_ALIGN, 0))
    seg_loc = jnp.cumsum(seg_rd, axis=1) - seg_rd
    seg_glob = pstarts[None, :] + before - head
    tile_rd = jnp.sum(seg_rd, axis=1)
    tile_wr = jnp.sum(seg_wr, axis=1)
    seg_glob, seg_loc, seg_rd, seg_wr = (a.reshape(-1) for a in (seg_glob, seg_loc, seg_rd, seg_wr))
    m = n * TOP_K
    n_tiles = (m + N_EXPERTS * (TM_MOE - 1) + TM_MOE - 1) // TM_MOE
    tile_start = jnp.arange(n_tiles, dtype=I32) * TM_MOE
    tile_expert = jnp.minimum(
        jnp.sum(tile_start[:, None] >= pends[None, :], axis=-1), N_EXPERTS - 1).astype(I32)
    own = tile_expert[:, None] == jnp.arange(N_EXPERTS, dtype=I32)[None, :]
    left = jnp.sum(jnp.where(own, (pstarts + sizes)[None, :] - tile_start[:, None], 0), axis=-1)
    tile_rows = jnp.clip(left, 0, TM_MOE)
    ids = jnp.arange(N_EXPERTS, dtype=I32)
    later = jnp.where((sizes > 0)[None, :] & (ids[None, :] > ids[:, None]), ids[None, :], N_EXPERTS)
    next_expert = jnp.min(later, axis=1)
    next_expert = jnp.where(next_expert == N_EXPERTS, -1, next_expert)
    tile_next = jnp.sum(jnp.where(own, next_expert[None, :], 0), axis=-1).astype(I32)
    n_used = (pends[-1] // TM_MOE).astype(I32)[None]

    xs = _dispatch(seg_glob, seg_loc, seg_rd, seg_wr, tile_wr, tile_rd, h2, pos_t, n_tiles * TM_MOE)
    ys = _moe(tile_expert, n_used, tile_rows.astype(I32), tile_next, xs,
              w_gate, b_gate, w_up, b_up, w_down, b_down)
    return _combine(seg_glob, seg_loc, seg_rd, tile_rd, x1, pw, out_g[None, :], ys)


def kernel(x, positions, norm_mix_g, w_in, b_in, attn_sinks, w_o_attn, b_o_attn, w_pool_mix,
           pool_scale, w_pool_up, w_out, norm_ffn_g, w_router, b_router, w_gate, b_gate,
           w_up, b_up, w_down, b_down, norm_final_g):
    b, s, d = x.shape
    depth = w_in.shape[0]
    assert (s, d, depth) == (SEQ, D_MODEL, 1)
    n = b * s
    x2 = x.reshape(n, d)
    pos_b = jnp.repeat(positions.reshape(n // 4, 4), LANES // 4, axis=1)
    inv_freq = ROPE_THETA ** (-jnp.arange(0, HEAD_DIM, 2, dtype=F32) / HEAD_DIM)
    invf = jnp.tile(inv_freq, LANES // (HEAD_DIM // 2))[None, :]
    out = _layer(x2, pos_b, invf, norm_mix_g[0], w_in[0], b_in[0], attn_sinks[0], w_o_attn[0],
                 b_o_attn[0], w_pool_mix[0], pool_scale[0], w_pool_up[0], w_out[0],
                 norm_ffn_g[0], w_router[0], b_router[0], w_gate[0], b_gate[0], w_up[0],
                 b_up[0], w_down[0], b_down[0], norm_final_g)
    return out.reshape(b, s, d)
```

```python
import functools

import numpy as np
import jax
import jax.numpy as jnp
from jax import lax
from jax.experimental import pallas as pl
from jax.experimental.pallas import tpu as pltpu

F32 = jnp.float32
BF16 = jnp.bfloat16
I32 = jnp.int32
U32 = jnp.uint32

D_MODEL = 1024
SEQ = 4096
HEAD_DIM = 64
N_Q_HEADS = 16
WINDOW = 128
ROPE_THETA = 10000.0
Q_WIDTH = N_Q_HEADS * HEAD_DIM
KV_WIDTH = 128
POOL_WINDOWS = (2, 4, 8, 16)
POOL_WIDTH = 512
POOL_GROUP = 128
POOL_HALO = 16
POOL_PAD = 32
N_EXPERTS = 32
TOP_K = 4
SWIGLU_LIMIT = 7.0
SWIGLU_ALPHA = 1.702
RMS_EPS = 1e-5
NEG_BIG = -1e30
LOG2_E = 1.4426950408889634

LANES = 128
TM_IN = 1024
TQ = 512
TM_POST = 512
TM_MOE = 1024
MOE_ROWS = 512
TT = 256
SEG_ALIGN = 8
LROWS = TT * TOP_K + 2 * N_EXPERTS * SEG_ALIGN
LROWS_MAIN = LROWS - 256
PACKED = D_MODEL // 2
VMEM_LIMIT = 56 * 1024 * 1024


def _rms_norm(x, g):
    ms = jnp.mean(x * x, axis=-1, keepdims=True)
    return (x * lax.rsqrt(ms + RMS_EPS)) * g


def _dot(a, b):
    return jnp.dot(a, b, preferred_element_type=F32)


def _inproj_body(x_ref, pos_ref, invf_ref, g_ref, w_ref, b_ref,
                 q_ref, kb_ref, vb_ref, u_ref, gate_ref, cos_scr, sin_scr):
    h = _rms_norm(x_ref[...], g_ref[...]).astype(BF16)

    theta = pos_ref[...].astype(F32) * invf_ref[...]
    lane4 = lax.broadcasted_iota(I32, theta.shape, 1)
    for packed, scr in ((jnp.cos(theta), cos_scr), (jnp.sin(theta), sin_scr)):
        for jt in range(4):
            seg = packed if jt == 0 else pltpu.roll(packed, LANES - 32 * jt, 1)
            seg = jnp.where(lane4 < 32, seg, pltpu.roll(seg, 32, 1))
            seg = jnp.where(lane4 < 64, seg, pltpu.roll(seg, 64, 1))
            scr[pl.ds(jt, TM_IN // 4, stride=4), :] = seg
    cos = cos_scr[...]
    sin = sin_scr[...]
    lane = lax.broadcasted_iota(I32, cos.shape, 1)
    first_half = (lane & 32) == 0
    sin_signed = jnp.where(first_half, -sin, sin)
    low_head = lane < HEAD_DIM

    def rope(t):
        swapped = jnp.where(first_half, pltpu.roll(t, 96, 1), pltpu.roll(t, 32, 1))
        return t * cos + swapped * sin_signed

    def band_layout(t, out_ref):
        tr = pltpu.roll(t, 64, 1)
        zero = jnp.zeros_like(t)
        chunks = (jnp.where(low_head, t, zero), jnp.where(low_head, zero, tr),
                  jnp.where(low_head, tr, zero), jnp.where(low_head, zero, t))
        for c, val in enumerate(chunks):
            out_ref[:, c * LANES:(c + 1) * LANES] = val.astype(BF16)

    qk_w = Q_WIDTH + KV_WIDTH
    zqk = _dot(h, w_ref[:, :qk_w]) + b_ref[:, :qk_w]
    scale = HEAD_DIM ** -0.5 * LOG2_E
    for j in range(Q_WIDTH // LANES):
        sl = slice(j * LANES, (j + 1) * LANES)
        q_ref[:, sl] = (rope(zqk[:, sl]) * scale).astype(BF16)
    band_layout(rope(zqk[:, Q_WIDTH:qk_w]), kb_ref)

    v0 = qk_w
    zv = _dot(h, w_ref[:, v0:v0 + KV_WIDTH]) + b_ref[:, v0:v0 + KV_WIDTH]
    band_layout(zv, vb_ref)

    u0 = v0 + KV_WIDTH
    u_ref[...] = _dot(h, w_ref[:, u0:u0 + POOL_WIDTH]) + b_ref[:, u0:u0 + POOL_WIDTH]

    g0 = u0 + POOL_WIDTH
    for c in range(2):
        sl = slice(g0 + c * D_MODEL, g0 + (c + 1) * D_MODEL)
        zg = _dot(h, w_ref[:, sl]) + b_ref[:, sl]
        gate_ref[:, c * D_MODEL:(c + 1) * D_MODEL] = jax.nn.sigmoid(zg).astype(BF16)


def _inproj(x2, pos_b, invf, g, w, b):
    n = x2.shape[0]
    in_width = w.shape[1]
    row = lambda i: (i, 0)
    const = lambda i: (0, 0)
    return pl.pallas_call(
        _inproj_body,
        grid=(n // TM_IN,),
        in_specs=[
            pl.BlockSpec((TM_IN, D_MODEL), row),
            pl.BlockSpec((TM_IN // 4, LANES), row),
            pl.BlockSpec((1, LANES), const),
            pl.BlockSpec((1, D_MODEL), const),
            pl.BlockSpec((D_MODEL, in_width), const),
            pl.BlockSpec((1, in_width), const),
        ],
        out_specs=[
            pl.BlockSpec((TM_IN, Q_WIDTH), row),
            pl.BlockSpec((TM_IN, 4 * LANES), row),
            pl.BlockSpec((TM_IN, 4 * LANES), row),
            pl.BlockSpec((TM_IN, POOL_WIDTH), row),
            pl.BlockSpec((TM_IN, 2 * D_MODEL), row),
        ],
        out_shape=[
            jax.ShapeDtypeStruct((n, Q_WIDTH), BF16),
            jax.ShapeDtypeStruct((n, 4 * LANES), BF16),
            jax.ShapeDtypeStruct((n, 4 * LANES), BF16),
            jax.ShapeDtypeStruct((n, POOL_WIDTH), F32),
            jax.ShapeDtypeStruct((n, 2 * D_MODEL), BF16),
        ],
        scratch_shapes=[pltpu.VMEM((TM_IN, LANES), F32), pltpu.VMEM((TM_IN, LANES), F32)],
        compiler_params=pltpu.CompilerParams(
            dimension_semantics=("arbitrary",), vmem_limit_bytes=VMEM_LIMIT),
        name="inproj",
    )(x2, pos_b, invf, g, w, b)


def _attn_body(sinks_ref, q_ref, kbc_ref, kbp_ref, vbc_ref, vbp_ref, bias_ref,
               o_ref, p_scr):
    t = pl.program_id(0)
    seq_first = (t % (SEQ // TQ)) == 0
    lane = lax.broadcasted_iota(I32, (WINDOW, LANES), 1)
    low_head = lane < HEAD_DIM
    for n in range(TQ // WINDOW):
        rows = slice(n * WINDOW, (n + 1) * WINDOW)
        if n == 0:
            k_prev, v_prev = kbp_ref[...], vbp_ref[...]
            bias = jnp.where(seq_first, bias_ref[1], bias_ref[0])
        else:
            prev_rows = slice((n - 1) * WINDOW, n * WINDOW)
            k_prev, v_prev = kbc_ref[prev_rows, :], vbc_ref[prev_rows, :]
            bias = bias_ref[0]
        k_cur, v_cur = kbc_ref[rows, :], vbc_ref[rows, :]
        for g in range(2):
            def stack(prev, cur):
                lo = slice((2 * g) * LANES, (2 * g + 1) * LANES)
                hi = slice((2 * g + 1) * LANES, (2 * g + 2) * LANES)
                return jnp.concatenate([prev[:, lo], cur[:, lo], prev[:, hi], cur[:, hi]], axis=0)
            kmat = stack(k_prev, k_cur)
            vmat = stack(v_prev, v_cur)
            qs = jnp.concatenate(
                [q_ref[rows, (4 * g + p) * LANES:(4 * g + p + 1) * LANES] for p in range(4)],
                axis=0)
            s = lax.dot_general(qs, kmat, (((1,), (1,)), ((), ())),
                                preferred_element_type=F32) + bias
            inv = []
            for p in range(4):
                pr = slice(p * WINDOW, (p + 1) * WINDOW)
                ms, sums = [], []
                for j in range(2):
                    cols = slice(j * 2 * WINDOW, (j + 1) * 2 * WINDOW)
                    sp = s[pr, cols]
                    sink = sinks_ref[8 * g + 2 * p + j] * LOG2_E
                    m = jnp.maximum(jnp.max(sp, axis=-1, keepdims=True), sink)
                    e = jnp.exp2(sp - m)
                    p_scr[pr, cols] = e.astype(BF16)
                    ms.append(m)
                    sums.append(jnp.sum(e, axis=-1, keepdims=True))
                sink_pair = jnp.where(low_head, sinks_ref[8 * g + 2 * p] * LOG2_E,
                                      sinks_ref[8 * g + 2 * p + 1] * LOG2_E)
                m_pair = jnp.where(low_head, ms[0], ms[1])
                den = jnp.where(low_head, sums[0], sums[1]) + jnp.exp2(sink_pair - m_pair)
                inv.append(1.0 / den)
            o = _dot(p_scr[...], vmat)
            for p in range(4):
                pr = slice(p * WINDOW, (p + 1) * WINDOW)
                o_ref[rows, (4 * g + p) * LANES:(4 * g + p + 1) * LANES] = (o[pr, :] * inv[p]).astype(BF16)


def _attn_bias():
    r = np.arange(4 * WINDOW)[:, None] % WINDOW
    c = np.arange(4 * WINDOW)[None, :] % (2 * WINDOW)
    band = (c > r) & (c <= r + WINDOW)
    first = band & (c >= WINDOW)
    return np.stack([np.where(band, 0.0, NEG_BIG), np.where(first, 0.0, NEG_BIG)]).astype(np.float32)


def _attn(sinks, q, kb, vb):
    n = q.shape[0]
    blocks_per_tile = TQ // WINDOW
    cur = lambda t: (t, 0)
    prev = lambda t: (jnp.maximum(t * blocks_per_tile - 1, 0), 0)
    bias = jnp.asarray(_attn_bias())
    return pl.pallas_call(
        _attn_body,
        grid=(n // TQ,),
        in_specs=[
            pl.BlockSpec(memory_space=pltpu.SMEM),
            pl.BlockSpec((TQ, Q_WIDTH), cur),
            pl.BlockSpec((TQ, 4 * LANES), cur),
            pl.BlockSpec((WINDOW, 4 * LANES), prev),
            pl.BlockSpec((TQ, 4 * LANES), cur),
            pl.BlockSpec((WINDOW, 4 * LANES), prev),
            pl.BlockSpec((2, 4 * WINDOW, 4 * WINDOW), lambda t: (0, 0, 0)),
        ],
        out_specs=pl.BlockSpec((TQ, Q_WIDTH), cur),
        out_shape=jax.ShapeDtypeStruct((n, Q_WIDTH), BF16),
        scratch_shapes=[pltpu.VMEM((4 * WINDOW, 4 * WINDOW), BF16)],
        compiler_params=pltpu.CompilerParams(
            dimension_semantics=("arbitrary",), vmem_limit_bytes=VMEM_LIMIT),
        name="attn",
    )(sinks, q, kb, kb, vb, vb, bias)


def _post_body(attn_ref, u_ref, uprev_ref, gate_ref, x_ref,
               wo_ref, bo_ref, wmix_ref, pscale_ref, wup_ref, wout_ref,
               gffn_ref, wrt_ref, brt_ref,
               x1_ref, h2_ref, pos_t_ref, pw_ref, tc_ref,
               win_a, win_b, carry):
    i = pl.program_id(0)
    tiles_per_seq = SEQ // TM_POST
    seq_first = (i % tiles_per_seq) == 0

    @pl.when(i == 0)
    def _():
        carry[...] = jnp.zeros_like(carry)

    y_attn = _dot(attn_ref[...], wo_ref[...]) + bo_ref[...]

    rows = TM_POST + POOL_PAD
    win_a[0:POOL_PAD - POOL_HALO, :] = jnp.zeros((POOL_PAD - POOL_HALO, POOL_WIDTH), F32)
    win_a[POOL_PAD - POOL_HALO:POOL_PAD, :] = jnp.where(seq_first, 0.0, uprev_ref[...])
    win_a[POOL_PAD:, :] = u_ref[...]
    src, dst = win_a, win_b
    sums = {}
    for level, w in enumerate((1, 2, 4, 8)):
        lo = 8 * (level + 1)
        c0 = level * POOL_GROUP
        dst[lo:, c0:] = src[lo:, c0:] + src[lo - w:rows - w, c0:]
        sums[2 * w] = dst
        src, dst = dst, (win_a if dst is win_b else win_b)
    row = lax.broadcasted_iota(I32, (TM_POST, 1), 0)
    tpos = (i % tiles_per_seq) * TM_POST + row
    mixed = []
    for gi, w in enumerate(POOL_WINDOWS):
        cols = slice(gi * POOL_GROUP, (gi + 1) * POOL_GROUP)
        cnt = jnp.minimum(tpos + 1, w).astype(F32)
        pooled = sums[w][POOL_PAD:, cols] / cnt - u_ref[:, cols]
        mixed.append(_dot(pooled.astype(BF16), wmix_ref[gi]) * pscale_ref[:, cols])
    mixed = jnp.concatenate(mixed, axis=-1).astype(BF16)
    y_pool = _dot(mixed, wup_ref[...])
    merged = (gate_ref[:, :D_MODEL].astype(F32) * y_attn
              + gate_ref[:, D_MODEL:].astype(F32) * y_pool)
    x1 = x_ref[...] + _dot(merged.astype(BF16), wout_ref[...])
    x1_ref[...] = x1
    h2 = _rms_norm(x1, gffn_ref[...])
    h2_bf = h2.astype(BF16)
    h2_ref[...] = h2_bf

    logits_t = lax.dot_general(wrt_ref[...], h2_bf, (((1,), (1,)), ((), ())),
                               preferred_element_type=F32) + brt_ref[...]
    erow = lax.broadcasted_iota(I32, (N_EXPERTS, TM_POST), 0)
    work = logits_t
    top_v, onehots = [], []
    for _ in range(TOP_K):
        m = jnp.max(work, axis=0, keepdims=True)
        idx = jnp.min(jnp.where(work == m, erow, N_EXPERTS), axis=0, keepdims=True)
        hit = erow == idx
        top_v.append(m)
        onehots.append(hit)
        work = jnp.where(hit, -jnp.inf, work)
    exps = [jnp.exp(v - top_v[0]) for v in top_v]
    denom = exps[0] + exps[1] + exps[2] + exps[3]
    comb = [e / denom for e in exps]
    chosen = jnp.zeros((N_EXPERTS, TM_POST), F32)
    for hit in onehots:
        chosen = chosen + hit.astype(F32)

    e_r = lax.broadcasted_iota(I32, (N_EXPERTS, N_EXPERTS), 0)
    e_c = lax.broadcasted_iota(I32, (N_EXPERTS, N_EXPERTS), 1)
    earlier_expert = (e_c < e_r).astype(BF16)
    t_r = lax.broadcasted_iota(I32, (TT, TT), 0)
    t_c = lax.broadcasted_iota(I32, (TT, TT), 1)
    earlier_token = (t_r < t_c).astype(BF16)
    lane = lax.broadcasted_iota(I32, (N_EXPERTS, LANES), 1)
    counts = jnp.zeros((N_EXPERTS, LANES), F32)
    pos_parts = [[] for _ in range(TOP_K)]
    n_sub = TM_POST // TT
    for s in range(n_sub):
        cols = slice(s * TT, (s + 1) * TT)
        ch = chosen[:, cols]
        cnt = jnp.broadcast_to(jnp.sum(ch, axis=1, keepdims=True), (N_EXPERTS, LANES))
        before = carry[...]
        head = before - SEG_ALIGN * jnp.floor(before / SEG_ALIGN)
        flush = jnp.logical_and(i == pl.num_programs(0) - 1, s == n_sub - 1)
        present = (cnt > 0) | (flush & (head > 0))
        blocks = jnp.where(present, jnp.floor((head + cnt + (SEG_ALIGN - 1)) / SEG_ALIGN), 0.0)
        seg_loc = SEG_ALIGN * _dot(earlier_expert, blocks.astype(BF16))
        base = seg_loc + head
        rank_in_tile = _dot(ch.astype(BF16), earlier_token)
        full = jnp.concatenate([base] * (TT // LANES), axis=1) + rank_in_tile
        for k in range(TOP_K):
            pos_parts[k].append(
                jnp.sum(jnp.where(onehots[k][:, cols], full, 0.0), axis=0, keepdims=True))
        counts = jnp.where(lane == s, cnt, counts)
        carry[...] = before + cnt
    tc_ref[...] = counts

    pos = [jnp.concatenate(parts, axis=1) for parts in pos_parts]
    row8 = lax.broadcasted_iota(I32, (8, TM_POST), 0)
    pos_rows = jnp.full((8, TM_POST), -1.0, F32)
    both = jnp.zeros((8, TM_POST), F32)
    for k in range(TOP_K):
        pos_rows = jnp.where(row8 == k, pos[k], pos_rows)
        both = jnp.where(row8 == k, pos[k], both)
        both = jnp.where(row8 == TOP_K + k, comb[k], both)
    pos_t_ref[...] = pos_rows.astype(I32)
    padded = jnp.concatenate([both, jnp.zeros((LANES - 8, TM_POST), F32)], axis=0)
    pw_ref[...] = padded.T


def _post(attn, u, gates, x2, wo, bo, wmix, pscale, wup, wout, gffn, wr, br):
    n = x2.shape[0]
    row = lambda i: (i, 0)
    const = lambda i: (0, 0)
    halo_blocks = TM_POST // POOL_HALO
    prev = lambda i: (jnp.maximum(i * halo_blocks - 1, 0), 0)
    return pl.pallas_call(
        _post_body,
        grid=(n // TM_POST,),
        in_specs=[
            pl.BlockSpec((TM_POST, Q_WIDTH), row),
            pl.BlockSpec((TM_POST, POOL_WIDTH), row),
            pl.BlockSpec((POOL_HALO, POOL_WIDTH), prev),
            pl.BlockSpec((TM_POST, 2 * D_MODEL), row),
            pl.BlockSpec((TM_POST, D_MODEL), row),
            pl.BlockSpec((Q_WIDTH, D_MODEL), const),
            pl.BlockSpec((1, D_MODEL), const),
            pl.BlockSpec((len(POOL_WINDOWS), POOL_GROUP, POOL_GROUP), lambda i: (0, 0, 0)),
            pl.BlockSpec((1, POOL_WIDTH), const),
            pl.BlockSpec((POOL_WIDTH, D_MODEL), const),
            pl.BlockSpec((D_MODEL, D_MODEL), const),
            pl.BlockSpec((1, D_MODEL), const),
            pl.BlockSpec((N_EXPERTS, D_MODEL), const),
            pl.BlockSpec((N_EXPERTS, TM_POST), const),
        ],
        out_specs=[
            pl.BlockSpec((TM_POST, D_MODEL), row),
            pl.BlockSpec((TM_POST, D_MODEL), row),
            pl.BlockSpec((8, TM_POST), lambda i: (0, i)),
            pl.BlockSpec((TM_POST, LANES), row),
            pl.BlockSpec((N_EXPERTS, LANES), row),
        ],
        out_shape=[
            jax.ShapeDtypeStruct((n, D_MODEL), F32),
            jax.ShapeDtypeStruct((n, D_MODEL), BF16),
            jax.ShapeDtypeStruct((8, n), I32),
            jax.ShapeDtypeStruct((n, LANES), F32),
            jax.ShapeDtypeStruct((n // TM_POST * N_EXPERTS, LANES), F32),
        ],
        scratch_shapes=[pltpu.VMEM((TM_POST + POOL_PAD, POOL_WIDTH), F32),
                        pltpu.VMEM((TM_POST + POOL_PAD, POOL_WIDTH), F32),
                        pltpu.VMEM((N_EXPERTS, LANES), F32)],
        compiler_params=pltpu.CompilerParams(
            dimension_semantics=("arbitrary",), vmem_limit_bytes=VMEM_LIMIT),
        name="post",
    )(attn, u, u, gates, x2, wo, bo, wmix, pscale, wup, wout, gffn, wr, br)


def _start_segments(i, glob_ref, loc_ref, cnt_ref, make_copy):
    def body(e, c):
        idx = i * N_EXPERTS + e
        rows = pl.multiple_of(cnt_ref[idx], SEG_ALIGN)

        @pl.when(rows > 0)
        def _():
            loc = pl.multiple_of(loc_ref[idx], SEG_ALIGN)
            glob = pl.multiple_of(glob_ref[idx], SEG_ALIGN)
            make_copy(pl.ds(loc, rows), pl.ds(glob, rows)).start()
        return c
    lax.fori_loop(0, N_EXPERTS, body, 0, unroll=4)


def _wait_segments(i, total_ref, make_copy):
    rows = pl.multiple_of(total_ref[i], SEG_ALIGN)

    @pl.when(rows > 0)
    def _():
        make_copy(pl.ds(0, rows), pl.ds(0, rows)).wait()


def _pack_halves(v):
    c = v.shape[1] // 2
    bits = lax.bitcast_convert_type(v, U32)
    return (bits[:, :c] >> 16) | bits[:, c:]


def _unpack_halves(w):
    lo = lax.bitcast_convert_type(w << 16, F32)
    hi = lax.bitcast_convert_type(w & jnp.uint32(0xFFFF0000), F32)
    return jnp.concatenate([lo, hi], axis=1).astype(BF16)


def _dispatch_body(glob_ref, loc_ref, rd_ref, wr_ref, total_ref, rows_ref, h2_ref, pos_ref,
                   xs_hbm, local, tail, sem):
    i = pl.program_id(0)
    nt = pl.num_programs(0)
    slot = i % 2

    def copy_from(s):
        def make_copy(loc_rows, glob_rows):
            return pltpu.make_async_copy(local.at[s, loc_rows], xs_hbm.at[glob_rows], sem.at[s])
        return make_copy

    @pl.when(i == 0)
    def _():
        tail[...] = jnp.zeros_like(tail)

    @pl.when(i >= 2)
    def _():
        _wait_segments(i - 2, total_ref, copy_from(slot))

    def group(r0, nrows):
        j = lax.broadcasted_iota(I32, (nrows, TT), 0) + r0
        hit = j == pos_ref[0:1, :]
        for k in range(1, TOP_K):
            hit = hit | (j == pos_ref[k:k + 1, :])
        perm = jnp.where(hit, 1.0, 0.0).astype(BF16)
        local[slot, r0:r0 + nrows, :] = _pack_halves(_dot(perm, h2_ref[...]))

    group(0, LROWS_MAIN)

    @pl.when(rows_ref[i] + SEG_ALIGN > LROWS_MAIN)
    def _():
        group(LROWS_MAIN, LROWS - LROWS_MAIN)

    for e in range(N_EXPERTS):
        idx = i * N_EXPERTS + e
        rd = rd_ref[idx]
        wr = wr_ref[idx]
        loc = pl.multiple_of(loc_ref[idx], SEG_ALIGN)
        first = local[slot, pl.ds(loc, SEG_ALIGN), :]
        last = local[slot, pl.ds(pl.multiple_of(loc + wr, SEG_ALIGN), SEG_ALIGN), :]
        merged = jnp.where(rd > 0, first | tail[e], first)
        local[slot, pl.ds(loc, SEG_ALIGN), :] = merged
        last = jnp.where(wr == 0, merged, last)
        tail[e] = jnp.where(rd > 0, jnp.where(wr < rd, last, jnp.uint32(0)), tail[e])
    _start_segments(i, glob_ref, loc_ref, wr_ref, copy_from(slot))

    @pl.when(i == nt - 1)
    def _():
        _wait_segments(i - 1, total_ref, copy_from(1 - slot))
        _wait_segments(i, total_ref, copy_from(slot))


def _dispatch(seg_glob, seg_loc, seg_rd, seg_wr, tile_wr, tile_rd, h2, pos_t, n_rows):
    n = h2.shape[0]
    grid_spec = pltpu.PrefetchScalarGridSpec(
        num_scalar_prefetch=6,
        grid=(n // TT,),
        in_specs=[
            pl.BlockSpec((TT, D_MODEL), lambda i, *_: (i, 0)),
            pl.BlockSpec((8, TT), lambda i, *_: (0, i)),
        ],
        out_specs=pl.BlockSpec(memory_space=pl.ANY),
        scratch_shapes=[pltpu.VMEM((2, LROWS, PACKED), U32),
                        pltpu.VMEM((N_EXPERTS, SEG_ALIGN, PACKED), U32),
                        pltpu.SemaphoreType.DMA((2,))],
    )
    return pl.pallas_call(
        _dispatch_body,
        grid_spec=grid_spec,
        out_shape=jax.ShapeDtypeStruct((n_rows, PACKED), U32),
        compiler_params=pltpu.CompilerParams(
            dimension_semantics=("arbitrary",), vmem_limit_bytes=VMEM_LIMIT),
        name="dispatch",
    )(seg_glob, seg_loc, seg_rd, seg_wr, tile_wr, tile_rd, h2, pos_t)


def _moe_body(te_ref, nused_ref, rows_ref, next_ref, xs_ref, bias_ref,
              wg_hbm, wu_hbm, wd_hbm, y_ref, wf32, wbf, sem, slot_ref):
    i = pl.program_id(0)

    def fetch(expert, slot):
        return [pltpu.make_async_copy(w.at[expert], wf32.at[slot, m], sem.at[slot, m])
                for m, w in enumerate((wg_hbm, wu_hbm, wd_hbm))]

    @pl.when(i == 0)
    def _():
        slot_ref[0] = 0

    @pl.when(i < nused_ref[0])
    def _():
        expert = te_ref[i]

        @pl.when((i == 0) | (expert != te_ref[jnp.maximum(i - 1, 0)]))
        def _():
            slot = slot_ref[0]

            @pl.when(i == 0)
            def _():
                for cp in fetch(expert, slot):
                    cp.start()
            for cp in fetch(expert, slot):
                cp.wait()
            nxt = next_ref[i]

            @pl.when(nxt >= 0)
            def _():
                for cp in fetch(nxt, 1 - slot):
                    cp.start()
            for m in range(3):
                wbf[m] = wf32[slot, m].astype(BF16)
            slot_ref[0] = 1 - slot

        def mlp(r0):
            rows = slice(r0, r0 + MOE_ROWS)
            row = lax.broadcasted_iota(I32, (MOE_ROWS, 1), 0) + r0
            x = _unpack_halves(jnp.where(row < rows_ref[i], xs_ref[rows, :], jnp.uint32(0)))
            bias = bias_ref[expert]
            g = _dot(x, wbf[0]) + bias[0:1, :]
            u = _dot(x, wbf[1]) + bias[1:2, :]
            g = jnp.minimum(g, SWIGLU_LIMIT)
            u = jnp.clip(u, -SWIGLU_LIMIT, SWIGLU_LIMIT)
            a = (g * jax.nn.sigmoid(SWIGLU_ALPHA * g) * (u + 1.0)).astype(BF16)
            y = _dot(a, wbf[2]) + bias[2:3, :]
            y_ref[rows, :] = _pack_halves(y.astype(BF16).astype(F32))

        mlp(0)
        for r0 in range(MOE_ROWS, TM_MOE, MOE_ROWS):
            @pl.when(rows_ref[i] > r0)
            def _():
                mlp(r0)


def _moe(tile_expert, n_used, tile_rows, tile_next, xs, wg, bg, wu, bu, wd, bd):
    n_rows = xs.shape[0]
    n_tiles = n_rows // TM_MOE
    d_ff = wg.shape[2]
    assert d_ff == D_MODEL
    row = lambda i, te, nu, *_: (jnp.minimum(i, nu[0] - 1), 0)
    bias = jnp.zeros((N_EXPERTS, 8, D_MODEL), F32)
    bias = bias.at[:, 0].set(bg).at[:, 1].set(bu).at[:, 2].set(bd)
    grid_spec = pltpu.PrefetchScalarGridSpec(
        num_scalar_prefetch=4,
        grid=(n_tiles,),
        in_specs=[
            pl.BlockSpec((TM_MOE, PACKED), row),
            pl.BlockSpec((N_EXPERTS, 8, D_MODEL), lambda i, *_: (0, 0, 0)),
            pl.BlockSpec(memory_space=pl.ANY),
            pl.BlockSpec(memory_space=pl.ANY),
            pl.BlockSpec(memory_space=pl.ANY),
        ],
        out_specs=pl.BlockSpec((TM_MOE, PACKED), row),
        scratch_shapes=[pltpu.VMEM((2, 3, D_MODEL, D_MODEL), F32),
                        pltpu.VMEM((3, D_MODEL, D_MODEL), BF16),
                        pltpu.SemaphoreType.DMA((2, 3)),
                        pltpu.SMEM((1,), I32)],
    )
    return pl.pallas_call(
        _moe_body,
        grid_spec=grid_spec,
        out_shape=jax.ShapeDtypeStruct((n_rows, PACKED), U32),
        compiler_params=pltpu.CompilerParams(
            dimension_semantics=("arbitrary",), vmem_limit_bytes=VMEM_LIMIT),
        name="moe",
    )(tile_expert, n_used, tile_rows, tile_next, xs, bias, wg, wu, wd)


def _combine_body(glob_ref, loc_ref, cnt_ref, total_ref, x1_ref, pw_ref, gfin_ref,
                  ys_hbm, o_ref, local, sem):
    i = pl.program_id(0)
    nt = pl.num_programs(0)
    slot = i % 2

    def copy_into(s):
        def make_copy(loc_rows, glob_rows):
            return pltpu.make_async_copy(ys_hbm.at[glob_rows], local.at[s, loc_rows], sem.at[s])
        return make_copy

    @pl.when(i == 0)
    def _():
        local[...] = jnp.zeros_like(local)
        _start_segments(0, glob_ref, loc_ref, cnt_ref, copy_into(0))

    @pl.when(i + 1 < nt)
    def _():
        _start_segments(i + 1, glob_ref, loc_ref, cnt_ref, copy_into(1 - slot))
    _wait_segments(i, total_ref, copy_into(slot))

    j = lax.broadcasted_iota(I32, (TT, LROWS), 1).astype(F32)
    wc = jnp.zeros((TT, LROWS), F32)
    for k in range(TOP_K):
        wc = jnp.where(j == pw_ref[:, k:k + 1], pw_ref[:, TOP_K + k:TOP_K + k + 1], wc)
    acc = x1_ref[...] + _dot(wc.astype(BF16), _unpack_halves(local[slot]))
    o_ref[...] = _rms_norm(acc, gfin_ref[...])


def _combine(seg_glob, seg_loc, seg_cnt, tile_rd, x1, pw, gfin, ys):
    n = x1.shape[0]
    grid_spec = pltpu.PrefetchScalarGridSpec(
        num_scalar_prefetch=4,
        grid=(n // TT,),
        in_specs=[
            pl.BlockSpec((TT, D_MODEL), lambda i, *_: (i, 0)),
            pl.BlockSpec((TT, LANES), lambda i, *_: (i, 0)),
            pl.BlockSpec((1, D_MODEL), lambda i, *_: (0, 0)),
            pl.BlockSpec(memory_space=pl.ANY),
        ],
        out_specs=pl.BlockSpec((TT, D_MODEL), lambda i, *_: (i, 0)),
        scratch_shapes=[pltpu.VMEM((2, LROWS, PACKED), U32),
                        pltpu.SemaphoreType.DMA((2,))],
    )
    return pl.pallas_call(
        _combine_body,
        grid_spec=grid_spec,
        out_shape=jax.ShapeDtypeStruct((n, D_MODEL), F32),
        compiler_params=pltpu.CompilerParams(
            dimension_semantics=("arbitrary",), vmem_limit_bytes=VMEM_LIMIT),
        name="combine",
    )(seg_glob, seg_loc, seg_cnt, tile_rd, x1, pw, gfin, ys)


def _layer(x2, pos_b, invf, norm_mix_g, w_in, b_in, attn_sinks, w_o_attn, b_o_attn,
           w_pool_mix, pool_scale, w_pool_up, w_out, norm_ffn_g, w_router, b_router,
           w_gate, b_gate, w_up, b_up, w_down, b_down, out_g):
    n = x2.shape[0]
    q, kb, vb, u, gates = _inproj(
        x2, pos_b, invf, norm_mix_g[None, :], w_in.astype(BF16), b_in[None, :])
    attn = _attn(attn_sinks, q, kb, vb)

    wrt = w_router.T.astype(BF16)
    brt = jnp.broadcast_to(b_router[:, None], (N_EXPERTS, TM_POST))
    x1, h2, pos_t, pw, tc = _post(
        attn, u, gates, x2, w_o_attn.astype(BF16), b_o_attn[None, :],
        w_pool_mix.astype(BF16), pool_scale[None, :], w_pool_up.astype(BF16),
        w_out.astype(BF16), norm_ffn_g[None, :], wrt, brt)

    nt = n // TT
    sub = TM_POST // TT
    seg_cnt = (tc.reshape(n // TM_POST, N_EXPERTS, LANES)[:, :, :sub]
               .transpose(0, 2, 1).reshape(nt, N_EXPERTS).astype(I32))
    before = jnp.cumsum(seg_cnt, axis=0) - seg_cnt
    sizes = jnp.sum(seg_cnt, axis=0)
    padded = (sizes + TM_MOE - 1) // TM_MOE * TM_MOE
    pends = jnp.cumsum(padded)
    pstarts = pends - padded
    head = before % SEG_ALIGN
    last_tile = (jnp.arange(nt, dtype=I32) == nt - 1)[:, None]
    blocks_up = (head + seg_cnt + SEG_ALIGN - 1) // SEG_ALIGN
    blocks_dn = (head + seg_cnt) // SEG_ALIGN
    seg_rd = jnp.where((seg_cnt > 0) | (last_tile & (head > 0)), blocks_up, 0) * SEG_ALIGN
    seg_wr = jnp.where(last_tile, seg_rd, jnp.where(seg_cnt > 0, blocks_dn * SEG_ALIGN, 0))
    seg_loc = jnp.cumsum(seg_rd, axis=1) - seg_rd
    seg_glob = pstarts[None, :] + before - head
    tile_rd = jnp.sum(seg_rd, axis=1)
    tile_wr = jnp.sum(seg_wr, axis=1)
    seg_glob, seg_loc, seg_rd, seg_wr = (a.reshape(-1) for a in (seg_glob, seg_loc, seg_rd, seg_wr))
    m = n * TOP_K
    n_tiles = (m + N_EXPERTS * (TM_MOE - 1) + TM_MOE - 1) // TM_MOE
    tile_start = jnp.arange(n_tiles, dtype=I32) * TM_MOE
    tile_expert = jnp.minimum(
        jnp.sum(tile_start[:, None] >= pends[None, :], axis=-1), N_EXPERTS - 1).astype(I32)
    own = tile_expert[:, None] == jnp.arange(N_EXPERTS, dtype=I32)[None, :]
    left = jnp.sum(jnp.where(own, (pstarts + sizes)[None, :] - tile_start[:, None], 0), axis=-1)
    tile_rows = jnp.clip(left, 0, TM_MOE)
    ids = jnp.arange(N_EXPERTS, dtype=I32)
    later = jnp.where((sizes > 0)[None, :] & (ids[None, :] > ids[:, None]), ids[None, :], N_EXPERTS)
    next_expert = jnp.min(later, axis=1)
    next_expert = jnp.where(next_expert == N_EXPERTS, -1, next_expert)
    tile_next = jnp.sum(jnp.where(own, next_expert[None, :], 0), axis=-1).astype(I32)
    n_used = (pends[-1] // TM_MOE).astype(I32)[None]

    xs = _dispatch(seg_glob, seg_loc, seg_rd, seg_wr, tile_wr, tile_rd, h2, pos_t, n_tiles * TM_MOE)
    ys = _moe(tile_expert, n_used, tile_rows.astype(I32), tile_next, xs,
              w_gate, b_gate, w_up, b_up, w_down, b_down)
    return _combine(seg_glob, seg_loc, seg_rd, tile_rd, x1, pw, out_g[None, :], ys)


def kernel(x, positions, norm_mix_g, w_in, b_in, attn_sinks, w_o_attn, b_o_attn, w_pool_mix,
           pool_scale, w_pool_up, w_out, norm_ffn_g, w_router, b_router, w_gate, b_gate,
           w_up, b_up, w_down, b_down, norm_final_g):
    b, s, d = x.shape
    depth = w_in.shape[0]
    assert (s, d, depth) == (SEQ, D_MODEL, 1)
    n = b * s
    x2 = x.reshape(n, d)
    pos_b = jnp.repeat(positions.reshape(n // 4, 4), LANES // 4, axis=1)
    inv_freq = ROPE_THETA ** (-jnp.arange(0, HEAD_DIM, 2, dtype=F32) / HEAD_DIM)
    invf = jnp.tile(inv_freq, LANES // (HEAD_DIM // 2))[None, :]
    out = _layer(x2, pos_b, invf, norm_mix_g[0], w_in[0], b_in[0], attn_sinks[0], w_o_attn[0],
                 b_o_attn[0], w_pool_mix[0], pool_scale[0], w_pool_up[0], w_out[0],
                 norm_ffn_g[0], w_router[0], b_router[0], w_gate[0], b_gate[0], w_up[0],
                 b_up[0], w_down[0], b_down[0], norm_final_g)
    return out.reshape(b, s, d)
```

```python
import functools

import numpy as np
import jax
import jax.numpy as jnp
from jax import lax
from jax.experimental import pallas as pl
from jax.experimental.pallas import tpu as pltpu

F32 = jnp.float32
BF16 = jnp.bfloat16
I32 = jnp.int32
U32 = jnp.uint32

D_MODEL = 1024
SEQ = 4096
HEAD_DIM = 64
N_Q_HEADS = 16
WINDOW = 128
ROPE_THETA = 10000.0
Q_WIDTH = N_Q_HEADS * HEAD_DIM
KV_WIDTH = 128
POOL_WINDOWS = (2, 4, 8, 16)
POOL_WIDTH = 512
POOL_GROUP = 128
POOL_HALO = 16
POOL_PAD = 32
N_EXPERTS = 32
TOP_K = 4
SWIGLU_LIMIT = 7.0
SWIGLU_ALPHA = 1.702
RMS_EPS = 1e-5
NEG_BIG = -1e30
LOG2_E = 1.4426950408889634

LANES = 128
TM_IN = 1024
TQ = 512
TM_POST = 512
TM_MOE = 1024
MOE_ROWS = 512
TT = 256
SEG_ALIGN = 8
LROWS = TT * TOP_K + 2 * N_EXPERTS * SEG_ALIGN
LROWS_MAIN = LROWS - 256
PACKED = D_MODEL // 2
VMEM_LIMIT = 56 * 1024 * 1024


def _rms_norm(x, g):
    ms = jnp.mean(x * x, axis=-1, keepdims=True)
    return (x * lax.rsqrt(ms + RMS_EPS)) * g


def _dot(a, b):
    return jnp.dot(a, b, preferred_element_type=F32)


def _inproj_body(x_ref, pos_ref, invf_ref, g_ref, w_ref, b_ref,
                 q_ref, kb_ref, vb_ref, u_ref, gate_ref, cos_scr, sin_scr):
    h = _rms_norm(x_ref[...], g_ref[...]).astype(BF16)

    theta = pos_ref[...].astype(F32) * invf_ref[...]
    lane4 = lax.broadcasted_iota(I32, theta.shape, 1)
    for packed, scr in ((jnp.cos(theta), cos_scr), (jnp.sin(theta), sin_scr)):
        for jt in range(4):
            seg = packed if jt == 0 else pltpu.roll(packed, LANES - 32 * jt, 1)
            seg = jnp.where(lane4 < 32, seg, pltpu.roll(seg, 32, 1))
            seg = jnp.where(lane4 < 64, seg, pltpu.roll(seg, 64, 1))
            scr[pl.ds(jt, TM_IN // 4, stride=4), :] = seg
    cos = cos_scr[...]
    sin = sin_scr[...]
    lane = lax.broadcasted_iota(I32, cos.shape, 1)
    first_half = (lane & 32) == 0
    sin_signed = jnp.where(first_half, -sin, sin)
    low_head = lane < HEAD_DIM

    def rope(t):
        swapped = jnp.where(first_half, pltpu.roll(t, 96, 1), pltpu.roll(t, 32, 1))
        return t * cos + swapped * sin_signed

    def band_layout(t, out_ref):
        tr = pltpu.roll(t, 64, 1)
        zero = jnp.zeros_like(t)
        chunks = (jnp.where(low_head, t, zero), jnp.where(low_head, zero, tr),
                  jnp.where(low_head, tr, zero), jnp.where(low_head, zero, t))
        for c, val in enumerate(chunks):
            out_ref[:, c * LANES:(c + 1) * LANES] = val.astype(BF16)

    qk_w = Q_WIDTH + KV_WIDTH
    zqk = _dot(h, w_ref[:, :qk_w]) + b_ref[:, :qk_w]
    scale = HEAD_DIM ** -0.5 * LOG2_E
    for j in range(Q_WIDTH // LANES):
        sl = slice(j * LANES, (j + 1) * LANES)
        q_ref[:, sl] = (rope(zqk[:, sl]) * scale).astype(BF16)
    band_layout(rope(zqk[:, Q_WIDTH:qk_w]), kb_ref)

    v0 = qk_w
    zv = _dot(h, w_ref[:, v0:v0 + KV_WIDTH]) + b_ref[:, v0:v0 + KV_WIDTH]
    band_layout(zv, vb_ref)

    u0 = v0 + KV_WIDTH
    u_ref[...] = _dot(h, w_ref[:, u0:u0 + POOL_WIDTH]) + b_ref[:, u0:u0 + POOL_WIDTH]

    g0 = u0 + POOL_WIDTH
    for c in range(2):
        sl = slice(g0 + c * D_MODEL, g0 + (c + 1) * D_MODEL)
        zg = _dot(h, w_ref[:, sl]) + b_ref[:, sl]
        gate_ref[:, c * D_MODEL:(c + 1) * D_MODEL] = jax.nn.sigmoid(zg).astype(BF16)


def _inproj(x2, pos_b, invf, g, w, b):
    n = x2.shape[0]
    in_width = w.shape[1]
    row = lambda i: (i, 0)
    const = lambda i: (0, 0)
    return pl.pallas_call(
        _inproj_body,
        grid=(n // TM_IN,),
        in_specs=[
            pl.BlockSpec((TM_IN, D_MODEL), row),
            pl.BlockSpec((TM_IN // 4, LANES), row),
            pl.BlockSpec((1, LANES), const),
            pl.BlockSpec((1, D_MODEL), const),
            pl.BlockSpec((D_MODEL, in_width), const),
            pl.BlockSpec((1, in_width), const),
        ],
        out_specs=[
            pl.BlockSpec((TM_IN, Q_WIDTH), row),
            pl.BlockSpec((TM_IN, 4 * LANES), row),
            pl.BlockSpec((TM_IN, 4 * LANES), row),
            pl.BlockSpec((TM_IN, POOL_WIDTH), row),
            pl.BlockSpec((TM_IN, 2 * D_MODEL), row),
        ],
        out_shape=[
            jax.ShapeDtypeStruct((n, Q_WIDTH), BF16),
            jax.ShapeDtypeStruct((n, 4 * LANES), BF16),
            jax.ShapeDtypeStruct((n, 4 * LANES), BF16),
            jax.ShapeDtypeStruct((n, POOL_WIDTH), F32),
            jax.ShapeDtypeStruct((n, 2 * D_MODEL), BF16),
        ],
        scratch_shapes=[pltpu.VMEM((TM_IN, LANES), F32), pltpu.VMEM((TM_IN, LANES), F32)],
        compiler_params=pltpu.CompilerParams(
            dimension_semantics=("arbitrary",), vmem_limit_bytes=VMEM_LIMIT),
        name="inproj",
    )(x2, pos_b, invf, g, w, b)


def _attn_body(sinks_ref, q_ref, kbc_ref, kbp_ref, vbc_ref, vbp_ref, bias_ref,
               o_ref, p_scr):
    t = pl.program_id(0)
    seq_first = (t % (SEQ // TQ)) == 0
    lane = lax.broadcasted_iota(I32, (WINDOW, LANES), 1)
    low_head = lane < HEAD_DIM
    key_row = lax.broadcasted_iota(I32, (4 * WINDOW, LANES), 0)
    key_lane = lax.broadcasted_iota(I32, (4 * WINDOW, LANES), 1)
    head_ones = jnp.where((key_row < 2 * WINDOW) == (key_lane < HEAD_DIM), 1.0, 0.0).astype(BF16)
    for n in range(TQ // WINDOW):
        rows = slice(n * WINDOW, (n + 1) * WINDOW)
        if n == 0:
            k_prev, v_prev = kbp_ref[...], vbp_ref[...]
            bias = jnp.where(seq_first, bias_ref[1], bias_ref[0])
        else:
            prev_rows = slice((n - 1) * WINDOW, n * WINDOW)
            k_prev, v_prev = kbc_ref[prev_rows, :], vbc_ref[prev_rows, :]
            bias = bias_ref[0]
        k_cur, v_cur = kbc_ref[rows, :], vbc_ref[rows, :]
        for g in range(2):
            def stack(prev, cur):
                lo = slice((2 * g) * LANES, (2 * g + 1) * LANES)
                hi = slice((2 * g + 1) * LANES, (2 * g + 2) * LANES)
                return jnp.concatenate([prev[:, lo], cur[:, lo], prev[:, hi], cur[:, hi]], axis=0)
            kmat = stack(k_prev, k_cur)
            vmat = stack(v_prev, v_cur)
            qs = jnp.concatenate(
                [q_ref[rows, (4 * g + p) * LANES:(4 * g + p + 1) * LANES] for p in range(4)],
                axis=0)
            s = lax.dot_general(qs, kmat, (((1,), (1,)), ((), ())),
                                preferred_element_type=F32) + bias
            sink_terms = []
            for p in range(4):
                pr = slice(p * WINDOW, (p + 1) * WINDOW)
                ms = []
                for j in range(2):
                    cols = slice(j * 2 * WINDOW, (j + 1) * 2 * WINDOW)
                    sp = s[pr, cols]
                    sink = sinks_ref[8 * g + 2 * p + j] * LOG2_E
                    m = jnp.maximum(jnp.max(sp, axis=-1, keepdims=True), sink)
                    p_scr[pr, cols] = jnp.exp2(sp - m).astype(BF16)
                    ms.append(m)
                sink_pair = jnp.where(low_head, sinks_ref[8 * g + 2 * p] * LOG2_E,
                                      sinks_ref[8 * g + 2 * p + 1] * LOG2_E)
                sink_terms.append(jnp.exp2(sink_pair - jnp.where(low_head, ms[0], ms[1])))
            o = _dot(p_scr[...], jnp.concatenate([vmat, head_ones], axis=1))
            for p in range(4):
                pr = slice(p * WINDOW, (p + 1) * WINDOW)
                inv = 1.0 / (o[pr, LANES:] + sink_terms[p])
                o_ref[rows, (4 * g + p) * LANES:(4 * g + p + 1) * LANES] = (o[pr, :LANES] * inv).astype(BF16)


def _attn_bias():
    r = np.arange(4 * WINDOW)[:, None] % WINDOW
    c = np.arange(4 * WINDOW)[None, :] % (2 * WINDOW)
    band = (c > r) & (c <= r + WINDOW)
    first = band & (c >= WINDOW)
    return np.stack([np.where(band, 0.0, NEG_BIG), np.where(first, 0.0, NEG_BIG)]).astype(np.float32)


def _attn(sinks, q, kb, vb):
    n = q.shape[0]
    blocks_per_tile = TQ // WINDOW
    cur = lambda t: (t, 0)
    prev = lambda t: (jnp.maximum(t * blocks_per_tile - 1, 0), 0)
    bias = jnp.asarray(_attn_bias())
    return pl.pallas_call(
        _attn_body,
        grid=(n // TQ,),
        in_specs=[
            pl.BlockSpec(memory_space=pltpu.SMEM),
            pl.BlockSpec((TQ, Q_WIDTH), cur),
            pl.BlockSpec((TQ, 4 * LANES), cur),
            pl.BlockSpec((WINDOW, 4 * LANES), prev),
            pl.BlockSpec((TQ, 4 * LANES), cur),
            pl.BlockSpec((WINDOW, 4 * LANES), prev),
            pl.BlockSpec((2, 4 * WINDOW, 4 * WINDOW), lambda t: (0, 0, 0)),
        ],
        out_specs=pl.BlockSpec((TQ, Q_WIDTH), cur),
        out_shape=jax.ShapeDtypeStruct((n, Q_WIDTH), BF16),
        scratch_shapes=[pltpu.VMEM((4 * WINDOW, 4 * WINDOW), BF16)],
        compiler_params=pltpu.CompilerParams(
            dimension_semantics=("arbitrary",), vmem_limit_bytes=VMEM_LIMIT),
        name="attn",
    )(sinks, q, kb, kb, vb, vb, bias)


def _post_body(attn_ref, u_ref, uprev_ref, gate_ref, x_ref,
               wo_ref, bo_ref, wmix_ref, pscale_ref, wup_ref, wout_ref,
               gffn_ref, wrt_ref, brt_ref,
               x1_ref, h2_ref, pos_t_ref, pw_ref, tc_ref,
               win_a, win_b, carry):
    i = pl.program_id(0)
    tiles_per_seq = SEQ // TM_POST
    seq_first = (i % tiles_per_seq) == 0

    @pl.when(i == 0)
    def _():
        carry[...] = jnp.zeros_like(carry)

    y_attn = _dot(attn_ref[...], wo_ref[...]) + bo_ref[...]

    rows = TM_POST + POOL_PAD
    win_a[0:POOL_PAD - POOL_HALO, :] = jnp.zeros((POOL_PAD - POOL_HALO, POOL_WIDTH), F32)
    win_a[POOL_PAD - POOL_HALO:POOL_PAD, :] = jnp.where(seq_first, 0.0, uprev_ref[...])
    win_a[POOL_PAD:, :] = u_ref[...]
    src, dst = win_a, win_b
    sums = {}
    for level, w in enumerate((1, 2, 4, 8)):
        lo = 8 * (level + 1)
        c0 = level * POOL_GROUP
        dst[lo:, c0:] = src[lo:, c0:] + src[lo - w:rows - w, c0:]
        sums[2 * w] = dst
        src, dst = dst, (win_a if dst is win_b else win_b)
    row = lax.broadcasted_iota(I32, (TM_POST, 1), 0)
    tpos = (i % tiles_per_seq) * TM_POST + row
    mixed = []
    for gi, w in enumerate(POOL_WINDOWS):
        cols = slice(gi * POOL_GROUP, (gi + 1) * POOL_GROUP)
        cnt = jnp.minimum(tpos + 1, w).astype(F32)
        pooled = sums[w][POOL_PAD:, cols] / cnt - u_ref[:, cols]
        mixed.append(_dot(pooled.astype(BF16), wmix_ref[gi]) * pscale_ref[:, cols])
    mixed = jnp.concatenate(mixed, axis=-1).astype(BF16)
    y_pool = _dot(mixed, wup_ref[...])
    merged = (gate_ref[:, :D_MODEL].astype(F32) * y_attn
              + gate_ref[:, D_MODEL:].astype(F32) * y_pool)
    x1 = x_ref[...] + _dot(merged.astype(BF16), wout_ref[...])
    x1_ref[...] = x1
    h2 = _rms_norm(x1, gffn_ref[...])
    h2_bf = h2.astype(BF16)
    h2_ref[...] = h2_bf

    logits_t = lax.dot_general(wrt_ref[...], h2_bf, (((1,), (1,)), ((), ())),
                               preferred_element_type=F32) + brt_ref[...]
    erow = lax.broadcasted_iota(I32, (N_EXPERTS, TM_POST), 0)
    work = logits_t
    top_v, onehots = [], []
    for _ in range(TOP_K):
        m = jnp.max(work, axis=0, keepdims=True)
        idx = jnp.min(jnp.where(work == m, erow, N_EXPERTS), axis=0, keepdims=True)
        hit = erow == idx
        top_v.append(m)
        onehots.append(hit)
        work = jnp.where(hit, -jnp.inf, work)
    exps = [jnp.exp(v - top_v[0]) for v in top_v]
    denom = exps[0] + exps[1] + exps[2] + exps[3]
    comb = [e / denom for e in exps]
    chosen = jnp.zeros((N_EXPERTS, TM_POST), F32)
    for hit in onehots:
        chosen = chosen + hit.astype(F32)

    e_r = lax.broadcasted_iota(I32, (N_EXPERTS, N_EXPERTS), 0)
    e_c = lax.broadcasted_iota(I32, (N_EXPERTS, N_EXPERTS), 1)
    earlier_expert = (e_c < e_r).astype(BF16)
    t_r = lax.broadcasted_iota(I32, (TT, TT), 0)
    t_c = lax.broadcasted_iota(I32, (TT, TT), 1)
    earlier_token = (t_r < t_c).astype(BF16)
    lane = lax.broadcasted_iota(I32, (N_EXPERTS, LANES), 1)
    counts = jnp.zeros((N_EXPERTS, LANES), F32)
    pos_parts = [[] for _ in range(TOP_K)]
    n_sub = TM_POST // TT
    for s in range(n_sub):
        cols = slice(s * TT, (s + 1) * TT)
        ch = chosen[:, cols]
        cnt = jnp.broadcast_to(jnp.sum(ch, axis=1, keepdims=True), (N_EXPERTS, LANES))
        before = carry[...]
        head = before - SEG_ALIGN * jnp.floor(before / SEG_ALIGN)
        flush = jnp.logical_and(i == pl.num_programs(0) - 1, s == n_sub - 1)
        present = (cnt > 0) | (flush & (head > 0))
        blocks = jnp.where(present, jnp.floor((head + cnt + (SEG_ALIGN - 1)) / SEG_ALIGN), 0.0)
        seg_loc = SEG_ALIGN * _dot(earlier_expert, blocks.astype(BF16))
        base = seg_loc + head
        rank_in_tile = _dot(ch.astype(BF16), earlier_token)
        full = jnp.concatenate([base] * (TT // LANES), axis=1) + rank_in_tile
        for k in range(TOP_K):
            pos_parts[k].append(
                jnp.sum(jnp.where(onehots[k][:, cols], full, 0.0), axis=0, keepdims=True))
        counts = jnp.where(lane == s, cnt, counts)
        carry[...] = before + cnt
    tc_ref[...] = counts

    pos = [jnp.concatenate(parts, axis=1) for parts in pos_parts]
    row8 = lax.broadcasted_iota(I32, (8, TM_POST), 0)
    pos_rows = jnp.full((8, TM_POST), -1.0, F32)
    both = jnp.zeros((8, TM_POST), F32)
    for k in range(TOP_K):
        pos_rows = jnp.where(row8 == k, pos[k], pos_rows)
        both = jnp.where(row8 == k, pos[k], both)
        both = jnp.where(row8 == TOP_K + k, comb[k], both)
    pos_t_ref[...] = pos_rows.astype(I32)
    padded = jnp.concatenate([both, jnp.zeros((LANES - 8, TM_POST), F32)], axis=0)
    pw_ref[...] = padded.T


def _post(attn, u, gates, x2, wo, bo, wmix, pscale, wup, wout, gffn, wr, br):
    n = x2.shape[0]
    row = lambda i: (i, 0)
    const = lambda i: (0, 0)
    halo_blocks = TM_POST // POOL_HALO
    prev = lambda i: (jnp.maximum(i * halo_blocks - 1, 0), 0)
    return pl.pallas_call(
        _post_body,
        grid=(n // TM_POST,),
        in_specs=[
            pl.BlockSpec((TM_POST, Q_WIDTH), row),
            pl.BlockSpec((TM_POST, POOL_WIDTH), row),
            pl.BlockSpec((POOL_HALO, POOL_WIDTH), prev),
            pl.BlockSpec((TM_POST, 2 * D_MODEL), row),
            pl.BlockSpec((TM_POST, D_MODEL), row),
            pl.BlockSpec((Q_WIDTH, D_MODEL), const),
            pl.BlockSpec((1, D_MODEL), const),
            pl.BlockSpec((len(POOL_WINDOWS), POOL_GROUP, POOL_GROUP), lambda i: (0, 0, 0)),
            pl.BlockSpec((1, POOL_WIDTH), const),
            pl.BlockSpec((POOL_WIDTH, D_MODEL), const),
            pl.BlockSpec((D_MODEL, D_MODEL), const),
            pl.BlockSpec((1, D_MODEL), const),
            pl.BlockSpec((N_EXPERTS, D_MODEL), const),
            pl.BlockSpec((N_EXPERTS, TM_POST), const),
        ],
        out_specs=[
            pl.BlockSpec((TM_POST, D_MODEL), row),
            pl.BlockSpec((TM_POST, D_MODEL), row),
            pl.BlockSpec((8, TM_POST), lambda i: (0, i)),
            pl.BlockSpec((TM_POST, LANES), row),
            pl.BlockSpec((N_EXPERTS, LANES), row),
        ],
        out_shape=[
            jax.ShapeDtypeStruct((n, D_MODEL), F32),
            jax.ShapeDtypeStruct((n, D_MODEL), BF16),
            jax.ShapeDtypeStruct((8, n), I32),
            jax.ShapeDtypeStruct((n, LANES), F32),
            jax.ShapeDtypeStruct((n // TM_POST * N_EXPERTS, LANES), F32),
        ],
        scratch_shapes=[pltpu.VMEM((TM_POST + POOL_PAD, POOL_WIDTH), F32),
                        pltpu.VMEM((TM_POST + POOL_PAD, POOL_WIDTH), F32),
                        pltpu.VMEM((N_EXPERTS, LANES), F32)],
        compiler_params=pltpu.CompilerParams(
            dimension_semantics=("arbitrary",), vmem_limit_bytes=VMEM_LIMIT),
        name="post",
    )(attn, u, u, gates, x2, wo, bo, wmix, pscale, wup, wout, gffn, wr, br)


def _start_segments(i, glob_ref, loc_ref, cnt_ref, make_copy):
    def body(e, c):
        idx = i * N_EXPERTS + e
        rows = pl.multiple_of(cnt_ref[idx], SEG_ALIGN)

        @pl.when(rows > 0)
        def _():
            loc = pl.multiple_of(loc_ref[idx], SEG_ALIGN)
            glob = pl.multiple_of(glob_ref[idx], SEG_ALIGN)
            make_copy(pl.ds(loc, rows), pl.ds(glob, rows)).start()
        return c
    lax.fori_loop(0, N_EXPERTS, body, 0, unroll=4)


def _wait_segments(i, total_ref, make_copy):
    rows = pl.multiple_of(total_ref[i], SEG_ALIGN)

    @pl.when(rows > 0)
    def _():
        make_copy(pl.ds(0, rows), pl.ds(0, rows)).wait()


def _pack_halves(v):
    c = v.shape[1] // 2
    bits = lax.bitcast_convert_type(v, U32)
    return (bits[:, :c] >> 16) | bits[:, c:]


def _unpack_halves(w):
    lo = lax.bitcast_convert_type(w << 16, F32)
    hi = lax.bitcast_convert_type(w & jnp.uint32(0xFFFF0000), F32)
    return jnp.concatenate([lo, hi], axis=1).astype(BF16)


def _dispatch_body(glob_ref, loc_ref, rd_ref, wr_ref, total_ref, rows_ref, h2_ref, pos_ref,
                   xs_hbm, local, tail, sem):
    i = pl.program_id(0)
    nt = pl.num_programs(0)
    slot = i % 2

    def copy_from(s):
        def make_copy(loc_rows, glob_rows):
            return pltpu.make_async_copy(local.at[s, loc_rows], xs_hbm.at[glob_rows], sem.at[s])
        return make_copy

    @pl.when(i == 0)
    def _():
        tail[...] = jnp.zeros_like(tail)

    @pl.when(i >= 2)
    def _():
        _wait_segments(i - 2, total_ref, copy_from(slot))

    def group(r0, nrows):
        j = lax.broadcasted_iota(I32, (nrows, TT), 0) + r0
        hit = j == pos_ref[0:1, :]
        for k in range(1, TOP_K):
            hit = hit | (j == pos_ref[k:k + 1, :])
        perm = jnp.where(hit, 1.0, 0.0).astype(BF16)
        local[slot, r0:r0 + nrows, :] = _pack_halves(_dot(perm, h2_ref[...]))

    group(0, LROWS_MAIN)

    @pl.when(rows_ref[i] + SEG_ALIGN > LROWS_MAIN)
    def _():
        group(LROWS_MAIN, LROWS - LROWS_MAIN)

    for e in range(N_EXPERTS):
        idx = i * N_EXPERTS + e
        rd = rd_ref[idx]
        wr = wr_ref[idx]
        loc = pl.multiple_of(loc_ref[idx], SEG_ALIGN)
        first = local[slot, pl.ds(loc, SEG_ALIGN), :]
        last = local[slot, pl.ds(pl.multiple_of(loc + wr, SEG_ALIGN), SEG_ALIGN), :]
        merged = jnp.where(rd > 0, first | tail[e], first)
        local[slot, pl.ds(loc, SEG_ALIGN), :] = merged
        last = jnp.where(wr == 0, merged, last)
        tail[e] = jnp.where(rd > 0, jnp.where(wr < rd, last, jnp.uint32(0)), tail[e])
    _start_segments(i, glob_ref, loc_ref, wr_ref, copy_from(slot))

    @pl.when(i == nt - 1)
    def _():
        _wait_segments(i - 1, total_ref, copy_from(1 - slot))
        _wait_segments(i, total_ref, copy_from(slot))


def _dispatch(seg_glob, seg_loc, seg_rd, seg_wr, tile_wr, tile_rd, h2, pos_t, n_rows):
    n = h2.shape[0]
    grid_spec = pltpu.PrefetchScalarGridSpec(
        num_scalar_prefetch=6,
        grid=(n // TT,),
        in_specs=[
            pl.BlockSpec((TT, D_MODEL), lambda i, *_: (i, 0)),
            pl.BlockSpec((8, TT), lambda i, *_: (0, i)),
        ],
        out_specs=pl.BlockSpec(memory_space=pl.ANY),
        scratch_shapes=[pltpu.VMEM((2, LROWS, PACKED), U32),
                        pltpu.VMEM((N_EXPERTS, SEG_ALIGN, PACKED), U32),
                        pltpu.SemaphoreType.DMA((2,))],
    )
    return pl.pallas_call(
        _dispatch_body,
        grid_spec=grid_spec,
        out_shape=jax.ShapeDtypeStruct((n_rows, PACKED), U32),
        compiler_params=pltpu.CompilerParams(
            dimension_semantics=("arbitrary",), vmem_limit_bytes=VMEM_LIMIT),
        name="dispatch",
    )(seg_glob, seg_loc, seg_rd, seg_wr, tile_wr, tile_rd, h2, pos_t)


def _moe_body(te_ref, nused_ref, rows_ref, next_ref, xs_ref, bias_ref,
              wg_hbm, wu_hbm, wd_hbm, y_ref, wf32, wbf, sem, slot_ref):
    i = pl.program_id(0)

    def fetch(expert, slot):
        return [pltpu.make_async_copy(w.at[expert], wf32.at[slot, m], sem.at[slot, m])
                for m, w in enumerate((wg_hbm, wu_hbm, wd_hbm))]

    @pl.when(i == 0)
    def _():
        slot_ref[0] = 0

    @pl.when(i < nused_ref[0])
    def _():
        expert = te_ref[i]

        @pl.when((i == 0) | (expert != te_ref[jnp.maximum(i - 1, 0)]))
        def _():
            slot = slot_ref[0]

            @pl.when(i == 0)
            def _():
                for cp in fetch(expert, slot):
                    cp.start()
            for cp in fetch(expert, slot):
                cp.wait()
            nxt = next_ref[i]

            @pl.when(nxt >= 0)
            def _():
                for cp in fetch(nxt, 1 - slot):
                    cp.start()
            for m in range(3):
                wbf[m] = wf32[slot, m].astype(BF16)
            slot_ref[0] = 1 - slot

        def mlp(r0):
            rows = slice(r0, r0 + MOE_ROWS)
            row = lax.broadcasted_iota(I32, (MOE_ROWS, 1), 0) + r0
            x = _unpack_halves(jnp.where(row < rows_ref[i], xs_ref[rows, :], jnp.uint32(0)))
            bias = bias_ref[expert]
            g = _dot(x, wbf[0]) + bias[0:1, :]
            u = _dot(x, wbf[1]) + bias[1:2, :]
            g = jnp.minimum(g, SWIGLU_LIMIT)
            u = jnp.clip(u, -SWIGLU_LIMIT, SWIGLU_LIMIT)
            a = (g * jax.nn.sigmoid(SWIGLU_ALPHA * g) * (u + 1.0)).astype(BF16)
            y = _dot(a, wbf[2]) + bias[2:3, :]
            y_ref[rows, :] = _pack_halves(y.astype(BF16).astype(F32))

        mlp(0)
        for r0 in range(MOE_ROWS, TM_MOE, MOE_ROWS):
            @pl.when(rows_ref[i] > r0)
            def _():
                mlp(r0)


def _moe(tile_expert, n_used, tile_rows, tile_next, xs, wg, bg, wu, bu, wd, bd):
    n_rows = xs.shape[0]
    n_tiles = n_rows // TM_MOE
    d_ff = wg.shape[2]
    assert d_ff == D_MODEL
    row = lambda i, te, nu, *_: (jnp.minimum(i, nu[0] - 1), 0)
    bias = jnp.zeros((N_EXPERTS, 8, D_MODEL), F32)
    bias = bias.at[:, 0].set(bg).at[:, 1].set(bu).at[:, 2].set(bd)
    grid_spec = pltpu.PrefetchScalarGridSpec(
        num_scalar_prefetch=4,
        grid=(n_tiles,),
        in_specs=[
            pl.BlockSpec((TM_MOE, PACKED), row),
            pl.BlockSpec((N_EXPERTS, 8, D_MODEL), lambda i, *_: (0, 0, 0)),
            pl.BlockSpec(memory_space=pl.ANY),
            pl.BlockSpec(memory_space=pl.ANY),
            pl.BlockSpec(memory_space=pl.ANY),
        ],
        out_specs=pl.BlockSpec((TM_MOE, PACKED), row),
        scratch_shapes=[pltpu.VMEM((2, 3, D_MODEL, D_MODEL), F32),
                        pltpu.VMEM((3, D_MODEL, D_MODEL), BF16),
                        pltpu.SemaphoreType.DMA((2, 3)),
                        pltpu.SMEM((1,), I32)],
    )
    return pl.pallas_call(
        _moe_body,
        grid_spec=grid_spec,
        out_shape=jax.ShapeDtypeStruct((n_rows, PACKED), U32),
        compiler_params=pltpu.CompilerParams(
            dimension_semantics=("arbitrary",), vmem_limit_bytes=VMEM_LIMIT),
        name="moe",
    )(tile_expert, n_used, tile_rows, tile_next, xs, bias, wg, wu, wd)


def _combine_body(glob_ref, loc_ref, cnt_ref, total_ref, x1_ref, pw_ref, gfin_ref,
                  ys_hbm, o_ref, local, sem):
    i = pl.program_id(0)
    nt = pl.num_programs(0)
    slot = i % 2

    def copy_into(s):
        def make_copy(loc_rows, glob_rows):
            return pltpu.make_async_copy(ys_hbm.at[glob_rows], local.at[s, loc_rows], sem.at[s])
        return make_copy

    @pl.when(i == 0)
    def _():
        local[...] = jnp.zeros_like(local)
        _start_segments(0, glob_ref, loc_ref, cnt_ref, copy_into(0))

    @pl.when(i + 1 < nt)
    def _():
        _start_segments(i + 1, glob_ref, loc_ref, cnt_ref, copy_into(1 - slot))
    _wait_segments(i, total_ref, copy_into(slot))

    j = lax.broadcasted_iota(I32, (TT, LROWS), 1).astype(F32)
    wc = jnp.zeros((TT, LROWS), F32)
    for k in range(TOP_K):
        wc = jnp.where(j == pw_ref[:, k:k + 1], pw_ref[:, TOP_K + k:TOP_K + k + 1], wc)
    acc = x1_ref[...] + _dot(wc.astype(BF16), _unpack_halves(local[slot]))
    o_ref[...] = _rms_norm(acc, gfin_ref[...])


def _combine(seg_glob, seg_loc, seg_cnt, tile_rd, x1, pw, gfin, ys):
    n = x1.shape[0]
    grid_spec = pltpu.PrefetchScalarGridSpec(
        num_scalar_prefetch=4,
        grid=(n // TT,),
        in_specs=[
            pl.BlockSpec((TT, D_MODEL), lambda i, *_: (i, 0)),
            pl.BlockSpec((TT, LANES), lambda i, *_: (i, 0)),
            pl.BlockSpec((1, D_MODEL), lambda i, *_: (0, 0)),
            pl.BlockSpec(memory_space=pl.ANY),
        ],
        out_specs=pl.BlockSpec((TT, D_MODEL), lambda i, *_: (i, 0)),
        scratch_shapes=[pltpu.VMEM((2, LROWS, PACKED), U32),
                        pltpu.SemaphoreType.DMA((2,))],
    )
    return pl.pallas_call(
        _combine_body,
        grid_spec=grid_spec,
        out_shape=jax.ShapeDtypeStruct((n, D_MODEL), F32),
        compiler_params=pltpu.CompilerParams(
            dimension_semantics=("arbitrary",), vmem_limit_bytes=VMEM_LIMIT),
        name="combine",
    )(seg_glob, seg_loc, seg_cnt, tile_rd, x1, pw, gfin, ys)


def _layer(x2, pos_b, invf, norm_mix_g, w_in, b_in, attn_sinks, w_o_attn, b_o_attn,
           w_pool_mix, pool_scale, w_pool_up, w_out, norm_ffn_g, w_router, b_router,
           w_gate, b_gate, w_up, b_up, w_down, b_down, out_g):
    n = x2.shape[0]
    q, kb, vb, u, gates = _inproj(
        x2, pos_b, invf, norm_mix_g[None, :], w_in.astype(BF16), b_in[None, :])
    attn = _attn(attn_sinks, q, kb, vb)

    wrt = w_router.T.astype(BF16)
    brt = jnp.broadcast_to(b_router[:, None], (N_EXPERTS, TM_POST))
    x1, h2, pos_t, pw, tc = _post(
        attn, u, gates, x2, w_o_attn.astype(BF16), b_o_attn[None, :],
        w_pool_mix.astype(BF16), pool_scale[None, :], w_pool_up.astype(BF16),
        w_out.astype(BF16), norm_ffn_g[None, :], wrt, brt)

    nt = n // TT
    sub = TM_POST // TT
    seg_cnt = (tc.reshape(n // TM_POST, N_EXPERTS, LANES)[:, :, :sub]
               .transpose(0, 2, 1).reshape(nt, N_EXPERTS).astype(I32))
    before = jnp.cumsum(seg_cnt, axis=0) - seg_cnt
    sizes = jnp.sum(seg_cnt, axis=0)
    padded = (sizes + TM_MOE - 1) // TM_MOE * TM_MOE
    pends = jnp.cumsum(padded)
    pstarts = pends - padded
    head = before % SEG_ALIGN
    last_tile = (jnp.arange(nt, dtype=I32) == nt - 1)[:, None]
    blocks_up = (head + seg_cnt + SEG_ALIGN - 1) // SEG_ALIGN
    blocks_dn = (head + seg_cnt) // SEG_ALIGN
    seg_rd = jnp.where((seg_cnt > 0) | (last_tile & (head > 0)), blocks_up, 0) * SEG_ALIGN
    seg_wr = jnp.where(last_tile, seg_rd, jnp.where(seg_cnt > 0, blocks_dn * SEG_ALIGN, 0))
    seg_loc = jnp.cumsum(seg_rd, axis=1) - seg_rd
    seg_glob = pstarts[None, :] + before - head
    tile_rd = jnp.sum(seg_rd, axis=1)
    tile_wr = jnp.sum(seg_wr, axis=1)
    seg_glob, seg_loc, seg_rd, seg_wr = (a.reshape(-1) for a in (seg_glob, seg_loc, seg_rd, seg_wr))
    m = n * TOP_K
    n_tiles = (m + N_EXPERTS * (TM_MOE - 1) + TM_MOE - 1) // TM_MOE
    tile_start = jnp.arange(n_tiles, dtype=I32) * TM_MOE
    tile_expert = jnp.minimum(
        jnp.sum(tile_start[:, None] >= pends[None, :], axis=-1), N_EXPERTS - 1).astype(I32)
    own = tile_expert[:, None] == jnp.arange(N_EXPERTS, dtype=I32)[None, :]
    left = jnp.sum(jnp.where(own, (pstarts + sizes)[None, :] - tile_start[:, None], 0), axis=-1)
    tile_rows = jnp.clip(left, 0, TM_MOE)
    ids = jnp.arange(N_EXPERTS, dtype=I32)
    later = jnp.where((sizes > 0)[None, :] & (ids[None, :] > ids[:, None]), ids[None, :], N_EXPERTS)
    next_expert = jnp.min(later, axis=1)
    next_expert = jnp.where(next_expert == N_EXPERTS, -1, next_expert)
    tile_next = jnp.sum(jnp.where(own, next_expert[None, :], 0), axis=-1).astype(I32)
    n_used = (pends[-1] // TM_MOE).astype(I32)[None]

    xs = _dispatch(seg_glob, seg_loc, seg_rd, seg_wr, tile_wr, tile_rd, h2, pos_t, n_tiles * TM_MOE)
    ys = _moe(tile_expert, n_used, tile_rows.astype(I32), tile_next, xs,
              w_gate, b_gate, w_up, b_up, w_down, b_down)
    return _combine(seg_glob, seg_loc, seg_rd, tile_rd, x1, pw, out_g[None, :], ys)


def kernel(x, positions, norm_mix_g, w_in, b_in, attn_sinks, w_o_attn, b_o_attn, w_pool_mix,
           pool_scale, w_pool_up, w_out, norm_ffn_g, w_router, b_router, w_gate, b_gate,
           w_up, b_up, w_down, b_down, norm_final_g):
    b, s, d = x.shape
    depth = w_in.shape[0]
    assert (s, d, depth) == (SEQ, D_MODEL, 1)
    n = b * s
    x2 = x.reshape(n, d)
    pos_b = jnp.repeat(positions.reshape(n // 4, 4), LANES // 4, axis=1)
    inv_freq = ROPE_THETA ** (-jnp.arange(0, HEAD_DIM, 2, dtype=F32) / HEAD_DIM)
    invf = jnp.tile(inv_freq, LANES // (HEAD_DIM // 2))[None, :]
    out = _layer(x2, pos_b, invf, norm_mix_g[0], w_in[0], b_in[0], attn_sinks[0], w_o_attn[0],
                 b_o_attn[0], w_pool_mix[0], pool_scale[0], w_pool_up[0], w_out[0],
                 norm_ffn_g[0], w_router[0], b_router[0], w_gate[0], b_gate[0], w_up[0],
                 b_up[0], w_down[0], b_down[0], norm_final_g)
    return out.reshape(b, s, d)
```

```python
import functools

import numpy as np
import jax
import jax.numpy as jnp
from jax import lax
from jax.experimental import pallas as pl
from jax.experimental.pallas import tpu as pltpu

F32 = jnp.float32
BF16 = jnp.bfloat16
I32 = jnp.int32
U32 = jnp.uint32

D_MODEL = 1024
SEQ = 4096
HEAD_DIM = 64
N_Q_HEADS = 16
WINDOW = 128
ROPE_THETA = 10000.0
Q_WIDTH = N_Q_HEADS * HEAD_DIM
KV_WIDTH = 128
POOL_WINDOWS = (2, 4, 8, 16)
POOL_WIDTH = 512
POOL_GROUP = 128
POOL_HALO = 16
POOL_PAD = 32
N_EXPERTS = 32
TOP_K = 4
SWIGLU_LIMIT = 7.0
SWIGLU_ALPHA = 1.702
RMS_EPS = 1e-5
NEG_BIG = -1e30
LOG2_E = 1.4426950408889634

LANES = 128
TM_IN = 1024
TQ = 512
TM_POST = 512
TM_MOE = 1024
MOE_ROWS = 512
TT = 256
SEG_ALIGN = 8
LROWS = TT * TOP_K + 2 * N_EXPERTS * SEG_ALIGN
LROWS_MAIN = LROWS - 256
PACKED = D_MODEL // 2
VMEM_LIMIT = 56 * 1024 * 1024


def _rms_norm(x, g):
    ms = jnp.mean(x * x, axis=-1, keepdims=True)
    return (x * lax.rsqrt(ms + RMS_EPS)) * g


def _dot(a, b):
    return jnp.dot(a, b, preferred_element_type=F32)


def _inproj_body(x_ref, pos_ref, invf_ref, g_ref, w_ref, b_ref,
                 q_ref, kb_ref, vb_ref, u_ref, gate_ref, cos_scr, sin_scr):
    h = _rms_norm(x_ref[...], g_ref[...]).astype(BF16)
    qk_w = Q_WIDTH + KV_WIDTH
    v0 = qk_w
    u0 = v0 + KV_WIDTH
    g0 = u0 + POOL_WIDTH

    def gate_chunk(c):
        sl = slice(g0 + c * D_MODEL, g0 + (c + 1) * D_MODEL)
        zg = _dot(h, w_ref[:, sl]) + b_ref[:, sl]
        gate_ref[:, c * D_MODEL:(c + 1) * D_MODEL] = jax.nn.sigmoid(zg).astype(BF16)

    gate_chunk(0)

    theta = pos_ref[...].astype(F32) * invf_ref[...]
    lane4 = lax.broadcasted_iota(I32, theta.shape, 1)
    for packed, scr in ((jnp.cos(theta), cos_scr), (jnp.sin(theta), sin_scr)):
        for jt in range(4):
            seg = packed if jt == 0 else pltpu.roll(packed, LANES - 32 * jt, 1)
            seg = jnp.where(lane4 < 32, seg, pltpu.roll(seg, 32, 1))
            seg = jnp.where(lane4 < 64, seg, pltpu.roll(seg, 64, 1))
            scr[pl.ds(jt, TM_IN // 4, stride=4), :] = seg
    cos = cos_scr[...]
    sin = sin_scr[...]
    lane = lax.broadcasted_iota(I32, cos.shape, 1)
    first_half = (lane & 32) == 0
    sin_signed = jnp.where(first_half, -sin, sin)
    low_head = lane < HEAD_DIM

    def rope(t):
        swapped = jnp.where(first_half, pltpu.roll(t, 96, 1), pltpu.roll(t, 32, 1))
        return t * cos + swapped * sin_signed

    def band_layout(t, out_ref):
        tr = pltpu.roll(t, 64, 1)
        zero = jnp.zeros_like(t)
        chunks = (jnp.where(low_head, t, zero), jnp.where(low_head, zero, tr),
                  jnp.where(low_head, tr, zero), jnp.where(low_head, zero, t))
        for c, val in enumerate(chunks):
            out_ref[:, c * LANES:(c + 1) * LANES] = val.astype(BF16)

    zqk = _dot(h, w_ref[:, :qk_w]) + b_ref[:, :qk_w]
    gate_chunk(1)
    scale = HEAD_DIM ** -0.5 * LOG2_E
    for j in range(Q_WIDTH // LANES):
        sl = slice(j * LANES, (j + 1) * LANES)
        q_ref[:, sl] = (rope(zqk[:, sl]) * scale).astype(BF16)
    band_layout(rope(zqk[:, Q_WIDTH:qk_w]), kb_ref)

    zv = _dot(h, w_ref[:, v0:v0 + KV_WIDTH]) + b_ref[:, v0:v0 + KV_WIDTH]
    band_layout(zv, vb_ref)
    u_ref[...] = _dot(h, w_ref[:, u0:u0 + POOL_WIDTH]) + b_ref[:, u0:u0 + POOL_WIDTH]


def _inproj(x2, pos_b, invf, g, w, b):
    n = x2.shape[0]
    in_width = w.shape[1]
    row = lambda i: (i, 0)
    const = lambda i: (0, 0)
    return pl.pallas_call(
        _inproj_body,
        grid=(n // TM_IN,),
        in_specs=[
            pl.BlockSpec((TM_IN, D_MODEL), row),
            pl.BlockSpec((TM_IN // 4, LANES), row),
            pl.BlockSpec((1, LANES), const),
            pl.BlockSpec((1, D_MODEL), const),
            pl.BlockSpec((D_MODEL, in_width), const),
            pl.BlockSpec((1, in_width), const),
        ],
        out_specs=[
            pl.BlockSpec((TM_IN, Q_WIDTH), row),
            pl.BlockSpec((TM_IN, 4 * LANES), row),
            pl.BlockSpec((TM_IN, 4 * LANES), row),
            pl.BlockSpec((TM_IN, POOL_WIDTH), row),
            pl.BlockSpec((TM_IN, 2 * D_MODEL), row),
        ],
        out_shape=[
            jax.ShapeDtypeStruct((n, Q_WIDTH), BF16),
            jax.ShapeDtypeStruct((n, 4 * LANES), BF16),
            jax.ShapeDtypeStruct((n, 4 * LANES), BF16),
            jax.ShapeDtypeStruct((n, POOL_WIDTH), F32),
            jax.ShapeDtypeStruct((n, 2 * D_MODEL), BF16),
        ],
        scratch_shapes=[pltpu.VMEM((TM_IN, LANES), F32), pltpu.VMEM((TM_IN, LANES), F32)],
        compiler_params=pltpu.CompilerParams(
            dimension_semantics=("arbitrary",), vmem_limit_bytes=VMEM_LIMIT),
        name="inproj",
    )(x2, pos_b, invf, g, w, b)


def _attn_body(sinks_ref, q_ref, kbc_ref, kbp_ref, vbc_ref, vbp_ref, bias_ref,
               o_ref, p_scr):
    t = pl.program_id(0)
    seq_first = (t % (SEQ // TQ)) == 0
    lane = lax.broadcasted_iota(I32, (WINDOW, LANES), 1)
    low_head = lane < HEAD_DIM
    key_row = lax.broadcasted_iota(I32, (4 * WINDOW, LANES), 0)
    key_lane = lax.broadcasted_iota(I32, (4 * WINDOW, LANES), 1)
    head_ones = jnp.where((key_row < 2 * WINDOW) == (key_lane < HEAD_DIM), 1.0, 0.0).astype(BF16)
    for n in range(TQ // WINDOW):
        rows = slice(n * WINDOW, (n + 1) * WINDOW)
        if n == 0:
            k_prev, v_prev = kbp_ref[...], vbp_ref[...]
            bias = jnp.where(seq_first, bias_ref[1], bias_ref[0])
        else:
            prev_rows = slice((n - 1) * WINDOW, n * WINDOW)
            k_prev, v_prev = kbc_ref[prev_rows, :], vbc_ref[prev_rows, :]
            bias = bias_ref[0]
        k_cur, v_cur = kbc_ref[rows, :], vbc_ref[rows, :]
        for g in range(2):
            def stack(prev, cur):
                lo = slice((2 * g) * LANES, (2 * g + 1) * LANES)
                hi = slice((2 * g + 1) * LANES, (2 * g + 2) * LANES)
                return jnp.concatenate([prev[:, lo], cur[:, lo], prev[:, hi], cur[:, hi]], axis=0)
            kmat = stack(k_prev, k_cur)
            vmat = stack(v_prev, v_cur)
            qs = jnp.concatenate(
                [q_ref[rows, (4 * g + p) * LANES:(4 * g + p + 1) * LANES] for p in range(4)],
                axis=0)
            s = lax.dot_general(qs, kmat, (((1,), (1,)), ((), ())),
                                preferred_element_type=F32) + bias
            sink_terms = []
            for p in range(4):
                pr = slice(p * WINDOW, (p + 1) * WINDOW)
                ms = []
                for j in range(2):
                    cols = slice(j * 2 * WINDOW, (j + 1) * 2 * WINDOW)
                    sp = s[pr, cols]
                    sink = sinks_ref[8 * g + 2 * p + j] * LOG2_E
                    m = jnp.maximum(jnp.max(sp, axis=-1, keepdims=True), sink)
                    p_scr[pr, cols] = jnp.exp2(sp - m).astype(BF16)
                    ms.append(m)
                sink_pair = jnp.where(low_head, sinks_ref[8 * g + 2 * p] * LOG2_E,
                                      sinks_ref[8 * g + 2 * p + 1] * LOG2_E)
                sink_terms.append(jnp.exp2(sink_pair - jnp.where(low_head, ms[0], ms[1])))
            o = _dot(p_scr[...], jnp.concatenate([vmat, head_ones], axis=1))
            for p in range(4):
                pr = slice(p * WINDOW, (p + 1) * WINDOW)
                inv = 1.0 / (o[pr, LANES:] + sink_terms[p])
                o_ref[rows, (4 * g + p) * LANES:(4 * g + p + 1) * LANES] = (o[pr, :LANES] * inv).astype(BF16)


def _attn_bias():
    r = np.arange(4 * WINDOW)[:, None] % WINDOW
    c = np.arange(4 * WINDOW)[None, :] % (2 * WINDOW)
    band = (c > r) & (c <= r + WINDOW)
    first = band & (c >= WINDOW)
    return np.stack([np.where(band, 0.0, NEG_BIG), np.where(first, 0.0, NEG_BIG)]).astype(np.float32)


def _attn(sinks, q, kb, vb):
    n = q.shape[0]
    blocks_per_tile = TQ // WINDOW
    cur = lambda t: (t, 0)
    prev = lambda t: (jnp.maximum(t * blocks_per_tile - 1, 0), 0)
    bias = jnp.asarray(_attn_bias())
    return pl.pallas_call(
        _attn_body,
        grid=(n // TQ,),
        in_specs=[
            pl.BlockSpec(memory_space=pltpu.SMEM),
            pl.BlockSpec((TQ, Q_WIDTH), cur),
            pl.BlockSpec((TQ, 4 * LANES), cur),
            pl.BlockSpec((WINDOW, 4 * LANES), prev),
            pl.BlockSpec((TQ, 4 * LANES), cur),
            pl.BlockSpec((WINDOW, 4 * LANES), prev),
            pl.BlockSpec((2, 4 * WINDOW, 4 * WINDOW), lambda t: (0, 0, 0)),
        ],
        out_specs=pl.BlockSpec((TQ, Q_WIDTH), cur),
        out_shape=jax.ShapeDtypeStruct((n, Q_WIDTH), BF16),
        scratch_shapes=[pltpu.VMEM((4 * WINDOW, 4 * WINDOW), BF16)],
        compiler_params=pltpu.CompilerParams(
            dimension_semantics=("arbitrary",), vmem_limit_bytes=VMEM_LIMIT),
        name="attn",
    )(sinks, q, kb, kb, vb, vb, bias)


def _post_body(attn_ref, u_ref, uprev_ref, gate_ref, x_ref,
               wo_ref, bo_ref, wmix_ref, pscale_ref, wup_ref, wout_ref,
               gffn_ref, wrt_ref, brt_ref,
               x1_ref, h2_ref, pos_t_ref, pw_ref, tc_ref,
               win_a, win_b, carry):
    i = pl.program_id(0)
    tiles_per_seq = SEQ // TM_POST
    seq_first = (i % tiles_per_seq) == 0

    @pl.when(i == 0)
    def _():
        carry[...] = jnp.zeros_like(carry)

    y_attn = _dot(attn_ref[...], wo_ref[...]) + bo_ref[...]

    rows = TM_POST + POOL_PAD
    win_a[0:POOL_PAD - POOL_HALO, :] = jnp.zeros((POOL_PAD - POOL_HALO, POOL_WIDTH), F32)
    win_a[POOL_PAD - POOL_HALO:POOL_PAD, :] = jnp.where(seq_first, 0.0, uprev_ref[...])
    win_a[POOL_PAD:, :] = u_ref[...]
    src, dst = win_a, win_b
    sums = {}
    for level, w in enumerate((1, 2, 4, 8)):
        lo = 8 * (level + 1)
        c0 = level * POOL_GROUP
        dst[lo:, c0:] = src[lo:, c0:] + src[lo - w:rows - w, c0:]
        sums[2 * w] = dst
        src, dst = dst, (win_a if dst is win_b else win_b)
    row = lax.broadcasted_iota(I32, (TM_POST, 1), 0)
    tpos = (i % tiles_per_seq) * TM_POST + row
    mixed = []
    for gi, w in enumerate(POOL_WINDOWS):
        cols = slice(gi * POOL_GROUP, (gi + 1) * POOL_GROUP)
        cnt = jnp.minimum(tpos + 1, w).astype(F32)
        pooled = sums[w][POOL_PAD:, cols] / cnt - u_ref[:, cols]
        mixed.append(_dot(pooled.astype(BF16), wmix_ref[gi]) * pscale_ref[:, cols])
    mixed = jnp.concatenate(mixed, axis=-1).astype(BF16)
    y_pool = _dot(mixed, wup_ref[...])
    merged = (gate_ref[:, :D_MODEL].astype(F32) * y_attn
              + gate_ref[:, D_MODEL:].astype(F32) * y_pool)
    x1 = x_ref[...] + _dot(merged.astype(BF16), wout_ref[...])
    x1_ref[...] = x1
    h2 = _rms_norm(x1, gffn_ref[...])
    h2_bf = h2.astype(BF16)
    h2_ref[...] = h2_bf

    logits_t = lax.dot_general(wrt_ref[...], h2_bf, (((1,), (1,)), ((), ())),
                               preferred_element_type=F32) + brt_ref[...]
    erow = lax.broadcasted_iota(I32, (N_EXPERTS, TM_POST), 0)
    work = logits_t
    top_v, onehots = [], []
    for _ in range(TOP_K):
        m = jnp.max(work, axis=0, keepdims=True)
        idx = jnp.min(jnp.where(work == m, erow, N_EXPERTS), axis=0, keepdims=True)
        hit = erow == idx
        top_v.append(m)
        onehots.append(hit)
        work = jnp.where(hit, -jnp.inf, work)
    exps = [jnp.exp(v - top_v[0]) for v in top_v]
    denom = exps[0] + exps[1] + exps[2] + exps[3]
    comb = [e / denom for e in exps]
    chosen = jnp.zeros((N_EXPERTS, TM_POST), F32)
    for hit in onehots:
        chosen = chosen + hit.astype(F32)

    e_r = lax.broadcasted_iota(I32, (N_EXPERTS, N_EXPERTS), 0)
    e_c = lax.broadcasted_iota(I32, (N_EXPERTS, N_EXPERTS), 1)
    earlier_expert = (e_c < e_r).astype(BF16)
    t_r = lax.broadcasted_iota(I32, (TT, TT), 0)
    t_c = lax.broadcasted_iota(I32, (TT, TT), 1)
    earlier_token = (t_r < t_c).astype(BF16)
    lane = lax.broadcasted_iota(I32, (N_EXPERTS, LANES), 1)
    counts = jnp.zeros((N_EXPERTS, LANES), F32)
    pos_parts = [[] for _ in range(TOP_K)]
    n_sub = TM_POST // TT
    for s in range(n_sub):
        cols = slice(s * TT, (s + 1) * TT)
        ch = chosen[:, cols]
        cnt = jnp.broadcast_to(jnp.sum(ch, axis=1, keepdims=True), (N_EXPERTS, LANES))
        before = carry[...]
        head = before - SEG_ALIGN * jnp.floor(before / SEG_ALIGN)
        flush = jnp.logical_and(i == pl.num_programs(0) - 1, s == n_sub - 1)
        present = (cnt > 0) | (flush & (head > 0))
        blocks = jnp.where(present, jnp.floor((head + cnt + (SEG_ALIGN - 1)) / SEG_ALIGN), 0.0)
        seg_loc = SEG_ALIGN * _dot(earlier_expert, blocks.astype(BF16))
        base = seg_loc + head
        rank_in_tile = _dot(ch.astype(BF16), earlier_token)
        full = jnp.concatenate([base] * (TT // LANES), axis=1) + rank_in_tile
        for k in range(TOP_K):
            pos_parts[k].append(
                jnp.sum(jnp.where(onehots[k][:, cols], full, 0.0), axis=0, keepdims=True))
        counts = jnp.where(lane == s, cnt, counts)
        carry[...] = before + cnt
    tc_ref[...] = counts

    pos = [jnp.concatenate(parts, axis=1) for parts in pos_parts]
    row8 = lax.broadcasted_iota(I32, (8, TM_POST), 0)
    pos_rows = jnp.full((8, TM_POST), -1.0, F32)
    both = jnp.zeros((8, TM_POST), F32)
    for k in range(TOP_K):
        pos_rows = jnp.where(row8 == k, pos[k], pos_rows)
        both = jnp.where(row8 == k, pos[k], both)
        both = jnp.where(row8 == TOP_K + k, comb[k], both)
    pos_t_ref[...] = pos_rows.astype(I32)
    padded = jnp.concatenate([both, jnp.zeros((LANES - 8, TM_POST), F32)], axis=0)
    pw_ref[...] = padded.T


def _post(attn, u, gates, x2, wo, bo, wmix, pscale, wup, wout, gffn, wr, br):
    n = x2.shape[0]
    row = lambda i: (i, 0)
    const = lambda i: (0, 0)
    halo_blocks = TM_POST // POOL_HALO
    prev = lambda i: (jnp.maximum(i * halo_blocks - 1, 0), 0)
    return pl.pallas_call(
        _post_body,
        grid=(n // TM_POST,),
        in_specs=[
            pl.BlockSpec((TM_POST, Q_WIDTH), row),
            pl.BlockSpec((TM_POST, POOL_WIDTH), row),
            pl.BlockSpec((POOL_HALO, POOL_WIDTH), prev),
            pl.BlockSpec((TM_POST, 2 * D_MODEL), row),
            pl.BlockSpec((TM_POST, D_MODEL), row),
            pl.BlockSpec((Q_WIDTH, D_MODEL), const),
            pl.BlockSpec((1, D_MODEL), const),
            pl.BlockSpec((len(POOL_WINDOWS), POOL_GROUP, POOL_GROUP), lambda i: (0, 0, 0)),
            pl.BlockSpec((1, POOL_WIDTH), const),
            pl.BlockSpec((POOL_WIDTH, D_MODEL), const),
            pl.BlockSpec((D_MODEL, D_MODEL), const),
            pl.BlockSpec((1, D_MODEL), const),
            pl.BlockSpec((N_EXPERTS, D_MODEL), const),
            pl.BlockSpec((N_EXPERTS, TM_POST), const),
        ],
        out_specs=[
            pl.BlockSpec((TM_POST, D_MODEL), row),
            pl.BlockSpec((TM_POST, D_MODEL), row),
            pl.BlockSpec((8, TM_POST), lambda i: (0, i)),
            pl.BlockSpec((TM_POST, LANES), row),
            pl.BlockSpec((N_EXPERTS, LANES), row),
        ],
        out_shape=[
            jax.ShapeDtypeStruct((n, D_MODEL), F32),
            jax.ShapeDtypeStruct((n, D_MODEL), BF16),
            jax.ShapeDtypeStruct((8, n), I32),
            jax.ShapeDtypeStruct((n, LANES), F32),
            jax.ShapeDtypeStruct((n // TM_POST * N_EXPERTS, LANES), F32),
        ],
        scratch_shapes=[pltpu.VMEM((TM_POST + POOL_PAD, POOL_WIDTH), F32),
                        pltpu.VMEM((TM_POST + POOL_PAD, POOL_WIDTH), F32),
                        pltpu.VMEM((N_EXPERTS, LANES), F32)],
        compiler_params=pltpu.CompilerParams(
            dimension_semantics=("arbitrary",), vmem_limit_bytes=VMEM_LIMIT),
        name="post",
    )(attn, u, u, gates, x2, wo, bo, wmix, pscale, wup, wout, gffn, wr, br)


def _start_segments(i, glob_ref, loc_ref, cnt_ref, make_copy):
    def body(e, c):
        idx = i * N_EXPERTS + e
        rows = pl.multiple_of(cnt_ref[idx], SEG_ALIGN)

        @pl.when(rows > 0)
        def _():
            loc = pl.multiple_of(loc_ref[idx], SEG_ALIGN)
            glob = pl.multiple_of(glob_ref[idx], SEG_ALIGN)
            make_copy(pl.ds(loc, rows), pl.ds(glob, rows)).start()
        return c
    lax.fori_loop(0, N_EXPERTS, body, 0, unroll=4)


def _wait_segments(i, total_ref, make_copy):
    rows = pl.multiple_of(total_ref[i], SEG_ALIGN)

    @pl.when(rows > 0)
    def _():
        make_copy(pl.ds(0, rows), pl.ds(0, rows)).wait()


def _pack_halves(v):
    c = v.shape[1] // 2
    bits = lax.bitcast_convert_type(v, U32)
    return (bits[:, :c] >> 16) | bits[:, c:]


def _unpack_halves(w):
    lo = lax.bitcast_convert_type(w << 16, F32)
    hi = lax.bitcast_convert_type(w & jnp.uint32(0xFFFF0000), F32)
    return jnp.concatenate([lo, hi], axis=1).astype(BF16)


def _dispatch_body(glob_ref, loc_ref, rd_ref, wr_ref, total_ref, rows_ref, h2_ref, pos_ref,
                   xs_hbm, local, tail, sem):
    i = pl.program_id(0)
    nt = pl.num_programs(0)
    slot = i % 2

    def copy_from(s):
        def make_copy(loc_rows, glob_rows):
            return pltpu.make_async_copy(local.at[s, loc_rows], xs_hbm.at[glob_rows], sem.at[s])
        return make_copy

    @pl.when(i == 0)
    def _():
        tail[...] = jnp.zeros_like(tail)

    @pl.when(i >= 2)
    def _():
        _wait_segments(i - 2, total_ref, copy_from(slot))

    def group(r0, nrows):
        j = lax.broadcasted_iota(I32, (nrows, TT), 0) + r0
        hit = j == pos_ref[0:1, :]
        for k in range(1, TOP_K):
            hit = hit | (j == pos_ref[k:k + 1, :])
        perm = jnp.where(hit, 1.0, 0.0).astype(BF16)
        local[slot, r0:r0 + nrows, :] = _pack_halves(_dot(perm, h2_ref[...]))

    group(0, LROWS_MAIN)

    @pl.when(rows_ref[i] + SEG_ALIGN > LROWS_MAIN)
    def _():
        group(LROWS_MAIN, LROWS - LROWS_MAIN)

    for e in range(N_EXPERTS):
        idx = i * N_EXPERTS + e
        rd = rd_ref[idx]
        wr = wr_ref[idx]
        loc = pl.multiple_of(loc_ref[idx], SEG_ALIGN)
        first = local[slot, pl.ds(loc, SEG_ALIGN), :]
        last = local[slot, pl.ds(pl.multiple_of(loc + wr, SEG_ALIGN), SEG_ALIGN), :]
        merged = jnp.where(rd > 0, first | tail[e], first)
        local[slot, pl.ds(loc, SEG_ALIGN), :] = merged
        last = jnp.where(wr == 0, merged, last)
        tail[e] = jnp.where(rd > 0, jnp.where(wr < rd, last, jnp.uint32(0)), tail[e])
    _start_segments(i, glob_ref, loc_ref, wr_ref, copy_from(slot))

    @pl.when(i == nt - 1)
    def _():
        _wait_segments(i - 1, total_ref, copy_from(1 - slot))
        _wait_segments(i, total_ref, copy_from(slot))


def _dispatch(seg_glob, seg_loc, seg_rd, seg_wr, tile_wr, tile_rd, h2, pos_t, n_rows):
    n = h2.shape[0]
    grid_spec = pltpu.PrefetchScalarGridSpec(
        num_scalar_prefetch=6,
        grid=(n // TT,),
        in_specs=[
            pl.BlockSpec((TT, D_MODEL), lambda i, *_: (i, 0)),
            pl.BlockSpec((8, TT), lambda i, *_: (0, i)),
        ],
        out_specs=pl.BlockSpec(memory_space=pl.ANY),
        scratch_shapes=[pltpu.VMEM((2, LROWS, PACKED), U32),
                        pltpu.VMEM((N_EXPERTS, SEG_ALIGN, PACKED), U32),
                        pltpu.SemaphoreType.DMA((2,))],
    )
    return pl.pallas_call(
        _dispatch_body,
        grid_spec=grid_spec,
        out_shape=jax.ShapeDtypeStruct((n_rows, PACKED), U32),
        compiler_params=pltpu.CompilerParams(
            dimension_semantics=("arbitrary",), vmem_limit_bytes=VMEM_LIMIT),
        name="dispatch",
    )(seg_glob, seg_loc, seg_rd, seg_wr, tile_wr, tile_rd, h2, pos_t)


def _moe_body(te_ref, nused_ref, rows_ref, next_ref, xs_ref, bias_ref,
              wg_hbm, wu_hbm, wd_hbm, y_ref, wf32, wbf, sem, slot_ref):
    i = pl.program_id(0)

    def fetch(expert, slot):
        return [pltpu.make_async_copy(w.at[expert], wf32.at[slot, m], sem.at[slot, m])
                for m, w in enumerate((wg_hbm, wu_hbm, wd_hbm))]

    @pl.when(i == 0)
    def _():
        slot_ref[0] = 0

    @pl.when(i < nused_ref[0])
    def _():
        expert = te_ref[i]

        @pl.when((i == 0) | (expert != te_ref[jnp.maximum(i - 1, 0)]))
        def _():
            slot = slot_ref[0]

            @pl.when(i == 0)
            def _():
                for cp in fetch(expert, slot):
                    cp.start()
            for cp in fetch(expert, slot):
                cp.wait()
            nxt = next_ref[i]

            @pl.when(nxt >= 0)
            def _():
                for cp in fetch(nxt, 1 - slot):
                    cp.start()
            for m in range(3):
                wbf[m] = wf32[slot, m].astype(BF16)
            slot_ref[0] = 1 - slot

        def mlp(r0):
            rows = slice(r0, r0 + MOE_ROWS)
            row = lax.broadcasted_iota(I32, (MOE_ROWS, 1), 0) + r0
            x = _unpack_halves(jnp.where(row < rows_ref[i], xs_ref[rows, :], jnp.uint32(0)))
            bias = bias_ref[expert]
            g = _dot(x, wbf[0]) + bias[0:1, :]
            u = _dot(x, wbf[1]) + bias[1:2, :]
            g = jnp.minimum(g, SWIGLU_LIMIT)
            u = jnp.clip(u, -SWIGLU_LIMIT, SWIGLU_LIMIT)
            a = (g * jax.nn.sigmoid(SWIGLU_ALPHA * g) * (u + 1.0)).astype(BF16)
            y = _dot(a, wbf[2]) + bias[2:3, :]
            y_ref[rows, :] = _pack_halves(y.astype(BF16).astype(F32))

        mlp(0)
        for r0 in range(MOE_ROWS, TM_MOE, MOE_ROWS):
            @pl.when(rows_ref[i] > r0)
            def _():
                mlp(r0)


def _moe(tile_expert, n_used, tile_rows, tile_next, xs, wg, bg, wu, bu, wd, bd):
    n_rows = xs.shape[0]
    n_tiles = n_rows // TM_MOE
    d_ff = wg.shape[2]
    assert d_ff == D_MODEL
    row = lambda i, te, nu, *_: (jnp.minimum(i, nu[0] - 1), 0)
    bias = jnp.zeros((N_EXPERTS, 8, D_MODEL), F32)
    bias = bias.at[:, 0].set(bg).at[:, 1].set(bu).at[:, 2].set(bd)
    grid_spec = pltpu.PrefetchScalarGridSpec(
        num_scalar_prefetch=4,
        grid=(n_tiles,),
        in_specs=[
            pl.BlockSpec((TM_MOE, PACKED), row),
            pl.BlockSpec((N_EXPERTS, 8, D_MODEL), lambda i, *_: (0, 0, 0)),
            pl.BlockSpec(memory_space=pl.ANY),
            pl.BlockSpec(memory_space=pl.ANY),
            pl.BlockSpec(memory_space=pl.ANY),
        ],
        out_specs=pl.BlockSpec((TM_MOE, PACKED), row),
        scratch_shapes=[pltpu.VMEM((2, 3, D_MODEL, D_MODEL), F32),
                        pltpu.VMEM((3, D_MODEL, D_MODEL), BF16),
                        pltpu.SemaphoreType.DMA((2, 3)),
                        pltpu.SMEM((1,), I32)],
    )
    return pl.pallas_call(
        _moe_body,
        grid_spec=grid_spec,
        out_shape=jax.ShapeDtypeStruct((n_rows, PACKED), U32),
        compiler_params=pltpu.CompilerParams(
            dimension_semantics=("arbitrary",), vmem_limit_bytes=VMEM_LIMIT),
        name="moe",
    )(tile_expert, n_used, tile_rows, tile_next, xs, bias, wg, wu, wd)


def _combine_body(glob_ref, loc_ref, cnt_ref, total_ref, x1_ref, pw_ref, gfin_ref,
                  ys_hbm, o_ref, local, sem):
    i = pl.program_id(0)
    nt = pl.num_programs(0)
    slot = i % 2

    def copy_into(s):
        def make_copy(loc_rows, glob_rows):
            return pltpu.make_async_copy(ys_hbm.at[glob_rows], local.at[s, loc_rows], sem.at[s])
        return make_copy

    @pl.when(i == 0)
    def _():
        local[...] = jnp.zeros_like(local)
        _start_segments(0, glob_ref, loc_ref, cnt_ref, copy_into(0))

    @pl.when(i + 1 < nt)
    def _():
        _start_segments(i + 1, glob_ref, loc_ref, cnt_ref, copy_into(1 - slot))
    _wait_segments(i, total_ref, copy_into(slot))

    def finish(nrows):
        j = lax.broadcasted_iota(I32, (TT, nrows), 1).astype(F32)
        wc = jnp.zeros((TT, nrows), F32)
        for k in range(TOP_K):
            wc = jnp.where(j == pw_ref[:, k:k + 1], pw_ref[:, TOP_K + k:TOP_K + k + 1], wc)
        acc = x1_ref[...] + _dot(wc.astype(BF16), _unpack_halves(local[slot, 0:nrows, :]))
        o_ref[...] = _rms_norm(acc, gfin_ref[...])

    @pl.when(total_ref[i] <= LROWS_MAIN)
    def _():
        finish(LROWS_MAIN)

    @pl.when(total_ref[i] > LROWS_MAIN)
    def _():
        finish(LROWS)


def _combine(seg_glob, seg_loc, seg_cnt, tile_rd, x1, pw, gfin, ys):
    n = x1.shape[0]
    grid_spec = pltpu.PrefetchScalarGridSpec(
        num_scalar_prefetch=4,
        grid=(n // TT,),
        in_specs=[
            pl.BlockSpec((TT, D_MODEL), lambda i, *_: (i, 0)),
            pl.BlockSpec((TT, LANES), lambda i, *_: (i, 0)),
            pl.BlockSpec((1, D_MODEL), lambda i, *_: (0, 0)),
            pl.BlockSpec(memory_space=pl.ANY),
        ],
        out_specs=pl.BlockSpec((TT, D_MODEL), lambda i, *_: (i, 0)),
        scratch_shapes=[pltpu.VMEM((2, LROWS, PACKED), U32),
                        pltpu.SemaphoreType.DMA((2,))],
    )
    return pl.pallas_call(
        _combine_body,
        grid_spec=grid_spec,
        out_shape=jax.ShapeDtypeStruct((n, D_MODEL), F32),
        compiler_params=pltpu.CompilerParams(
            dimension_semantics=("arbitrary",), vmem_limit_bytes=VMEM_LIMIT),
        name="combine",
    )(seg_glob, seg_loc, seg_cnt, tile_rd, x1, pw, gfin, ys)


def _layer(x2, pos_b, invf, norm_mix_g, w_in, b_in, attn_sinks, w_o_attn, b_o_attn,
           w_pool_mix, pool_scale, w_pool_up, w_out, norm_ffn_g, w_router, b_router,
           w_gate, b_gate, w_up, b_up, w_down, b_down, out_g):
    n = x2.shape[0]
    q, kb, vb, u, gates = _inproj(
        x2, pos_b, invf, norm_mix_g[None, :], w_in.astype(BF16), b_in[None, :])
    attn = _attn(attn_sinks, q, kb, vb)

    wrt = w_router.T.astype(BF16)
    brt = jnp.broadcast_to(b_router[:, None], (N_EXPERTS, TM_POST))
    x1, h2, pos_t, pw, tc = _post(
        attn, u, gates, x2, w_o_attn.astype(BF16), b_o_attn[None, :],
        w_pool_mix.astype(BF16), pool_scale[None, :], w_pool_up.astype(BF16),
        w_out.astype(BF16), norm_ffn_g[None, :], wrt, brt)

    nt = n // TT
    sub = TM_POST // TT
    seg_cnt = (tc.reshape(n // TM_POST, N_EXPERTS, LANES)[:, :, :sub]
               .transpose(0, 2, 1).reshape(nt, N_EXPERTS).astype(I32))
    before = jnp.cumsum(seg_cnt, axis=0) - seg_cnt
    sizes = jnp.sum(seg_cnt, axis=0)
    padded = (sizes + TM_MOE - 1) // TM_MOE * TM_MOE
    pends = jnp.cumsum(padded)
    pstarts = pends - padded
    head = before % SEG_ALIGN
    last_tile = (jnp.arange(nt, dtype=I32) == nt - 1)[:, None]
    blocks_up = (head + seg_cnt + SEG_ALIGN - 1) // SEG_ALIGN
    blocks_dn = (head + seg_cnt) // SEG_ALIGN
    seg_rd = jnp.where((seg_cnt > 0) | (last_tile & (head > 0)), blocks_up, 0) * SEG_ALIGN
    seg_wr = jnp.where(last_tile, seg_rd, jnp.where(seg_cnt > 0, blocks_dn * SEG_ALIGN, 0))
    seg_loc = jnp.cumsum(seg_rd, axis=1) - seg_rd
    seg_glob = pstarts[None, :] + before - head
    tile_rd = jnp.sum(seg_rd, axis=1)
    tile_wr = jnp.sum(seg_wr, axis=1)
    seg_glob, seg_loc, seg_rd, seg_wr = (a.reshape(-1) for a in (seg_glob, seg_loc, seg_rd, seg_wr))
    m = n * TOP_K
    n_tiles = (m + N_EXPERTS * (TM_MOE - 1) + TM_MOE - 1) // TM_MOE
    tile_start = jnp.arange(n_tiles, dtype=I32) * TM_MOE
    tile_expert = jnp.minimum(
        jnp.sum(tile_start[:, None] >= pends[None, :], axis=-1), N_EXPERTS - 1).astype(I32)
    own = tile_expert[:, None] == jnp.arange(N_EXPERTS, dtype=I32)[None, :]
    left = jnp.sum(jnp.where(own, (pstarts + sizes)[None, :] - tile_start[:, None], 0), axis=-1)
    tile_rows = jnp.clip(left, 0, TM_MOE)
    ids = jnp.arange(N_EXPERTS, dtype=I32)
    later = jnp.where((sizes > 0)[None, :] & (ids[None, :] > ids[:, None]), ids[None, :], N_EXPERTS)
    next_expert = jnp.min(later, axis=1)
    next_expert = jnp.where(next_expert == N_EXPERTS, -1, next_expert)
    tile_next = jnp.sum(jnp.where(own, next_expert[None, :], 0), axis=-1).astype(I32)
    n_used = (pends[-1] // TM_MOE).astype(I32)[None]

    xs = _dispatch(seg_glob, seg_loc, seg_rd, seg_wr, tile_wr, tile_rd, h2, pos_t, n_tiles * TM_MOE)
    ys = _moe(tile_expert, n_used, tile_rows.astype(I32), tile_next, xs,
              w_gate, b_gate, w_up, b_up, w_down, b_down)
    return _combine(seg_glob, seg_loc, seg_rd, tile_rd, x1, pw, out_g[None, :], ys)


def kernel(x, positions, norm_mix_g, w_in, b_in, attn_sinks, w_o_attn, b_o_attn, w_pool_mix,
           pool_scale, w_pool_up, w_out, norm_ffn_g, w_router, b_router, w_gate, b_gate,
           w_up, b_up, w_down, b_down, norm_final_g):
    b, s, d = x.shape
    depth = w_in.shape[0]
    assert (s, d, depth) == (SEQ, D_MODEL, 1)
    n = b * s
    x2 = x.reshape(n, d)
    pos_b = jnp.repeat(positions.reshape(n // 4, 4), LANES // 4, axis=1)
    inv_freq = ROPE_THETA ** (-jnp.arange(0, HEAD_DIM, 2, dtype=F32) / HEAD_DIM)
    invf = jnp.tile(inv_freq, LANES // (HEAD_DIM // 2))[None, :]
    out = _layer(x2, pos_b, invf, norm_mix_g[0], w_in[0], b_in[0], attn_sinks[0], w_o_attn[0],
                 b_o_attn[0], w_pool_mix[0], pool_scale[0], w_pool_up[0], w_out[0],
                 norm_ffn_g[0], w_router[0], b_router[0], w_gate[0], b_gate[0], w_up[0],
                 b_up[0], w_down[0], b_down[0], norm_final_g)
    return out.reshape(b, s, d)
```

```python
import numpy as np
import jax
import jax.numpy as jnp
from jax import lax
from jax.experimental import pallas as pl
from jax.experimental.pallas import tpu as pltpu

F32 = jnp.float32
BF16 = jnp.bfloat16
I32 = jnp.int32
U32 = jnp.uint32

D_MODEL = 1024
SEQ = 4096
HEAD_DIM = 64
N_Q_HEADS = 16
WINDOW = 128
ROPE_THETA = 10000.0
Q_WIDTH = N_Q_HEADS * HEAD_DIM
KV_WIDTH = 128
POOL_WINDOWS = (2, 4, 8, 16)
POOL_WIDTH = 512
POOL_GROUP = 128
POOL_HALO = 16
POOL_PAD = 32
N_EXPERTS = 32
TOP_K = 4
SWIGLU_LIMIT = 7.0
SWIGLU_ALPHA = 1.702
RMS_EPS = 1e-5
NEG_BIG = -1e30
LOG2_E = 1.4426950408889634

LANES = 128
TM_IN = 1024
TQ = 512
TM_POST = 512
TM_MOE = 1024
MOE_ROWS = 512
TT = 256
SEG_ALIGN = 8
LROWS = TT * TOP_K + 2 * N_EXPERTS * SEG_ALIGN
LROWS_MAIN = LROWS - 256
PACKED = D_MODEL // 2
VMEM_LIMIT = 56 * 1024 * 1024


def _rms_norm(x, g):
    ms = jnp.mean(x * x, axis=-1, keepdims=True)
    return (x * lax.rsqrt(ms + RMS_EPS)) * g


def _dot(a, b):
    return jnp.dot(a, b, preferred_element_type=F32)


def _inproj_body(x_ref, pos_ref, invf_ref, g_ref, w_ref, b_ref,
                 q_ref, kb_ref, vb_ref, u_ref, gate_ref, cos_scr, sin_scr):
    h = _rms_norm(x_ref[...], g_ref[...]).astype(BF16)
    qk_w = Q_WIDTH + KV_WIDTH
    v0 = qk_w
    u0 = v0 + KV_WIDTH
    g0 = u0 + POOL_WIDTH

    def gate_chunk(c):
        sl = slice(g0 + c * D_MODEL, g0 + (c + 1) * D_MODEL)
        zg = _dot(h, w_ref[:, sl]) + b_ref[:, sl]
        gate_ref[:, c * D_MODEL:(c + 1) * D_MODEL] = jax.nn.sigmoid(zg).astype(BF16)

    gate_chunk(0)

    theta = pos_ref[...].astype(F32) * invf_ref[...]
    lane4 = lax.broadcasted_iota(I32, theta.shape, 1)
    for packed, scr in ((jnp.cos(theta), cos_scr), (jnp.sin(theta), sin_scr)):
        for jt in range(4):
            seg = packed if jt == 0 else pltpu.roll(packed, LANES - 32 * jt, 1)
            seg = jnp.where(lane4 < 32, seg, pltpu.roll(seg, 32, 1))
            seg = jnp.where(lane4 < 64, seg, pltpu.roll(seg, 64, 1))
            scr[pl.ds(jt, TM_IN // 4, stride=4), :] = seg
    cos = cos_scr[...]
    sin = sin_scr[...]
    lane = lax.broadcasted_iota(I32, cos.shape, 1)
    first_half = (lane & 32) == 0
    sin_signed = jnp.where(first_half, -sin, sin)
    low_head = lane < HEAD_DIM

    def rope(t):
        swapped = jnp.where(first_half, pltpu.roll(t, 96, 1), pltpu.roll(t, 32, 1))
        return t * cos + swapped * sin_signed

    def band_layout(t, out_ref):
        tr = pltpu.roll(t, 64, 1)
        zero = jnp.zeros_like(t)
        chunks = (jnp.where(low_head, t, zero), jnp.where(low_head, zero, tr),
                  jnp.where(low_head, tr, zero), jnp.where(low_head, zero, t))
        for c, val in enumerate(chunks):
            out_ref[:, c * LANES:(c + 1) * LANES] = val.astype(BF16)

    zqk = _dot(h, w_ref[:, :qk_w]) + b_ref[:, :qk_w]
    gate_chunk(1)
    scale = HEAD_DIM ** -0.5 * LOG2_E
    for j in range(Q_WIDTH // LANES):
        sl = slice(j * LANES, (j + 1) * LANES)
        q_ref[:, sl] = (rope(zqk[:, sl]) * scale).astype(BF16)
    band_layout(rope(zqk[:, Q_WIDTH:qk_w]), kb_ref)

    zv = _dot(h, w_ref[:, v0:v0 + KV_WIDTH]) + b_ref[:, v0:v0 + KV_WIDTH]
    band_layout(zv, vb_ref)
    u_ref[...] = _dot(h, w_ref[:, u0:u0 + POOL_WIDTH]) + b_ref[:, u0:u0 + POOL_WIDTH]


def _inproj(x2, pos_b, invf, g, w, b):
    n = x2.shape[0]
    in_width = w.shape[1]
    row = lambda i: (i, 0)
    const = lambda i: (0, 0)
    return pl.pallas_call(
        _inproj_body,
        grid=(n // TM_IN,),
        in_specs=[
            pl.BlockSpec((TM_IN, D_MODEL), row),
            pl.BlockSpec((TM_IN // 4, LANES), row),
            pl.BlockSpec((1, LANES), const),
            pl.BlockSpec((1, D_MODEL), const),
            pl.BlockSpec((D_MODEL, in_width), const),
            pl.BlockSpec((1, in_width), const),
        ],
        out_specs=[
            pl.BlockSpec((TM_IN, Q_WIDTH), row),
            pl.BlockSpec((TM_IN, 4 * LANES), row),
            pl.BlockSpec((TM_IN, 4 * LANES), row),
            pl.BlockSpec((TM_IN, POOL_WIDTH), row),
            pl.BlockSpec((TM_IN, 2 * D_MODEL), row),
        ],
        out_shape=[
            jax.ShapeDtypeStruct((n, Q_WIDTH), BF16),
            jax.ShapeDtypeStruct((n, 4 * LANES), BF16),
            jax.ShapeDtypeStruct((n, 4 * LANES), BF16),
            jax.ShapeDtypeStruct((n, POOL_WIDTH), F32),
            jax.ShapeDtypeStruct((n, 2 * D_MODEL), BF16),
        ],
        scratch_shapes=[pltpu.VMEM((TM_IN, LANES), F32), pltpu.VMEM((TM_IN, LANES), F32)],
        compiler_params=pltpu.CompilerParams(
            dimension_semantics=("arbitrary",), vmem_limit_bytes=VMEM_LIMIT),
        name="inproj",
    )(x2, pos_b, invf, g, w, b)


def _attn_body(sinks_ref, q_ref, kbc_ref, kbp_ref, vbc_ref, vbp_ref, bias_ref,
               o_ref, p_scr):
    t = pl.program_id(0)
    seq_first = (t % (SEQ // TQ)) == 0
    lane = lax.broadcasted_iota(I32, (WINDOW, LANES), 1)
    low_head = lane < HEAD_DIM
    key_row = lax.broadcasted_iota(I32, (4 * WINDOW, LANES), 0)
    key_lane = lax.broadcasted_iota(I32, (4 * WINDOW, LANES), 1)
    head_ones = jnp.where((key_row < 2 * WINDOW) == (key_lane < HEAD_DIM), 1.0, 0.0).astype(BF16)
    for n in range(TQ // WINDOW):
        rows = slice(n * WINDOW, (n + 1) * WINDOW)
        if n == 0:
            k_prev, v_prev = kbp_ref[...], vbp_ref[...]
            bias = bias_ref[seq_first.astype(I32)]
        else:
            prev_rows = slice((n - 1) * WINDOW, n * WINDOW)
            k_prev, v_prev = kbc_ref[prev_rows, :], vbc_ref[prev_rows, :]
            bias = bias_ref[0]
        k_cur, v_cur = kbc_ref[rows, :], vbc_ref[rows, :]
        for g in range(2):
            def stack(prev, cur):
                lo = slice((2 * g) * LANES, (2 * g + 1) * LANES)
                hi = slice((2 * g + 1) * LANES, (2 * g + 2) * LANES)
                return jnp.concatenate([prev[:, lo], cur[:, lo], prev[:, hi], cur[:, hi]], axis=0)
            kmat = stack(k_prev, k_cur)
            vmat = stack(v_prev, v_cur)
            qs = jnp.concatenate(
                [q_ref[rows, (4 * g + p) * LANES:(4 * g + p + 1) * LANES] for p in range(4)],
                axis=0)
            s = lax.dot_general(qs, kmat, (((1,), (1,)), ((), ())),
                                preferred_element_type=F32) + bias
            sink_terms = []
            for p in range(4):
                pr = slice(p * WINDOW, (p + 1) * WINDOW)
                ms = []
                for j in range(2):
                    cols = slice(j * 2 * WINDOW, (j + 1) * 2 * WINDOW)
                    sp = s[pr, cols]
                    sink = sinks_ref[8 * g + 2 * p + j] * LOG2_E
                    m = jnp.maximum(jnp.max(sp, axis=-1, keepdims=True), sink)
                    p_scr[pr, cols] = jnp.exp2(sp - m).astype(BF16)
                    ms.append(m)
                sink_pair = jnp.where(low_head, sinks_ref[8 * g + 2 * p] * LOG2_E,
                                      sinks_ref[8 * g + 2 * p + 1] * LOG2_E)
                sink_terms.append(jnp.exp2(sink_pair - jnp.where(low_head, ms[0], ms[1])))
            o = _dot(p_scr[...], jnp.concatenate([vmat, head_ones], axis=1))
            for p in range(4):
                pr = slice(p * WINDOW, (p + 1) * WINDOW)
                inv = 1.0 / (o[pr, LANES:] + sink_terms[p])
                o_ref[rows, (4 * g + p) * LANES:(4 * g + p + 1) * LANES] = (o[pr, :LANES] * inv).astype(BF16)


def _attn_bias():
    r = np.arange(4 * WINDOW)[:, None] % WINDOW
    c = np.arange(4 * WINDOW)[None, :] % (2 * WINDOW)
    band = (c > r) & (c <= r + WINDOW)
    first = band & (c >= WINDOW)
    return np.stack([np.where(band, 0.0, NEG_BIG), np.where(first, 0.0, NEG_BIG)]).astype(np.float32)


def _attn(sinks, q, kb, vb):
    n = q.shape[0]
    blocks_per_tile = TQ // WINDOW
    cur = lambda t: (t, 0)
    prev = lambda t: (jnp.maximum(t * blocks_per_tile - 1, 0), 0)
    bias = jnp.asarray(_attn_bias())
    return pl.pallas_call(
        _attn_body,
        grid=(n // TQ,),
        in_specs=[
            pl.BlockSpec(memory_space=pltpu.SMEM),
            pl.BlockSpec((TQ, Q_WIDTH), cur),
            pl.BlockSpec((TQ, 4 * LANES), cur),
            pl.BlockSpec((WINDOW, 4 * LANES), prev),
            pl.BlockSpec((TQ, 4 * LANES), cur),
            pl.BlockSpec((WINDOW, 4 * LANES), prev),
            pl.BlockSpec((2, 4 * WINDOW, 4 * WINDOW), lambda t: (0, 0, 0)),
        ],
        out_specs=pl.BlockSpec((TQ, Q_WIDTH), cur),
        out_shape=jax.ShapeDtypeStruct((n, Q_WIDTH), BF16),
        scratch_shapes=[pltpu.VMEM((4 * WINDOW, 4 * WINDOW), BF16)],
        compiler_params=pltpu.CompilerParams(
            dimension_semantics=("arbitrary",), vmem_limit_bytes=VMEM_LIMIT),
        name="attn",
    )(sinks, q, kb, kb, vb, vb, bias)


def _post_body(attn_ref, u_ref, uprev_ref, gate_ref, x_ref,
               wo_ref, bo_ref, wmix_ref, pscale_ref, wup_ref, wout_ref,
               gffn_ref, wrt_ref, brt_ref,
               x1_ref, h2_ref, pos_t_ref, pw_ref, tc_ref,
               win_a, win_b, carry):
    i = pl.program_id(0)
    tiles_per_seq = SEQ // TM_POST
    seq_first = (i % tiles_per_seq) == 0

    @pl.when(i == 0)
    def _():
        carry[...] = jnp.zeros_like(carry)

    y_attn = _dot(attn_ref[...], wo_ref[...]) + bo_ref[...]

    rows = TM_POST + POOL_PAD
    win_a[0:POOL_PAD - POOL_HALO, :] = jnp.zeros((POOL_PAD - POOL_HALO, POOL_WIDTH), F32)
    win_a[POOL_PAD - POOL_HALO:POOL_PAD, :] = jnp.where(seq_first, 0.0, uprev_ref[...])
    win_a[POOL_PAD:, :] = u_ref[...]
    src, dst = win_a, win_b
    sums = {}
    for level, w in enumerate((1, 2, 4, 8)):
        lo = 8 * (level + 1)
        c0 = level * POOL_GROUP
        dst[lo:, c0:] = src[lo:, c0:] + src[lo - w:rows - w, c0:]
        sums[2 * w] = dst
        src, dst = dst, (win_a if dst is win_b else win_b)
    row = lax.broadcasted_iota(I32, (TM_POST, 1), 0)
    tpos = (i % tiles_per_seq) * TM_POST + row
    mixed = []
    for gi, w in enumerate(POOL_WINDOWS):
        cols = slice(gi * POOL_GROUP, (gi + 1) * POOL_GROUP)
        cnt = jnp.minimum(tpos + 1, w).astype(F32)
        pooled = sums[w][POOL_PAD:, cols] / cnt - u_ref[:, cols]
        mixed.append(_dot(pooled.astype(BF16), wmix_ref[gi]) * pscale_ref[:, cols])
    mixed = jnp.concatenate(mixed, axis=-1).astype(BF16)
    y_pool = _dot(mixed, wup_ref[...])
    merged = (gate_ref[:, :D_MODEL].astype(F32) * y_attn
              + gate_ref[:, D_MODEL:].astype(F32) * y_pool)
    x1 = x_ref[...] + _dot(merged.astype(BF16), wout_ref[...])
    x1_ref[...] = x1
    h2 = _rms_norm(x1, gffn_ref[...])
    h2_bf = h2.astype(BF16)
    h2_ref[...] = h2_bf

    logits_t = lax.dot_general(wrt_ref[...], h2_bf, (((1,), (1,)), ((), ())),
                               preferred_element_type=F32) + brt_ref[...]
    erow = lax.broadcasted_iota(I32, (N_EXPERTS, TM_POST), 0)
    work = logits_t
    top_v, onehots = [], []
    for _ in range(TOP_K):
        m = jnp.max(work, axis=0, keepdims=True)
        idx = jnp.min(jnp.where(work == m, erow, N_EXPERTS), axis=0, keepdims=True)
        hit = erow == idx
        top_v.append(m)
        onehots.append(hit)
        work = jnp.where(hit, -jnp.inf, work)
    exps = [jnp.exp(v - top_v[0]) for v in top_v]
    denom = exps[0] + exps[1] + exps[2] + exps[3]
    comb = [e / denom for e in exps]
    chosen = jnp.zeros((N_EXPERTS, TM_POST), F32)
    for hit in onehots:
        chosen = chosen + hit.astype(F32)

    e_r = lax.broadcasted_iota(I32, (N_EXPERTS, N_EXPERTS), 0)
    e_c = lax.broadcasted_iota(I32, (N_EXPERTS, N_EXPERTS), 1)
    earlier_expert = (e_c < e_r).astype(BF16)
    t_r = lax.broadcasted_iota(I32, (TT, TT), 0)
    t_c = lax.broadcasted_iota(I32, (TT, TT), 1)
    earlier_token = (t_r < t_c).astype(BF16)
    lane = lax.broadcasted_iota(I32, (N_EXPERTS, LANES), 1)
    counts = jnp.zeros((N_EXPERTS, LANES), F32)
    pos_parts = [[] for _ in range(TOP_K)]
    n_sub = TM_POST // TT
    for s in range(n_sub):
        cols = slice(s * TT, (s + 1) * TT)
        ch = chosen[:, cols]
        cnt = jnp.broadcast_to(jnp.sum(ch, axis=1, keepdims=True), (N_EXPERTS, LANES))
        before = carry[...]
        head = before - SEG_ALIGN * jnp.floor(before / SEG_ALIGN)
        flush = jnp.logical_and(i == pl.num_programs(0) - 1, s == n_sub - 1)
        present = (cnt > 0) | (flush & (head > 0))
        blocks = jnp.where(present, jnp.floor((head + cnt + (SEG_ALIGN - 1)) / SEG_ALIGN), 0.0)
        seg_loc = SEG_ALIGN * _dot(earlier_expert, blocks.astype(BF16))
        base = seg_loc + head
        rank_in_tile = _dot(ch.astype(BF16), earlier_token)
        full = jnp.concatenate([base] * (TT // LANES), axis=1) + rank_in_tile
        for k in range(TOP_K):
            pos_parts[k].append(
                jnp.sum(jnp.where(onehots[k][:, cols], full, 0.0), axis=0, keepdims=True))
        counts = jnp.where(lane == s, cnt, counts)
        carry[...] = before + cnt
    tc_ref[...] = counts

    pos = [jnp.concatenate(parts, axis=1) for parts in pos_parts]
    row8 = lax.broadcasted_iota(I32, (8, TM_POST), 0)
    pos_rows = jnp.full((8, TM_POST), -1.0, F32)
    both = jnp.zeros((8, TM_POST), F32)
    for k in range(TOP_K):
        pos_rows = jnp.where(row8 == k, pos[k], pos_rows)
        both = jnp.where(row8 == k, pos[k], both)
        both = jnp.where(row8 == TOP_K + k, comb[k], both)
    pos_t_ref[...] = pos_rows.astype(I32)
    padded = jnp.concatenate([both, jnp.zeros((LANES - 8, TM_POST), F32)], axis=0)
    pw_ref[...] = padded.T


def _post(attn, u, gates, x2, wo, bo, wmix, pscale, wup, wout, gffn, wr, br):
    n = x2.shape[0]
    row = lambda i: (i, 0)
    const = lambda i: (0, 0)
    halo_blocks = TM_POST // POOL_HALO
    prev = lambda i: (jnp.maximum(i * halo_blocks - 1, 0), 0)
    return pl.pallas_call(
        _post_body,
        grid=(n // TM_POST,),
        in_specs=[
            pl.BlockSpec((TM_POST, Q_WIDTH), row),
            pl.BlockSpec((TM_POST, POOL_WIDTH), row),
            pl.BlockSpec((POOL_HALO, POOL_WIDTH), prev),
            pl.BlockSpec((TM_POST, 2 * D_MODEL), row),
            pl.BlockSpec((TM_POST, D_MODEL), row),
            pl.BlockSpec((Q_WIDTH, D_MODEL), const),
            pl.BlockSpec((1, D_MODEL), const),
            pl.BlockSpec((len(POOL_WINDOWS), POOL_GROUP, POOL_GROUP), lambda i: (0, 0, 0)),
            pl.BlockSpec((1, POOL_WIDTH), const),
            pl.BlockSpec((POOL_WIDTH, D_MODEL), const),
            pl.BlockSpec((D_MODEL, D_MODEL), const),
            pl.BlockSpec((1, D_MODEL), const),
            pl.BlockSpec((N_EXPERTS, D_MODEL), const),
            pl.BlockSpec((N_EXPERTS, TM_POST), const),
        ],
        out_specs=[
            pl.BlockSpec((TM_POST, D_MODEL), row),
            pl.BlockSpec((TM_POST, D_MODEL), row),
            pl.BlockSpec((8, TM_POST), lambda i: (0, i)),
            pl.BlockSpec((TM_POST, LANES), row),
            pl.BlockSpec((N_EXPERTS, LANES), row),
        ],
        out_shape=[
            jax.ShapeDtypeStruct((n, D_MODEL), F32),
            jax.ShapeDtypeStruct((n, D_MODEL), BF16),
            jax.ShapeDtypeStruct((8, n), I32),
            jax.ShapeDtypeStruct((n, LANES), F32),
            jax.ShapeDtypeStruct((n // TM_POST * N_EXPERTS, LANES), F32),
        ],
        scratch_shapes=[pltpu.VMEM((TM_POST + POOL_PAD, POOL_WIDTH), F32),
                        pltpu.VMEM((TM_POST + POOL_PAD, POOL_WIDTH), F32),
                        pltpu.VMEM((N_EXPERTS, LANES), F32)],
        compiler_params=pltpu.CompilerParams(
            dimension_semantics=("arbitrary",), vmem_limit_bytes=VMEM_LIMIT),
        name="post",
    )(attn, u, u, gates, x2, wo, bo, wmix, pscale, wup, wout, gffn, wr, br)


def _start_segments(i, glob_ref, loc_ref, cnt_ref, make_copy):
    def body(e, c):
        idx = i * N_EXPERTS + e
        rows = pl.multiple_of(cnt_ref[idx], SEG_ALIGN)

        @pl.when(rows > 0)
        def _():
            loc = pl.multiple_of(loc_ref[idx], SEG_ALIGN)
            glob = pl.multiple_of(glob_ref[idx], SEG_ALIGN)
            make_copy(pl.ds(loc, rows), pl.ds(glob, rows)).start()
        return c
    lax.fori_loop(0, N_EXPERTS, body, 0, unroll=4)


def _wait_segments(i, total_ref, make_copy):
    rows = pl.multiple_of(total_ref[i], SEG_ALIGN)

    @pl.when(rows > 0)
    def _():
        make_copy(pl.ds(0, rows), pl.ds(0, rows)).wait()


def _pack_halves(v):
    c = v.shape[1] // 2
    bits = lax.bitcast_convert_type(v, U32)
    return (bits[:, :c] >> 16) | bits[:, c:]


def _unpack_halves(w):
    lo = lax.bitcast_convert_type(w << 16, F32)
    hi = lax.bitcast_convert_type(w & jnp.uint32(0xFFFF0000), F32)
    return jnp.concatenate([lo, hi], axis=1).astype(BF16)


def _dispatch_body(glob_ref, loc_ref, rd_ref, wr_ref, total_ref, rows_ref, h2_ref, pos_ref,
                   xs_hbm, local, tail, sem):
    i = pl.program_id(0)
    nt = pl.num_programs(0)
    slot = i % 2

    def copy_from(s):
        def make_copy(loc_rows, glob_rows):
            return pltpu.make_async_copy(local.at[s, loc_rows], xs_hbm.at[glob_rows], sem.at[s])
        return make_copy

    @pl.when(i == 0)
    def _():
        tail[...] = jnp.zeros_like(tail)

    @pl.when(i >= 2)
    def _():
        _wait_segments(i - 2, total_ref, copy_from(slot))

    def group(r0, nrows):
        j = lax.broadcasted_iota(I32, (nrows, TT), 0) + r0
        hit = j == pos_ref[0:1, :]
        for k in range(1, TOP_K):
            hit = hit | (j == pos_ref[k:k + 1, :])
        perm = jnp.where(hit, 1.0, 0.0).astype(BF16)
        local[slot, r0:r0 + nrows, :] = _pack_halves(_dot(perm, h2_ref[...]))

    group(0, LROWS_MAIN)

    @pl.when(rows_ref[i] + SEG_ALIGN > LROWS_MAIN)
    def _():
        group(LROWS_MAIN, LROWS - LROWS_MAIN)

    for e in range(N_EXPERTS):
        idx = i * N_EXPERTS + e
        rd = rd_ref[idx]
        wr = wr_ref[idx]
        loc = pl.multiple_of(loc_ref[idx], SEG_ALIGN)
        first = local[slot, pl.ds(loc, SEG_ALIGN), :]
        last = local[slot, pl.ds(pl.multiple_of(loc + wr, SEG_ALIGN), SEG_ALIGN), :]
        merged = jnp.where(rd > 0, first | tail[e], first)
        local[slot, pl.ds(loc, SEG_ALIGN), :] = merged
        last = jnp.where(wr == 0, merged, last)
        tail[e] = jnp.where(rd > 0, jnp.where(wr < rd, last, jnp.uint32(0)), tail[e])
    _start_segments(i, glob_ref, loc_ref, wr_ref, copy_from(slot))

    @pl.when(i == nt - 1)
    def _():
        _wait_segments(i - 1, total_ref, copy_from(1 - slot))
        _wait_segments(i, total_ref, copy_from(slot))


def _dispatch(seg_glob, seg_loc, seg_rd, seg_wr, tile_wr, tile_rd, h2, pos_t, n_rows):
    n = h2.shape[0]
    grid_spec = pltpu.PrefetchScalarGridSpec(
        num_scalar_prefetch=6,
        grid=(n // TT,),
        in_specs=[
            pl.BlockSpec((TT, D_MODEL), lambda i, *_: (i, 0)),
            pl.BlockSpec((8, TT), lambda i, *_: (0, i)),
        ],
        out_specs=pl.BlockSpec(memory_space=pl.ANY),
        scratch_shapes=[pltpu.VMEM((2, LROWS, PACKED), U32),
                        pltpu.VMEM((N_EXPERTS, SEG_ALIGN, PACKED), U32),
                        pltpu.SemaphoreType.DMA((2,))],
    )
    return pl.pallas_call(
        _dispatch_body,
        grid_spec=grid_spec,
        out_shape=jax.ShapeDtypeStruct((n_rows, PACKED), U32),
        compiler_params=pltpu.CompilerParams(
            dimension_semantics=("arbitrary",), vmem_limit_bytes=VMEM_LIMIT),
        name="dispatch",
    )(seg_glob, seg_loc, seg_rd, seg_wr, tile_wr, tile_rd, h2, pos_t)


def _moe_body(te_ref, nused_ref, rows_ref, next_ref, xs_ref, bias_ref,
              wg_hbm, wu_hbm, wd_hbm, y_ref, wf32, wbf, sem, slot_ref):
    i = pl.program_id(0)

    def fetch(expert, slot):
        return [pltpu.make_async_copy(w.at[expert], wf32.at[slot, m], sem.at[slot, m])
                for m, w in enumerate((wg_hbm, wu_hbm, wd_hbm))]

    @pl.when(i == 0)
    def _():
        slot_ref[0] = 0

    @pl.when(i < nused_ref[0])
    def _():
        expert = te_ref[i]

        @pl.when((i == 0) | (expert != te_ref[jnp.maximum(i - 1, 0)]))
        def _():
            slot = slot_ref[0]

            @pl.when(i == 0)
            def _():
                for cp in fetch(expert, slot):
                    cp.start()
            for cp in fetch(expert, slot):
                cp.wait()
            nxt = next_ref[i]

            @pl.when(nxt >= 0)
            def _():
                for cp in fetch(nxt, 1 - slot):
                    cp.start()
            for m in range(3):
                wbf[m] = wf32[slot, m].astype(BF16)
            slot_ref[0] = 1 - slot

        def mlp(r0):
            rows = slice(r0, r0 + MOE_ROWS)
            row = lax.broadcasted_iota(I32, (MOE_ROWS, 1), 0) + r0
            x = _unpack_halves(jnp.where(row < rows_ref[i], xs_ref[rows, :], jnp.uint32(0)))
            bias = bias_ref[expert]
            g = _dot(x, wbf[0]) + bias[0:1, :]
            u = _dot(x, wbf[1]) + bias[1:2, :]
            g = jnp.minimum(g, SWIGLU_LIMIT)
            u = jnp.clip(u, -SWIGLU_LIMIT, SWIGLU_LIMIT)
            a = (g * jax.nn.sigmoid(SWIGLU_ALPHA * g) * (u + 1.0)).astype(BF16)
            y = _dot(a, wbf[2]) + bias[2:3, :]
            y_ref[rows, :] = _pack_halves(y.astype(BF16).astype(F32))

        mlp(0)
        for r0 in range(MOE_ROWS, TM_MOE, MOE_ROWS):
            @pl.when(rows_ref[i] > r0)
            def _():
                mlp(r0)


def _moe(tile_expert, n_used, tile_rows, tile_next, xs, wg, bg, wu, bu, wd, bd):
    n_rows = xs.shape[0]
    n_tiles = n_rows // TM_MOE
    d_ff = wg.shape[2]
    assert d_ff == D_MODEL
    row = lambda i, te, nu, *_: (jnp.minimum(i, nu[0] - 1), 0)
    bias = jnp.zeros((N_EXPERTS, 8, D_MODEL), F32)
    bias = bias.at[:, 0].set(bg).at[:, 1].set(bu).at[:, 2].set(bd)
    grid_spec = pltpu.PrefetchScalarGridSpec(
        num_scalar_prefetch=4,
        grid=(n_tiles,),
        in_specs=[
            pl.BlockSpec((TM_MOE, PACKED), row),
            pl.BlockSpec((N_EXPERTS, 8, D_MODEL), lambda i, *_: (0, 0, 0)),
            pl.BlockSpec(memory_space=pl.ANY),
            pl.BlockSpec(memory_space=pl.ANY),
            pl.BlockSpec(memory_space=pl.ANY),
        ],
        out_specs=pl.BlockSpec((TM_MOE, PACKED), row),
        scratch_shapes=[pltpu.VMEM((2, 3, D_MODEL, D_MODEL), F32),
                        pltpu.VMEM((3, D_MODEL, D_MODEL), BF16),
                        pltpu.SemaphoreType.DMA((2, 3)),
                        pltpu.SMEM((1,), I32)],
    )
    return pl.pallas_call(
        _moe_body,
        grid_spec=grid_spec,
        out_shape=jax.ShapeDtypeStruct((n_rows, PACKED), U32),
        compiler_params=pltpu.CompilerParams(
            dimension_semantics=("arbitrary",), vmem_limit_bytes=VMEM_LIMIT),
        name="moe",
    )(tile_expert, n_used, tile_rows, tile_next, xs, bias, wg, wu, wd)


def _combine_body(glob_ref, loc_ref, cnt_ref, total_ref, x1_ref, pw_ref, gfin_ref,
                  ys_hbm, o_ref, local, sem):
    i = pl.program_id(0)
    nt = pl.num_programs(0)
    slot = i % 2

    def copy_into(s):
        def make_copy(loc_rows, glob_rows):
            return pltpu.make_async_copy(ys_hbm.at[glob_rows], local.at[s, loc_rows], sem.at[s])
        return make_copy

    @pl.when(i == 0)
    def _():
        local[...] = jnp.zeros_like(local)
        _start_segments(0, glob_ref, loc_ref, cnt_ref, copy_into(0))

    @pl.when(i + 1 < nt)
    def _():
        _start_segments(i + 1, glob_ref, loc_ref, cnt_ref, copy_into(1 - slot))
    _wait_segments(i, total_ref, copy_into(slot))

    def finish(nrows):
        j = lax.broadcasted_iota(I32, (TT, nrows), 1).astype(F32)
        wc = jnp.zeros((TT, nrows), F32)
        for k in range(TOP_K):
            wc = jnp.where(j == pw_ref[:, k:k + 1], pw_ref[:, TOP_K + k:TOP_K + k + 1], wc)
        acc = x1_ref[...] + _dot(wc.astype(BF16), _unpack_halves(local[slot, 0:nrows, :]))
        o_ref[...] = _rms_norm(acc, gfin_ref[...])

    @pl.when(total_ref[i] <= LROWS_MAIN)
    def _():
        finish(LROWS_MAIN)

    @pl.when(total_ref[i] > LROWS_MAIN)
    def _():
        finish(LROWS)


def _combine(seg_glob, seg_loc, seg_cnt, tile_rd, x1, pw, gfin, ys):
    n = x1.shape[0]
    grid_spec = pltpu.PrefetchScalarGridSpec(
        num_scalar_prefetch=4,
        grid=(n // TT,),
        in_specs=[
            pl.BlockSpec((TT, D_MODEL), lambda i, *_: (i, 0)),
            pl.BlockSpec((TT, LANES), lambda i, *_: (i, 0)),
            pl.BlockSpec((1, D_MODEL), lambda i, *_: (0, 0)),
            pl.BlockSpec(memory_space=pl.ANY),
        ],
        out_specs=pl.BlockSpec((TT, D_MODEL), lambda i, *_: (i, 0)),
        scratch_shapes=[pltpu.VMEM((2, LROWS, PACKED), U32),
                        pltpu.SemaphoreType.DMA((2,))],
    )
    return pl.pallas_call(
        _combine_body,
        grid_spec=grid_spec,
        out_shape=jax.ShapeDtypeStruct((n, D_MODEL), F32),
        compiler_params=pltpu.CompilerParams(
            dimension_semantics=("arbitrary",), vmem_limit_bytes=VMEM_LIMIT),
        name="combine",
    )(seg_glob, seg_loc, seg_cnt, tile_rd, x1, pw, gfin, ys)


def _layer(x2, pos_b, invf, norm_mix_g, w_in, b_in, attn_sinks, w_o_attn, b_o_attn,
           w_pool_mix, pool_scale, w_pool_up, w_out, norm_ffn_g, w_router, b_router,
           w_gate, b_gate, w_up, b_up, w_down, b_down, out_g):
    n = x2.shape[0]
    q, kb, vb, u, gates = _inproj(
        x2, pos_b, invf, norm_mix_g[None, :], w_in.astype(BF16), b_in[None, :])
    attn = _attn(attn_sinks, q, kb, vb)

    wrt = w_router.T.astype(BF16)
    brt = jnp.broadcast_to(b_router[:, None], (N_EXPERTS, TM_POST))
    x1, h2, pos_t, pw, tc = _post(
        attn, u, gates, x2, w_o_attn.astype(BF16), b_o_attn[None, :],
        w_pool_mix.astype(BF16), pool_scale[None, :], w_pool_up.astype(BF16),
        w_out.astype(BF16), norm_ffn_g[None, :], wrt, brt)

    nt = n // TT
    sub = TM_POST // TT
    seg_cnt = (tc.reshape(n // TM_POST, N_EXPERTS, LANES)[:, :, :sub]
               .transpose(0, 2, 1).reshape(nt, N_EXPERTS).astype(I32))
    before = jnp.cumsum(seg_cnt, axis=0) - seg_cnt
    sizes = jnp.sum(seg_cnt, axis=0)
    padded = (sizes + TM_MOE - 1) // TM_MOE * TM_MOE
    pends = jnp.cumsum(padded)
    pstarts = pends - padded
    head = before % SEG_ALIGN
    last_tile = (jnp.arange(nt, dtype=I32) == nt - 1)[:, None]
    blocks_up = (head + seg_cnt + SEG_ALIGN - 1) // SEG_ALIGN
    blocks_dn = (head + seg_cnt) // SEG_ALIGN
    seg_rd = jnp.where((seg_cnt > 0) | (last_tile & (head > 0)), blocks_up, 0) * SEG_ALIGN
    seg_wr = jnp.where(last_tile, seg_rd, jnp.where(seg_cnt > 0, blocks_dn * SEG_ALIGN, 0))
    seg_loc = jnp.cumsum(seg_rd, axis=1) - seg_rd
    seg_glob = pstarts[None, :] + before - head
    tile_rd = jnp.sum(seg_rd, axis=1)
    tile_wr = jnp.sum(seg_wr, axis=1)
    seg_glob, seg_loc, seg_rd, seg_wr = (a.reshape(-1) for a in (seg_glob, seg_loc, seg_rd, seg_wr))
    m = n * TOP_K
    n_tiles = (m + N_EXPERTS * (TM_MOE - 1) + TM_MOE - 1) // TM_MOE
    tile_start = jnp.arange(n_tiles, dtype=I32) * TM_MOE
    tile_expert = jnp.minimum(
        jnp.sum(tile_start[:, None] >= pends[None, :], axis=-1), N_EXPERTS - 1).astype(I32)
    own = tile_expert[:, None] == jnp.arange(N_EXPERTS, dtype=I32)[None, :]
    left = jnp.sum(jnp.where(own, (pstarts + sizes)[None, :] - tile_start[:, None], 0), axis=-1)
    tile_rows = jnp.clip(left, 0, TM_MOE)
    ids = jnp.arange(N_EXPERTS, dtype=I32)
    later = jnp.where((sizes > 0)[None, :] & (ids[None, :] > ids[:, None]), ids[None, :], N_EXPERTS)
    next_expert = jnp.min(later, axis=1)
    next_expert = jnp.where(next_expert == N_EXPERTS, -1, next_expert)
    tile_next = jnp.sum(jnp.where(own, next_expert[None, :], 0), axis=-1).astype(I32)
    n_used = (pends[-1] // TM_MOE).astype(I32)[None]

    xs = _dispatch(seg_glob, seg_loc, seg_rd, seg_wr, tile_wr, tile_rd, h2, pos_t, n_tiles * TM_MOE)
    ys = _moe(tile_expert, n_used, tile_rows.astype(I32), tile_next, xs,
              w_gate, b_gate, w_up, b_up, w_down, b_down)
    return _combine(seg_glob, seg_loc, seg_rd, tile_rd, x1, pw, out_g[None, :], ys)


def kernel(x, positions, norm_mix_g, w_in, b_in, attn_sinks, w_o_attn, b_o_attn, w_pool_mix,
           pool_scale, w_pool_up, w_out, norm_ffn_g, w_router, b_router, w_gate, b_gate,
           w_up, b_up, w_down, b_down, norm_final_g):
    b, s, d = x.shape
    depth = w_in.shape[0]
    assert (s, d, depth) == (SEQ, D_MODEL, 1)
    n = b * s
    x2 = x.reshape(n, d)
    pos_b = jnp.repeat(positions.reshape(n // 4, 4), LANES // 4, axis=1)
    inv_freq = ROPE_THETA ** (-jnp.arange(0, HEAD_DIM, 2, dtype=F32) / HEAD_DIM)
    invf = jnp.tile(inv_freq, LANES // (HEAD_DIM // 2))[None, :]
    out = _layer(x2, pos_b, invf, norm_mix_g[0], w_in[0], b_in[0], attn_sinks[0], w_o_attn[0],
                 b_o_attn[0], w_pool_mix[0], pool_scale[0], w_pool_up[0], w_out[0],
                 norm_ffn_g[0], w_router[0], b_router[0], w_gate[0], b_gate[0], w_up[0],
                 b_up[0], w_down[0], b_down[0], norm_final_g)
    return out.reshape(b, s, d)
```

```python
import numpy as np
import jax
import jax.numpy as jnp
from jax import lax
from jax.experimental import pallas as pl
from jax.experimental.pallas import tpu as pltpu

F32 = jnp.float32
BF16 = jnp.bfloat16
I32 = jnp.int32
U32 = jnp.uint32

D_MODEL = 1024
SEQ = 4096
HEAD_DIM = 64
N_Q_HEADS = 16
WINDOW = 128
ROPE_THETA = 10000.0
Q_WIDTH = N_Q_HEADS * HEAD_DIM
KV_WIDTH = 128
POOL_WINDOWS = (2, 4, 8, 16)
POOL_WIDTH = 512
POOL_GROUP = 128
POOL_HALO = 16
POOL_PAD = 32
N_EXPERTS = 32
TOP_K = 4
SWIGLU_LIMIT = 7.0
SWIGLU_ALPHA = 1.702
RMS_EPS = 1e-5
NEG_BIG = -1e30
LOG2_E = 1.4426950408889634

LANES = 128
TM_IN = 1024
TQ = 512
TM_POST = 512
TM_MOE = 1024
MOE_ROWS = 512
TT = 256
SEG_ALIGN = 8
LROWS = TT * TOP_K + 2 * N_EXPERTS * SEG_ALIGN
MXU_DEPTH = 256
LROWS_MAIN = LROWS - MXU_DEPTH
PACKED = D_MODEL // 2
VMEM_LIMIT = 56 * 1024 * 1024


def _rms_norm(x, g):
    ms = jnp.mean(x * x, axis=-1, keepdims=True)
    return (x * lax.rsqrt(ms + RMS_EPS)) * g


def _dot(a, b):
    return jnp.dot(a, b, preferred_element_type=F32)


def _inproj_body(x_ref, pos_ref, invf_ref, g_ref, w_ref, b_ref,
                 q_ref, kb_ref, vb_ref, u_ref, gate_ref, cos_scr, sin_scr):
    h = _rms_norm(x_ref[...], g_ref[...]).astype(BF16)
    qk_w = Q_WIDTH + KV_WIDTH
    v0 = qk_w
    u0 = v0 + KV_WIDTH
    g0 = u0 + POOL_WIDTH

    def gate_chunk(c):
        sl = slice(g0 + c * D_MODEL, g0 + (c + 1) * D_MODEL)
        zg = _dot(h, w_ref[:, sl]) + b_ref[:, sl]
        gate_ref[:, c * D_MODEL:(c + 1) * D_MODEL] = jax.nn.sigmoid(zg).astype(BF16)

    gate_chunk(0)

    theta = pos_ref[...].astype(F32) * invf_ref[...]
    lane4 = lax.broadcasted_iota(I32, theta.shape, 1)
    for packed, scr in ((jnp.cos(theta), cos_scr), (jnp.sin(theta), sin_scr)):
        for jt in range(4):
            seg = packed if jt == 0 else pltpu.roll(packed, LANES - 32 * jt, 1)
            seg = jnp.where(lane4 < 32, seg, pltpu.roll(seg, 32, 1))
            seg = jnp.where(lane4 < 64, seg, pltpu.roll(seg, 64, 1))
            scr[pl.ds(jt, TM_IN // 4, stride=4), :] = seg
    cos = cos_scr[...]
    sin = sin_scr[...]
    lane = lax.broadcasted_iota(I32, cos.shape, 1)
    first_half = (lane & 32) == 0
    sin_signed = jnp.where(first_half, -sin, sin)
    low_head = lane < HEAD_DIM

    def rope(t):
        swapped = jnp.where(first_half, pltpu.roll(t, 96, 1), pltpu.roll(t, 32, 1))
        return t * cos + swapped * sin_signed

    def band_layout(t, out_ref):
        tr = pltpu.roll(t, 64, 1)
        zero = jnp.zeros_like(t)
        chunks = (jnp.where(low_head, t, zero), jnp.where(low_head, zero, tr),
                  jnp.where(low_head, tr, zero), jnp.where(low_head, zero, t))
        for c, val in enumerate(chunks):
            out_ref[:, c * LANES:(c + 1) * LANES] = val.astype(BF16)

    zqk = _dot(h, w_ref[:, :qk_w]) + b_ref[:, :qk_w]
    gate_chunk(1)
    scale = HEAD_DIM ** -0.5 * LOG2_E
    for j in range(Q_WIDTH // LANES):
        sl = slice(j * LANES, (j + 1) * LANES)
        q_ref[:, sl] = (rope(zqk[:, sl]) * scale).astype(BF16)
    band_layout(rope(zqk[:, Q_WIDTH:qk_w]), kb_ref)

    zv = _dot(h, w_ref[:, v0:v0 + KV_WIDTH]) + b_ref[:, v0:v0 + KV_WIDTH]
    band_layout(zv, vb_ref)
    u_ref[...] = _dot(h, w_ref[:, u0:u0 + POOL_WIDTH]) + b_ref[:, u0:u0 + POOL_WIDTH]


def _inproj(x2, pos_b, invf, g, w, b):
    n = x2.shape[0]
    in_width = w.shape[1]
    row = lambda i: (i, 0)
    const = lambda i: (0, 0)
    return pl.pallas_call(
        _inproj_body,
        grid=(n // TM_IN,),
        in_specs=[
            pl.BlockSpec((TM_IN, D_MODEL), row),
            pl.BlockSpec((TM_IN // 4, LANES), row),
            pl.BlockSpec((1, LANES), const),
            pl.BlockSpec((1, D_MODEL), const),
            pl.BlockSpec((D_MODEL, in_width), const),
            pl.BlockSpec((1, in_width), const),
        ],
        out_specs=[
            pl.BlockSpec((TM_IN, Q_WIDTH), row),
            pl.BlockSpec((TM_IN, 4 * LANES), row),
            pl.BlockSpec((TM_IN, 4 * LANES), row),
            pl.BlockSpec((TM_IN, POOL_WIDTH), row),
            pl.BlockSpec((TM_IN, 2 * D_MODEL), row),
        ],
        out_shape=[
            jax.ShapeDtypeStruct((n, Q_WIDTH), BF16),
            jax.ShapeDtypeStruct((n, 4 * LANES), BF16),
            jax.ShapeDtypeStruct((n, 4 * LANES), BF16),
            jax.ShapeDtypeStruct((n, POOL_WIDTH), F32),
            jax.ShapeDtypeStruct((n, 2 * D_MODEL), BF16),
        ],
        scratch_shapes=[pltpu.VMEM((TM_IN, LANES), F32), pltpu.VMEM((TM_IN, LANES), F32)],
        compiler_params=pltpu.CompilerParams(
            dimension_semantics=("arbitrary",), vmem_limit_bytes=VMEM_LIMIT),
        name="inproj",
    )(x2, pos_b, invf, g, w, b)


def _attn_body(sinks_ref, q_ref, kbc_ref, kbp_ref, vbc_ref, vbp_ref, bias_ref,
               o_ref, p_scr):
    t = pl.program_id(0)
    seq_first = (t % (SEQ // TQ)) == 0
    lane = lax.broadcasted_iota(I32, (WINDOW, LANES), 1)
    low_head = lane < HEAD_DIM
    key_row = lax.broadcasted_iota(I32, (4 * WINDOW, LANES), 0)
    key_lane = lax.broadcasted_iota(I32, (4 * WINDOW, LANES), 1)
    head_ones = jnp.where((key_row < 2 * WINDOW) == (key_lane < HEAD_DIM), 1.0, 0.0).astype(BF16)
    for n in range(TQ // WINDOW):
        rows = slice(n * WINDOW, (n + 1) * WINDOW)
        if n == 0:
            k_prev, v_prev = kbp_ref[...], vbp_ref[...]
            bias = bias_ref[seq_first.astype(I32)]
        else:
            prev_rows = slice((n - 1) * WINDOW, n * WINDOW)
            k_prev, v_prev = kbc_ref[prev_rows, :], vbc_ref[prev_rows, :]
            bias = bias_ref[0]
        k_cur, v_cur = kbc_ref[rows, :], vbc_ref[rows, :]
        for g in range(2):
            def stack(prev, cur):
                lo = slice((2 * g) * LANES, (2 * g + 1) * LANES)
                hi = slice((2 * g + 1) * LANES, (2 * g + 2) * LANES)
                return jnp.concatenate([prev[:, lo], cur[:, lo], prev[:, hi], cur[:, hi]], axis=0)
            kmat = stack(k_prev, k_cur)
            vmat = stack(v_prev, v_cur)
            qs = jnp.concatenate(
                [q_ref[rows, (4 * g + p) * LANES:(4 * g + p + 1) * LANES] for p in range(4)],
                axis=0)
            s = lax.dot_general(qs, kmat, (((1,), (1,)), ((), ())),
                                preferred_element_type=F32) + bias
            sink_terms = []
            for p in range(4):
                pr = slice(p * WINDOW, (p + 1) * WINDOW)
                ms = []
                for j in range(2):
                    cols = slice(j * 2 * WINDOW, (j + 1) * 2 * WINDOW)
                    sp = s[pr, cols]
                    sink = sinks_ref[8 * g + 2 * p + j] * LOG2_E
                    m = jnp.maximum(jnp.max(sp, axis=-1, keepdims=True), sink)
                    p_scr[pr, cols] = jnp.exp2(sp - m).astype(BF16)
                    ms.append(m)
                sink_pair = jnp.where(low_head, sinks_ref[8 * g + 2 * p] * LOG2_E,
                                      sinks_ref[8 * g + 2 * p + 1] * LOG2_E)
                sink_terms.append(jnp.exp2(sink_pair - jnp.where(low_head, ms[0], ms[1])))
            o = _dot(p_scr[...], jnp.concatenate([vmat, head_ones], axis=1))
            for p in range(4):
                pr = slice(p * WINDOW, (p + 1) * WINDOW)
                inv = 1.0 / (o[pr, LANES:] + sink_terms[p])
                o_ref[rows, (4 * g + p) * LANES:(4 * g + p + 1) * LANES] = (o[pr, :LANES] * inv).astype(BF16)


def _attn_bias():
    r = np.arange(4 * WINDOW)[:, None] % WINDOW
    c = np.arange(4 * WINDOW)[None, :] % (2 * WINDOW)
    band = (c > r) & (c <= r + WINDOW)
    first = band & (c >= WINDOW)
    return np.stack([np.where(band, 0.0, NEG_BIG), np.where(first, 0.0, NEG_BIG)]).astype(np.float32)


def _attn(sinks, q, kb, vb):
    n = q.shape[0]
    blocks_per_tile = TQ // WINDOW
    cur = lambda t: (t, 0)
    prev = lambda t: (jnp.maximum(t * blocks_per_tile - 1, 0), 0)
    bias = jnp.asarray(_attn_bias())
    return pl.pallas_call(
        _attn_body,
        grid=(n // TQ,),
        in_specs=[
            pl.BlockSpec(memory_space=pltpu.SMEM),
            pl.BlockSpec((TQ, Q_WIDTH), cur),
            pl.BlockSpec((TQ, 4 * LANES), cur),
            pl.BlockSpec((WINDOW, 4 * LANES), prev),
            pl.BlockSpec((TQ, 4 * LANES), cur),
            pl.BlockSpec((WINDOW, 4 * LANES), prev),
            pl.BlockSpec((2, 4 * WINDOW, 4 * WINDOW), lambda t: (0, 0, 0)),
        ],
        out_specs=pl.BlockSpec((TQ, Q_WIDTH), cur),
        out_shape=jax.ShapeDtypeStruct((n, Q_WIDTH), BF16),
        scratch_shapes=[pltpu.VMEM((4 * WINDOW, 4 * WINDOW), BF16)],
        compiler_params=pltpu.CompilerParams(
            dimension_semantics=("arbitrary",), vmem_limit_bytes=VMEM_LIMIT),
        name="attn",
    )(sinks, q, kb, kb, vb, vb, bias)


def _post_body(attn_ref, u_ref, uprev_ref, gate_ref, x_ref,
               wo_ref, bo_ref, wmix_ref, pscale_ref, wup_ref, wout_ref,
               gffn_ref, wrt_ref, brt_ref,
               x1_ref, h2_ref, pos_t_ref, pw_ref, tc_ref,
               win_a, win_b, carry):
    i = pl.program_id(0)
    tiles_per_seq = SEQ // TM_POST
    seq_first = (i % tiles_per_seq) == 0

    @pl.when(i == 0)
    def _():
        carry[...] = jnp.zeros_like(carry)

    y_attn = _dot(attn_ref[...], wo_ref[...]) + bo_ref[...]

    rows = TM_POST + POOL_PAD
    win_a[0:POOL_PAD - POOL_HALO, :] = jnp.zeros((POOL_PAD - POOL_HALO, POOL_WIDTH), F32)
    win_a[POOL_PAD - POOL_HALO:POOL_PAD, :] = jnp.where(seq_first, 0.0, uprev_ref[...])
    win_a[POOL_PAD:, :] = u_ref[...]
    src, dst = win_a, win_b
    sums = {}
    for level, w in enumerate((1, 2, 4, 8)):
        lo = 8 * (level + 1)
        c0 = level * POOL_GROUP
        dst[lo:, c0:] = src[lo:, c0:] + src[lo - w:rows - w, c0:]
        sums[2 * w] = dst
        src, dst = dst, (win_a if dst is win_b else win_b)
    row = lax.broadcasted_iota(I32, (TM_POST, 1), 0)
    tpos = (i % tiles_per_seq) * TM_POST + row
    mixed = []
    for gi, w in enumerate(POOL_WINDOWS):
        cols = slice(gi * POOL_GROUP, (gi + 1) * POOL_GROUP)
        cnt = jnp.minimum(tpos + 1, w).astype(F32)
        pooled = sums[w][POOL_PAD:, cols] / cnt - u_ref[:, cols]
        mixed.append(_dot(pooled.astype(BF16), wmix_ref[gi]) * pscale_ref[:, cols])
    mixed = jnp.concatenate(mixed, axis=-1).astype(BF16)
    y_pool = _dot(mixed, wup_ref[...])
    merged = (gate_ref[:, :D_MODEL].astype(F32) * y_attn
              + gate_ref[:, D_MODEL:].astype(F32) * y_pool)
    x1 = x_ref[...] + _dot(merged.astype(BF16), wout_ref[...])
    x1_ref[...] = x1
    h2 = _rms_norm(x1, gffn_ref[...])
    h2_bf = h2.astype(BF16)
    h2_ref[...] = h2_bf

    logits_t = lax.dot_general(wrt_ref[...], h2_bf, (((1,), (1,)), ((), ())),
                               preferred_element_type=F32) + brt_ref[...]
    erow = lax.broadcasted_iota(I32, (N_EXPERTS, TM_POST), 0)
    work = logits_t
    top_v, onehots = [], []
    for _ in range(TOP_K):
        m = jnp.max(work, axis=0, keepdims=True)
        idx = jnp.min(jnp.where(work == m, erow, N_EXPERTS), axis=0, keepdims=True)
        hit = erow == idx
        top_v.append(m)
        onehots.append(hit)
        work = jnp.where(hit, -jnp.inf, work)
    exps = [jnp.exp(v - top_v[0]) for v in top_v]
    denom = exps[0] + exps[1] + exps[2] + exps[3]
    comb = [e / denom for e in exps]
    chosen = jnp.zeros((N_EXPERTS, TM_POST), F32)
    for hit in onehots:
        chosen = chosen + hit.astype(F32)

    e_r = lax.broadcasted_iota(I32, (N_EXPERTS, N_EXPERTS), 0)
    e_c = lax.broadcasted_iota(I32, (N_EXPERTS, N_EXPERTS), 1)
    earlier_expert = (e_c < e_r).astype(BF16)
    t_r = lax.broadcasted_iota(I32, (TT, TT), 0)
    t_c = lax.broadcasted_iota(I32, (TT, TT), 1)
    earlier_token = (t_r < t_c).astype(BF16)
    lane = lax.broadcasted_iota(I32, (N_EXPERTS, LANES), 1)
    counts = jnp.zeros((N_EXPERTS, LANES), F32)
    pos_parts = [[] for _ in range(TOP_K)]
    n_sub = TM_POST // TT
    for s in range(n_sub):
        cols = slice(s * TT, (s + 1) * TT)
        ch = chosen[:, cols]
        cnt = jnp.broadcast_to(jnp.sum(ch, axis=1, keepdims=True), (N_EXPERTS, LANES))
        before = carry[...]
        head = before - SEG_ALIGN * jnp.floor(before / SEG_ALIGN)
        flush = jnp.logical_and(i == pl.num_programs(0) - 1, s == n_sub - 1)
        present = (cnt > 0) | (flush & (head > 0))
        blocks = jnp.where(present, jnp.floor((head + cnt + (SEG_ALIGN - 1)) / SEG_ALIGN), 0.0)
        seg_loc = SEG_ALIGN * _dot(earlier_expert, blocks.astype(BF16))
        base = seg_loc + head
        rank_in_tile = _dot(ch.astype(BF16), earlier_token)
        full = jnp.concatenate([base] * (TT // LANES), axis=1) + rank_in_tile
        for k in range(TOP_K):
            pos_parts[k].append(
                jnp.sum(jnp.where(onehots[k][:, cols], full, 0.0), axis=0, keepdims=True))
        counts = jnp.where(lane == s, cnt, counts)
        carry[...] = before + cnt
    tc_ref[...] = counts

    pos = [jnp.concatenate(parts, axis=1) for parts in pos_parts]
    row8 = lax.broadcasted_iota(I32, (8, TM_POST), 0)
    pos_rows = jnp.full((8, TM_POST), -1.0, F32)
    both = jnp.zeros((8, TM_POST), F32)
    for k in range(TOP_K):
        pos_rows = jnp.where(row8 == k, pos[k], pos_rows)
        both = jnp.where(row8 == k, pos[k], both)
        both = jnp.where(row8 == TOP_K + k, comb[k], both)
    pos_t_ref[...] = pos_rows.astype(I32)
    padded = jnp.concatenate([both, jnp.zeros((LANES - 8, TM_POST), F32)], axis=0)
    pw_ref[...] = padded.T


def _post(attn, u, gates, x2, wo, bo, wmix, pscale, wup, wout, gffn, wr, br):
    n = x2.shape[0]
    row = lambda i: (i, 0)
    const = lambda i: (0, 0)
    halo_blocks = TM_POST // POOL_HALO
    prev = lambda i: (jnp.maximum(i * halo_blocks - 1, 0), 0)
    return pl.pallas_call(
        _post_body,
        grid=(n // TM_POST,),
        in_specs=[
            pl.BlockSpec((TM_POST, Q_WIDTH), row),
            pl.BlockSpec((TM_POST, POOL_WIDTH), row),
            pl.BlockSpec((POOL_HALO, POOL_WIDTH), prev),
            pl.BlockSpec((TM_POST, 2 * D_MODEL), row),
            pl.BlockSpec((TM_POST, D_MODEL), row),
            pl.BlockSpec((Q_WIDTH, D_MODEL), const),
            pl.BlockSpec((1, D_MODEL), const),
            pl.BlockSpec((len(POOL_WINDOWS), POOL_GROUP, POOL_GROUP), lambda i: (0, 0, 0)),
            pl.BlockSpec((1, POOL_WIDTH), const),
            pl.BlockSpec((POOL_WIDTH, D_MODEL), const),
            pl.BlockSpec((D_MODEL, D_MODEL), const),
            pl.BlockSpec((1, D_MODEL), const),
            pl.BlockSpec((N_EXPERTS, D_MODEL), const),
            pl.BlockSpec((N_EXPERTS, TM_POST), const),
        ],
        out_specs=[
            pl.BlockSpec((TM_POST, D_MODEL), row),
            pl.BlockSpec((TM_POST, D_MODEL), row),
            pl.BlockSpec((8, TM_POST), lambda i: (0, i)),
            pl.BlockSpec((TM_POST, LANES), row),
            pl.BlockSpec((N_EXPERTS, LANES), row),
        ],
        out_shape=[
            jax.ShapeDtypeStruct((n, D_MODEL), F32),
            jax.ShapeDtypeStruct((n, D_MODEL), BF16),
            jax.ShapeDtypeStruct((8, n), I32),
            jax.ShapeDtypeStruct((n, LANES), F32),
            jax.ShapeDtypeStruct((n // TM_POST * N_EXPERTS, LANES), F32),
        ],
        scratch_shapes=[pltpu.VMEM((TM_POST + POOL_PAD, POOL_WIDTH), F32),
                        pltpu.VMEM((TM_POST + POOL_PAD, POOL_WIDTH), F32),
                        pltpu.VMEM((N_EXPERTS, LANES), F32)],
        compiler_params=pltpu.CompilerParams(
            dimension_semantics=("arbitrary",), vmem_limit_bytes=VMEM_LIMIT),
        name="post",
    )(attn, u, u, gates, x2, wo, bo, wmix, pscale, wup, wout, gffn, wr, br)


def _start_segments(i, glob_ref, loc_ref, cnt_ref, make_copy):
    def body(e, c):
        idx = i * N_EXPERTS + e
        rows = pl.multiple_of(cnt_ref[idx], SEG_ALIGN)

        @pl.when(rows > 0)
        def _():
            loc = pl.multiple_of(loc_ref[idx], SEG_ALIGN)
            glob = pl.multiple_of(glob_ref[idx], SEG_ALIGN)
            make_copy(pl.ds(loc, rows), pl.ds(glob, rows)).start()
        return c
    lax.fori_loop(0, N_EXPERTS, body, 0, unroll=4)


def _wait_segments(i, total_ref, make_copy):
    rows = pl.multiple_of(total_ref[i], SEG_ALIGN)

    @pl.when(rows > 0)
    def _():
        make_copy(pl.ds(0, rows), pl.ds(0, rows)).wait()


def _pack_halves(v):
    c = v.shape[1] // 2
    bits = lax.bitcast_convert_type(v, U32)
    return (bits[:, :c] >> 16) | bits[:, c:]


def _unpack_halves(w):
    lo = lax.bitcast_convert_type(w << 16, F32)
    hi = lax.bitcast_convert_type(w & jnp.uint32(0xFFFF0000), F32)
    return jnp.concatenate([lo, hi], axis=1).astype(BF16)


def _dispatch_body(glob_ref, loc_ref, rd_ref, wr_ref, total_ref, rows_ref, h2_ref, pos_ref,
                   xs_hbm, local, tail, sem):
    i = pl.program_id(0)
    nt = pl.num_programs(0)
    slot = i % 2

    def copy_from(s):
        def make_copy(loc_rows, glob_rows):
            return pltpu.make_async_copy(local.at[s, loc_rows], xs_hbm.at[glob_rows], sem.at[s])
        return make_copy

    @pl.when(i == 0)
    def _():
        tail[...] = jnp.zeros_like(tail)

    @pl.when(i >= 2)
    def _():
        _wait_segments(i - 2, total_ref, copy_from(slot))

    def group(r0, nrows):
        j = lax.broadcasted_iota(I32, (nrows, TT), 0) + r0
        hit = j == pos_ref[0:1, :]
        for k in range(1, TOP_K):
            hit = hit | (j == pos_ref[k:k + 1, :])
        perm = jnp.where(hit, 1.0, 0.0).astype(BF16)
        local[slot, r0:r0 + nrows, :] = _pack_halves(_dot(perm, h2_ref[...]))

    for r0 in range(0, LROWS_MAIN, MXU_DEPTH):
        group(r0, MXU_DEPTH)

    @pl.when(rows_ref[i] + SEG_ALIGN > LROWS_MAIN)
    def _():
        group(LROWS_MAIN, LROWS - LROWS_MAIN)

    for e in range(N_EXPERTS):
        idx = i * N_EXPERTS + e
        rd = rd_ref[idx]
        wr = wr_ref[idx]
        loc = pl.multiple_of(loc_ref[idx], SEG_ALIGN)
        first = local[slot, pl.ds(loc, SEG_ALIGN), :]
        last = local[slot, pl.ds(pl.multiple_of(loc + wr, SEG_ALIGN), SEG_ALIGN), :]
        merged = jnp.where(rd > 0, first | tail[e], first)
        local[slot, pl.ds(loc, SEG_ALIGN), :] = merged
        last = jnp.where(wr == 0, merged, last)
        tail[e] = jnp.where(rd > 0, jnp.where(wr < rd, last, jnp.uint32(0)), tail[e])
    _start_segments(i, glob_ref, loc_ref, wr_ref, copy_from(slot))

    @pl.when(i == nt - 1)
    def _():
        _wait_segments(i - 1, total_ref, copy_from(1 - slot))
        _wait_segments(i, total_ref, copy_from(slot))


def _dispatch(seg_glob, seg_loc, seg_rd, seg_wr, tile_wr, tile_rd, h2, pos_t, n_rows):
    n = h2.shape[0]
    grid_spec = pltpu.PrefetchScalarGridSpec(
        num_scalar_prefetch=6,
        grid=(n // TT,),
        in_specs=[
            pl.BlockSpec((TT, D_MODEL), lambda i, *_: (i, 0)),
            pl.BlockSpec((8, TT), lambda i, *_: (0, i)),
        ],
        out_specs=pl.BlockSpec(memory_space=pl.ANY),
        scratch_shapes=[pltpu.VMEM((2, LROWS, PACKED), U32),
                        pltpu.VMEM((N_EXPERTS, SEG_ALIGN, PACKED), U32),
                        pltpu.SemaphoreType.DMA((2,))],
    )
    return pl.pallas_call(
        _dispatch_body,
        grid_spec=grid_spec,
        out_shape=jax.ShapeDtypeStruct((n_rows, PACKED), U32),
        compiler_params=pltpu.CompilerParams(
            dimension_semantics=("arbitrary",), vmem_limit_bytes=VMEM_LIMIT),
        name="dispatch",
    )(seg_glob, seg_loc, seg_rd, seg_wr, tile_wr, tile_rd, h2, pos_t)


def _moe_body(te_ref, nused_ref, rows_ref, next_ref, xs_ref, bias_ref,
              wg_hbm, wu_hbm, wd_hbm, y_ref, wf32, wbf, sem, slot_ref):
    i = pl.program_id(0)

    def fetch(expert, slot):
        return [pltpu.make_async_copy(w.at[expert], wf32.at[slot, m], sem.at[slot, m])
                for m, w in enumerate((wg_hbm, wu_hbm, wd_hbm))]

    @pl.when(i == 0)
    def _():
        slot_ref[0] = 0

    @pl.when(i < nused_ref[0])
    def _():
        expert = te_ref[i]

        @pl.when((i == 0) | (expert != te_ref[jnp.maximum(i - 1, 0)]))
        def _():
            slot = slot_ref[0]

            @pl.when(i == 0)
            def _():
                for cp in fetch(expert, slot):
                    cp.start()
            for cp in fetch(expert, slot):
                cp.wait()
            nxt = next_ref[i]

            @pl.when(nxt >= 0)
            def _():
                for cp in fetch(nxt, 1 - slot):
                    cp.start()
            for m in range(3):
                wbf[m] = wf32[slot, m].astype(BF16)
            slot_ref[0] = 1 - slot

        def mlp(r0):
            rows = slice(r0, r0 + MOE_ROWS)
            row = lax.broadcasted_iota(I32, (MOE_ROWS, 1), 0) + r0
            x = _unpack_halves(jnp.where(row < rows_ref[i], xs_ref[rows, :], jnp.uint32(0)))
            bias = bias_ref[expert]
            g = _dot(x, wbf[0]) + bias[0:1, :]
            u = _dot(x, wbf[1]) + bias[1:2, :]
            g = jnp.minimum(g, SWIGLU_LIMIT)
            u = jnp.clip(u, -SWIGLU_LIMIT, SWIGLU_LIMIT)
            a = (g * jax.nn.sigmoid(SWIGLU_ALPHA * g) * (u + 1.0)).astype(BF16)
            y = _dot(a, wbf[2]) + bias[2:3, :]
            y_ref[rows, :] = _pack_halves(y.astype(BF16).astype(F32))

        mlp(0)
        for r0 in range(MOE_ROWS, TM_MOE, MOE_ROWS):
            @pl.when(rows_ref[i] > r0)
            def _():
                mlp(r0)


def _moe(tile_expert, n_used, tile_rows, tile_next, xs, wg, bg, wu, bu, wd, bd):
    n_rows = xs.shape[0]
    n_tiles = n_rows // TM_MOE
    d_ff = wg.shape[2]
    assert d_ff == D_MODEL
    row = lambda i, te, nu, *_: (jnp.minimum(i, nu[0] - 1), 0)
    bias = jnp.zeros((N_EXPERTS, 8, D_MODEL), F32)
    bias = bias.at[:, 0].set(bg).at[:, 1].set(bu).at[:, 2].set(bd)
    grid_spec = pltpu.PrefetchScalarGridSpec(
        num_scalar_prefetch=4,
        grid=(n_tiles,),
        in_specs=[
            pl.BlockSpec((TM_MOE, PACKED), row),
            pl.BlockSpec((N_EXPERTS, 8, D_MODEL), lambda i, *_: (0, 0, 0)),
            pl.BlockSpec(memory_space=pl.ANY),
            pl.BlockSpec(memory_space=pl.ANY),
            pl.BlockSpec(memory_space=pl.ANY),
        ],
        out_specs=pl.BlockSpec((TM_MOE, PACKED), row),
        scratch_shapes=[pltpu.VMEM((2, 3, D_MODEL, D_MODEL), F32),
                        pltpu.VMEM((3, D_MODEL, D_MODEL), BF16),
                        pltpu.SemaphoreType.DMA((2, 3)),
                        pltpu.SMEM((1,), I32)],
    )
    return pl.pallas_call(
        _moe_body,
        grid_spec=grid_spec,
        out_shape=jax.ShapeDtypeStruct((n_rows, PACKED), U32),
        compiler_params=pltpu.CompilerParams(
            dimension_semantics=("arbitrary",), vmem_limit_bytes=VMEM_LIMIT),
        name="moe",
    )(tile_expert, n_used, tile_rows, tile_next, xs, bias, wg, wu, wd)


def _combine_body(glob_ref, loc_ref, cnt_ref, total_ref, x1_ref, pw_ref, gfin_ref,
                  ys_hbm, o_ref, local, sem):
    i = pl.program_id(0)
    nt = pl.num_programs(0)
    slot = i % 2

    def copy_into(s):
        def make_copy(loc_rows, glob_rows):
            return pltpu.make_async_copy(ys_hbm.at[glob_rows], local.at[s, loc_rows], sem.at[s])
        return make_copy

    @pl.when(i == 0)
    def _():
        local[...] = jnp.zeros_like(local)
        _start_segments(0, glob_ref, loc_ref, cnt_ref, copy_into(0))

    @pl.when(i + 1 < nt)
    def _():
        _start_segments(i + 1, glob_ref, loc_ref, cnt_ref, copy_into(1 - slot))
    _wait_segments(i, total_ref, copy_into(slot))

    def finish(nrows):
        j = lax.broadcasted_iota(I32, (TT, nrows), 1).astype(F32)
        wc = jnp.zeros((TT, nrows), F32)
        for k in range(TOP_K):
            wc = jnp.where(j == pw_ref[:, k:k + 1], pw_ref[:, TOP_K + k:TOP_K + k + 1], wc)
        acc = x1_ref[...] + _dot(wc.astype(BF16), _unpack_halves(local[slot, 0:nrows, :]))
        o_ref[...] = _rms_norm(acc, gfin_ref[...])

    @pl.when(total_ref[i] <= LROWS_MAIN)
    def _():
        finish(LROWS_MAIN)

    @pl.when(total_ref[i] > LROWS_MAIN)
    def _():
        finish(LROWS)


def _combine(seg_glob, seg_loc, seg_cnt, tile_rd, x1, pw, gfin, ys):
    n = x1.shape[0]
    grid_spec = pltpu.PrefetchScalarGridSpec(
        num_scalar_prefetch=4,
        grid=(n // TT,),
        in_specs=[
            pl.BlockSpec((TT, D_MODEL), lambda i, *_: (i, 0)),
            pl.BlockSpec((TT, LANES), lambda i, *_: (i, 0)),
            pl.BlockSpec((1, D_MODEL), lambda i, *_: (0, 0)),
            pl.BlockSpec(memory_space=pl.ANY),
        ],
        out_specs=pl.BlockSpec((TT, D_MODEL), lambda i, *_: (i, 0)),
        scratch_shapes=[pltpu.VMEM((2, LROWS, PACKED), U32),
                        pltpu.SemaphoreType.DMA((2,))],
    )
    return pl.pallas_call(
        _combine_body,
        grid_spec=grid_spec,
        out_shape=jax.ShapeDtypeStruct((n, D_MODEL), F32),
        compiler_params=pltpu.CompilerParams(
            dimension_semantics=("arbitrary",), vmem_limit_bytes=VMEM_LIMIT),
        name="combine",
    )(seg_glob, seg_loc, seg_cnt, tile_rd, x1, pw, gfin, ys)


def _layer(x2, pos_b, invf, norm_mix_g, w_in, b_in, attn_sinks, w_o_attn, b_o_attn,
           w_pool_mix, pool_scale, w_pool_up, w_out, norm_ffn_g, w_router, b_router,
           w_gate, b_gate, w_up, b_up, w_down, b_down, out_g):
    n = x2.shape[0]
    q, kb, vb, u, gates = _inproj(
        x2, pos_b, invf, norm_mix_g[None, :], w_in.astype(BF16), b_in[None, :])
    attn = _attn(attn_sinks, q, kb, vb)

    wrt = w_router.T.astype(BF16)
    brt = jnp.broadcast_to(b_router[:, None], (N_EXPERTS, TM_POST))
    x1, h2, pos_t, pw, tc = _post(
        attn, u, gates, x2, w_o_attn.astype(BF16), b_o_attn[None, :],
        w_pool_mix.astype(BF16), pool_scale[None, :], w_pool_up.astype(BF16),
        w_out.astype(BF16), norm_ffn_g[None, :], wrt, brt)

    nt = n // TT
    sub = TM_POST // TT
    seg_cnt = (tc.reshape(n // TM_POST, N_EXPERTS, LANES)[:, :, :sub]
               .transpose(0, 2, 1).reshape(nt, N_EXPERTS).astype(I32))
    before = jnp.cumsum(seg_cnt, axis=0) - seg_cnt
    sizes = jnp.sum(seg_cnt, axis=0)
    padded = (sizes + TM_MOE - 1) // TM_MOE * TM_MOE
    pends = jnp.cumsum(padded)
    pstarts = pends - padded
    head = before % SEG_ALIGN
    last_tile = (jnp.arange(nt, dtype=I32) == nt - 1)[:, None]
    blocks_up = (head + seg_cnt + SEG_ALIGN - 1) // SEG_ALIGN
    blocks_dn = (head + seg_cnt) // SEG_ALIGN
    seg_rd = jnp.where((seg_cnt > 0) | (last_tile & (head > 0)), blocks_up, 0) * SEG_ALIGN
    seg_wr = jnp.where(last_tile, seg_rd, jnp.where(seg_cnt > 0, blocks_dn * SEG_ALIGN, 0))
    seg_loc = jnp.cumsum(seg_rd, axis=1) - seg_rd
    seg_glob = pstarts[None, :] + before - head
    tile_rd = jnp.sum(seg_rd, axis=1)
    tile_wr = jnp.sum(seg_wr, axis=1)
    seg_glob, seg_loc, seg_rd, seg_wr = (a.reshape(-1) for a in (seg_glob, seg_loc, seg_rd, seg_wr))
    m = n * TOP_K
    n_tiles = (m + N_EXPERTS * (TM_MOE - 1) + TM_MOE - 1) // TM_MOE
    tile_start = jnp.arange(n_tiles, dtype=I32) * TM_MOE
    tile_expert = jnp.minimum(
        jnp.sum(tile_start[:, None] >= pends[None, :], axis=-1), N_EXPERTS - 1).astype(I32)
    own = tile_expert[:, None] == jnp.arange(N_EXPERTS, dtype=I32)[None, :]
    left = jnp.sum(jnp.where(own, (pstarts + sizes)[None, :] - tile_start[:, None], 0), axis=-1)
    tile_rows = jnp.clip(left, 0, TM_MOE)
    ids = jnp.arange(N_EXPERTS, dtype=I32)
    later = jnp.where((sizes > 0)[None, :] & (ids[None, :] > ids[:, None]), ids[None, :], N_EXPERTS)
    next_expert = jnp.min(later, axis=1)
    next_expert = jnp.where(next_expert == N_EXPERTS, -1, next_expert)
    tile_next = jnp.sum(jnp.where(own, next_expert[None, :], 0), axis=-1).astype(I32)
    n_used = (pends[-1] // TM_MOE).astype(I32)[None]

    xs = _dispatch(seg_glob, seg_loc, seg_rd, seg_wr, tile_wr, tile_rd, h2, pos_t, n_tiles * TM_MOE)
    ys = _moe(tile_expert, n_used, tile_rows.astype(I32), tile_next, xs,
              w_gate, b_gate, w_up, b_up, w_down, b_down)
    return _combine(seg_glob, seg_loc, seg_rd, tile_rd, x1, pw, out_g[None, :], ys)


def kernel(x, positions, norm_mix_g, w_in, b_in, attn_sinks, w_o_attn, b_o_attn, w_pool_mix,
           pool_scale, w_pool_up, w_out, norm_ffn_g, w_router, b_router, w_gate, b_gate,
           w_up, b_up, w_down, b_down, norm_final_g):
    b, s, d = x.shape
    depth = w_in.shape[0]
    assert (s, d, depth) == (SEQ, D_MODEL, 1)
    n = b * s
    x2 = x.reshape(n, d)
    pos_b = jnp.repeat(positions.reshape(n // 4, 4), LANES // 4, axis=1)
    inv_freq = ROPE_THETA ** (-jnp.arange(0, HEAD_DIM, 2, dtype=F32) / HEAD_DIM)
    invf = jnp.tile(inv_freq, LANES // (HEAD_DIM // 2))[None, :]
    out = _layer(x2, pos_b, invf, norm_mix_g[0], w_in[0], b_in[0], attn_sinks[0], w_o_attn[0],
                 b_o_attn[0], w_pool_mix[0], pool_scale[0], w_pool_up[0], w_out[0],
                 norm_ffn_g[0], w_router[0], b_router[0], w_gate[0], b_gate[0], w_up[0],
                 b_up[0], w_down[0], b_down[0], norm_final_g)
    return out.reshape(b, s, d)
```

```python
import numpy as np
import jax
import jax.numpy as jnp
from jax import lax
from jax.experimental import pallas as pl
from jax.experimental.pallas import tpu as pltpu

F32 = jnp.float32
BF16 = jnp.bfloat16
I32 = jnp.int32
U32 = jnp.uint32

D_MODEL = 1024
SEQ = 4096
HEAD_DIM = 64
N_Q_HEADS = 16
WINDOW = 128
ROPE_THETA = 10000.0
Q_WIDTH = N_Q_HEADS * HEAD_DIM
KV_WIDTH = 128
POOL_WINDOWS = (2, 4, 8, 16)
POOL_WIDTH = 512
POOL_GROUP = 128
POOL_HALO = 16
POOL_PAD = 32
N_EXPERTS = 32
TOP_K = 4
SWIGLU_LIMIT = 7.0
SWIGLU_ALPHA = 1.702
RMS_EPS = 1e-5
NEG_BIG = -1e30
LOG2_E = 1.4426950408889634

LANES = 128
TM_IN = 1024
TQ = 512
TM_POST = 512
TM_MOE = 1024
MOE_ROWS = 512
TT = 256
SEG_ALIGN = 8
LROWS = TT * TOP_K + 2 * N_EXPERTS * SEG_ALIGN
MXU_DEPTH = 256
LROWS_MAIN = LROWS - MXU_DEPTH
PACKED = D_MODEL // 2
VMEM_LIMIT = 56 * 1024 * 1024


def _rms_norm(x, g):
    ms = jnp.mean(x * x, axis=-1, keepdims=True)
    return (x * lax.rsqrt(ms + RMS_EPS)) * g


def _dot(a, b):
    return jnp.dot(a, b, preferred_element_type=F32)


def _inproj_body(x_ref, pos_ref, invf_ref, g_ref, w_ref, b_ref,
                 q_ref, kb_ref, vb_ref, u_ref, gate_ref, cos_scr, sin_scr):
    h = _rms_norm(x_ref[...], g_ref[...]).astype(BF16)
    qk_w = Q_WIDTH + KV_WIDTH
    v0 = qk_w
    u0 = v0 + KV_WIDTH
    g0 = u0 + POOL_WIDTH

    def gate_chunk(c):
        sl = slice(g0 + c * D_MODEL, g0 + (c + 1) * D_MODEL)
        zg = _dot(h, w_ref[:, sl]) + b_ref[:, sl]
        gate_ref[:, c * D_MODEL:(c + 1) * D_MODEL] = jax.nn.sigmoid(zg).astype(BF16)

    gate_chunk(0)

    theta = pos_ref[...].astype(F32) * invf_ref[...]
    lane4 = lax.broadcasted_iota(I32, theta.shape, 1)
    for packed, scr in ((jnp.cos(theta), cos_scr), (jnp.sin(theta), sin_scr)):
        for jt in range(4):
            seg = packed if jt == 0 else pltpu.roll(packed, LANES - 32 * jt, 1)
            seg = jnp.where(lane4 < 32, seg, pltpu.roll(seg, 32, 1))
            seg = jnp.where(lane4 < 64, seg, pltpu.roll(seg, 64, 1))
            scr[pl.ds(jt, TM_IN // 4, stride=4), :] = seg
    cos = cos_scr[...]
    sin = sin_scr[...]
    lane = lax.broadcasted_iota(I32, cos.shape, 1)
    first_half = (lane & 32) == 0
    sin_signed = jnp.where(first_half, -sin, sin)
    low_head = lane < HEAD_DIM

    def rope(t):
        swapped = jnp.where(first_half, pltpu.roll(t, 96, 1), pltpu.roll(t, 32, 1))
        return t * cos + swapped * sin_signed

    def band_layout(t, out_ref):
        tr = pltpu.roll(t, 64, 1)
        zero = jnp.zeros_like(t)
        chunks = (jnp.where(low_head, t, zero), jnp.where(low_head, zero, tr),
                  jnp.where(low_head, tr, zero), jnp.where(low_head, zero, t))
        for c, val in enumerate(chunks):
            out_ref[:, c * LANES:(c + 1) * LANES] = val.astype(BF16)

    zqk = _dot(h, w_ref[:, :qk_w]) + b_ref[:, :qk_w]
    gate_chunk(1)
    scale = HEAD_DIM ** -0.5 * LOG2_E
    for j in range(Q_WIDTH // LANES):
        sl = slice(j * LANES, (j + 1) * LANES)
        q_ref[:, sl] = (rope(zqk[:, sl]) * scale).astype(BF16)
    band_layout(rope(zqk[:, Q_WIDTH:qk_w]), kb_ref)

    zv = _dot(h, w_ref[:, v0:v0 + KV_WIDTH]) + b_ref[:, v0:v0 + KV_WIDTH]
    band_layout(zv, vb_ref)
    u_ref[...] = _dot(h, w_ref[:, u0:u0 + POOL_WIDTH]) + b_ref[:, u0:u0 + POOL_WIDTH]


def _inproj(x2, pos_b, invf, g, w, b):
    n = x2.shape[0]
    in_width = w.shape[1]
    row = lambda i: (i, 0)
    const = lambda i: (0, 0)
    return pl.pallas_call(
        _inproj_body,
        grid=(n // TM_IN,),
        in_specs=[
            pl.BlockSpec((TM_IN, D_MODEL), row),
            pl.BlockSpec((TM_IN // 4, LANES), row),
            pl.BlockSpec((1, LANES), const),
            pl.BlockSpec((1, D_MODEL), const),
            pl.BlockSpec((D_MODEL, in_width), const),
            pl.BlockSpec((1, in_width), const),
        ],
        out_specs=[
            pl.BlockSpec((TM_IN, Q_WIDTH), row),
            pl.BlockSpec((TM_IN, 4 * LANES), row),
            pl.BlockSpec((TM_IN, 4 * LANES), row),
            pl.BlockSpec((TM_IN, POOL_WIDTH), row),
            pl.BlockSpec((TM_IN, 2 * D_MODEL), row),
        ],
        out_shape=[
            jax.ShapeDtypeStruct((n, Q_WIDTH), BF16),
            jax.ShapeDtypeStruct((n, 4 * LANES), BF16),
            jax.ShapeDtypeStruct((n, 4 * LANES), BF16),
            jax.ShapeDtypeStruct((n, POOL_WIDTH), F32),
            jax.ShapeDtypeStruct((n, 2 * D_MODEL), BF16),
        ],
        scratch_shapes=[pltpu.VMEM((TM_IN, LANES), F32), pltpu.VMEM((TM_IN, LANES), F32)],
        compiler_params=pltpu.CompilerParams(
            dimension_semantics=("arbitrary",), vmem_limit_bytes=VMEM_LIMIT),
        name="inproj",
    )(x2, pos_b, invf, g, w, b)


def _attn_body(sinks_ref, q_ref, kbc_ref, kbp_ref, vbc_ref, vbp_ref, bias_ref,
               o_ref, p_scr):
    t = pl.program_id(0)
    seq_first = (t % (SEQ // TQ)) == 0
    lane = lax.broadcasted_iota(I32, (WINDOW, LANES), 1)
    low_head = lane < HEAD_DIM
    key_row = lax.broadcasted_iota(I32, (4 * WINDOW, LANES), 0)
    key_lane = lax.broadcasted_iota(I32, (4 * WINDOW, LANES), 1)
    head_ones = jnp.where((key_row < 2 * WINDOW) == (key_lane < HEAD_DIM), 1.0, 0.0).astype(BF16)
    for n in range(TQ // WINDOW):
        rows = slice(n * WINDOW, (n + 1) * WINDOW)
        if n == 0:
            k_prev, v_prev = kbp_ref[...], vbp_ref[...]
            bias = bias_ref[seq_first.astype(I32)]
        else:
            prev_rows = slice((n - 1) * WINDOW, n * WINDOW)
            k_prev, v_prev = kbc_ref[prev_rows, :], vbc_ref[prev_rows, :]
            bias = bias_ref[0]
        k_cur, v_cur = kbc_ref[rows, :], vbc_ref[rows, :]
        for g in range(2):
            def stack(prev, cur):
                lo = slice((2 * g) * LANES, (2 * g + 1) * LANES)
                hi = slice((2 * g + 1) * LANES, (2 * g + 2) * LANES)
                return jnp.concatenate([prev[:, lo], cur[:, lo], prev[:, hi], cur[:, hi]], axis=0)
            kmat = stack(k_prev, k_cur)
            vmat = stack(v_prev, v_cur)
            qs = jnp.concatenate(
                [q_ref[rows, (4 * g + p) * LANES:(4 * g + p + 1) * LANES] for p in range(4)],
                axis=0)
            s = lax.dot_general(qs, kmat, (((1,), (1,)), ((), ())),
                                preferred_element_type=F32) + bias
            sink_terms = []
            for p in range(4):
                pr = slice(p * WINDOW, (p + 1) * WINDOW)
                ms = []
                for j in range(2):
                    cols = slice(j * 2 * WINDOW, (j + 1) * 2 * WINDOW)
                    sp = s[pr, cols]
                    sink = sinks_ref[8 * g + 2 * p + j] * LOG2_E
                    m = jnp.maximum(jnp.max(sp, axis=-1, keepdims=True), sink)
                    p_scr[pr, cols] = jnp.exp2(sp - m).astype(BF16)
                    ms.append(m)
                sink_pair = jnp.where(low_head, sinks_ref[8 * g + 2 * p] * LOG2_E,
                                      sinks_ref[8 * g + 2 * p + 1] * LOG2_E)
                sink_terms.append(jnp.exp2(sink_pair - jnp.where(low_head, ms[0], ms[1])))
            o = _dot(p_scr[...], jnp.concatenate([vmat, head_ones], axis=1))
            for p in range(4):
                pr = slice(p * WINDOW, (p + 1) * WINDOW)
                inv = 1.0 / (o[pr, LANES:] + sink_terms[p])
                o_ref[rows, (4 * g + p) * LANES:(4 * g + p + 1) * LANES] = (o[pr, :LANES] * inv).astype(BF16)


def _attn_bias():
    r = np.arange(4 * WINDOW)[:, None] % WINDOW
    c = np.arange(4 * WINDOW)[None, :] % (2 * WINDOW)
    band = (c > r) & (c <= r + WINDOW)
    first = band & (c >= WINDOW)
    return np.stack([np.where(band, 0.0, NEG_BIG), np.where(first, 0.0, NEG_BIG)]).astype(np.float32)


def _attn(sinks, q, kb, vb):
    n = q.shape[0]
    blocks_per_tile = TQ // WINDOW
    cur = lambda t: (t, 0)
    prev = lambda t: (jnp.maximum(t * blocks_per_tile - 1, 0), 0)
    bias = jnp.asarray(_attn_bias())
    return pl.pallas_call(
        _attn_body,
        grid=(n // TQ,),
        in_specs=[
            pl.BlockSpec(memory_space=pltpu.SMEM),
            pl.BlockSpec((TQ, Q_WIDTH), cur),
            pl.BlockSpec((TQ, 4 * LANES), cur),
            pl.BlockSpec((WINDOW, 4 * LANES), prev),
            pl.BlockSpec((TQ, 4 * LANES), cur),
            pl.BlockSpec((WINDOW, 4 * LANES), prev),
            pl.BlockSpec((2, 4 * WINDOW, 4 * WINDOW), lambda t: (0, 0, 0)),
        ],
        out_specs=pl.BlockSpec((TQ, Q_WIDTH), cur),
        out_shape=jax.ShapeDtypeStruct((n, Q_WIDTH), BF16),
        scratch_shapes=[pltpu.VMEM((4 * WINDOW, 4 * WINDOW), BF16)],
        compiler_params=pltpu.CompilerParams(
            dimension_semantics=("arbitrary",), vmem_limit_bytes=VMEM_LIMIT),
        name="attn",
    )(sinks, q, kb, kb, vb, vb, bias)


def _post_body(attn_ref, u_ref, uprev_ref, gate_ref, x_ref,
               wo_ref, bo_ref, wmix_ref, pscale_ref, wup_ref, wout_ref,
               gffn_ref, wrt_ref, brt_ref,
               x1_ref, h2_ref, pos_t_ref, pw_ref, tc_ref,
               win_a, win_b, carry):
    i = pl.program_id(0)
    tiles_per_seq = SEQ // TM_POST
    seq_first = (i % tiles_per_seq) == 0

    @pl.when(i == 0)
    def _():
        carry[...] = jnp.zeros_like(carry)

    rows = TM_POST + POOL_PAD
    win_a[0:POOL_PAD - POOL_HALO, :] = jnp.zeros((POOL_PAD - POOL_HALO, POOL_WIDTH), F32)
    win_a[POOL_PAD - POOL_HALO:POOL_PAD, :] = jnp.where(seq_first, 0.0, uprev_ref[...])
    win_a[POOL_PAD:, :] = u_ref[...]
    src, dst = win_a, win_b
    sums = {}
    for level, w in enumerate((1, 2, 4, 8)):
        lo = 8 * (level + 1)
        c0 = level * POOL_GROUP
        dst[lo:, c0:] = src[lo:, c0:] + src[lo - w:rows - w, c0:]
        sums[2 * w] = dst
        src, dst = dst, (win_a if dst is win_b else win_b)
    row = lax.broadcasted_iota(I32, (TM_POST, 1), 0)
    tpos = (i % tiles_per_seq) * TM_POST + row
    mixed = []
    for gi, w in enumerate(POOL_WINDOWS):
        cols = slice(gi * POOL_GROUP, (gi + 1) * POOL_GROUP)
        cnt = jnp.minimum(tpos + 1, w).astype(F32)
        pooled = sums[w][POOL_PAD:, cols] / cnt - u_ref[:, cols]
        mixed.append(_dot(pooled.astype(BF16), wmix_ref[gi]) * pscale_ref[:, cols])
    mixed = jnp.concatenate(mixed, axis=-1).astype(BF16)
    attn_bf = attn_ref[...]
    merged = []
    for c0 in range(0, D_MODEL, MXU_DEPTH):
        cols = slice(c0, c0 + MXU_DEPTH)
        y_attn = _dot(attn_bf, wo_ref[:, cols]) + bo_ref[:, cols]
        y_pool = _dot(mixed, wup_ref[:, cols])
        merged.append((gate_ref[:, cols].astype(F32) * y_attn
                       + gate_ref[:, D_MODEL + c0:D_MODEL + c0 + MXU_DEPTH].astype(F32) * y_pool
                       ).astype(BF16))
    x1 = x_ref[...] + _dot(jnp.concatenate(merged, axis=1), wout_ref[...])
    x1_ref[...] = x1
    h2 = _rms_norm(x1, gffn_ref[...])
    h2_bf = h2.astype(BF16)
    h2_ref[...] = h2_bf

    logits_t = lax.dot_general(wrt_ref[...], h2_bf, (((1,), (1,)), ((), ())),
                               preferred_element_type=F32) + brt_ref[...]
    erow = lax.broadcasted_iota(I32, (N_EXPERTS, TM_POST), 0)
    work = logits_t
    top_v, onehots = [], []
    for _ in range(TOP_K):
        m = jnp.max(work, axis=0, keepdims=True)
        idx = jnp.min(jnp.where(work == m, erow, N_EXPERTS), axis=0, keepdims=True)
        hit = erow == idx
        top_v.append(m)
        onehots.append(hit)
        work = jnp.where(hit, -jnp.inf, work)
    exps = [jnp.exp(v - top_v[0]) for v in top_v]
    denom = exps[0] + exps[1] + exps[2] + exps[3]
    comb = [e / denom for e in exps]
    chosen = jnp.zeros((N_EXPERTS, TM_POST), F32)
    for hit in onehots:
        chosen = chosen + hit.astype(F32)

    e_r = lax.broadcasted_iota(I32, (N_EXPERTS, N_EXPERTS), 0)
    e_c = lax.broadcasted_iota(I32, (N_EXPERTS, N_EXPERTS), 1)
    earlier_expert = (e_c < e_r).astype(BF16)
    t_r = lax.broadcasted_iota(I32, (TT, TT), 0)
    t_c = lax.broadcasted_iota(I32, (TT, TT), 1)
    earlier_token = (t_r < t_c).astype(BF16)
    lane = lax.broadcasted_iota(I32, (N_EXPERTS, LANES), 1)
    counts = jnp.zeros((N_EXPERTS, LANES), F32)
    pos_parts = [[] for _ in range(TOP_K)]
    n_sub = TM_POST // TT
    for s in range(n_sub):
        cols = slice(s * TT, (s + 1) * TT)
        ch = chosen[:, cols]
        cnt = jnp.broadcast_to(jnp.sum(ch, axis=1, keepdims=True), (N_EXPERTS, LANES))
        before = carry[...]
        head = before - SEG_ALIGN * jnp.floor(before / SEG_ALIGN)
        flush = jnp.logical_and(i == pl.num_programs(0) - 1, s == n_sub - 1)
        present = (cnt > 0) | (flush & (head > 0))
        blocks = jnp.where(present, jnp.floor((head + cnt + (SEG_ALIGN - 1)) / SEG_ALIGN), 0.0)
        seg_loc = SEG_ALIGN * _dot(earlier_expert, blocks.astype(BF16))
        base = seg_loc + head
        rank_in_tile = _dot(ch.astype(BF16), earlier_token)
        full = jnp.concatenate([base] * (TT // LANES), axis=1) + rank_in_tile
        for k in range(TOP_K):
            pos_parts[k].append(
                jnp.sum(jnp.where(onehots[k][:, cols], full, 0.0), axis=0, keepdims=True))
        counts = jnp.where(lane == s, cnt, counts)
        carry[...] = before + cnt
    tc_ref[...] = counts

    pos = [jnp.concatenate(parts, axis=1) for parts in pos_parts]
    row8 = lax.broadcasted_iota(I32, (8, TM_POST), 0)
    pos_rows = jnp.full((8, TM_POST), -1.0, F32)
    both = jnp.zeros((8, TM_POST), F32)
    for k in range(TOP_K):
        pos_rows = jnp.where(row8 == k, pos[k], pos_rows)
        both = jnp.where(row8 == k, pos[k], both)
        both = jnp.where(row8 == TOP_K + k, comb[k], both)
    pos_t_ref[...] = pos_rows.astype(I32)
    padded = jnp.concatenate([both, jnp.zeros((LANES - 8, TM_POST), F32)], axis=0)
    pw_ref[...] = padded.T


def _post(attn, u, gates, x2, wo, bo, wmix, pscale, wup, wout, gffn, wr, br):
    n = x2.shape[0]
    row = lambda i: (i, 0)
    const = lambda i: (0, 0)
    halo_blocks = TM_POST // POOL_HALO
    prev = lambda i: (jnp.maximum(i * halo_blocks - 1, 0), 0)
    return pl.pallas_call(
        _post_body,
        grid=(n // TM_POST,),
        in_specs=[
            pl.BlockSpec((TM_POST, Q_WIDTH), row),
            pl.BlockSpec((TM_POST, POOL_WIDTH), row),
            pl.BlockSpec((POOL_HALO, POOL_WIDTH), prev),
            pl.BlockSpec((TM_POST, 2 * D_MODEL), row),
            pl.BlockSpec((TM_POST, D_MODEL), row),
            pl.BlockSpec((Q_WIDTH, D_MODEL), const),
            pl.BlockSpec((1, D_MODEL), const),
            pl.BlockSpec((len(POOL_WINDOWS), POOL_GROUP, POOL_GROUP), lambda i: (0, 0, 0)),
            pl.BlockSpec((1, POOL_WIDTH), const),
            pl.BlockSpec((POOL_WIDTH, D_MODEL), const),
            pl.BlockSpec((D_MODEL, D_MODEL), const),
            pl.BlockSpec((1, D_MODEL), const),
            pl.BlockSpec((N_EXPERTS, D_MODEL), const),
            pl.BlockSpec((N_EXPERTS, TM_POST), const),
        ],
        out_specs=[
            pl.BlockSpec((TM_POST, D_MODEL), row),
            pl.BlockSpec((TM_POST, D_MODEL), row),
            pl.BlockSpec((8, TM_POST), lambda i: (0, i)),
            pl.BlockSpec((TM_POST, LANES), row),
            pl.BlockSpec((N_EXPERTS, LANES), row),
        ],
        out_shape=[
            jax.ShapeDtypeStruct((n, D_MODEL), F32),
            jax.ShapeDtypeStruct((n, D_MODEL), BF16),
            jax.ShapeDtypeStruct((8, n), I32),
            jax.ShapeDtypeStruct((n, LANES), F32),
            jax.ShapeDtypeStruct((n // TM_POST * N_EXPERTS, LANES), F32),
        ],
        scratch_shapes=[pltpu.VMEM((TM_POST + POOL_PAD, POOL_WIDTH), F32),
                        pltpu.VMEM((TM_POST + POOL_PAD, POOL_WIDTH), F32),
                        pltpu.VMEM((N_EXPERTS, LANES), F32)],
        compiler_params=pltpu.CompilerParams(
            dimension_semantics=("arbitrary",), vmem_limit_bytes=VMEM_LIMIT),
        name="post",
    )(attn, u, u, gates, x2, wo, bo, wmix, pscale, wup, wout, gffn, wr, br)


def _start_segments(i, glob_ref, loc_ref, cnt_ref, make_copy):
    def body(e, c):
        idx = i * N_EXPERTS + e
        rows = pl.multiple_of(cnt_ref[idx], SEG_ALIGN)

        @pl.when(rows > 0)
        def _():
            loc = pl.multiple_of(loc_ref[idx], SEG_ALIGN)
            glob = pl.multiple_of(glob_ref[idx], SEG_ALIGN)
            make_copy(pl.ds(loc, rows), pl.ds(glob, rows)).start()
        return c
    lax.fori_loop(0, N_EXPERTS, body, 0, unroll=4)


def _wait_segments(i, total_ref, make_copy):
    rows = pl.multiple_of(total_ref[i], SEG_ALIGN)

    @pl.when(rows > 0)
    def _():
        make_copy(pl.ds(0, rows), pl.ds(0, rows)).wait()


def _pack_halves(v):
    c = v.shape[1] // 2
    bits = lax.bitcast_convert_type(v, U32)
    return (bits[:, :c] >> 16) | bits[:, c:]


def _unpack_halves(w):
    lo = lax.bitcast_convert_type(w << 16, F32)
    hi = lax.bitcast_convert_type(w & jnp.uint32(0xFFFF0000), F32)
    return jnp.concatenate([lo, hi], axis=1).astype(BF16)


def _dispatch_body(glob_ref, loc_ref, rd_ref, wr_ref, total_ref, rows_ref, h2_ref, pos_ref,
                   xs_hbm, local, tail, sem):
    i = pl.program_id(0)
    nt = pl.num_programs(0)
    slot = i % 2

    def copy_from(s):
        def make_copy(loc_rows, glob_rows):
            return pltpu.make_async_copy(local.at[s, loc_rows], xs_hbm.at[glob_rows], sem.at[s])
        return make_copy

    @pl.when(i == 0)
    def _():
        tail[...] = jnp.zeros_like(tail)

    @pl.when(i >= 2)
    def _():
        _wait_segments(i - 2, total_ref, copy_from(slot))

    def group(r0, nrows):
        j = lax.broadcasted_iota(I32, (nrows, TT), 0) + r0
        hit = j == pos_ref[0:1, :]
        for k in range(1, TOP_K):
            hit = hit | (j == pos_ref[k:k + 1, :])
        perm = jnp.where(hit, 1.0, 0.0).astype(BF16)
        local[slot, r0:r0 + nrows, :] = _pack_halves(_dot(perm, h2_ref[...]))

    group(0, LROWS_MAIN)

    @pl.when(rows_ref[i] + SEG_ALIGN > LROWS_MAIN)
    def _():
        group(LROWS_MAIN, LROWS - LROWS_MAIN)

    for e in range(N_EXPERTS):
        idx = i * N_EXPERTS + e
        rd = rd_ref[idx]
        wr = wr_ref[idx]
        loc = pl.multiple_of(loc_ref[idx], SEG_ALIGN)
        first = local[slot, pl.ds(loc, SEG_ALIGN), :]
        last = local[slot, pl.ds(pl.multiple_of(loc + wr, SEG_ALIGN), SEG_ALIGN), :]
        merged = jnp.where(rd > 0, first | tail[e], first)
        local[slot, pl.ds(loc, SEG_ALIGN), :] = merged
        last = jnp.where(wr == 0, merged, last)
        tail[e] = jnp.where(rd > 0, jnp.where(wr < rd, last, jnp.uint32(0)), tail[e])
    _start_segments(i, glob_ref, loc_ref, wr_ref, copy_from(slot))

    @pl.when(i == nt - 1)
    def _():
        _wait_segments(i - 1, total_ref, copy_from(1 - slot))
        _wait_segments(i, total_ref, copy_from(slot))


def _dispatch(seg_glob, seg_loc, seg_rd, seg_wr, tile_wr, tile_rd, h2, pos_t, n_rows):
    n = h2.shape[0]
    grid_spec = pltpu.PrefetchScalarGridSpec(
        num_scalar_prefetch=6,
        grid=(n // TT,),
        in_specs=[
            pl.BlockSpec((TT, D_MODEL), lambda i, *_: (i, 0)),
            pl.BlockSpec((8, TT), lambda i, *_: (0, i)),
        ],
        out_specs=pl.BlockSpec(memory_space=pl.ANY),
        scratch_shapes=[pltpu.VMEM((2, LROWS, PACKED), U32),
                        pltpu.VMEM((N_EXPERTS, SEG_ALIGN, PACKED), U32),
                        pltpu.SemaphoreType.DMA((2,))],
    )
    return pl.pallas_call(
        _dispatch_body,
        grid_spec=grid_spec,
        out_shape=jax.ShapeDtypeStruct((n_rows, PACKED), U32),
        compiler_params=pltpu.CompilerParams(
            dimension_semantics=("arbitrary",), vmem_limit_bytes=VMEM_LIMIT),
        name="dispatch",
    )(seg_glob, seg_loc, seg_rd, seg_wr, tile_wr, tile_rd, h2, pos_t)


def _moe_body(te_ref, nused_ref, rows_ref, next_ref, xs_ref, bias_ref,
              wg_hbm, wu_hbm, wd_hbm, y_ref, wf32, wbf, sem, slot_ref):
    i = pl.program_id(0)

    def fetch(expert, slot):
        return [pltpu.make_async_copy(w.at[expert], wf32.at[slot, m], sem.at[slot, m])
                for m, w in enumerate((wg_hbm, wu_hbm, wd_hbm))]

    @pl.when(i == 0)
    def _():
        slot_ref[0] = 0

    @pl.when(i < nused_ref[0])
    def _():
        expert = te_ref[i]

        @pl.when((i == 0) | (expert != te_ref[jnp.maximum(i - 1, 0)]))
        def _():
            slot = slot_ref[0]

            @pl.when(i == 0)
            def _():
                for cp in fetch(expert, slot):
                    cp.start()
            for cp in fetch(expert, slot):
                cp.wait()
            nxt = next_ref[i]

            @pl.when(nxt >= 0)
            def _():
                for cp in fetch(nxt, 1 - slot):
                    cp.start()
            for m in range(3):
                wbf[m] = wf32[slot, m].astype(BF16)
            slot_ref[0] = 1 - slot

        def mlp(r0):
            rows = slice(r0, r0 + MOE_ROWS)
            row = lax.broadcasted_iota(I32, (MOE_ROWS, 1), 0) + r0
            x = _unpack_halves(jnp.where(row < rows_ref[i], xs_ref[rows, :], jnp.uint32(0)))
            bias = bias_ref[expert]
            g = _dot(x, wbf[0]) + bias[0:1, :]
            u = _dot(x, wbf[1]) + bias[1:2, :]
            g = jnp.minimum(g, SWIGLU_LIMIT)
            u = jnp.clip(u, -SWIGLU_LIMIT, SWIGLU_LIMIT)
            a = (g * jax.nn.sigmoid(SWIGLU_ALPHA * g) * (u + 1.0)).astype(BF16)
            y = _dot(a, wbf[2]) + bias[2:3, :]
            y_ref[rows, :] = _pack_halves(y.astype(BF16).astype(F32))

        mlp(0)
        for r0 in range(MOE_ROWS, TM_MOE, MOE_ROWS):
            @pl.when(rows_ref[i] > r0)
            def _():
                mlp(r0)


def _moe(tile_expert, n_used, tile_rows, tile_next, xs, wg, bg, wu, bu, wd, bd):
    n_rows = xs.shape[0]
    n_tiles = n_rows // TM_MOE
    d_ff = wg.shape[2]
    assert d_ff == D_MODEL
    row = lambda i, te, nu, *_: (jnp.minimum(i, nu[0] - 1), 0)
    bias = jnp.zeros((N_EXPERTS, 8, D_MODEL), F32)
    bias = bias.at[:, 0].set(bg).at[:, 1].set(bu).at[:, 2].set(bd)
    grid_spec = pltpu.PrefetchScalarGridSpec(
        num_scalar_prefetch=4,
        grid=(n_tiles,),
        in_specs=[
            pl.BlockSpec((TM_MOE, PACKED), row),
            pl.BlockSpec((N_EXPERTS, 8, D_MODEL), lambda i, *_: (0, 0, 0)),
            pl.BlockSpec(memory_space=pl.ANY),
            pl.BlockSpec(memory_space=pl.ANY),
            pl.BlockSpec(memory_space=pl.ANY),
        ],
        out_specs=pl.BlockSpec((TM_MOE, PACKED), row),
        scratch_shapes=[pltpu.VMEM((2, 3, D_MODEL, D_MODEL), F32),
                        pltpu.VMEM((3, D_MODEL, D_MODEL), BF16),
                        pltpu.SemaphoreType.DMA((2, 3)),
                        pltpu.SMEM((1,), I32)],
    )
    return pl.pallas_call(
        _moe_body,
        grid_spec=grid_spec,
        out_shape=jax.ShapeDtypeStruct((n_rows, PACKED), U32),
        compiler_params=pltpu.CompilerParams(
            dimension_semantics=("arbitrary",), vmem_limit_bytes=VMEM_LIMIT),
        name="moe",
    )(tile_expert, n_used, tile_rows, tile_next, xs, bias, wg, wu, wd)


def _combine_body(glob_ref, loc_ref, cnt_ref, total_ref, x1_ref, pw_ref, gfin_ref,
                  ys_hbm, o_ref, local, sem):
    i = pl.program_id(0)
    nt = pl.num_programs(0)
    slot = i % 2

    def copy_into(s):
        def make_copy(loc_rows, glob_rows):
            return pltpu.make_async_copy(ys_hbm.at[glob_rows], local.at[s, loc_rows], sem.at[s])
        return make_copy

    @pl.when(i == 0)
    def _():
        local[...] = jnp.zeros_like(local)
        _start_segments(0, glob_ref, loc_ref, cnt_ref, copy_into(0))

    @pl.when(i + 1 < nt)
    def _():
        _start_segments(i + 1, glob_ref, loc_ref, cnt_ref, copy_into(1 - slot))
    _wait_segments(i, total_ref, copy_into(slot))

    def finish(nrows):
        j = lax.broadcasted_iota(I32, (TT, nrows), 1).astype(F32)
        wc = jnp.zeros((TT, nrows), F32)
        for k in range(TOP_K):
            wc = jnp.where(j == pw_ref[:, k:k + 1], pw_ref[:, TOP_K + k:TOP_K + k + 1], wc)
        acc = x1_ref[...] + _dot(wc.astype(BF16), _unpack_halves(local[slot, 0:nrows, :]))
        o_ref[...] = _rms_norm(acc, gfin_ref[...])

    @pl.when(total_ref[i] <= LROWS_MAIN)
    def _():
        finish(LROWS_MAIN)

    @pl.when(total_ref[i] > LROWS_MAIN)
    def _():
        finish(LROWS)


def _combine(seg_glob, seg_loc, seg_cnt, tile_rd, x1, pw, gfin, ys):
    n = x1.shape[0]
    grid_spec = pltpu.PrefetchScalarGridSpec(
        num_scalar_prefetch=4,
        grid=(n // TT,),
        in_specs=[
            pl.BlockSpec((TT, D_MODEL), lambda i, *_: (i, 0)),
            pl.BlockSpec((TT, LANES), lambda i, *_: (i, 0)),
            pl.BlockSpec((1, D_MODEL), lambda i, *_: (0, 0)),
            pl.BlockSpec(memory_space=pl.ANY),
        ],
        out_specs=pl.BlockSpec((TT, D_MODEL), lambda i, *_: (i, 0)),
        scratch_shapes=[pltpu.VMEM((2, LROWS, PACKED), U32),
                        pltpu.SemaphoreType.DMA((2,))],
    )
    return pl.pallas_call(
        _combine_body,
        grid_spec=grid_spec,
        out_shape=jax.ShapeDtypeStruct((n, D_MODEL), F32),
        compiler_params=pltpu.CompilerParams(
            dimension_semantics=("arbitrary",), vmem_limit_bytes=VMEM_LIMIT),
        name="combine",
    )(seg_glob, seg_loc, seg_cnt, tile_rd, x1, pw, gfin, ys)


def _layer(x2, pos_b, invf, norm_mix_g, w_in, b_in, attn_sinks, w_o_attn, b_o_attn,
           w_pool_mix, pool_scale, w_pool_up, w_out, norm_ffn_g, w_router, b_router,
           w_gate, b_gate, w_up, b_up, w_down, b_down, out_g):
    n = x2.shape[0]
    q, kb, vb, u, gates = _inproj(
        x2, pos_b, invf, norm_mix_g[None, :], w_in.astype(BF16), b_in[None, :])
    attn = _attn(attn_sinks, q, kb, vb)

    wrt = w_router.T.astype(BF16)
    brt = jnp.broadcast_to(b_router[:, None], (N_EXPERTS, TM_POST))
    x1, h2, pos_t, pw, tc = _post(
        attn, u, gates, x2, w_o_attn.astype(BF16), b_o_attn[None, :],
        w_pool_mix.astype(BF16), pool_scale[None, :], w_pool_up.astype(BF16),
        w_out.astype(BF16), norm_ffn_g[None, :], wrt, brt)

    nt = n // TT
    sub = TM_POST // TT
    seg_cnt = (tc.reshape(n // TM_POST, N_EXPERTS, LANES)[:, :, :sub]
               .transpose(0, 2, 1).reshape(nt, N_EXPERTS).astype(I32))
    before = jnp.cumsum(seg_cnt, axis=0) - seg_cnt
    sizes = jnp.sum(seg_cnt, axis=0)
    padded = (sizes + TM_MOE - 1) // TM_MOE * TM_MOE
    pends = jnp.cumsum(padded)
    pstarts = pends - padded
    head = before % SEG_ALIGN
    last_tile = (jnp.arange(nt, dtype=I32) == nt - 1)[:, None]
    blocks_up = (head + seg_cnt + SEG_ALIGN - 1) // SEG_ALIGN
    blocks_dn = (head + seg_cnt) // SEG_ALIGN
    seg_rd = jnp.where((seg_cnt > 0) | (last_tile & (head > 0)), blocks_up, 0) * SEG_ALIGN
    seg_wr = jnp.where(last_tile, seg_rd, jnp.where(seg_cnt > 0, blocks_dn * SEG_ALIGN, 0))
    seg_loc = jnp.cumsum(seg_rd, axis=1) - seg_rd
    seg_glob = pstarts[None, :] + before - head
    tile_rd = jnp.sum(seg_rd, axis=1)
    tile_wr = jnp.sum(seg_wr, axis=1)
    seg_glob, seg_loc, seg_rd, seg_wr = (a.reshape(-1) for a in (seg_glob, seg_loc, seg_rd, seg_wr))
    m = n * TOP_K
    n_tiles = (m + N_EXPERTS * (TM_MOE - 1) + TM_MOE - 1) // TM_MOE
    tile_start = jnp.arange(n_tiles, dtype=I32) * TM_MOE
    tile_expert = jnp.minimum(
        jnp.sum(tile_start[:, None] >= pends[None, :], axis=-1), N_EXPERTS - 1).astype(I32)
    own = tile_expert[:, None] == jnp.arange(N_EXPERTS, dtype=I32)[None, :]
    left = jnp.sum(jnp.where(own, (pstarts + sizes)[None, :] - tile_start[:, None], 0), axis=-1)
    tile_rows = jnp.clip(left, 0, TM_MOE)
    ids = jnp.arange(N_EXPERTS, dtype=I32)
    later = jnp.where((sizes > 0)[None, :] & (ids[None, :] > ids[:, None]), ids[None, :], N_EXPERTS)
    next_expert = jnp.min(later, axis=1)
    next_expert = jnp.where(next_expert == N_EXPERTS, -1, next_expert)
    tile_next = jnp.sum(jnp.where(own, next_expert[None, :], 0), axis=-1).astype(I32)
    n_used = (pends[-1] // TM_MOE).astype(I32)[None]

    xs = _dispatch(seg_glob, seg_loc, seg_rd, seg_wr, tile_wr, tile_rd, h2, pos_t, n_tiles * TM_MOE)
    ys = _moe(tile_expert, n_used, tile_rows.astype(I32), tile_next, xs,
              w_gate, b_gate, w_up, b_up, w_down, b_down)
    return _combine(seg_glob, seg_loc, seg_rd, tile_rd, x1, pw, out_g[None, :], ys)


def kernel(x, positions, norm_mix_g, w_in, b_in, attn_sinks, w_o_attn, b_o_attn, w_pool_mix,
           pool_scale, w_pool_up, w_out, norm_ffn_g, w_router, b_router, w_gate, b_gate,
           w_up, b_up, w_down, b_down, norm_final_g):
    b, s, d = x.shape
    depth = w_in.shape[0]
    assert (s, d, depth) == (SEQ, D_MODEL, 1)
    n = b * s
    x2 = x.reshape(n, d)
    pos_b = jnp.repeat(positions.reshape(n // 4, 4), LANES // 4, axis=1)
    inv_freq = ROPE_THETA ** (-jnp.arange(0, HEAD_DIM, 2, dtype=F32) / HEAD_DIM)
    invf = jnp.tile(inv_freq, LANES // (HEAD_DIM // 2))[None, :]
    out = _layer(x2, pos_b, invf, norm_mix_g[0], w_in[0], b_in[0], attn_sinks[0], w_o_attn[0],
                 b_o_attn[0], w_pool_mix[0], pool_scale[0], w_pool_up[0], w_out[0],
                 norm_ffn_g[0], w_router[0], b_router[0], w_gate[0], b_gate[0], w_up[0],
                 b_up[0], w_down[0], b_down[0], norm_final_g)
    return out.reshape(b, s, d)
```

```python
import numpy as np
import jax
import jax.numpy as jnp
from jax import lax
from jax.experimental import pallas as pl
from jax.experimental.pallas import tpu as pltpu

F32 = jnp.float32
BF16 = jnp.bfloat16
I32 = jnp.int32
U32 = jnp.uint32

D_MODEL = 1024
SEQ = 4096
HEAD_DIM = 64
N_Q_HEADS = 16
WINDOW = 128
ROPE_THETA = 10000.0
Q_WIDTH = N_Q_HEADS * HEAD_DIM
KV_WIDTH = 128
POOL_WINDOWS = (2, 4, 8, 16)
POOL_WIDTH = 512
POOL_GROUP = 128
POOL_HALO = 16
POOL_PAD = 32
N_EXPERTS = 32
TOP_K = 4
SWIGLU_LIMIT = 7.0
SWIGLU_ALPHA = 1.702
RMS_EPS = 1e-5
NEG_BIG = -1e30
LOG2_E = 1.4426950408889634

LANES = 128
TM_IN = 1024
TQ = 512
TM_POST = 512
TM_MOE = 1024
MOE_ROWS = 512
TT = 256
SEG_ALIGN = 8
LROWS = TT * TOP_K + 2 * N_EXPERTS * SEG_ALIGN
MXU_DEPTH = 256
LROWS_MAIN = LROWS - MXU_DEPTH
PACKED = D_MODEL // 2
VMEM_LIMIT = 56 * 1024 * 1024


def _rms_norm(x, g):
    ms = jnp.mean(x * x, axis=-1, keepdims=True)
    return (x * lax.rsqrt(ms + RMS_EPS)) * g


def _dot(a, b):
    return jnp.dot(a, b, preferred_element_type=F32)


def _inproj_body(x_ref, pos_ref, invf_ref, g_ref, w_ref, b_ref,
                 q_ref, kb_ref, vb_ref, u_ref, gate_ref, cos_scr, sin_scr):
    h = _rms_norm(x_ref[...], g_ref[...]).astype(BF16)
    qk_w = Q_WIDTH + KV_WIDTH
    v0 = qk_w
    u0 = v0 + KV_WIDTH
    g0 = u0 + POOL_WIDTH

    def gate_chunk(c):
        sl = slice(g0 + c * D_MODEL, g0 + (c + 1) * D_MODEL)
        zg = _dot(h, w_ref[:, sl]) + b_ref[:, sl]
        gate_ref[:, c * D_MODEL:(c + 1) * D_MODEL] = jax.nn.sigmoid(zg).astype(BF16)

    gate_chunk(0)

    theta = pos_ref[...].astype(F32) * invf_ref[...]
    lane4 = lax.broadcasted_iota(I32, theta.shape, 1)
    for packed, scr in ((jnp.cos(theta), cos_scr), (jnp.sin(theta), sin_scr)):
        for jt in range(4):
            seg = packed if jt == 0 else pltpu.roll(packed, LANES - 32 * jt, 1)
            seg = jnp.where(lane4 < 32, seg, pltpu.roll(seg, 32, 1))
            seg = jnp.where(lane4 < 64, seg, pltpu.roll(seg, 64, 1))
            scr[pl.ds(jt, TM_IN // 4, stride=4), :] = seg
    cos = cos_scr[...]
    sin = sin_scr[...]
    lane = lax.broadcasted_iota(I32, cos.shape, 1)
    first_half = (lane & 32) == 0
    sin_signed = jnp.where(first_half, -sin, sin)
    low_head = lane < HEAD_DIM

    def rope(t):
        swapped = jnp.where(first_half, pltpu.roll(t, 96, 1), pltpu.roll(t, 32, 1))
        return t * cos + swapped * sin_signed

    def band_layout(t, out_ref):
        tr = pltpu.roll(t, 64, 1)
        zero = jnp.zeros_like(t)
        chunks = (jnp.where(low_head, t, zero), jnp.where(low_head, zero, tr),
                  jnp.where(low_head, tr, zero), jnp.where(low_head, zero, t))
        for c, val in enumerate(chunks):
            out_ref[:, c * LANES:(c + 1) * LANES] = val.astype(BF16)

    zqk = _dot(h, w_ref[:, :qk_w]) + b_ref[:, :qk_w]
    gate_chunk(1)
    scale = HEAD_DIM ** -0.5 * LOG2_E
    for j in range(Q_WIDTH // LANES):
        sl = slice(j * LANES, (j + 1) * LANES)
        q_ref[:, sl] = (rope(zqk[:, sl]) * scale).astype(BF16)
    band_layout(rope(zqk[:, Q_WIDTH:qk_w]), kb_ref)

    zv = _dot(h, w_ref[:, v0:v0 + KV_WIDTH]) + b_ref[:, v0:v0 + KV_WIDTH]
    band_layout(zv, vb_ref)
    u_ref[...] = _dot(h, w_ref[:, u0:u0 + POOL_WIDTH]) + b_ref[:, u0:u0 + POOL_WIDTH]


def _inproj(x2, pos_b, invf, g, w, b):
    n = x2.shape[0]
    in_width = w.shape[1]
    row = lambda i: (i, 0)
    const = lambda i: (0, 0)
    return pl.pallas_call(
        _inproj_body,
        grid=(n // TM_IN,),
        in_specs=[
            pl.BlockSpec((TM_IN, D_MODEL), row),
            pl.BlockSpec((TM_IN // 4, LANES), row),
            pl.BlockSpec((1, LANES), const),
            pl.BlockSpec((1, D_MODEL), const),
            pl.BlockSpec((D_MODEL, in_width), const),
            pl.BlockSpec((1, in_width), const),
        ],
        out_specs=[
            pl.BlockSpec((TM_IN, Q_WIDTH), row),
            pl.BlockSpec((TM_IN, 4 * LANES), row),
            pl.BlockSpec((TM_IN, 4 * LANES), row),
            pl.BlockSpec((TM_IN, POOL_WIDTH), row),
            pl.BlockSpec((TM_IN, 2 * D_MODEL), row),
        ],
        out_shape=[
            jax.ShapeDtypeStruct((n, Q_WIDTH), BF16),
            jax.ShapeDtypeStruct((n, 4 * LANES), BF16),
            jax.ShapeDtypeStruct((n, 4 * LANES), BF16),
            jax.ShapeDtypeStruct((n, POOL_WIDTH), F32),
            jax.ShapeDtypeStruct((n, 2 * D_MODEL), BF16),
        ],
        scratch_shapes=[pltpu.VMEM((TM_IN, LANES), F32), pltpu.VMEM((TM_IN, LANES), F32)],
        compiler_params=pltpu.CompilerParams(
            dimension_semantics=("arbitrary",), vmem_limit_bytes=VMEM_LIMIT),
        name="inproj",
    )(x2, pos_b, invf, g, w, b)


def _attn_body(sinks_ref, q_ref, kbc_ref, kbp_ref, vbc_ref, vbp_ref, bias_ref,
               o_ref, p_scr):
    t = pl.program_id(0)
    seq_first = (t % (SEQ // TQ)) == 0
    lane = lax.broadcasted_iota(I32, (WINDOW, LANES), 1)
    low_head = lane < HEAD_DIM
    key_row = lax.broadcasted_iota(I32, (4 * WINDOW, LANES), 0)
    key_lane = lax.broadcasted_iota(I32, (4 * WINDOW, LANES), 1)
    head_ones = jnp.where((key_row < 2 * WINDOW) == (key_lane < HEAD_DIM), 1.0, 0.0).astype(BF16)
    for n in range(TQ // WINDOW):
        rows = slice(n * WINDOW, (n + 1) * WINDOW)
        if n == 0:
            k_prev, v_prev = kbp_ref[...], vbp_ref[...]
            bias = bias_ref[seq_first.astype(I32)]
        else:
            prev_rows = slice((n - 1) * WINDOW, n * WINDOW)
            k_prev, v_prev = kbc_ref[prev_rows, :], vbc_ref[prev_rows, :]
            bias = bias_ref[0]
        k_cur, v_cur = kbc_ref[rows, :], vbc_ref[rows, :]
        for g in range(2):
            def stack(prev, cur):
                lo = slice((2 * g) * LANES, (2 * g + 1) * LANES)
                hi = slice((2 * g + 1) * LANES, (2 * g + 2) * LANES)
                return jnp.concatenate([prev[:, lo], cur[:, lo], prev[:, hi], cur[:, hi]], axis=0)
            kmat = stack(k_prev, k_cur)
            vmat = stack(v_prev, v_cur)
            qs = jnp.concatenate(
                [q_ref[rows, (4 * g + p) * LANES:(4 * g + p + 1) * LANES] for p in range(4)],
                axis=0)
            s = lax.dot_general(qs, kmat, (((1,), (1,)), ((), ())),
                                preferred_element_type=F32) + bias
            sink_terms = []
            for p in range(4):
                pr = slice(p * WINDOW, (p + 1) * WINDOW)
                ms = []
                for j in range(2):
                    cols = slice(j * 2 * WINDOW, (j + 1) * 2 * WINDOW)
                    sp = s[pr, cols]
                    sink = sinks_ref[8 * g + 2 * p + j] * LOG2_E
                    m = jnp.maximum(jnp.max(sp, axis=-1, keepdims=True), sink)
                    p_scr[pr, cols] = jnp.exp2(sp - m).astype(BF16)
                    ms.append(m)
                sink_pair = jnp.where(low_head, sinks_ref[8 * g + 2 * p] * LOG2_E,
                                      sinks_ref[8 * g + 2 * p + 1] * LOG2_E)
                sink_terms.append(jnp.exp2(sink_pair - jnp.where(low_head, ms[0], ms[1])))
            o = _dot(p_scr[...], jnp.concatenate([vmat, head_ones], axis=1))
            for p in range(4):
                pr = slice(p * WINDOW, (p + 1) * WINDOW)
                inv = 1.0 / (o[pr, LANES:] + sink_terms[p])
                o_ref[rows, (4 * g + p) * LANES:(4 * g + p + 1) * LANES] = (o[pr, :LANES] * inv).astype(BF16)


def _attn_bias():
    r = np.arange(4 * WINDOW)[:, None] % WINDOW
    c = np.arange(4 * WINDOW)[None, :] % (2 * WINDOW)
    band = (c > r) & (c <= r + WINDOW)
    first = band & (c >= WINDOW)
    return np.stack([np.where(band, 0.0, NEG_BIG), np.where(first, 0.0, NEG_BIG)]).astype(np.float32)


def _attn(sinks, q, kb, vb):
    n = q.shape[0]
    blocks_per_tile = TQ // WINDOW
    cur = lambda t: (t, 0)
    prev = lambda t: (jnp.maximum(t * blocks_per_tile - 1, 0), 0)
    bias = jnp.asarray(_attn_bias())
    return pl.pallas_call(
        _attn_body,
        grid=(n // TQ,),
        in_specs=[
            pl.BlockSpec(memory_space=pltpu.SMEM),
            pl.BlockSpec((TQ, Q_WIDTH), cur),
            pl.BlockSpec((TQ, 4 * LANES), cur),
            pl.BlockSpec((WINDOW, 4 * LANES), prev),
            pl.BlockSpec((TQ, 4 * LANES), cur),
            pl.BlockSpec((WINDOW, 4 * LANES), prev),
            pl.BlockSpec((2, 4 * WINDOW, 4 * WINDOW), lambda t: (0, 0, 0)),
        ],
        out_specs=pl.BlockSpec((TQ, Q_WIDTH), cur),
        out_shape=jax.ShapeDtypeStruct((n, Q_WIDTH), BF16),
        scratch_shapes=[pltpu.VMEM((4 * WINDOW, 4 * WINDOW), BF16)],
        compiler_params=pltpu.CompilerParams(
            dimension_semantics=("arbitrary",), vmem_limit_bytes=VMEM_LIMIT),
        name="attn",
    )(sinks, q, kb, kb, vb, vb, bias)


def _post_body(attn_ref, u_ref, uprev_ref, gate_ref, x_ref,
               wo_ref, bo_ref, wmix_ref, pscale_ref, wup_ref, wout_ref,
               gffn_ref, wrt_ref, brt_ref,
               x1_ref, h2_ref, pos_t_ref, pw_ref, tc_ref,
               win_a, win_b, carry):
    i = pl.program_id(0)
    tiles_per_seq = SEQ // TM_POST
    seq_first = (i % tiles_per_seq) == 0

    @pl.when(i == 0)
    def _():
        carry[...] = jnp.zeros_like(carry)

    rows = TM_POST + POOL_PAD
    win_a[0:POOL_PAD - POOL_HALO, :] = jnp.zeros((POOL_PAD - POOL_HALO, POOL_WIDTH), F32)
    win_a[POOL_PAD - POOL_HALO:POOL_PAD, :] = jnp.where(seq_first, 0.0, uprev_ref[...])
    win_a[POOL_PAD:, :] = u_ref[...]
    src, dst = win_a, win_b
    sums = {}
    for level, w in enumerate((1, 2, 4, 8)):
        lo = 8 * (level + 1)
        c0 = level * POOL_GROUP
        dst[lo:, c0:] = src[lo:, c0:] + src[lo - w:rows - w, c0:]
        sums[2 * w] = dst
        src, dst = dst, (win_a if dst is win_b else win_b)
    row = lax.broadcasted_iota(I32, (TM_POST, 1), 0)
    tpos = (i % tiles_per_seq) * TM_POST + row
    mixed = []
    for gi, w in enumerate(POOL_WINDOWS):
        cols = slice(gi * POOL_GROUP, (gi + 1) * POOL_GROUP)
        cnt = jnp.minimum(tpos + 1, w).astype(F32)
        pooled = sums[w][POOL_PAD:, cols] / cnt - u_ref[:, cols]
        mixed.append(_dot(pooled.astype(BF16), wmix_ref[gi]) * pscale_ref[:, cols])
    mixed = jnp.concatenate(mixed, axis=-1).astype(BF16)
    attn_bf = attn_ref[...]
    merged = []
    for c0 in range(0, D_MODEL, MXU_DEPTH):
        cols = slice(c0, c0 + MXU_DEPTH)
        y_attn = _dot(attn_bf, wo_ref[:, cols]) + bo_ref[:, cols]
        y_pool = _dot(mixed, wup_ref[:, cols])
        merged.append((gate_ref[:, cols].astype(F32) * y_attn
                       + gate_ref[:, D_MODEL + c0:D_MODEL + c0 + MXU_DEPTH].astype(F32) * y_pool
                       ).astype(BF16))
    x1 = x_ref[...] + _dot(jnp.concatenate(merged, axis=1), wout_ref[...])
    x1_ref[...] = x1
    h2 = _rms_norm(x1, gffn_ref[...])
    h2_bf = h2.astype(BF16)
    h2_ref[...] = h2_bf

    logits_t = lax.dot_general(wrt_ref[...], h2_bf, (((1,), (1,)), ((), ())),
                               preferred_element_type=F32) + brt_ref[...]
    erow = lax.broadcasted_iota(I32, (N_EXPERTS, TM_POST), 0)
    work = logits_t
    top_v, onehots = [], []
    for _ in range(TOP_K):
        m = jnp.max(work, axis=0, keepdims=True)
        idx = jnp.min(jnp.where(work == m, erow, N_EXPERTS), axis=0, keepdims=True)
        hit = erow == idx
        top_v.append(m)
        onehots.append(hit)
        work = jnp.where(hit, -jnp.inf, work)
    exps = [jnp.exp(v - top_v[0]) for v in top_v]
    denom = exps[0] + exps[1] + exps[2] + exps[3]
    comb = [e / denom for e in exps]
    chosen = jnp.zeros((N_EXPERTS, TM_POST), F32)
    for hit in onehots:
        chosen = chosen + hit.astype(F32)

    e_r = lax.broadcasted_iota(I32, (N_EXPERTS, N_EXPERTS), 0)
    e_c = lax.broadcasted_iota(I32, (N_EXPERTS, N_EXPERTS), 1)
    earlier_expert = (e_c < e_r).astype(BF16)
    t_r = lax.broadcasted_iota(I32, (TT, TT), 0)
    t_c = lax.broadcasted_iota(I32, (TT, TT), 1)
    earlier_token = (t_r < t_c).astype(BF16)
    lane = lax.broadcasted_iota(I32, (N_EXPERTS, LANES), 1)
    counts = jnp.zeros((N_EXPERTS, LANES), F32)
    pos_parts = [[] for _ in range(TOP_K)]
    n_sub = TM_POST // TT
    for s in range(n_sub):
        cols = slice(s * TT, (s + 1) * TT)
        ch = chosen[:, cols]
        cnt = jnp.broadcast_to(jnp.sum(ch, axis=1, keepdims=True), (N_EXPERTS, LANES))
        before = carry[...]
        head = before - SEG_ALIGN * jnp.floor(before / SEG_ALIGN)
        flush = jnp.logical_and(i == pl.num_programs(0) - 1, s == n_sub - 1)
        present = (cnt > 0) | (flush & (head > 0))
        blocks = jnp.where(present, jnp.floor((head + cnt + (SEG_ALIGN - 1)) / SEG_ALIGN), 0.0)
        seg_loc = SEG_ALIGN * _dot(earlier_expert, blocks.astype(BF16))
        base = seg_loc + head
        rank_in_tile = _dot(ch.astype(BF16), earlier_token)
        full = jnp.concatenate([base] * (TT // LANES), axis=1) + rank_in_tile
        for k in range(TOP_K):
            pos_parts[k].append(
                jnp.sum(jnp.where(onehots[k][:, cols], full, 0.0), axis=0, keepdims=True))
        counts = jnp.where(lane == s, cnt, counts)
        carry[...] = before + cnt
    tc_ref[...] = counts

    pos = [jnp.concatenate(parts, axis=1) for parts in pos_parts]
    row8 = lax.broadcasted_iota(I32, (8, TM_POST), 0)
    pos_rows = jnp.full((8, TM_POST), -1.0, F32)
    both = jnp.zeros((8, TM_POST), F32)
    for k in range(TOP_K):
        pos_rows = jnp.where(row8 == k, pos[k], pos_rows)
        both = jnp.where(row8 == k, pos[k], both)
        both = jnp.where(row8 == TOP_K + k, comb[k], both)
    pos_t_ref[...] = pos_rows.astype(I32)
    padded = jnp.concatenate([both, jnp.zeros((LANES - 8, TM_POST), F32)], axis=0)
    pw_ref[...] = padded.T


def _post(attn, u, gates, x2, wo, bo, wmix, pscale, wup, wout, gffn, wr, br):
    n = x2.shape[0]
    row = lambda i: (i, 0)
    const = lambda i: (0, 0)
    halo_blocks = TM_POST // POOL_HALO
    prev = lambda i: (jnp.maximum(i * halo_blocks - 1, 0), 0)
    return pl.pallas_call(
        _post_body,
        grid=(n // TM_POST,),
        in_specs=[
            pl.BlockSpec((TM_POST, Q_WIDTH), row),
            pl.BlockSpec((TM_POST, POOL_WIDTH), row),
            pl.BlockSpec((POOL_HALO, POOL_WIDTH), prev),
            pl.BlockSpec((TM_POST, 2 * D_MODEL), row),
            pl.BlockSpec((TM_POST, D_MODEL), row),
            pl.BlockSpec((Q_WIDTH, D_MODEL), const),
            pl.BlockSpec((1, D_MODEL), const),
            pl.BlockSpec((len(POOL_WINDOWS), POOL_GROUP, POOL_GROUP), lambda i: (0, 0, 0)),
            pl.BlockSpec((1, POOL_WIDTH), const),
            pl.BlockSpec((POOL_WIDTH, D_MODEL), const),
            pl.BlockSpec((D_MODEL, D_MODEL), const),
            pl.BlockSpec((1, D_MODEL), const),
            pl.BlockSpec((N_EXPERTS, D_MODEL), const),
            pl.BlockSpec((N_EXPERTS, TM_POST), const),
        ],
        out_specs=[
            pl.BlockSpec((TM_POST, D_MODEL), row),
            pl.BlockSpec((TM_POST, D_MODEL), row),
            pl.BlockSpec((8, TM_POST), lambda i: (0, i)),
            pl.BlockSpec((TM_POST, LANES), row),
            pl.BlockSpec((N_EXPERTS, LANES), row),
        ],
        out_shape=[
            jax.ShapeDtypeStruct((n, D_MODEL), F32),
            jax.ShapeDtypeStruct((n, D_MODEL), BF16),
            jax.ShapeDtypeStruct((8, n), I32),
            jax.ShapeDtypeStruct((n, LANES), F32),
            jax.ShapeDtypeStruct((n // TM_POST * N_EXPERTS, LANES), F32),
        ],
        scratch_shapes=[pltpu.VMEM((TM_POST + POOL_PAD, POOL_WIDTH), F32),
                        pltpu.VMEM((TM_POST + POOL_PAD, POOL_WIDTH), F32),
                        pltpu.VMEM((N_EXPERTS, LANES), F32)],
        compiler_params=pltpu.CompilerParams(
            dimension_semantics=("arbitrary",), vmem_limit_bytes=VMEM_LIMIT),
        name="post",
    )(attn, u, u, gates, x2, wo, bo, wmix, pscale, wup, wout, gffn, wr, br)


def _start_segments(i, glob_ref, loc_ref, cnt_ref, make_copy):
    def body(pair, c):
        for priority in range(2):
            idx = i * N_EXPERTS + 2 * pair + priority
            rows = pl.multiple_of(cnt_ref[idx], SEG_ALIGN)

            @pl.when(rows > 0)
            def _():
                loc = pl.multiple_of(loc_ref[idx], SEG_ALIGN)
                glob = pl.multiple_of(glob_ref[idx], SEG_ALIGN)
                make_copy(pl.ds(loc, rows), pl.ds(glob, rows)).start(priority=priority)
        return c
    lax.fori_loop(0, N_EXPERTS // 2, body, 0, unroll=2)


def _wait_segments(i, total_ref, make_copy):
    rows = pl.multiple_of(total_ref[i], SEG_ALIGN)

    @pl.when(rows > 0)
    def _():
        make_copy(pl.ds(0, rows), pl.ds(0, rows)).wait()


def _pack_halves(v):
    c = v.shape[1] // 2
    bits = lax.bitcast_convert_type(v, U32)
    return (bits[:, :c] >> 16) | bits[:, c:]


def _unpack_halves(w):
    lo = lax.bitcast_convert_type(w << 16, F32)
    hi = lax.bitcast_convert_type(w & jnp.uint32(0xFFFF0000), F32)
    return jnp.concatenate([lo, hi], axis=1).astype(BF16)


def _dispatch_body(glob_ref, loc_ref, rd_ref, wr_ref, total_ref, rows_ref, h2_ref, pos_ref,
                   xs_hbm, local, tail, sem):
    i = pl.program_id(0)
    nt = pl.num_programs(0)
    slot = i % 2

    def copy_from(s):
        def make_copy(loc_rows, glob_rows):
            return pltpu.make_async_copy(local.at[s, loc_rows], xs_hbm.at[glob_rows], sem.at[s])
        return make_copy

    @pl.when(i == 0)
    def _():
        tail[...] = jnp.zeros_like(tail)

    @pl.when(i >= 2)
    def _():
        _wait_segments(i - 2, total_ref, copy_from(slot))

    def group(r0, nrows):
        j = lax.broadcasted_iota(I32, (nrows, TT), 0) + r0
        hit = j == pos_ref[0:1, :]
        for k in range(1, TOP_K):
            hit = hit | (j == pos_ref[k:k + 1, :])
        perm = jnp.where(hit, 1.0, 0.0).astype(BF16)
        local[slot, r0:r0 + nrows, :] = _pack_halves(_dot(perm, h2_ref[...]))

    group(0, LROWS_MAIN)

    @pl.when(rows_ref[i] + SEG_ALIGN > LROWS_MAIN)
    def _():
        group(LROWS_MAIN, LROWS - LROWS_MAIN)

    for e in range(N_EXPERTS):
        idx = i * N_EXPERTS + e
        rd = rd_ref[idx]
        wr = wr_ref[idx]
        loc = pl.multiple_of(loc_ref[idx], SEG_ALIGN)
        first = local[slot, pl.ds(loc, SEG_ALIGN), :]
        last = local[slot, pl.ds(pl.multiple_of(loc + wr, SEG_ALIGN), SEG_ALIGN), :]
        merged = jnp.where(rd > 0, first | tail[e], first)
        local[slot, pl.ds(loc, SEG_ALIGN), :] = merged
        last = jnp.where(wr == 0, merged, last)
        tail[e] = jnp.where(rd > 0, jnp.where(wr < rd, last, jnp.uint32(0)), tail[e])
    _start_segments(i, glob_ref, loc_ref, wr_ref, copy_from(slot))

    @pl.when(i == nt - 1)
    def _():
        _wait_segments(i - 1, total_ref, copy_from(1 - slot))
        _wait_segments(i, total_ref, copy_from(slot))


def _dispatch(seg_glob, seg_loc, seg_rd, seg_wr, tile_wr, tile_rd, h2, pos_t, n_rows):
    n = h2.shape[0]
    grid_spec = pltpu.PrefetchScalarGridSpec(
        num_scalar_prefetch=6,
        grid=(n // TT,),
        in_specs=[
            pl.BlockSpec((TT, D_MODEL), lambda i, *_: (i, 0)),
            pl.BlockSpec((8, TT), lambda i, *_: (0, i)),
        ],
        out_specs=pl.BlockSpec(memory_space=pl.ANY),
        scratch_shapes=[pltpu.VMEM((2, LROWS, PACKED), U32),
                        pltpu.VMEM((N_EXPERTS, SEG_ALIGN, PACKED), U32),
                        pltpu.SemaphoreType.DMA((2,))],
    )
    return pl.pallas_call(
        _dispatch_body,
        grid_spec=grid_spec,
        out_shape=jax.ShapeDtypeStruct((n_rows, PACKED), U32),
        compiler_params=pltpu.CompilerParams(
            dimension_semantics=("arbitrary",), vmem_limit_bytes=VMEM_LIMIT),
        name="dispatch",
    )(seg_glob, seg_loc, seg_rd, seg_wr, tile_wr, tile_rd, h2, pos_t)


def _moe_body(te_ref, nused_ref, rows_ref, next_ref, xs_ref, bias_ref,
              wg_hbm, wu_hbm, wd_hbm, y_ref, wf32, wbf, sem, slot_ref):
    i = pl.program_id(0)

    def fetch(expert, slot):
        return [pltpu.make_async_copy(w.at[expert], wf32.at[slot, m], sem.at[slot, m])
                for m, w in enumerate((wg_hbm, wu_hbm, wd_hbm))]

    @pl.when(i == 0)
    def _():
        slot_ref[0] = 0

    @pl.when(i < nused_ref[0])
    def _():
        expert = te_ref[i]

        @pl.when((i == 0) | (expert != te_ref[jnp.maximum(i - 1, 0)]))
        def _():
            slot = slot_ref[0]

            @pl.when(i == 0)
            def _():
                for cp in fetch(expert, slot):
                    cp.start()
            for cp in fetch(expert, slot):
                cp.wait()
            nxt = next_ref[i]

            @pl.when(nxt >= 0)
            def _():
                for cp in fetch(nxt, 1 - slot):
                    cp.start()
            for m in range(3):
                wbf[m] = wf32[slot, m].astype(BF16)
            slot_ref[0] = 1 - slot

        def mlp(r0):
            rows = slice(r0, r0 + MOE_ROWS)
            row = lax.broadcasted_iota(I32, (MOE_ROWS, 1), 0) + r0
            x = _unpack_halves(jnp.where(row < rows_ref[i], xs_ref[rows, :], jnp.uint32(0)))
            bias = bias_ref[expert]
            g = _dot(x, wbf[0]) + bias[0:1, :]
            u = _dot(x, wbf[1]) + bias[1:2, :]
            g = jnp.minimum(g, SWIGLU_LIMIT)
            u = jnp.clip(u, -SWIGLU_LIMIT, SWIGLU_LIMIT)
            a = (g * jax.nn.sigmoid(SWIGLU_ALPHA * g) * (u + 1.0)).astype(BF16)
            y = _dot(a, wbf[2]) + bias[2:3, :]
            y_ref[rows, :] = _pack_halves(y.astype(BF16).astype(F32))

        mlp(0)
        for r0 in range(MOE_ROWS, TM_MOE, MOE_ROWS):
            @pl.when(rows_ref[i] > r0)
            def _():
                mlp(r0)


def _moe(tile_expert, n_used, tile_rows, tile_next, xs, wg, bg, wu, bu, wd, bd):
    n_rows = xs.shape[0]
    n_tiles = n_rows // TM_MOE
    d_ff = wg.shape[2]
    assert d_ff == D_MODEL
    row = lambda i, te, nu, *_: (jnp.minimum(i, nu[0] - 1), 0)
    bias = jnp.zeros((N_EXPERTS, 8, D_MODEL), F32)
    bias = bias.at[:, 0].set(bg).at[:, 1].set(bu).at[:, 2].set(bd)
    grid_spec = pltpu.PrefetchScalarGridSpec(
        num_scalar_prefetch=4,
        grid=(n_tiles,),
        in_specs=[
            pl.BlockSpec((TM_MOE, PACKED), row),
            pl.BlockSpec((N_EXPERTS, 8, D_MODEL), lambda i, *_: (0, 0, 0)),
            pl.BlockSpec(memory_space=pl.ANY),
            pl.BlockSpec(memory_space=pl.ANY),
            pl.BlockSpec(memory_space=pl.ANY),
        ],
        out_specs=pl.BlockSpec((TM_MOE, PACKED), row),
        scratch_shapes=[pltpu.VMEM((2, 3, D_MODEL, D_MODEL), F32),
                        pltpu.VMEM((3, D_MODEL, D_MODEL), BF16),
                        pltpu.SemaphoreType.DMA((2, 3)),
                        pltpu.SMEM((1,), I32)],
    )
    return pl.pallas_call(
        _moe_body,
        grid_spec=grid_spec,
        out_shape=jax.ShapeDtypeStruct((n_rows, PACKED), U32),
        compiler_params=pltpu.CompilerParams(
            dimension_semantics=("arbitrary",), vmem_limit_bytes=VMEM_LIMIT),
        name="moe",
    )(tile_expert, n_used, tile_rows, tile_next, xs, bias, wg, wu, wd)


def _combine_body(glob_ref, loc_ref, cnt_ref, total_ref, x1_ref, pw_ref, gfin_ref,
                  ys_hbm, o_ref, local, sem):
    i = pl.program_id(0)
    nt = pl.num_programs(0)
    slot = i % 2

    def copy_into(s):
        def make_copy(loc_rows, glob_rows):
            return pltpu.make_async_copy(ys_hbm.at[glob_rows], local.at[s, loc_rows], sem.at[s])
        return make_copy

    @pl.when(i == 0)
    def _():
        local[...] = jnp.zeros_like(local)
        _start_segments(0, glob_ref, loc_ref, cnt_ref, copy_into(0))

    @pl.when(i + 1 < nt)
    def _():
        _start_segments(i + 1, glob_ref, loc_ref, cnt_ref, copy_into(1 - slot))
    _wait_segments(i, total_ref, copy_into(slot))

    def finish(nrows):
        j = lax.broadcasted_iota(I32, (TT, nrows), 1).astype(F32)
        wc = jnp.zeros((TT, nrows), F32)
        for k in range(TOP_K):
            wc = jnp.where(j == pw_ref[:, k:k + 1], pw_ref[:, TOP_K + k:TOP_K + k + 1], wc)
        acc = x1_ref[...] + _dot(wc.astype(BF16), _unpack_halves(local[slot, 0:nrows, :]))
        o_ref[...] = _rms_norm(acc, gfin_ref[...])

    @pl.when(total_ref[i] <= LROWS_MAIN)
    def _():
        finish(LROWS_MAIN)

    @pl.when(total_ref[i] > LROWS_MAIN)
    def _():
        finish(LROWS)


def _combine(seg_glob, seg_loc, seg_cnt, tile_rd, x1, pw, gfin, ys):
    n = x1.shape[0]
    grid_spec = pltpu.PrefetchScalarGridSpec(
        num_scalar_prefetch=4,
        grid=(n // TT,),
        in_specs=[
            pl.BlockSpec((TT, D_MODEL), lambda i, *_: (i, 0)),
            pl.BlockSpec((TT, LANES), lambda i, *_: (i, 0)),
            pl.BlockSpec((1, D_MODEL), lambda i, *_: (0, 0)),
            pl.BlockSpec(memory_space=pl.ANY),
        ],
        out_specs=pl.BlockSpec((TT, D_MODEL), lambda i, *_: (i, 0)),
        scratch_shapes=[pltpu.VMEM((2, LROWS, PACKED), U32),
                        pltpu.SemaphoreType.DMA((2,))],
    )
    return pl.pallas_call(
        _combine_body,
        grid_spec=grid_spec,
        out_shape=jax.ShapeDtypeStruct((n, D_MODEL), F32),
        compiler_params=pltpu.CompilerParams(
            dimension_semantics=("arbitrary",), vmem_limit_bytes=VMEM_LIMIT),
        name="combine",
    )(seg_glob, seg_loc, seg_cnt, tile_rd, x1, pw, gfin, ys)


def _layer(x2, pos_b, invf, norm_mix_g, w_in, b_in, attn_sinks, w_o_attn, b_o_attn,
           w_pool_mix, pool_scale, w_pool_up, w_out, norm_ffn_g, w_router, b_router,
           w_gate, b_gate, w_up, b_up, w_down, b_down, out_g):
    n = x2.shape[0]
    q, kb, vb, u, gates = _inproj(
        x2, pos_b, invf, norm_mix_g[None, :], w_in.astype(BF16), b_in[None, :])
    attn = _attn(attn_sinks, q, kb, vb)

    wrt = w_router.T.astype(BF16)
    brt = jnp.broadcast_to(b_router[:, None], (N_EXPERTS, TM_POST))
    x1, h2, pos_t, pw, tc = _post(
        attn, u, gates, x2, w_o_attn.astype(BF16), b_o_attn[None, :],
        w_pool_mix.astype(BF16), pool_scale[None, :], w_pool_up.astype(BF16),
        w_out.astype(BF16), norm_ffn_g[None, :], wrt, brt)

    nt = n // TT
    sub = TM_POST // TT
    seg_cnt = (tc.reshape(n // TM_POST, N_EXPERTS, LANES)[:, :, :sub]
               .transpose(0, 2, 1).reshape(nt, N_EXPERTS).astype(I32))
    before = jnp.cumsum(seg_cnt, axis=0) - seg_cnt
    sizes = jnp.sum(seg_cnt, axis=0)
    padded = (sizes + TM_MOE - 1) // TM_MOE * TM_MOE
    pends = jnp.cumsum(padded)
    pstarts = pends - padded
    head = before % SEG_ALIGN
    last_tile = (jnp.arange(nt, dtype=I32) == nt - 1)[:, None]
    blocks_up = (head + seg_cnt + SEG_ALIGN - 1) // SEG_ALIGN
    blocks_dn = (head + seg_cnt) // SEG_ALIGN
    seg_rd = jnp.where((seg_cnt > 0) | (last_tile & (head > 0)), blocks_up, 0) * SEG_ALIGN
    seg_wr = jnp.where(last_tile, seg_rd, jnp.where(seg_cnt > 0, blocks_dn * SEG_ALIGN, 0))
    seg_loc = jnp.cumsum(seg_rd, axis=1) - seg_rd
    seg_glob = pstarts[None, :] + before - head
    tile_rd = jnp.sum(seg_rd, axis=1)
    tile_wr = jnp.sum(seg_wr, axis=1)
    seg_glob, seg_loc, seg_rd, seg_wr = (a.reshape(-1) for a in (seg_glob, seg_loc, seg_rd, seg_wr))
    m = n * TOP_K
    n_tiles = (m + N_EXPERTS * (TM_MOE - 1) + TM_MOE - 1) // TM_MOE
    tile_start = jnp.arange(n_tiles, dtype=I32) * TM_MOE
    tile_expert = jnp.minimum(
        jnp.sum(tile_start[:, None] >= pends[None, :], axis=-1), N_EXPERTS - 1).astype(I32)
    own = tile_expert[:, None] == jnp.arange(N_EXPERTS, dtype=I32)[None, :]
    left = jnp.sum(jnp.where(own, (pstarts + sizes)[None, :] - tile_start[:, None], 0), axis=-1)
    tile_rows = jnp.clip(left, 0, TM_MOE)
    ids = jnp.arange(N_EXPERTS, dtype=I32)
    later = jnp.where((sizes > 0)[None, :] & (ids[None, :] > ids[:, None]), ids[None, :], N_EXPERTS)
    next_expert = jnp.min(later, axis=1)
    next_expert = jnp.where(next_expert == N_EXPERTS, -1, next_expert)
    tile_next = jnp.sum(jnp.where(own, next_expert[None, :], 0), axis=-1).astype(I32)
    n_used = (pends[-1] // TM_MOE).astype(I32)[None]

    xs = _dispatch(seg_glob, seg_loc, seg_rd, seg_wr, tile_wr, tile_rd, h2, pos_t, n_tiles * TM_MOE)
    ys = _moe(tile_expert, n_used, tile_rows.astype(I32), tile_next, xs,
              w_gate, b_gate, w_up, b_up, w_down, b_down)
    return _combine(seg_glob, seg_loc, seg_rd, tile_rd, x1, pw, out_g[None, :], ys)


def kernel(x, positions, norm_mix_g, w_in, b_in, attn_sinks, w_o_attn, b_o_attn, w_pool_mix,
           pool_scale, w_pool_up, w_out, norm_ffn_g, w_router, b_router, w_gate, b_gate,
           w_up, b_up, w_down, b_down, norm_final_g):
    b, s, d = x.shape
    depth = w_in.shape[0]
    assert (s, d, depth) == (SEQ, D_MODEL, 1)
    n = b * s
    x2 = x.reshape(n, d)
    pos_b = jnp.repeat(positions.reshape(n // 4, 4), LANES // 4, axis=1)
    inv_freq = ROPE_THETA ** (-jnp.arange(0, HEAD_DIM, 2, dtype=F32) / HEAD_DIM)
    invf = jnp.tile(inv_freq, LANES // (HEAD_DIM // 2))[None, :]
    out = _layer(x2, pos_b, invf, norm_mix_g[0], w_in[0], b_in[0], attn_sinks[0], w_o_attn[0],
                 b_o_attn[0], w_pool_mix[0], pool_scale[0], w_pool_up[0], w_out[0],
                 norm_ffn_g[0], w_router[0], b_router[0], w_gate[0], b_gate[0], w_up[0],
                 b_up[0], w_down[0], b_down[0], norm_final_g)
    return out.reshape(b, s, d)
```
